```python
import math
import jax, jax.numpy as jnp
from jax import lax
import numpy as np

D_MODEL = 1024
BATCH = 8
SEQ = 8192
DEPTH = 2

DA_HEADS = 4
DA_HEAD_DIM = 64
DA_V_DIM = 2 * DA_HEAD_DIM
ROPE_THETA = 500000.0
ROPE_ROT_DIM = DA_HEAD_DIM // 4
Q_BLOCK = 128
RET_HEADS = 4
RET_QK_DIM = 64
RET_V_DIM = 2 * RET_QK_DIM
RET_CHUNK = 128
RET_ROPE_THETA = 10000.0
DA_Q_W = DA_HEADS * 2 * DA_HEAD_DIM
DA_K_W = DA_HEADS * 2 * DA_HEAD_DIM
DA_V_W = DA_HEADS * DA_V_DIM
RET_Q_W = RET_HEADS * RET_QK_DIM
RET_K_W = RET_HEADS * RET_QK_DIM
RET_V_W = RET_HEADS * RET_V_DIM
RET_G_W = RET_V_W
GATE_W = 2 * D_MODEL
IN_W = DA_Q_W + DA_K_W + DA_V_W + RET_Q_W + RET_K_W + RET_V_W + RET_G_W + GATE_W
D_FF = 2816
N_EXPERTS = 8
TOP_K = 2
N_DENSE = (DEPTH + 1) // 2
N_MOE = DEPTH // 2
EPS = 1e-6

kernel_name = "hybrid_diffattn_retention_gated_moe"


def rms_norm(x, g):
    xf = x.astype(jnp.float32)
    y = xf * lax.rsqrt(jnp.mean(xf * xf, axis=-1, keepdims=True) + EPS)
    return (y * g.astype(jnp.float32)).astype(x.dtype)


def group_norm(x):
    xf = x.astype(jnp.float32)
    mu = jnp.mean(xf, axis=-1, keepdims=True)
    xc = xf - mu
    return xc * lax.rsqrt(jnp.mean(xc * xc, axis=-1, keepdims=True) + EPS)


def rope(x, pos, rot_dim, theta):
    half = rot_dim // 2
    inv = 1.0 / (theta ** (jnp.arange(half, dtype=jnp.float32) / half))
    ang = pos.astype(jnp.float32)[..., None] * inv
    cos = jnp.cos(ang)[:, :, None, :].astype(x.dtype)
    sin = jnp.sin(ang)[:, :, None, :].astype(x.dtype)
    x1 = x[..., :half]
    x2 = x[..., half:rot_dim]
    rot = jnp.concatenate([x1 * cos - x2 * sin, x1 * sin + x2 * cos], axis=-1)
    return jnp.concatenate([rot, x[..., rot_dim:]], axis=-1)


def diff_attention(q, k, v, lam):
    B, S = q.shape[0], q.shape[1]
    nb = S // Q_BLOCK
    scale = DA_HEAD_DIM ** -0.5
    qb = q.reshape(B, nb, Q_BLOCK, DA_HEADS, 2, DA_HEAD_DIM).transpose(1, 0, 2, 3, 4, 5)
    kpos = jnp.arange(S)

    def block(args):
        qi, i = args
        s = jnp.einsum('bqhcd,bkhcd->bhcqk', qi, k,
                       preferred_element_type=jnp.float32) * scale
        qpos = i * Q_BLOCK + jnp.arange(Q_BLOCK)
        mask = kpos[None, :] <= qpos[:, None]
        p = jax.nn.softmax(jnp.where(mask, s, -jnp.inf), axis=-1)
        pd = p[:, :, 0] - lam * p[:, :, 1]
        return jnp.einsum('bhqk,bkhd->bqhd', pd.astype(v.dtype), v)

    out = lax.map(block, (qb, jnp.arange(nb)))
    return out.transpose(1, 0, 2, 3, 4).reshape(B, S, DA_HEADS, DA_V_DIM)


def retention(q, k, v):
    B, S = q.shape[0], q.shape[1]
    C = RET_CHUNK
    n = S // C
    log_gamma = jnp.log1p(-jnp.exp2(-5.0 - jnp.arange(RET_HEADS, dtype=jnp.float32)))
    k = k * (RET_QK_DIM ** -0.5)
    qc = q.reshape(B, n, C, RET_HEADS, RET_QK_DIM)
    kc = k.reshape(B, n, C, RET_HEADS, RET_QK_DIM)
    vc = v.reshape(B, n, C, RET_HEADS, RET_V_DIM)
    idx = jnp.arange(C, dtype=jnp.float32)
    rel = idx[:, None] - idx[None, :]
    decay = jnp.where(rel[None] >= 0,
                      jnp.exp(log_gamma[:, None, None] * jnp.maximum(rel, 0.0)[None]), 0.0)
    inner = jnp.einsum('bnqhd,bnkhd->bnhqk', qc, kc,
                       preferred_element_type=jnp.float32) * decay
    o_inner = jnp.einsum('bnhqk,bnkhe->bnqhe', inner, vc.astype(jnp.float32))
    k_dec = jnp.exp(log_gamma[None, :] * (C - 1.0 - idx)[:, None])
    kv = jnp.einsum('bnkhd,kh,bnkhe->nbhde', kc.astype(jnp.float32), k_dec,
                    vc.astype(jnp.float32))
    chunk_decay = jnp.exp(log_gamma * C)[None, :, None, None]

    def step(R, kv_i):
        return R * chunk_decay + kv_i, R

    R0 = jnp.zeros((B, RET_HEADS, RET_QK_DIM, RET_V_DIM), jnp.float32)
    _, R_prev = lax.scan(step, R0, kv)
    q_dec = jnp.exp(log_gamma[None, :] * (idx + 1.0)[:, None])
    o_cross = jnp.einsum('bnqhd,qh,nbhde->bnqhe', qc.astype(jnp.float32), q_dec, R_prev)
    return (o_inner + o_cross).reshape(B, S, RET_HEADS, RET_V_DIM)


def token_mixers(hn, pos, layer, w_in, da_q_norm, da_k_norm, lam_q1, lam_k1, lam_q2, lam_k2,
                 da_subln, w_proj_da, w_proj_ret, w_out):
    B, S, _ = hn.shape
    z = hn @ w_in
    offs = np.cumsum([DA_Q_W, DA_K_W, DA_V_W, RET_Q_W, RET_K_W, RET_V_W, RET_G_W, D_MODEL])
    qa, ka, va, qr, kr, vr, gr, ga, gb = jnp.split(z, [int(o) for o in offs], axis=-1)

    qa = rope(rms_norm(qa.reshape(B, S, 2 * DA_HEADS, DA_HEAD_DIM), da_q_norm), pos,
              ROPE_ROT_DIM, ROPE_THETA).reshape(B, S, DA_HEADS, 2, DA_HEAD_DIM)
    ka = rope(rms_norm(ka.reshape(B, S, 2 * DA_HEADS, DA_HEAD_DIM), da_k_norm), pos,
              ROPE_ROT_DIM, ROPE_THETA).reshape(B, S, DA_HEADS, 2, DA_HEAD_DIM)
    va = va.reshape(B, S, DA_HEADS, DA_V_DIM)
    lambda_init = 0.8 - 0.6 * math.exp(-0.3 * layer)
    lam = (jnp.exp(jnp.sum(lam_q1.astype(jnp.float32) * lam_k1.astype(jnp.float32)))
           - jnp.exp(jnp.sum(lam_q2.astype(jnp.float32) * lam_k2.astype(jnp.float32)))
           + lambda_init)
    oa = diff_attention(qa, ka, va, lam)
    ya = (rms_norm(oa, da_subln) * (1.0 - lambda_init)).reshape(B, S, DA_V_W)

    qr = rope(qr.reshape(B, S, RET_HEADS, RET_QK_DIM), pos, RET_QK_DIM, RET_ROPE_THETA)
    kr = rope(kr.reshape(B, S, RET_HEADS, RET_QK_DIM), pos, RET_QK_DIM, RET_ROPE_THETA)
    orr = retention(qr, kr, vr.reshape(B, S, RET_HEADS, RET_V_DIM))
    yr = (group_norm(orr).reshape(B, S, RET_V_W)
          * jax.nn.silu(gr.astype(jnp.float32))).astype(hn.dtype)

    merged = jax.nn.sigmoid(ga) * (ya @ w_proj_da) + jax.nn.sigmoid(gb) * (yr @ w_proj_ret)
    return merged @ w_out


def swiglu(x, w1, w3, w2):
    return (jax.nn.silu(x @ w1) * (x @ w3)) @ w2


def moe(hn, w_router, w1, w3, w2):
    B, S, D = hn.shape
    t = hn.reshape(B * S, D)
    logits = (t @ w_router).astype(jnp.float32)
    top_v, top_i = lax.top_k(logits, TOP_K)
    top_w = jax.nn.softmax(top_v, axis=-1)
    combine = jnp.sum(jax.nn.one_hot(top_i, N_EXPERTS, dtype=jnp.float32) * top_w[..., None], axis=1)
    out = jnp.zeros_like(t)
    for e in range(N_EXPERTS):
        out = out + combine[:, e:e + 1].astype(t.dtype) * swiglu(t, w1[e], w3[e], w2[e])
    return out.reshape(B, S, D)


def setup_inputs(seed: int = 0) -> dict:
    key = jax.random.key(seed)
    ks = jax.random.split(key, 24)
    f32 = jnp.float32

    def nrm(k, shape, fan_in):
        return jax.random.normal(k, shape, f32) * (fan_in ** -0.5)

    def gain(k, shape):
        return 1.0 + 0.05 * jax.random.normal(k, shape, f32)

    x = jax.random.normal(ks[0], (BATCH, SEQ, D_MODEL), f32)
    positions = jnp.broadcast_to(jnp.arange(SEQ, dtype=jnp.int32)[None, :], (BATCH, SEQ))
    return {
        "x": x,
        "positions": positions,
        "attn_norm": gain(ks[1], (DEPTH, D_MODEL)),
        "w_in": nrm(ks[2], (DEPTH, D_MODEL, IN_W), D_MODEL),
        "da_q_norm": gain(ks[3], (DEPTH, DA_HEAD_DIM)),
        "da_k_norm": gain(ks[4], (DEPTH, DA_HEAD_DIM)),
        "lam_q1": 0.1 * jax.random.normal(ks[5], (DEPTH, DA_HEAD_DIM), f32),
        "lam_k1": 0.1 * jax.random.normal(ks[6], (DEPTH, DA_HEAD_DIM), f32),
        "lam_q2": 0.1 * jax.random.normal(ks[7], (DEPTH, DA_HEAD_DIM), f32),
        "lam_k2": 0.1 * jax.random.normal(ks[8], (DEPTH, DA_HEAD_DIM), f32),
        "da_subln": gain(ks[9], (DEPTH, DA_V_DIM)),
        "w_proj_da": nrm(ks[10], (DEPTH, DA_V_W, D_MODEL), DA_V_W),
        "w_proj_ret": nrm(ks[11], (DEPTH, RET_V_W, D_MODEL), RET_V_W),
        "w_out": nrm(ks[12], (DEPTH, D_MODEL, D_MODEL), D_MODEL),
        "ffn_norm": gain(ks[13], (DEPTH, D_MODEL)),
        "dense_w1": nrm(ks[14], (N_DENSE, D_MODEL, D_FF), D_MODEL),
        "dense_w3": nrm(ks[15], (N_DENSE, D_MODEL, D_FF), D_MODEL),
        "dense_w2": nrm(ks[16], (N_DENSE, D_FF, D_MODEL), D_FF),
        "w_router": nrm(ks[17], (N_MOE, D_MODEL, N_EXPERTS), D_MODEL),
        "moe_w1": nrm(ks[18], (N_MOE, N_EXPERTS, D_MODEL, D_FF), D_MODEL),
        "moe_w3": nrm(ks[19], (N_MOE, N_EXPERTS, D_MODEL, D_FF), D_MODEL),
        "moe_w2": nrm(ks[20], (N_MOE, N_EXPERTS, D_FF, D_MODEL), D_FF),
    }


def reference(x, positions, attn_norm, w_in, da_q_norm, da_k_norm, lam_q1, lam_k1, lam_q2, lam_k2,
              da_subln, w_proj_da, w_proj_ret, w_out, ffn_norm, dense_w1, dense_w3, dense_w2,
              w_router, moe_w1, moe_w3, moe_w2):
    h = x
    for layer in range(DEPTH):
        hn = rms_norm(h, attn_norm[layer])
        h = h + token_mixers(hn, positions, layer, w_in[layer], da_q_norm[layer], da_k_norm[layer],
                             lam_q1[layer], lam_k1[layer], lam_q2[layer], lam_k2[layer],
                             da_subln[layer], w_proj_da[layer], w_proj_ret[layer], w_out[layer])
        hn = rms_norm(h, ffn_norm[layer])
        j = layer // 2
        if layer % 2 == 0:
            h = h + swiglu(hn, dense_w1[j], dense_w3[j], dense_w2[j])
        else:
            h = h + moe(hn, w_router[j], moe_w1[j], moe_w3[j], moe_w2[j])
    return h
```

```python
import functools
import math

import jax
import jax.numpy as jnp
import numpy as np
from jax import lax
from jax.experimental import pallas as pl
from jax.experimental.pallas import tpu as pltpu

F32 = jnp.float32
BF16 = jnp.bfloat16

D_MODEL = 1024
DA_HEADS = 4
DA_HEAD_DIM = 64
DA_V_DIM = 128
ROPE_THETA = 500000.0
ROPE_ROT_DIM = 16
RET_HEADS = 4
RET_QK_DIM = 64
RET_V_DIM = 128
RET_CHUNK = 128
RET_ROPE_THETA = 10000.0
D_FF = 2816
N_EXPERTS = 8
EPS = 1e-6

LANES = 128
OFF_QA, OFF_KA, OFF_VA = 0, 512, 1024
OFF_QR, OFF_KR, OFF_VR, OFF_GR = 1536, 1792, 2048, 2560
OFF_GA, OFF_GB = 3072, 4096
IN_W = 5120
PROJ_CHUNK = 512
FF_CHUNK = 256
VMEM_LIMIT = 56 * 1024 * 1024


def _cparams(sem):
    return pltpu.CompilerParams(dimension_semantics=sem, vmem_limit_bytes=VMEM_LIMIT)


def _const_spec(shape):
    nd = len(shape)
    return pl.BlockSpec(shape, lambda *_: (0,) * nd)


def _rope_table_kernel(pos_ref, inv_ref, sgn_ref, cos_ref, sin_ref):
    ang = pos_ref[...] * inv_ref[...]
    cos_ref[...] = jnp.cos(ang)
    sin_ref[...] = jnp.sin(ang) * sgn_ref[...]


def _rope_tables(pos_b, inv_row, sgn_row, tm):
    T = pos_b.shape[0]
    row = pl.BlockSpec((tm, LANES), lambda i: (i, 0))
    return pl.pallas_call(
        _rope_table_kernel,
        grid=(T // tm,),
        in_specs=[row, _const_spec((1, LANES)), _const_spec((1, LANES))],
        out_specs=[row, row],
        out_shape=[jax.ShapeDtypeStruct((T, LANES), F32)] * 2,
        compiler_params=_cparams(("parallel",)),
        name="rope_tables",
    )(pos_b, inv_row, sgn_row)


def _rope_rows(rot_dim, theta):
    half = rot_dim // 2
    inv = 1.0 / (theta ** (jnp.arange(half, dtype=F32) / half))
    jj = np.arange(LANES) % 64
    inv_row = jnp.where(jj < rot_dim, inv[jj % half], 0.0).astype(F32)
    sgn_row = np.where(jj < half, -1.0, np.where(jj < rot_dim, 1.0, 0.0)).astype(np.float32)
    return inv_row.reshape(1, LANES), jnp.asarray(sgn_row).reshape(1, LANES)


def _rotate_half(x, half):
    jj = lax.broadcasted_iota(jnp.int32, x.shape, 1) % 64
    return jnp.where(jj < half, pltpu.roll(x, LANES - half, 1), pltpu.roll(x, half, 1))


def _sigmoid(x):
    return 1.0 / (1.0 + jnp.exp(-x))


def _in_proj_kernel(h_ref, g_ref, w_ref, gq_ref, gk_ref, gsum_ref, ca_ref, sa_ref, cr_ref, sr_ref, z_ref):
    x = h_ref[...]
    ms = jnp.mean(x * x, axis=-1, keepdims=True)
    hn = (x * lax.rsqrt(ms + EPS) * g_ref[...]).astype(BF16)

    def head_rms(zc, gain):
        sq = zc * zc
        hi = sq.astype(BF16)
        lo = (sq - hi.astype(F32)).astype(BF16)
        outs = []
        for s in range(PROJ_CHUNK // 256):
            sl = slice(s * 256, (s + 1) * 256)
            tot = (jnp.dot(hi[:, sl], gsum_ref[...], preferred_element_type=F32)
                   + jnp.dot(lo[:, sl], gsum_ref[...], preferred_element_type=F32))
            outs.append(zc[:, sl] * lax.rsqrt(tot * (1.0 / 64.0) + EPS))
        return jnp.concatenate(outs, axis=1) * gain

    def rope(y, cos, sin, half):
        outs = []
        for s in range(y.shape[1] // LANES):
            ys = y[:, s * LANES:(s + 1) * LANES]
            outs.append(ys * cos + _rotate_half(ys, half) * sin)
        return jnp.concatenate(outs, axis=1)

    for c in range(IN_W // PROJ_CHUNK):
        c0 = c * PROJ_CHUNK
        zc = jnp.dot(hn, w_ref[:, c0:c0 + PROJ_CHUNK], preferred_element_type=F32)
        if c0 == OFF_QA:
            zc = rope(head_rms(zc, gq_ref[...]), ca_ref[...], sa_ref[...], ROPE_ROT_DIM // 2)
            zc = zc * (DA_HEAD_DIM ** -0.5)
        elif c0 == OFF_KA:
            zc = rope(head_rms(zc, gk_ref[...]), ca_ref[...], sa_ref[...], ROPE_ROT_DIM // 2)
        elif c0 == OFF_QR:
            zc = rope(zc, cr_ref[...], sr_ref[...], RET_QK_DIM // 2)
            col = lax.broadcasted_iota(jnp.int32, zc.shape, 1)
            zc = jnp.where(col >= OFF_KR - OFF_QR, zc * (RET_QK_DIM ** -0.5), zc)
        elif c0 == OFF_GR:
            zc = zc * _sigmoid(zc)
        elif c0 >= OFF_GA:
            zc = _sigmoid(zc)
        z_ref[:, c0:c0 + PROJ_CHUNK] = zc.astype(BF16)


def _in_proj(h, g, w, gq, gk, gsum, tabs, tm):
    T = h.shape[0]
    row = lambda width: pl.BlockSpec((tm, width), lambda i: (i, 0))
    return pl.pallas_call(
        _in_proj_kernel,
        grid=(T // tm,),
        in_specs=[row(D_MODEL), _const_spec((1, D_MODEL)), _const_spec((D_MODEL, IN_W)),
                  _const_spec((1, PROJ_CHUNK)), _const_spec((1, PROJ_CHUNK)), _const_spec((256, 256)),
                  row(LANES), row(LANES), row(LANES), row(LANES)],
        out_specs=row(IN_W),
        out_shape=jax.ShapeDtypeStruct((T, IN_W), BF16),
        compiler_params=_cparams(("parallel",)),
        name="in_proj",
    )(h, g, w, gq, gk, gsum, *tabs)


def _diff_attn_kernel(q_ref, k_ref, v_ref, lam_ref, subln_ref, o_ref, acc_sc, m_sc, l_sc, *, tq, lambda_init):
    qi = pl.program_id(2)
    q = q_ref[...]
    lane = lax.broadcasted_iota(jnp.int32, q.shape, 1)
    qs = (jnp.where(lane < DA_HEAD_DIM, q, jnp.zeros_like(q)),
          jnp.where(lane >= DA_HEAD_DIM, q, jnp.zeros_like(q)))
    m_sc[...] = jnp.full(m_sc.shape, -jnp.inf, F32)
    l_sc[...] = jnp.zeros(l_sc.shape, F32)
    acc_sc[...] = jnp.zeros(acc_sc.shape, F32)

    def block(ki, masked):
        r0 = pl.multiple_of(ki * tq, tq)
        k = k_ref[pl.ds(r0, tq), :]
        v = v_ref[pl.ds(r0, tq), :]
        for c in range(2):
            s = lax.dot_general(qs[c], k, (((1,), (1,)), ((), ())), preferred_element_type=F32)
            if masked:
                row = lax.broadcasted_iota(jnp.int32, s.shape, 0)
                col = lax.broadcasted_iota(jnp.int32, s.shape, 1)
                s = jnp.where(col <= row, s, -jnp.inf)
            m_old = m_sc[c]
            m_new = jnp.maximum(m_old, jnp.max(s, axis=1, keepdims=True))
            alpha = jnp.exp(m_old - m_new)
            p = jnp.exp(s - pltpu.repeat(m_new, tq // LANES, axis=1))
            l_sc[c] = alpha * l_sc[c] + jnp.sum(p, axis=1, keepdims=True)
            acc_sc[c] = alpha * acc_sc[c] + jnp.dot(p.astype(BF16), v, preferred_element_type=F32)
            m_sc[c] = m_new

    def body(ki, carry):
        block(ki, False)
        return carry

    lax.fori_loop(0, qi, body, 0)
    block(qi, True)

    lv = lam_ref[...]
    lam = (jnp.exp(jnp.sum(lv[0:1] * lv[1:2], axis=1, keepdims=True))
           - jnp.exp(jnp.sum(lv[2:3] * lv[3:4], axis=1, keepdims=True)) + lambda_init)
    o = acc_sc[0] / l_sc[0] - lam * (acc_sc[1] / l_sc[1])
    y = o * lax.rsqrt(jnp.mean(o * o, axis=-1, keepdims=True) + EPS) * subln_ref[...]
    o_ref[...] = (y * (1.0 - lambda_init)).astype(BF16)


def _diff_attn(z, lam_rows, subln, B, S, tq, lambda_init):
    T = B * S
    nq = S // tq
    kern = functools.partial(_diff_attn_kernel, tq=tq, lambda_init=lambda_init)
    return pl.pallas_call(
        kern,
        grid=(B, DA_HEADS, nq),
        in_specs=[pl.BlockSpec((tq, LANES), lambda b, h, i: (b * nq + i, OFF_QA // LANES + h)),
                  pl.BlockSpec((S, LANES), lambda b, h, i: (b, OFF_KA // LANES + h)),
                  pl.BlockSpec((S, LANES), lambda b, h, i: (b, OFF_VA // LANES + h)),
                  _const_spec((4, DA_HEAD_DIM)), _const_spec((1, DA_V_DIM))],
        out_specs=pl.BlockSpec((tq, LANES), lambda b, h, i: (b * nq + i, h)),
        out_shape=jax.ShapeDtypeStruct((T, DA_HEADS * DA_V_DIM), BF16),
        scratch_shapes=[pltpu.VMEM((2, tq, LANES), F32)] * 3,
        compiler_params=_cparams(("parallel", "parallel", "arbitrary")),
        name="diff_attn",
    )(z, z, z, lam_rows, subln)


def _retention_kernel(q_ref, k_ref, v_ref, g_ref, dec_ref, qdec_ref, kdec_ref, cdec_ref, o_ref, r_sc, *, tc):
    @pl.when(pl.program_id(1) == 0)
    def _():
        r_sc[...] = jnp.zeros(r_sc.shape, F32)

    C = RET_CHUNK
    lane = lax.broadcasted_iota(jnp.int32, (C, LANES), 1)
    rowi = lax.broadcasted_iota(jnp.int32, (C, LANES), 0)
    for n in range(tc // C):
        rows = slice(n * C, (n + 1) * C)
        for j in range(RET_HEADS // 2):
            cols = slice(j * LANES, (j + 1) * LANES)
            qp = q_ref[rows, cols]
            kp = k_ref[rows, cols]
            qd = (qp.astype(F32) * qdec_ref[:, cols]).astype(BF16)
            kdt = (kp.astype(F32) * kdec_ref[:, cols]).T.astype(BF16)
            r_b = r_sc[j].astype(BF16)
            kv = []
            for hh in range(2):
                h = 2 * j + hh
                mine = (lane >= hh * RET_QK_DIM) & (lane < (hh + 1) * RET_QK_DIM)
                vh = v_ref[rows, h * RET_V_DIM:(h + 1) * RET_V_DIM]
                s = lax.dot_general(jnp.where(mine, qp, jnp.zeros_like(qp)), kp, (((1,), (1,)), ((), ())),
                                    preferred_element_type=F32)
                inner = (s * dec_ref[h]).astype(BF16)
                o = (jnp.dot(inner, vh, preferred_element_type=F32)
                     + jnp.dot(jnp.where(mine, qd, jnp.zeros_like(qd)), r_b, preferred_element_type=F32))
                oc = o - jnp.mean(o, axis=-1, keepdims=True)
                y = oc * lax.rsqrt(jnp.mean(oc * oc, axis=-1, keepdims=True) + EPS)
                gs = slice(h * RET_V_DIM, (h + 1) * RET_V_DIM)
                o_ref[rows, gs] = (y * g_ref[rows, gs].astype(F32)).astype(BF16)
                kv.append(jnp.dot(kdt, vh, preferred_element_type=F32))
            r_sc[j] = r_sc[j] * cdec_ref[j] + jnp.where(rowi < RET_QK_DIM, kv[0], kv[1])


def _retention(z, consts, B, S, tc):
    T = B * S
    n = S // tc
    dec, qdec, kdec, cdec = consts
    kern = functools.partial(_retention_kernel, tc=tc)
    blk = lambda width, off: pl.BlockSpec((tc, width), lambda b, i: (b * n + i, off // width))
    return pl.pallas_call(
        kern,
        grid=(B, n),
        in_specs=[blk(256, OFF_QR), blk(256, OFF_KR), blk(512, OFF_VR), blk(512, OFF_GR),
                  _const_spec(dec.shape), _const_spec(qdec.shape), _const_spec(kdec.shape),
                  _const_spec(cdec.shape)],
        out_specs=pl.BlockSpec((tc, 512), lambda b, i: (b * n + i, 0)),
        out_shape=jax.ShapeDtypeStruct((T, RET_HEADS * RET_V_DIM), BF16),
        scratch_shapes=[pltpu.VMEM((RET_HEADS // 2, LANES, LANES), F32)],
        compiler_params=_cparams(("parallel", "arbitrary")),
        name="retention",
    )(z, z, z, z, dec, qdec, kdec, cdec)


def _retention_consts():
    C = RET_CHUNK
    log_gamma = jnp.log1p(-jnp.exp2(-5.0 - jnp.arange(RET_HEADS, dtype=F32)))
    idx = jnp.arange(C, dtype=F32)
    rel = idx[:, None] - idx[None, :]
    dec = jnp.where(rel[None] >= 0, jnp.exp(log_gamma[:, None, None] * jnp.maximum(rel, 0.0)[None]), 0.0)
    q_dec = jnp.exp(log_gamma[None, :] * (idx + 1.0)[:, None])
    k_dec = jnp.exp(log_gamma[None, :] * (C - 1.0 - idx)[:, None])
    qdec = jnp.repeat(q_dec, RET_QK_DIM, axis=1)
    kdec = jnp.repeat(k_dec, RET_QK_DIM, axis=1)
    chunk_decay = jnp.repeat(jnp.exp(log_gamma * C), RET_QK_DIM)
    cdec = jnp.broadcast_to(chunk_decay.reshape(RET_HEADS // 2, LANES, 1), (RET_HEADS // 2, LANES, LANES))
    return dec.astype(F32), qdec.astype(F32), kdec.astype(F32), cdec.astype(F32)


def _merge_kernel(ya_ref, yr_ref, ga_ref, gb_ref, h_ref, wa_ref, wr_ref, wo_ref, gf_ref, h_out, hn_out):
    pa = jnp.dot(ya_ref[...], wa_ref[...], preferred_element_type=F32)
    pr = jnp.dot(yr_ref[...], wr_ref[...], preferred_element_type=F32)
    merged = ga_ref[...].astype(F32) * pa + gb_ref[...].astype(F32) * pr
    hnew = h_ref[...] + jnp.dot(merged.astype(BF16), wo_ref[...], preferred_element_type=F32)
    h_out[...] = hnew
    ms = jnp.mean(hnew * hnew, axis=-1, keepdims=True)
    hn_out[...] = (hnew * lax.rsqrt(ms + EPS) * gf_ref[...]).astype(BF16)


def _merge(ya, yr, z, h, wa, wr, wo, gf, tm):
    T = h.shape[0]
    row = lambda width, cb=0: pl.BlockSpec((tm, width), lambda i: (i, cb))
    return pl.pallas_call(
        _merge_kernel,
        grid=(T // tm,),
        in_specs=[row(512), row(512), row(D_MODEL, OFF_GA // D_MODEL), row(D_MODEL, OFF_GB // D_MODEL),
                  row(D_MODEL), _const_spec(wa.shape), _const_spec(wr.shape), _const_spec(wo.shape),
                  _const_spec((1, D_MODEL))],
        out_specs=[row(D_MODEL), row(D_MODEL)],
        out_shape=[jax.ShapeDtypeStruct((T, D_MODEL), F32), jax.ShapeDtypeStruct((T, D_MODEL), BF16)],
        compiler_params=_cparams(("parallel",)),
        name="merge_out",
    )(ya, yr, z, z, h, wa, wr, wo, gf)


def _ffn_kernel(*refs, expert):
    if expert is None:
        hn_ref, prev_ref, w1_ref, w3_ref, w2_ref, o_ref, acc_sc = refs
    else:
        hn_ref, prev_ref, comb_ref, w1_ref, w3_ref, w2_ref, o_ref, acc_sc = refs
    x = hn_ref[...]
    for c in range(D_FF // FF_CHUNK):
        cs = slice(c * FF_CHUNK, (c + 1) * FF_CHUNK)
        u = jnp.dot(x, w1_ref[:, cs], preferred_element_type=F32)
        v = jnp.dot(x, w3_ref[:, cs], preferred_element_type=F32)
        a = (u * _sigmoid(u) * v).astype(BF16)
        d = jnp.dot(a, w2_ref[cs, :], preferred_element_type=F32)
        if c == 0:
            acc_sc[...] = d
        else:
            acc_sc[...] += d
    if expert is None:
        o_ref[...] = prev_ref[...] + acc_sc[...]
    else:
        o_ref[...] = prev_ref[...] + comb_ref[:, expert:expert + 1] * acc_sc[...]


def _ffn(hn, prev, w1, w3, w2, tm, comb=None, expert=None):
    T = hn.shape[0]
    row = lambda width: pl.BlockSpec((tm, width), lambda i: (i, 0))
    wspec = lambda w: pl.BlockSpec(w.shape, lambda i: (0, 0), pipeline_mode=pl.Buffered(1))
    args = [hn, prev] + ([] if comb is None else [comb]) + [w1, w3, w2]
    specs = [row(D_MODEL), row(D_MODEL)] + ([] if comb is None else [row(LANES)]) + [wspec(w1), wspec(w3), wspec(w2)]
    return pl.pallas_call(
        functools.partial(_ffn_kernel, expert=expert),
        grid=(T // tm,),
        in_specs=specs,
        out_specs=row(D_MODEL),
        out_shape=jax.ShapeDtypeStruct((T, D_MODEL), F32),
        scratch_shapes=[pltpu.VMEM((tm, D_MODEL), F32)],
        input_output_aliases={1: 0},
        compiler_params=_cparams(("parallel",)),
        name="ffn" if expert is None else "moe_ffn",
    )(*args)


def _router_kernel(h_ref, g_ref, wr_ref, comb_ref):
    x = h_ref[...]
    ms = jnp.mean(x * x, axis=-1, keepdims=True)
    hn = x * lax.rsqrt(ms + EPS) * g_ref[...]
    logits = jnp.dot(hn, wr_ref[...], preferred_element_type=F32, precision=lax.Precision.HIGHEST)
    lane = lax.broadcasted_iota(jnp.int32, logits.shape, 1)
    logits = jnp.where(lane < N_EXPERTS, logits, -jnp.inf)
    m1 = jnp.max(logits, axis=1, keepdims=True)
    i1 = jnp.min(jnp.where(logits == m1, lane, LANES), axis=1, keepdims=True)
    rest = jnp.where(lane == i1, -jnp.inf, logits)
    m2 = jnp.max(rest, axis=1, keepdims=True)
    i2 = jnp.min(jnp.where(rest == m2, lane, LANES), axis=1, keepdims=True)
    e = jnp.exp(m2 - m1)
    w1 = 1.0 / (1.0 + e)
    w2 = e / (1.0 + e)
    comb_ref[...] = jnp.where(lane == i1, w1, 0.0) + jnp.where(lane == i2, w2, 0.0)


def _router(h, g, wr_pad, tm):
    T = h.shape[0]
    return pl.pallas_call(
        _router_kernel,
        grid=(T // tm,),
        in_specs=[pl.BlockSpec((tm, D_MODEL), lambda i: (i, 0)), _const_spec((1, D_MODEL)),
                  _const_spec((D_MODEL, LANES))],
        out_specs=pl.BlockSpec((tm, LANES), lambda i: (i, 0)),
        out_shape=jax.ShapeDtypeStruct((T, LANES), F32),
        compiler_params=_cparams(("parallel",)),
        name="router",
    )(h, g, wr_pad)


def kernel(x, positions, attn_norm, w_in, da_q_norm, da_k_norm, lam_q1, lam_k1, lam_q2, lam_k2, da_subln,
           w_proj_da, w_proj_ret, w_out, ffn_norm, dense_w1, dense_w3, dense_w2, w_router, moe_w1, moe_w3,
           moe_w2):
    B, S, D = x.shape
    assert D == D_MODEL and S % RET_CHUNK == 0
    T = B * S
    depth = attn_norm.shape[0]
    tm = min(512, S)
    tq = min(512, S)
    tc = min(512, S)

    pos_b = jnp.broadcast_to(positions.reshape(T, 1).astype(F32), (T, LANES))
    tabs = (_rope_tables(pos_b, *_rope_rows(ROPE_ROT_DIM, ROPE_THETA), tm)
            + _rope_tables(pos_b, *_rope_rows(RET_QK_DIM, RET_ROPE_THETA), tm))
    ret_consts = _retention_consts()
    gsum = jnp.asarray(np.kron(np.eye(4), np.ones((64, 64))), BF16)

    h = x.reshape(T, D)
    for layer in range(depth):
        lambda_init = 0.8 - 0.6 * math.exp(-0.3 * layer)
        gq = jnp.tile(da_q_norm[layer], PROJ_CHUNK // DA_HEAD_DIM).reshape(1, PROJ_CHUNK)
        gk = jnp.tile(da_k_norm[layer], PROJ_CHUNK // DA_HEAD_DIM).reshape(1, PROJ_CHUNK)
        z = _in_proj(h, attn_norm[layer].reshape(1, D), w_in[layer].astype(BF16), gq, gk, gsum, tabs, tm)
        lam_rows = jnp.stack([lam_q1[layer], lam_k1[layer], lam_q2[layer], lam_k2[layer]])
        ya = _diff_attn(z, lam_rows, da_subln[layer].reshape(1, DA_V_DIM), B, S, tq, lambda_init)
        yr = _retention(z, ret_consts, B, S, tc)
        h, hn = _merge(ya, yr, z, h, w_proj_da[layer].astype(BF16), w_proj_ret[layer].astype(BF16),
                       w_out[layer].astype(BF16), ffn_norm[layer].reshape(1, D), tm)
        j = layer // 2
        if layer % 2 == 0:
            h = _ffn(hn, h, dense_w1[j].astype(BF16), dense_w3[j].astype(BF16), dense_w2[j].astype(BF16), tm)
        else:
            wr_pad = jnp.pad(w_router[j], ((0, 0), (0, LANES - N_EXPERTS)))
            comb = _router(h, ffn_norm[layer].reshape(1, D), wr_pad, tm)
            for e in range(N_EXPERTS):
                h = _ffn(hn, h, moe_w1[j, e].astype(BF16), moe_w3[j, e].astype(BF16),
                         moe_w2[j, e].astype(BF16), tm, comb=comb, expert=e)
    return h.reshape(B, S, D)
```

```python
import functools
import math

import jax
import jax.numpy as jnp
import numpy as np
from jax import lax
from jax.experimental import pallas as pl
from jax.experimental.pallas import tpu as pltpu

F32 = jnp.float32
BF16 = jnp.bfloat16

D_MODEL = 1024
DA_HEADS = 4
DA_HEAD_DIM = 64
DA_V_DIM = 128
ROPE_THETA = 500000.0
ROPE_ROT_DIM = 16
RET_HEADS = 4
RET_QK_DIM = 64
RET_V_DIM = 128
RET_CHUNK = 128
RET_ROPE_THETA = 10000.0
D_FF = 2816
N_EXPERTS = 8
EPS = 1e-6

LANES = 128
OFF_QA, OFF_KA, OFF_VA = 0, 512, 1024
OFF_QR, OFF_KR, OFF_VR, OFF_GR = 1536, 1792, 2048, 2560
OFF_GA, OFF_GB = 3072, 4096
IN_W = 5120
PROJ_CHUNK = 512
FF_CHUNK = 256
VMEM_LIMIT = 56 * 1024 * 1024


def _cparams(sem):
    return pltpu.CompilerParams(dimension_semantics=sem, vmem_limit_bytes=VMEM_LIMIT)


def _const_spec(shape):
    nd = len(shape)
    return pl.BlockSpec(shape, lambda *_: (0,) * nd)


def _rope_table_kernel(pos_ref, inv_ref, sgn_ref, cos_ref, sin_ref):
    ang = pos_ref[...] * inv_ref[...]
    cos_ref[...] = jnp.cos(ang)
    sin_ref[...] = jnp.sin(ang) * sgn_ref[...]


def _rope_tables(pos_b, inv_row, sgn_row, tm):
    T = pos_b.shape[0]
    row = pl.BlockSpec((tm, LANES), lambda i: (i, 0))
    return pl.pallas_call(
        _rope_table_kernel,
        grid=(T // tm,),
        in_specs=[row, _const_spec((1, LANES)), _const_spec((1, LANES))],
        out_specs=[row, row],
        out_shape=[jax.ShapeDtypeStruct((T, LANES), F32)] * 2,
        compiler_params=_cparams(("parallel",)),
        name="rope_tables",
    )(pos_b, inv_row, sgn_row)


def _rope_rows(rot_dim, theta):
    half = rot_dim // 2
    inv = 1.0 / (theta ** (jnp.arange(half, dtype=F32) / half))
    jj = np.arange(LANES) % 64
    inv_row = jnp.where(jj < rot_dim, inv[jj % half], 0.0).astype(F32)
    sgn_row = np.where(jj < half, -1.0, np.where(jj < rot_dim, 1.0, 0.0)).astype(np.float32)
    return inv_row.reshape(1, LANES), jnp.asarray(sgn_row).reshape(1, LANES)


def _rotate_half(x, half):
    jj = lax.broadcasted_iota(jnp.int32, x.shape, 1) % 64
    return jnp.where(jj < half, pltpu.roll(x, LANES - half, 1), pltpu.roll(x, half, 1))


def _sigmoid(x):
    return 1.0 / (1.0 + jnp.exp(-x))


def _in_proj_kernel(h_ref, g_ref, w_ref, gq_ref, gk_ref, gsum_ref, ca_ref, sa_ref, cr_ref, sr_ref, z_ref):
    x = h_ref[...]
    ms = jnp.mean(x * x, axis=-1, keepdims=True)
    hn = (x * lax.rsqrt(ms + EPS) * g_ref[...]).astype(BF16)

    def head_rms(zc, gain):
        sq = zc * zc
        hi = sq.astype(BF16)
        lo = (sq - hi.astype(F32)).astype(BF16)
        outs = []
        for s in range(PROJ_CHUNK // 256):
            sl = slice(s * 256, (s + 1) * 256)
            tot = (jnp.dot(hi[:, sl], gsum_ref[...], preferred_element_type=F32)
                   + jnp.dot(lo[:, sl], gsum_ref[...], preferred_element_type=F32))
            outs.append(zc[:, sl] * lax.rsqrt(tot * (1.0 / 64.0) + EPS))
        return jnp.concatenate(outs, axis=1) * gain

    def rope(y, cos, sin, half):
        outs = []
        for s in range(y.shape[1] // LANES):
            ys = y[:, s * LANES:(s + 1) * LANES]
            outs.append(ys * cos + _rotate_half(ys, half) * sin)
        return jnp.concatenate(outs, axis=1)

    for c in range(IN_W // PROJ_CHUNK):
        c0 = c * PROJ_CHUNK
        zc = jnp.dot(hn, w_ref[:, c0:c0 + PROJ_CHUNK], preferred_element_type=F32)
        if c0 == OFF_QA:
            zc = rope(head_rms(zc, gq_ref[...]), ca_ref[...], sa_ref[...], ROPE_ROT_DIM // 2)
            zc = zc * (DA_HEAD_DIM ** -0.5)
        elif c0 == OFF_KA:
            zc = rope(head_rms(zc, gk_ref[...]), ca_ref[...], sa_ref[...], ROPE_ROT_DIM // 2)
        elif c0 == OFF_QR:
            zc = rope(zc, cr_ref[...], sr_ref[...], RET_QK_DIM // 2)
            col = lax.broadcasted_iota(jnp.int32, zc.shape, 1)
            zc = jnp.where(col >= OFF_KR - OFF_QR, zc * (RET_QK_DIM ** -0.5), zc)
        elif c0 == OFF_GR:
            zc = zc * _sigmoid(zc)
        elif c0 >= OFF_GA:
            zc = _sigmoid(zc)
        z_ref[:, c0:c0 + PROJ_CHUNK] = zc.astype(BF16)


def _in_proj(h, g, w, gq, gk, gsum, tabs, tm):
    T = h.shape[0]
    row = lambda width: pl.BlockSpec((tm, width), lambda i: (i, 0))
    return pl.pallas_call(
        _in_proj_kernel,
        grid=(T // tm,),
        in_specs=[row(D_MODEL), _const_spec((1, D_MODEL)), _const_spec((D_MODEL, IN_W)),
                  _const_spec((1, PROJ_CHUNK)), _const_spec((1, PROJ_CHUNK)), _const_spec((256, 256)),
                  row(LANES), row(LANES), row(LANES), row(LANES)],
        out_specs=row(IN_W),
        out_shape=jax.ShapeDtypeStruct((T, IN_W), BF16),
        compiler_params=_cparams(("parallel",)),
        name="in_proj",
    )(h, g, w, gq, gk, gsum, *tabs)


def _diff_attn_kernel(q_ref, k_ref, v_ref, lam_ref, subln_ref, o_ref, acc_sc, m_sc, l_sc, *, tq, lambda_init):
    qi = pl.program_id(2)
    q = q_ref[...]
    lane = lax.broadcasted_iota(jnp.int32, q.shape, 1)
    qs = (jnp.where(lane < DA_HEAD_DIM, q, jnp.zeros_like(q)),
          jnp.where(lane >= DA_HEAD_DIM, q, jnp.zeros_like(q)))
    m_sc[...] = jnp.full(m_sc.shape, -jnp.inf, F32)
    l_sc[...] = jnp.zeros(l_sc.shape, F32)
    acc_sc[...] = jnp.zeros(acc_sc.shape, F32)

    def block(ki, masked):
        r0 = pl.multiple_of(ki * tq, tq)
        k = k_ref[pl.ds(r0, tq), :]
        v = v_ref[pl.ds(r0, tq), :]
        for c in range(2):
            s = lax.dot_general(qs[c], k, (((1,), (1,)), ((), ())), preferred_element_type=F32)
            if masked:
                row = lax.broadcasted_iota(jnp.int32, s.shape, 0)
                col = lax.broadcasted_iota(jnp.int32, s.shape, 1)
                s = jnp.where(col <= row, s, -jnp.inf)
            m_old = m_sc[c]
            m_new = jnp.maximum(m_old, jnp.max(s, axis=1, keepdims=True))
            alpha = jnp.exp(m_old - m_new)
            p = jnp.exp(s - jnp.concatenate([m_new] * (tq // LANES), axis=1))
            l_sc[c] = alpha * l_sc[c] + jnp.sum(p, axis=1, keepdims=True)
            acc_sc[c] = alpha * acc_sc[c] + jnp.dot(p.astype(BF16), v, preferred_element_type=F32)
            m_sc[c] = m_new

    def body(ki, carry):
        block(ki, False)
        return carry

    lax.fori_loop(0, qi, body, 0)
    block(qi, True)

    lv = lam_ref[...]
    lam = (jnp.exp(jnp.sum(lv[0:1] * lv[1:2], axis=1, keepdims=True))
           - jnp.exp(jnp.sum(lv[2:3] * lv[3:4], axis=1, keepdims=True)) + lambda_init)
    o = acc_sc[0] / l_sc[0] - lam * (acc_sc[1] / l_sc[1])
    y = o * lax.rsqrt(jnp.mean(o * o, axis=-1, keepdims=True) + EPS) * subln_ref[...]
    o_ref[...] = (y * (1.0 - lambda_init)).astype(BF16)


def _diff_attn(z, lam_rows, subln, B, S, tq, lambda_init):
    T = B * S
    nq = S // tq
    kern = functools.partial(_diff_attn_kernel, tq=tq, lambda_init=lambda_init)
    return pl.pallas_call(
        kern,
        grid=(B, DA_HEADS, nq),
        in_specs=[pl.BlockSpec((tq, LANES), lambda b, h, i: (b * nq + i, OFF_QA // LANES + h)),
                  pl.BlockSpec((S, LANES), lambda b, h, i: (b, OFF_KA // LANES + h)),
                  pl.BlockSpec((S, LANES), lambda b, h, i: (b, OFF_VA // LANES + h)),
                  _const_spec((4, DA_HEAD_DIM)), _const_spec((1, DA_V_DIM))],
        out_specs=pl.BlockSpec((tq, LANES), lambda b, h, i: (b * nq + i, h)),
        out_shape=jax.ShapeDtypeStruct((T, DA_HEADS * DA_V_DIM), BF16),
        scratch_shapes=[pltpu.VMEM((2, tq, LANES), F32)] * 3,
        compiler_params=_cparams(("parallel", "parallel", "arbitrary")),
        name="diff_attn",
    )(z, z, z, lam_rows, subln)


def _retention_kernel(q_ref, k_ref, v_ref, g_ref, dec_ref, qdec_ref, kdec_ref, cdec_ref, o_ref, r_sc, *, tc):
    @pl.when(pl.program_id(1) == 0)
    def _():
        r_sc[...] = jnp.zeros(r_sc.shape, F32)

    C = RET_CHUNK
    lane = lax.broadcasted_iota(jnp.int32, (C, LANES), 1)
    rowi = lax.broadcasted_iota(jnp.int32, (C, LANES), 0)
    for n in range(tc // C):
        rows = slice(n * C, (n + 1) * C)
        for j in range(RET_HEADS // 2):
            cols = slice(j * LANES, (j + 1) * LANES)
            qp = q_ref[rows, cols]
            kp = k_ref[rows, cols]
            qd = (qp.astype(F32) * qdec_ref[:, cols]).astype(BF16)
            kdt = (kp.astype(F32) * kdec_ref[:, cols]).T.astype(BF16)
            r_b = r_sc[j].astype(BF16)
            kv = []
            for hh in range(2):
                h = 2 * j + hh
                mine = (lane >= hh * RET_QK_DIM) & (lane < (hh + 1) * RET_QK_DIM)
                vh = v_ref[rows, h * RET_V_DIM:(h + 1) * RET_V_DIM]
                s = lax.dot_general(jnp.where(mine, qp, jnp.zeros_like(qp)), kp, (((1,), (1,)), ((), ())),
                                    preferred_element_type=F32)
                inner = (s * dec_ref[h]).astype(BF16)
                o = (jnp.dot(inner, vh, preferred_element_type=F32)
                     + jnp.dot(jnp.where(mine, qd, jnp.zeros_like(qd)), r_b, preferred_element_type=F32))
                oc = o - jnp.mean(o, axis=-1, keepdims=True)
                y = oc * lax.rsqrt(jnp.mean(oc * oc, axis=-1, keepdims=True) + EPS)
                gs = slice(h * RET_V_DIM, (h + 1) * RET_V_DIM)
                o_ref[rows, gs] = (y * g_ref[rows, gs].astype(F32)).astype(BF16)
                kv.append(jnp.dot(kdt, vh, preferred_element_type=F32))
            r_sc[j] = r_sc[j] * cdec_ref[j] + jnp.where(rowi < RET_QK_DIM, kv[0], kv[1])


def _retention(z, consts, B, S, tc):
    T = B * S
    n = S // tc
    dec, qdec, kdec, cdec = consts
    kern = functools.partial(_retention_kernel, tc=tc)
    blk = lambda width, off: pl.BlockSpec((tc, width), lambda b, i: (b * n + i, off // width))
    return pl.pallas_call(
        kern,
        grid=(B, n),
        in_specs=[blk(256, OFF_QR), blk(256, OFF_KR), blk(512, OFF_VR), blk(512, OFF_GR),
                  _const_spec(dec.shape), _const_spec(qdec.shape), _const_spec(kdec.shape),
                  _const_spec(cdec.shape)],
        out_specs=pl.BlockSpec((tc, 512), lambda b, i: (b * n + i, 0)),
        out_shape=jax.ShapeDtypeStruct((T, RET_HEADS * RET_V_DIM), BF16),
        scratch_shapes=[pltpu.VMEM((RET_HEADS // 2, LANES, LANES), F32)],
        compiler_params=_cparams(("parallel", "arbitrary")),
        name="retention",
    )(z, z, z, z, dec, qdec, kdec, cdec)


def _retention_consts():
    C = RET_CHUNK
    log_gamma = jnp.log1p(-jnp.exp2(-5.0 - jnp.arange(RET_HEADS, dtype=F32)))
    idx = jnp.arange(C, dtype=F32)
    rel = idx[:, None] - idx[None, :]
    dec = jnp.where(rel[None] >= 0, jnp.exp(log_gamma[:, None, None] * jnp.maximum(rel, 0.0)[None]), 0.0)
    q_dec = jnp.exp(log_gamma[None, :] * (idx + 1.0)[:, None])
    k_dec = jnp.exp(log_gamma[None, :] * (C - 1.0 - idx)[:, None])
    qdec = jnp.repeat(q_dec, RET_QK_DIM, axis=1)
    kdec = jnp.repeat(k_dec, RET_QK_DIM, axis=1)
    chunk_decay = jnp.repeat(jnp.exp(log_gamma * C), RET_QK_DIM)
    cdec = jnp.broadcast_to(chunk_decay.reshape(RET_HEADS // 2, LANES, 1), (RET_HEADS // 2, LANES, LANES))
    return dec.astype(F32), qdec.astype(F32), kdec.astype(F32), cdec.astype(F32)


def _merge_kernel(ya_ref, yr_ref, ga_ref, gb_ref, h_ref, wa_ref, wr_ref, wo_ref, gf_ref, h_out, hn_out):
    pa = jnp.dot(ya_ref[...], wa_ref[...], preferred_element_type=F32)
    pr = jnp.dot(yr_ref[...], wr_ref[...], preferred_element_type=F32)
    merged = ga_ref[...].astype(F32) * pa + gb_ref[...].astype(F32) * pr
    hnew = h_ref[...] + jnp.dot(merged.astype(BF16), wo_ref[...], preferred_element_type=F32)
    h_out[...] = hnew
    ms = jnp.mean(hnew * hnew, axis=-1, keepdims=True)
    hn_out[...] = (hnew * lax.rsqrt(ms + EPS) * gf_ref[...]).astype(BF16)


def _merge(ya, yr, z, h, wa, wr, wo, gf, tm):
    T = h.shape[0]
    row = lambda width, cb=0: pl.BlockSpec((tm, width), lambda i: (i, cb))
    return pl.pallas_call(
        _merge_kernel,
        grid=(T // tm,),
        in_specs=[row(512), row(512), row(D_MODEL, OFF_GA // D_MODEL), row(D_MODEL, OFF_GB // D_MODEL),
                  row(D_MODEL), _const_spec(wa.shape), _const_spec(wr.shape), _const_spec(wo.shape),
                  _const_spec((1, D_MODEL))],
        out_specs=[row(D_MODEL), row(D_MODEL)],
        out_shape=[jax.ShapeDtypeStruct((T, D_MODEL), F32), jax.ShapeDtypeStruct((T, D_MODEL), BF16)],
        compiler_params=_cparams(("parallel",)),
        name="merge_out",
    )(ya, yr, z, z, h, wa, wr, wo, gf)


def _ffn_kernel(hn_ref, prev_ref, w1_ref, w3_ref, w2_ref, o_ref, acc_sc):
    x = hn_ref[...]
    for c in range(D_FF // FF_CHUNK):
        cs = slice(c * FF_CHUNK, (c + 1) * FF_CHUNK)
        u = jnp.dot(x, w1_ref[:, cs], preferred_element_type=F32)
        v = jnp.dot(x, w3_ref[:, cs], preferred_element_type=F32)
        a = (u * _sigmoid(u) * v).astype(BF16)
        d = jnp.dot(a, w2_ref[cs, :], preferred_element_type=F32)
        if c == 0:
            acc_sc[...] = d
        else:
            acc_sc[...] += d
    o_ref[...] = prev_ref[...] + acc_sc[...]


def _ffn(hn, prev, w1, w3, w2, tm):
    T = hn.shape[0]
    row = lambda width: pl.BlockSpec((tm, width), lambda i: (i, 0))
    wspec = lambda w: pl.BlockSpec(w.shape, lambda i: (0, 0), pipeline_mode=pl.Buffered(1))
    return pl.pallas_call(
        _ffn_kernel,
        grid=(T // tm,),
        in_specs=[row(D_MODEL), row(D_MODEL), wspec(w1), wspec(w3), wspec(w2)],
        out_specs=row(D_MODEL),
        out_shape=jax.ShapeDtypeStruct((T, D_MODEL), F32),
        scratch_shapes=[pltpu.VMEM((tm, D_MODEL), F32)],
        input_output_aliases={1: 0},
        compiler_params=_cparams(("parallel",)),
        name="ffn",
    )(hn, prev, w1, w3, w2)


R_I1, R_I2, R_W1, R_W2, R_RANK1, R_RANK2 = range(6)


def _router_kernel(h_ref, g_ref, wr_ref, info_ref, cnt_ref, carry_sc):
    @pl.when(pl.program_id(0) == 0)
    def _():
        carry_sc[...] = jnp.zeros(carry_sc.shape, F32)

    x = h_ref[...]
    ms = jnp.mean(x * x, axis=-1, keepdims=True)
    hn = x * lax.rsqrt(ms + EPS) * g_ref[...]
    logits = jnp.dot(hn, wr_ref[...], preferred_element_type=F32, precision=lax.Precision.HIGHEST)
    lane = lax.broadcasted_iota(jnp.int32, logits.shape, 1)
    logits = jnp.where(lane < N_EXPERTS, logits, -jnp.inf)
    m1 = jnp.max(logits, axis=1, keepdims=True)
    i1 = jnp.min(jnp.where(logits == m1, lane, LANES), axis=1, keepdims=True)
    rest = jnp.where(lane == i1, -jnp.inf, logits)
    m2 = jnp.max(rest, axis=1, keepdims=True)
    i2 = jnp.min(jnp.where(rest == m2, lane, LANES), axis=1, keepdims=True)
    e = jnp.exp(m2 - m1)
    w1 = 1.0 / (1.0 + e)
    w2 = e / (1.0 + e)
    sel = jnp.where(lane == i1, 1.0, jnp.where(lane == i2, 1.0, 0.0))
    tm = sel.shape[0]
    rr = lax.broadcasted_iota(jnp.int32, (tm, tm), 0)
    cc = lax.broadcasted_iota(jnp.int32, (tm, tm), 1)
    ltri = jnp.where(cc < rr, 1.0, 0.0).astype(BF16)
    before = jnp.dot(ltri, sel.astype(BF16), preferred_element_type=F32) + carry_sc[0:1, :]
    r1 = jnp.sum(jnp.where(lane == i1, before, 0.0), axis=1, keepdims=True)
    r2 = jnp.sum(jnp.where(lane == i2, before, 0.0), axis=1, keepdims=True)
    info = jnp.where(lane == R_I1, i1.astype(F32), 0.0)
    for ln, val in ((R_I2, i2.astype(F32)), (R_W1, w1), (R_W2, w2), (R_RANK1, r1), (R_RANK2, r2)):
        info = jnp.where(lane == ln, val, info)
    info_ref[...] = info
    carry_sc[...] = carry_sc[...] + jnp.sum(sel, axis=0, keepdims=True)
    cnt_ref[...] = carry_sc[...]


def _router(h, g, wr_pad, tm):
    T = h.shape[0]
    return pl.pallas_call(
        _router_kernel,
        grid=(T // tm,),
        in_specs=[pl.BlockSpec((tm, D_MODEL), lambda i: (i, 0)), _const_spec((1, D_MODEL)),
                  _const_spec((D_MODEL, LANES))],
        out_specs=[pl.BlockSpec((tm, LANES), lambda i: (i, 0)), _const_spec((8, LANES))],
        out_shape=[jax.ShapeDtypeStruct((T, LANES), F32), jax.ShapeDtypeStruct((8, LANES), F32)],
        scratch_shapes=[pltpu.VMEM((8, LANES), F32)],
        compiler_params=_cparams(("arbitrary",)),
        name="router",
    )(h, g, wr_pad)


def _dispatch_kernel(pos_ref, h_hbm, xs_init_hbm, xs_hbm, sem, *, tb):
    del xs_init_hbm
    base = pl.program_id(0) * tb

    def issue(t, carry):
        for k in range(2):
            p = pos_ref[0, 0, k * tb + t]
            pltpu.make_async_copy(h_hbm.at[pl.ds(base + t, 1)], xs_hbm.at[pl.ds(p, 1)], sem).start()
        return carry

    lax.fori_loop(0, tb, issue, 0)

    def drain(t, carry):
        for k in range(2):
            pltpu.make_async_copy(h_hbm.at[pl.ds(0, 1)], xs_hbm.at[pl.ds(0, 1)], sem).wait()
        return carry

    lax.fori_loop(0, tb, drain, 0)


def _dispatch(pos_blocks, h, xs_init, tb):
    T = h.shape[0]
    return pl.pallas_call(
        functools.partial(_dispatch_kernel, tb=tb),
        grid=(T // tb,),
        in_specs=[pl.BlockSpec((1, 1, 2 * tb), lambda i: (i, 0, 0), memory_space=pltpu.SMEM),
                  pl.BlockSpec(memory_space=pl.ANY), pl.BlockSpec(memory_space=pl.ANY)],
        out_specs=pl.BlockSpec(memory_space=pl.ANY),
        out_shape=jax.ShapeDtypeStruct(xs_init.shape, F32),
        scratch_shapes=[pltpu.SemaphoreType.DMA(())],
        input_output_aliases={2: 0},
        compiler_params=pltpu.CompilerParams(dimension_semantics=("arbitrary",), has_side_effects=True),
        name="moe_dispatch",
    )(pos_blocks, h, xs_init)


def _expert_kernel(te_ref, nu_ref, xs_ref, g_ref, w1_ref, w3_ref, w2_ref, y_ref, acc_sc):
    del te_ref
    i = pl.program_id(0)

    @pl.when(i < nu_ref[0])
    def _():
        x = xs_ref[...]
        ms = jnp.mean(x * x, axis=-1, keepdims=True)
        hn = (x * lax.rsqrt(ms + EPS) * g_ref[...]).astype(BF16)
        for c in range(D_FF // FF_CHUNK):
            cs = slice(c * FF_CHUNK, (c + 1) * FF_CHUNK)
            u = jnp.dot(hn, w1_ref[0, :, cs], preferred_element_type=F32)
            v = jnp.dot(hn, w3_ref[0, :, cs], preferred_element_type=F32)
            a = (u * _sigmoid(u) * v).astype(BF16)
            d = jnp.dot(a, w2_ref[0, cs, :], preferred_element_type=F32)
            if c == 0:
                acc_sc[...] = d
            else:
                acc_sc[...] += d
        y_ref[...] = acc_sc[...]

    @pl.when(i >= nu_ref[0])
    def _():
        y_ref[...] = jnp.zeros(y_ref.shape, F32)


def _experts(tile_expert, n_used, xs, g, w1, w3, w2, te_rows):
    P = xs.shape[0]
    row = pl.BlockSpec((te_rows, D_MODEL), lambda i, te, nu: (jnp.minimum(i, nu[0] - 1), 0))
    wspec = lambda w: pl.BlockSpec((1,) + w.shape[1:], lambda i, te, nu: (te[i], 0, 0))
    return pl.pallas_call(
        _expert_kernel,
        grid_spec=pltpu.PrefetchScalarGridSpec(
            num_scalar_prefetch=2,
            grid=(P // te_rows,),
            in_specs=[row, pl.BlockSpec((1, D_MODEL), lambda i, te, nu: (0, 0)), wspec(w1), wspec(w3), wspec(w2)],
            out_specs=pl.BlockSpec((te_rows, D_MODEL), lambda i, te, nu: (i, 0)),
            scratch_shapes=[pltpu.VMEM((te_rows, D_MODEL), F32)]),
        out_shape=jax.ShapeDtypeStruct((P, D_MODEL), F32),
        compiler_params=_cparams(("arbitrary",)),
        name="moe_experts",
    )(tile_expert, n_used, xs, g, w1, w3, w2)


def _combine_kernel(pos_ref, info_ref, h_ref, y_hbm, o_ref, g_sc, sem, *, tb):
    def issue(t, carry):
        for k in range(2):
            p = pos_ref[0, 0, k * tb + t]
            pltpu.make_async_copy(y_hbm.at[pl.ds(p, 1)], g_sc.at[k, pl.ds(t, 1)], sem).start()
        return carry

    lax.fori_loop(0, tb, issue, 0)

    def drain(t, carry):
        for k in range(2):
            pltpu.make_async_copy(y_hbm.at[pl.ds(0, 1)], g_sc.at[k, pl.ds(t, 1)], sem).wait()
        return carry

    lax.fori_loop(0, tb, drain, 0)
    info = info_ref[...]
    o_ref[...] = h_ref[...] + info[:, R_W1:R_W1 + 1] * g_sc[0] + info[:, R_W2:R_W2 + 1] * g_sc[1]


def _combine(pos_blocks, info, h, y, tb):
    T = h.shape[0]
    row = lambda width: pl.BlockSpec((tb, width), lambda i: (i, 0))
    return pl.pallas_call(
        functools.partial(_combine_kernel, tb=tb),
        grid=(T // tb,),
        in_specs=[pl.BlockSpec((1, 1, 2 * tb), lambda i: (i, 0, 0), memory_space=pltpu.SMEM),
                  row(LANES), row(D_MODEL), pl.BlockSpec(memory_space=pl.ANY)],
        out_specs=row(D_MODEL),
        out_shape=jax.ShapeDtypeStruct((T, D_MODEL), F32),
        scratch_shapes=[pltpu.VMEM((2, tb, D_MODEL), F32), pltpu.SemaphoreType.DMA(())],
        compiler_params=_cparams(("arbitrary",)),
        name="moe_combine",
    )(pos_blocks, info, h, y)


def _pos_blocks(pos1, pos2, tb):
    nb = pos1.shape[0] // tb
    return jnp.concatenate([pos1.reshape(nb, 1, tb), pos2.reshape(nb, 1, tb)], axis=2)


def _moe(h, g, w_router, w1, w3, w2, tm):
    T = h.shape[0]
    te_rows = min(512, T)
    wr_pad = jnp.pad(w_router, ((0, 0), (0, LANES - N_EXPERTS)))
    info, cnt = _router(h, g, wr_pad, tm)
    counts = cnt[0, :N_EXPERTS].astype(jnp.int32)
    padded = ((counts + te_rows - 1) // te_rows) * te_rows
    ends = jnp.cumsum(padded)
    base = ends - padded
    i1 = info[:, R_I1].astype(jnp.int32)
    i2 = info[:, R_I2].astype(jnp.int32)
    pos1 = jnp.take(base, i1) + info[:, R_RANK1].astype(jnp.int32)
    pos2 = jnp.take(base, i2) + info[:, R_RANK2].astype(jnp.int32)
    n_tiles = (2 * T) // te_rows + N_EXPERTS
    tile_expert = jnp.minimum(jnp.searchsorted(ends // te_rows, jnp.arange(n_tiles), side="right"),
                              N_EXPERTS - 1).astype(jnp.int32)
    n_used = (ends[-1:] // te_rows).astype(jnp.int32)

    tb_d = min(4096, T)
    xs = _dispatch(_pos_blocks(pos1, pos2, tb_d), h, jnp.zeros((n_tiles * te_rows, D_MODEL), F32), tb_d)
    y = _experts(tile_expert, n_used, xs, g, w1, w3, w2, te_rows)
    tb_c = min(256, T)
    return _combine(_pos_blocks(pos1, pos2, tb_c), info, h, y, tb_c)


def kernel(x, positions, attn_norm, w_in, da_q_norm, da_k_norm, lam_q1, lam_k1, lam_q2, lam_k2, da_subln,
           w_proj_da, w_proj_ret, w_out, ffn_norm, dense_w1, dense_w3, dense_w2, w_router, moe_w1, moe_w3,
           moe_w2):
    B, S, D = x.shape
    assert D == D_MODEL and S % RET_CHUNK == 0
    T = B * S
    depth = attn_norm.shape[0]
    tm = min(512, S)
    tq = min(512, S)
    tc = min(512, S)

    pos_b = jnp.broadcast_to(positions.reshape(T, 1).astype(F32), (T, LANES))
    tabs = (_rope_tables(pos_b, *_rope_rows(ROPE_ROT_DIM, ROPE_THETA), tm)
            + _rope_tables(pos_b, *_rope_rows(RET_QK_DIM, RET_ROPE_THETA), tm))
    ret_consts = _retention_consts()
    gsum = jnp.asarray(np.kron(np.eye(4), np.ones((64, 64))), BF16)

    h = x.reshape(T, D)
    for layer in range(depth):
        lambda_init = 0.8 - 0.6 * math.exp(-0.3 * layer)
        gq = jnp.tile(da_q_norm[layer], PROJ_CHUNK // DA_HEAD_DIM).reshape(1, PROJ_CHUNK)
        gk = jnp.tile(da_k_norm[layer], PROJ_CHUNK // DA_HEAD_DIM).reshape(1, PROJ_CHUNK)
        z = _in_proj(h, attn_norm[layer].reshape(1, D), w_in[layer].astype(BF16), gq, gk, gsum, tabs, tm)
        lam_rows = jnp.stack([lam_q1[layer], lam_k1[layer], lam_q2[layer], lam_k2[layer]])
        ya = _diff_attn(z, lam_rows, da_subln[layer].reshape(1, DA_V_DIM), B, S, tq, lambda_init)
        yr = _retention(z, ret_consts, B, S, tc)
        h, hn = _merge(ya, yr, z, h, w_proj_da[layer].astype(BF16), w_proj_ret[layer].astype(BF16),
                       w_out[layer].astype(BF16), ffn_norm[layer].reshape(1, D), tm)
        j = layer // 2
        if layer % 2 == 0:
            h = _ffn(hn, h, dense_w1[j].astype(BF16), dense_w3[j].astype(BF16), dense_w2[j].astype(BF16), tm)
        else:
            h = _moe(h, ffn_norm[layer].reshape(1, D), w_router[j], moe_w1[j].astype(BF16),
                     moe_w3[j].astype(BF16), moe_w2[j].astype(BF16), tm)
    return h.reshape(B, S, D)
```

```python
import functools
import math

import jax
import jax.numpy as jnp
import numpy as np
from jax import lax
from jax.experimental import pallas as pl
from jax.experimental.pallas import tpu as pltpu

F32 = jnp.float32
BF16 = jnp.bfloat16

D_MODEL = 1024
DA_HEADS = 4
DA_HEAD_DIM = 64
DA_V_DIM = 128
ROPE_THETA = 500000.0
ROPE_ROT_DIM = 16
RET_HEADS = 4
RET_QK_DIM = 64
RET_V_DIM = 128
RET_CHUNK = 128
RET_ROPE_THETA = 10000.0
D_FF = 2816
N_EXPERTS = 8
EPS = 1e-6
LOG2_E = math.log2(math.e)

LANES = 128
OFF_QA, OFF_KA, OFF_VA = 0, 512, 1024
OFF_QR, OFF_KR, OFF_VR, OFF_GR = 1536, 1792, 2048, 2560
OFF_GA, OFF_GB = 3072, 4096
IN_W = 5120
PROJ_CHUNK = 512
FF_CHUNK = 256
VMEM_LIMIT = 56 * 1024 * 1024


def _cparams(sem):
    return pltpu.CompilerParams(dimension_semantics=sem, vmem_limit_bytes=VMEM_LIMIT)


def _const_spec(shape):
    nd = len(shape)
    return pl.BlockSpec(shape, lambda *_: (0,) * nd)


def _rope_table_kernel(pos_ref, inv_ref, sgn_ref, cos_ref, sin_ref):
    ang = pos_ref[...] * inv_ref[...]
    cos_ref[...] = jnp.cos(ang)
    sin_ref[...] = jnp.sin(ang) * sgn_ref[...]


def _rope_tables(pos_b, inv_row, sgn_row, tm):
    T = pos_b.shape[0]
    row = pl.BlockSpec((tm, LANES), lambda i: (i, 0))
    return pl.pallas_call(
        _rope_table_kernel,
        grid=(T // tm,),
        in_specs=[row, _const_spec((1, LANES)), _const_spec((1, LANES))],
        out_specs=[row, row],
        out_shape=[jax.ShapeDtypeStruct((T, LANES), F32)] * 2,
        compiler_params=_cparams(("parallel",)),
        name="rope_tables",
    )(pos_b, inv_row, sgn_row)


def _rope_rows(rot_dim, theta):
    half = rot_dim // 2
    inv = 1.0 / (theta ** (jnp.arange(half, dtype=F32) / half))
    jj = np.arange(LANES) % 64
    inv_row = jnp.where(jj < rot_dim, inv[jj % half], 0.0).astype(F32)
    sgn_row = np.where(jj < half, -1.0, np.where(jj < rot_dim, 1.0, 0.0)).astype(np.float32)
    return inv_row.reshape(1, LANES), jnp.asarray(sgn_row).reshape(1, LANES)


def _rotate_half(x, half):
    jj = lax.broadcasted_iota(jnp.int32, x.shape, 1) % 64
    return jnp.where(jj < half, pltpu.roll(x, LANES - half, 1), pltpu.roll(x, half, 1))


def _sigmoid(x):
    return 1.0 / (1.0 + jnp.exp(-x))


def _in_proj_kernel(h_ref, g_ref, w_ref, gq_ref, gk_ref, gsum_ref, ca_ref, sa_ref, cr_ref, sr_ref, z_ref, vt_ref):
    x = h_ref[...]
    ms = jnp.mean(x * x, axis=-1, keepdims=True)
    hn = (x * lax.rsqrt(ms + EPS) * g_ref[...]).astype(BF16)

    def head_rms(zc, gain):
        sq = zc * zc
        hi = sq.astype(BF16)
        lo = (sq - hi.astype(F32)).astype(BF16)
        outs = []
        for s in range(PROJ_CHUNK // 256):
            sl = slice(s * 256, (s + 1) * 256)
            tot = (jnp.dot(hi[:, sl], gsum_ref[...], preferred_element_type=F32)
                   + jnp.dot(lo[:, sl], gsum_ref[...], preferred_element_type=F32))
            outs.append(zc[:, sl] * lax.rsqrt(tot * (1.0 / 64.0) + EPS))
        return jnp.concatenate(outs, axis=1) * gain

    def rope(y, cos, sin, half):
        outs = []
        for s in range(y.shape[1] // LANES):
            ys = y[:, s * LANES:(s + 1) * LANES]
            outs.append(ys * cos + _rotate_half(ys, half) * sin)
        return jnp.concatenate(outs, axis=1)

    for c in range(IN_W // PROJ_CHUNK):
        c0 = c * PROJ_CHUNK
        zc = jnp.dot(hn, w_ref[:, c0:c0 + PROJ_CHUNK], preferred_element_type=F32)
        if c0 == OFF_QA:
            zc = rope(head_rms(zc, gq_ref[...]), ca_ref[...], sa_ref[...], ROPE_ROT_DIM // 2)
            zc = zc * (DA_HEAD_DIM ** -0.5 * LOG2_E)
        elif c0 == OFF_KA:
            zc = rope(head_rms(zc, gk_ref[...]), ca_ref[...], sa_ref[...], ROPE_ROT_DIM // 2)
        elif c0 == OFF_VA:
            vt_ref[0] = zc.T.astype(BF16)
        elif c0 == OFF_QR:
            zc = rope(zc, cr_ref[...], sr_ref[...], RET_QK_DIM // 2)
            col = lax.broadcasted_iota(jnp.int32, zc.shape, 1)
            zc = jnp.where(col >= OFF_KR - OFF_QR, zc * (RET_QK_DIM ** -0.5), zc)
        elif c0 == OFF_GR:
            zc = zc * _sigmoid(zc)
        elif c0 >= OFF_GA:
            zc = _sigmoid(zc)
        z_ref[:, c0:c0 + PROJ_CHUNK] = zc.astype(BF16)


def _in_proj(h, g, w, gq, gk, gsum, tabs, tm):
    T = h.shape[0]
    row = lambda width: pl.BlockSpec((tm, width), lambda i: (i, 0))
    return pl.pallas_call(
        _in_proj_kernel,
        grid=(T // tm,),
        in_specs=[row(D_MODEL), _const_spec((1, D_MODEL)), _const_spec((D_MODEL, IN_W)),
                  _const_spec((1, PROJ_CHUNK)), _const_spec((1, PROJ_CHUNK)), _const_spec((256, 256)),
                  row(LANES), row(LANES), row(LANES), row(LANES)],
        out_specs=[row(IN_W), pl.BlockSpec((1, DA_HEADS * DA_V_DIM, tm), lambda i: (i, 0, 0))],
        out_shape=[jax.ShapeDtypeStruct((T, IN_W), BF16),
                   jax.ShapeDtypeStruct((T // tm, DA_HEADS * DA_V_DIM, tm), BF16)],
        compiler_params=_cparams(("parallel",)),
        name="in_proj",
    )(h, g, w, gq, gk, gsum, *tabs)


def _diff_attn_kernel(q_ref, k_ref, vt_ref, lam_ref, subln_ref, o_ref, acc_sc, st_sc, *, tq, lambda_init):
    qi = pl.program_id(2)
    q = q_ref[...]
    lane = lax.broadcasted_iota(jnp.int32, q.shape, 1)
    qs = (jnp.where(lane < DA_HEAD_DIM, q, jnp.zeros_like(q)),
          jnp.where(lane >= DA_HEAD_DIM, q, jnp.zeros_like(q)))
    acc_sc[...] = jnp.zeros(acc_sc.shape, F32)

    tqh = tq // 2
    streams = [(c, hf) for hf in range(2) for c in range(2)]
    qsub = [qs[c][hf * tqh:(hf + 1) * tqh, :] for c, hf in streams]

    def scores(ki):
        k = k_ref[pl.ds(pl.multiple_of(ki * tq, tq), tq), :]
        return [lax.dot_general(k, qq, (((1,), (1,)), ((), ())), preferred_element_type=F32) for qq in qsub]

    def stash(sts):
        for s in range(len(streams)):
            st_sc[s] = sts[s]

    def consume(sts, ki, stats, masked):
        vt = vt_ref[ki]
        out = []
        for s, (c, hf) in enumerate(streams):
            m_old, l_old = stats[2 * s], stats[2 * s + 1]
            st = st_sc[s] if sts is None else sts[s]
            if masked:
                key = lax.broadcasted_iota(jnp.int32, st.shape, 0)
                qry = lax.broadcasted_iota(jnp.int32, st.shape, 1) + hf * tqh
                st = jnp.where(key <= qry, st, -jnp.inf)
            m_new = jnp.maximum(m_old, jnp.max(st, axis=0, keepdims=True))
            alpha = jnp.exp2(m_old - m_new)
            pt = jnp.exp2(st - m_new)
            out += [m_new, alpha * l_old + jnp.sum(pt, axis=0, keepdims=True)]
            cols = slice(hf * tqh, (hf + 1) * tqh)
            acc_sc[c, :, cols] = alpha * acc_sc[c, :, cols] + jnp.dot(vt, pt.astype(BF16),
                                                                     preferred_element_type=F32)
        return tuple(out)

    npairs = qi // 2
    stash(scores(0))

    def pair(j, stats):
        b0 = 2 * j
        odd = scores(b0 + 1)
        stats = consume(None, b0, stats, False)
        stash(scores(b0 + 2))
        return consume(odd, b0 + 1, stats, False)

    def leftover(stats):
        stats = consume(None, qi - 1, stats, False)
        stash(scores(qi))
        return stats

    init = (jnp.full((1, tqh), -jnp.inf, F32), jnp.zeros((1, tqh), F32)) * len(streams)
    stats = lax.fori_loop(0, npairs, pair, init)
    stats = lax.cond(qi - 2 * npairs == 1, leftover, lambda st: st, stats)
    stats = consume(None, qi, stats, True)
    l0 =jnp.concatenate([stats[1], stats[5]], axis=1)
    l1 = jnp.concatenate([stats[3], stats[7]], axis=1)

    lv = lam_ref[...]
    lam = (jnp.exp(jnp.sum(lv[0:1] * lv[1:2], axis=1, keepdims=True))
           - jnp.exp(jnp.sum(lv[2:3] * lv[3:4], axis=1, keepdims=True)) + lambda_init)
    ot = acc_sc[0] / l0 - lam * (acc_sc[1] / l1)
    yt = ot * lax.rsqrt(jnp.mean(ot * ot, axis=0, keepdims=True) + EPS)
    o_ref[...] = (yt.T * subln_ref[...] * (1.0 - lambda_init)).astype(BF16)


def _diff_attn(z, vt, lam_rows, subln, B, S, tq, lambda_init):
    T = B * S
    nq = S // tq
    kern = functools.partial(_diff_attn_kernel, tq=tq, lambda_init=lambda_init)
    return pl.pallas_call(
        kern,
        grid=(B, DA_HEADS, nq),
        in_specs=[pl.BlockSpec((tq, LANES), lambda b, h, i: (b * nq + i, OFF_QA // LANES + h)),
                  pl.BlockSpec((S, LANES), lambda b, h, i: (b, OFF_KA // LANES + h)),
                  pl.BlockSpec((nq, DA_V_DIM, tq), lambda b, h, i: (b, h, 0)),
                  _const_spec((4, DA_HEAD_DIM)), _const_spec((1, DA_V_DIM))],
        out_specs=pl.BlockSpec((tq, LANES), lambda b, h, i: (b * nq + i, h)),
        out_shape=jax.ShapeDtypeStruct((T, DA_HEADS * DA_V_DIM), BF16),
        scratch_shapes=[pltpu.VMEM((2, DA_V_DIM, tq), F32), pltpu.VMEM((4, tq, tq // 2), F32)],
        compiler_params=_cparams(("parallel", "parallel", "arbitrary")),
        name="diff_attn",
    )(z, z, vt, lam_rows, subln)


def _retention_kernel(q_ref, k_ref, v_ref, g_ref, dec_ref, qdec_ref, kdec_ref, cdec_ref, o_ref, r_sc, *, tc):
    @pl.when(pl.program_id(1) == 0)
    def _():
        r_sc[...] = jnp.zeros(r_sc.shape, F32)

    C = RET_CHUNK
    lane = lax.broadcasted_iota(jnp.int32, (C, LANES), 1)
    rowi = lax.broadcasted_iota(jnp.int32, (C, LANES), 0)
    for n in range(tc // C):
        rows = slice(n * C, (n + 1) * C)
        for j in range(RET_HEADS // 2):
            cols = slice(j * LANES, (j + 1) * LANES)
            qp = q_ref[rows, cols]
            kp = k_ref[rows, cols]
            qd = (qp.astype(F32) * qdec_ref[:, cols]).astype(BF16)
            kdt = (kp.astype(F32) * kdec_ref[:, cols]).T.astype(BF16)
            r_b = r_sc[j].astype(BF16)
            kv = []
            for hh in range(2):
                h = 2 * j + hh
                mine = (lane >= hh * RET_QK_DIM) & (lane < (hh + 1) * RET_QK_DIM)
                vh = v_ref[rows, h * RET_V_DIM:(h + 1) * RET_V_DIM]
                s = lax.dot_general(jnp.where(mine, qp, jnp.zeros_like(qp)), kp, (((1,), (1,)), ((), ())),
                                    preferred_element_type=F32)
                inner = (s * dec_ref[h]).astype(BF16)
                o = (jnp.dot(inner, vh, preferred_element_type=F32)
                     + jnp.dot(jnp.where(mine, qd, jnp.zeros_like(qd)), r_b, preferred_element_type=F32))
                oc = o - jnp.mean(o, axis=-1, keepdims=True)
                y = oc * lax.rsqrt(jnp.mean(oc * oc, axis=-1, keepdims=True) + EPS)
                gs = slice(h * RET_V_DIM, (h + 1) * RET_V_DIM)
                o_ref[rows, gs] = (y * g_ref[rows, gs].astype(F32)).astype(BF16)
                kv.append(jnp.dot(kdt, vh, preferred_element_type=F32))
            r_sc[j] = r_sc[j] * cdec_ref[j] + jnp.where(rowi < RET_QK_DIM, kv[0], kv[1])


def _retention(z, consts, B, S, tc):
    T = B * S
    n = S // tc
    dec, qdec, kdec, cdec = consts
    kern = functools.partial(_retention_kernel, tc=tc)
    blk = lambda width, off: pl.BlockSpec((tc, width), lambda b, i: (b * n + i, off // width))
    return pl.pallas_call(
        kern,
        grid=(B, n),
        in_specs=[blk(256, OFF_QR), blk(256, OFF_KR), blk(512, OFF_VR), blk(512, OFF_GR),
                  _const_spec(dec.shape), _const_spec(qdec.shape), _const_spec(kdec.shape),
                  _const_spec(cdec.shape)],
        out_specs=pl.BlockSpec((tc, 512), lambda b, i: (b * n + i, 0)),
        out_shape=jax.ShapeDtypeStruct((T, RET_HEADS * RET_V_DIM), BF16),
        scratch_shapes=[pltpu.VMEM((RET_HEADS // 2, LANES, LANES), F32)],
        compiler_params=_cparams(("parallel", "arbitrary")),
        name="retention",
    )(z, z, z, z, dec, qdec, kdec, cdec)


def _retention_consts():
    C = RET_CHUNK
    log_gamma = jnp.log1p(-jnp.exp2(-5.0 - jnp.arange(RET_HEADS, dtype=F32)))
    idx = jnp.arange(C, dtype=F32)
    rel = idx[:, None] - idx[None, :]
    dec = jnp.where(rel[None] >= 0, jnp.exp(log_gamma[:, None, None] * jnp.maximum(rel, 0.0)[None]), 0.0)
    q_dec = jnp.exp(log_gamma[None, :] * (idx + 1.0)[:, None])
    k_dec = jnp.exp(log_gamma[None, :] * (C - 1.0 - idx)[:, None])
    qdec = jnp.repeat(q_dec, RET_QK_DIM, axis=1)
    kdec = jnp.repeat(k_dec, RET_QK_DIM, axis=1)
    chunk_decay = jnp.repeat(jnp.exp(log_gamma * C), RET_QK_DIM)
    cdec = jnp.broadcast_to(chunk_decay.reshape(RET_HEADS // 2, LANES, 1), (RET_HEADS // 2, LANES, LANES))
    return dec.astype(F32), qdec.astype(F32), kdec.astype(F32), cdec.astype(F32)


def _merge_kernel(ya_ref, yr_ref, ga_ref, gb_ref, h_ref, wa_ref, wr_ref, wo_ref, gf_ref, h_out, hn_out):
    pa = jnp.dot(ya_ref[...], wa_ref[...], preferred_element_type=F32)
    pr = jnp.dot(yr_ref[...], wr_ref[...], preferred_element_type=F32)
    merged = ga_ref[...].astype(F32) * pa + gb_ref[...].astype(F32) * pr
    hnew = h_ref[...] + jnp.dot(merged.astype(BF16), wo_ref[...], preferred_element_type=F32)
    h_out[...] = hnew
    ms = jnp.mean(hnew * hnew, axis=-1, keepdims=True)
    hn_out[...] = (hnew * lax.rsqrt(ms + EPS) * gf_ref[...]).astype(BF16)


def _merge(ya, yr, z, h, wa, wr, wo, gf, tm):
    T = h.shape[0]
    row = lambda width, cb=0: pl.BlockSpec((tm, width), lambda i: (i, cb))
    return pl.pallas_call(
        _merge_kernel,
        grid=(T // tm,),
        in_specs=[row(512), row(512), row(D_MODEL, OFF_GA // D_MODEL), row(D_MODEL, OFF_GB // D_MODEL),
                  row(D_MODEL), _const_spec(wa.shape), _const_spec(wr.shape), _const_spec(wo.shape),
                  _const_spec((1, D_MODEL))],
        out_specs=[row(D_MODEL), row(D_MODEL)],
        out_shape=[jax.ShapeDtypeStruct((T, D_MODEL), F32), jax.ShapeDtypeStruct((T, D_MODEL), BF16)],
        compiler_params=_cparams(("parallel",)),
        name="merge_out",
    )(ya, yr, z, z, h, wa, wr, wo, gf)


def _ffn_kernel(hn_ref, prev_ref, w1_ref, w3_ref, w2_ref, o_ref, acc_sc):
    x = hn_ref[...]
    for c in range(D_FF // FF_CHUNK):
        cs = slice(c * FF_CHUNK, (c + 1) * FF_CHUNK)
        u = jnp.dot(x, w1_ref[:, cs], preferred_element_type=F32)
        v = jnp.dot(x, w3_ref[:, cs], preferred_element_type=F32)
        a = (u * _sigmoid(u) * v).astype(BF16)
        d = jnp.dot(a, w2_ref[cs, :], preferred_element_type=F32)
        if c == 0:
            acc_sc[...] = d
        else:
            acc_sc[...] += d
    o_ref[...] = prev_ref[...] + acc_sc[...]


def _ffn(hn, prev, w1, w3, w2, tm):
    T = hn.shape[0]
    row = lambda width: pl.BlockSpec((tm, width), lambda i: (i, 0))
    wspec = lambda w: pl.BlockSpec(w.shape, lambda i: (0, 0), pipeline_mode=pl.Buffered(1))
    return pl.pallas_call(
        _ffn_kernel,
        grid=(T // tm,),
        in_specs=[row(D_MODEL), row(D_MODEL), wspec(w1), wspec(w3), wspec(w2)],
        out_specs=row(D_MODEL),
        out_shape=jax.ShapeDtypeStruct((T, D_MODEL), F32),
        scratch_shapes=[pltpu.VMEM((tm, D_MODEL), F32)],
        input_output_aliases={1: 0},
        compiler_params=_cparams(("parallel",)),
        name="ffn",
    )(hn, prev, w1, w3, w2)


R_I1, R_I2, R_W1, R_W2, R_RANK1, R_RANK2 = range(6)


def _router_kernel(h_ref, g_ref, wr_ref, info_ref, cnt_ref, carry_sc):
    @pl.when(pl.program_id(0) == 0)
    def _():
        carry_sc[...] = jnp.zeros(carry_sc.shape, F32)

    x = h_ref[...]
    ms = jnp.mean(x * x, axis=-1, keepdims=True)
    hn = x * lax.rsqrt(ms + EPS) * g_ref[...]
    logits = jnp.dot(hn, wr_ref[...], preferred_element_type=F32, precision=lax.Precision.HIGHEST)
    lane = lax.broadcasted_iota(jnp.int32, logits.shape, 1)
    logits = jnp.where(lane < N_EXPERTS, logits, -jnp.inf)
    m1 = jnp.max(logits, axis=1, keepdims=True)
    i1 = jnp.min(jnp.where(logits == m1, lane, LANES), axis=1, keepdims=True)
    rest = jnp.where(lane == i1, -jnp.inf, logits)
    m2 = jnp.max(rest, axis=1, keepdims=True)
    i2 = jnp.min(jnp.where(rest == m2, lane, LANES), axis=1, keepdims=True)
    e = jnp.exp(m2 - m1)
    w1 = 1.0 / (1.0 + e)
    w2 = e / (1.0 + e)
    sel = jnp.where(lane == i1, 1.0, jnp.where(lane == i2, 1.0, 0.0))
    tm = sel.shape[0]
    rr = lax.broadcasted_iota(jnp.int32, (tm, tm), 0)
    cc = lax.broadcasted_iota(jnp.int32, (tm, tm), 1)
    ltri = jnp.where(cc < rr, 1.0, 0.0).astype(BF16)
    before = jnp.dot(ltri, sel.astype(BF16), preferred_element_type=F32) + carry_sc[0:1, :]
    r1 = jnp.sum(jnp.where(lane == i1, before, 0.0), axis=1, keepdims=True)
    r2 = jnp.sum(jnp.where(lane == i2, before, 0.0), axis=1, keepdims=True)
    info = jnp.where(lane == R_I1, i1.astype(F32), 0.0)
    for ln, val in ((R_I2, i2.astype(F32)), (R_W1, w1), (R_W2, w2), (R_RANK1, r1), (R_RANK2, r2)):
        info = jnp.where(lane == ln, val, info)
    info_ref[...] = info
    carry_sc[...] = carry_sc[...] + jnp.sum(sel, axis=0, keepdims=True)
    cnt_ref[...] = carry_sc[...]


def _router(h, g, wr_pad, tm):
    T = h.shape[0]
    return pl.pallas_call(
        _router_kernel,
        grid=(T // tm,),
        in_specs=[pl.BlockSpec((tm, D_MODEL), lambda i: (i, 0)), _const_spec((1, D_MODEL)),
                  _const_spec((D_MODEL, LANES))],
        out_specs=[pl.BlockSpec((tm, LANES), lambda i: (i, 0)), _const_spec((8, LANES))],
        out_shape=[jax.ShapeDtypeStruct((T, LANES), F32), jax.ShapeDtypeStruct((8, LANES), F32)],
        scratch_shapes=[pltpu.VMEM((8, LANES), F32)],
        compiler_params=_cparams(("arbitrary",)),
        name="router",
    )(h, g, wr_pad)


def _dispatch_kernel(pos_ref, h_ref, xs_init_hbm, xs_hbm, sem, *, tb):
    del xs_init_hbm

    def issue(t, carry):
        for k in range(2):
            p = pos_ref[0, 0, k * tb + t]
            pltpu.make_async_copy(h_ref.at[pl.ds(t, 1)], xs_hbm.at[pl.ds(p, 1)], sem).start()
        return carry

    lax.fori_loop(0, tb, issue, 0)

    def drain(t, carry):
        for k in range(2):
            pltpu.make_async_copy(h_ref.at[pl.ds(t, 1)], xs_hbm.at[pl.ds(0, 1)], sem).wait()
        return carry

    lax.fori_loop(0, tb, drain, 0)


def _dispatch(pos_blocks, h, xs_init, tb):
    T = h.shape[0]
    return pl.pallas_call(
        functools.partial(_dispatch_kernel, tb=tb),
        grid=(T // tb,),
        in_specs=[pl.BlockSpec((1, 1, 2 * tb), lambda i: (i, 0, 0), memory_space=pltpu.SMEM),
                  pl.BlockSpec((tb, D_MODEL), lambda i: (i, 0)), pl.BlockSpec(memory_space=pl.ANY)],
        out_specs=pl.BlockSpec(memory_space=pl.ANY),
        out_shape=jax.ShapeDtypeStruct(xs_init.shape, F32),
        scratch_shapes=[pltpu.SemaphoreType.DMA(())],
        input_output_aliases={2: 0},
        compiler_params=pltpu.CompilerParams(dimension_semantics=("arbitrary",), has_side_effects=True),
        name="moe_dispatch",
    )(pos_blocks, h, xs_init)


def _expert_kernel(te_ref, nu_ref, xs_ref, g_ref, w1_ref, w3_ref, w2_ref, y_ref, acc_sc):
    del te_ref
    i = pl.program_id(0)

    @pl.when(i < nu_ref[0])
    def _():
        x = xs_ref[...]
        ms = jnp.mean(x * x, axis=-1, keepdims=True)
        hn = (x * lax.rsqrt(ms + EPS) * g_ref[...]).astype(BF16)
        for c in range(D_FF // FF_CHUNK):
            cs = slice(c * FF_CHUNK, (c + 1) * FF_CHUNK)
            u = jnp.dot(hn, w1_ref[0, :, cs], preferred_element_type=F32)
            v = jnp.dot(hn, w3_ref[0, :, cs], preferred_element_type=F32)
            a = (u * _sigmoid(u) * v).astype(BF16)
            d = jnp.dot(a, w2_ref[0, cs, :], preferred_element_type=F32)
            if c == 0:
                acc_sc[...] = d
            else:
                acc_sc[...] += d
        y_ref[...] = acc_sc[...]

    @pl.when(i >= nu_ref[0])
    def _():
        y_ref[...] = jnp.zeros(y_ref.shape, F32)


def _experts(tile_expert, n_used, xs, g, w1, w3, w2, te_rows):
    P = xs.shape[0]
    row = pl.BlockSpec((te_rows, D_MODEL), lambda i, te, nu: (jnp.minimum(i, nu[0] - 1), 0))
    wspec = lambda w: pl.BlockSpec((1,) + w.shape[1:], lambda i, te, nu: (te[i], 0, 0))
    return pl.pallas_call(
        _expert_kernel,
        grid_spec=pltpu.PrefetchScalarGridSpec(
            num_scalar_prefetch=2,
            grid=(P // te_rows,),
            in_specs=[row, pl.BlockSpec((1, D_MODEL), lambda i, te, nu: (0, 0)), wspec(w1), wspec(w3), wspec(w2)],
            out_specs=pl.BlockSpec((te_rows, D_MODEL), lambda i, te, nu: (i, 0)),
            scratch_shapes=[pltpu.VMEM((te_rows, D_MODEL), F32)]),
        out_shape=jax.ShapeDtypeStruct((P, D_MODEL), F32),
        compiler_params=_cparams(("arbitrary",)),
        name="moe_experts",
    )(tile_expert, n_used, xs, g, w1, w3, w2)


def _combine_kernel(pos_ref, info_ref, h_ref, y_hbm, o_ref, g_sc, sem, *, tb):
    def issue(t, carry):
        for k in range(2):
            p = pos_ref[0, 0, k * tb + t]
            pltpu.make_async_copy(y_hbm.at[pl.ds(p, 1)], g_sc.at[k, pl.ds(t, 1)], sem).start()
        return carry

    lax.fori_loop(0, tb, issue, 0)

    def drain(t, carry):
        for k in range(2):
            pltpu.make_async_copy(y_hbm.at[pl.ds(0, 1)], g_sc.at[k, pl.ds(t, 1)], sem).wait()
        return carry

    lax.fori_loop(0, tb, drain, 0)
    info = info_ref[...]
    o_ref[...] = h_ref[...] + info[:, R_W1:R_W1 + 1] * g_sc[0] + info[:, R_W2:R_W2 + 1] * g_sc[1]


def _combine(pos_blocks, info, h, y, tb):
    T = h.shape[0]
    row = lambda width: pl.BlockSpec((tb, width), lambda i: (i, 0))
    return pl.pallas_call(
        functools.partial(_combine_kernel, tb=tb),
        grid=(T // tb,),
        in_specs=[pl.BlockSpec((1, 1, 2 * tb), lambda i: (i, 0, 0), memory_space=pltpu.SMEM),
                  row(LANES), row(D_MODEL), pl.BlockSpec(memory_space=pl.ANY)],
        out_specs=row(D_MODEL),
        out_shape=jax.ShapeDtypeStruct((T, D_MODEL), F32),
        scratch_shapes=[pltpu.VMEM((2, tb, D_MODEL), F32), pltpu.SemaphoreType.DMA(())],
        compiler_params=_cparams(("arbitrary",)),
        name="moe_combine",
    )(pos_blocks, info, h, y)


def _pos_blocks(pos1, pos2, tb):
    nb = pos1.shape[0] // tb
    return jnp.concatenate([pos1.reshape(nb, 1, tb), pos2.reshape(nb, 1, tb)], axis=2)


def _moe(h, g, w_router, w1, w3, w2, tm):
    T = h.shape[0]
    te_rows = min(512, T)
    wr_pad = jnp.pad(w_router, ((0, 0), (0, LANES - N_EXPERTS)))
    info, cnt = _router(h, g, wr_pad, tm)
    counts = cnt[0, :N_EXPERTS].astype(jnp.int32)
    padded = ((counts + te_rows - 1) // te_rows) * te_rows
    ends = jnp.cumsum(padded)
    base = ends - padded
    i1 = info[:, R_I1].astype(jnp.int32)
    i2 = info[:, R_I2].astype(jnp.int32)
    pos1 = jnp.take(base, i1) + info[:, R_RANK1].astype(jnp.int32)
    pos2 = jnp.take(base, i2) + info[:, R_RANK2].astype(jnp.int32)
    n_tiles = (2 * T) // te_rows + N_EXPERTS
    tile_expert = jnp.minimum(jnp.searchsorted(ends // te_rows, jnp.arange(n_tiles), side="right"),
                              N_EXPERTS - 1).astype(jnp.int32)
    n_used = (ends[-1:] // te_rows).astype(jnp.int32)

    tb_d = min(1024, T)
    xs = _dispatch(_pos_blocks(pos1, pos2, tb_d), h, jnp.zeros((n_tiles * te_rows, D_MODEL), F32), tb_d)
    y = _experts(tile_expert, n_used, xs, g, w1, w3, w2, te_rows)
    tb_c = min(256, T)
    return _combine(_pos_blocks(pos1, pos2, tb_c), info, h, y, tb_c)


def kernel(x, positions, attn_norm, w_in, da_q_norm, da_k_norm, lam_q1, lam_k1, lam_q2, lam_k2, da_subln,
           w_proj_da, w_proj_ret, w_out, ffn_norm, dense_w1, dense_w3, dense_w2, w_router, moe_w1, moe_w3,
           moe_w2):
    B, S, D = x.shape
    assert D == D_MODEL and S % RET_CHUNK == 0
    T = B * S
    depth = attn_norm.shape[0]
    tm = min(512, S)
    tq = min(512, S)
    tc = min(512, S)

    pos_b = jnp.broadcast_to(positions.reshape(T, 1).astype(F32), (T, LANES))
    tabs = (_rope_tables(pos_b, *_rope_rows(ROPE_ROT_DIM, ROPE_THETA), tm)
            + _rope_tables(pos_b, *_rope_rows(RET_QK_DIM, RET_ROPE_THETA), tm))
    ret_consts = _retention_consts()
    gsum = jnp.asarray(np.kron(np.eye(4), np.ones((64, 64))), BF16)

    h = x.reshape(T, D)
    for layer in range(depth):
        lambda_init = 0.8 - 0.6 * math.exp(-0.3 * layer)
        gq = jnp.tile(da_q_norm[layer], PROJ_CHUNK // DA_HEAD_DIM).reshape(1, PROJ_CHUNK)
        gk = jnp.tile(da_k_norm[layer], PROJ_CHUNK // DA_HEAD_DIM).reshape(1, PROJ_CHUNK)
        z, vt = _in_proj(h, attn_norm[layer].reshape(1, D), w_in[layer].astype(BF16), gq, gk, gsum, tabs, tm)
        lam_rows = jnp.stack([lam_q1[layer], lam_k1[layer], lam_q2[layer], lam_k2[layer]])
        ya = _diff_attn(z, vt, lam_rows, da_subln[layer].reshape(1, DA_V_DIM), B, S, tq, lambda_init)
        yr = _retention(z, ret_consts, B, S, tc)
        h, hn = _merge(ya, yr, z, h, w_proj_da[layer].astype(BF16), w_proj_ret[layer].astype(BF16),
                       w_out[layer].astype(BF16), ffn_norm[layer].reshape(1, D), tm)
        j = layer // 2
        if layer % 2 == 0:
            h = _ffn(hn, h, dense_w1[j].astype(BF16), dense_w3[j].astype(BF16), dense_w2[j].astype(BF16), tm)
        else:
            h = _moe(h, ffn_norm[layer].reshape(1, D), w_router[j], moe_w1[j].astype(BF16),
                     moe_w3[j].astype(BF16), moe_w2[j].astype(BF16), tm)
    return h.reshape(B, S, D)
```

```python
import functools
import math

import jax
import jax.numpy as jnp
import numpy as np
from jax import lax
from jax.experimental import pallas as pl
from jax.experimental.pallas import tpu as pltpu

F32 = jnp.float32
BF16 = jnp.bfloat16

D_MODEL = 1024
DA_HEADS = 4
DA_HEAD_DIM = 64
DA_V_DIM = 128
ROPE_THETA = 500000.0
ROPE_ROT_DIM = 16
RET_HEADS = 4
RET_QK_DIM = 64
RET_V_DIM = 128
RET_CHUNK = 128
RET_ROPE_THETA = 10000.0
D_FF = 2816
N_EXPERTS = 8
EPS = 1e-6
LOG2_E = math.log2(math.e)
DA_VT_ROWS = DA_V_DIM + 16

LANES = 128
OFF_QA, OFF_KA, OFF_VA = 0, 512, 1024
OFF_QR, OFF_KR, OFF_VR, OFF_GR = 1536, 1792, 2048, 2560
OFF_GA, OFF_GB = 3072, 4096
IN_W = 5120
PROJ_CHUNK = 512
FF_CHUNK = 256
VMEM_LIMIT = 56 * 1024 * 1024
DMA_UNROLL = 8


def _cparams(sem):
    return pltpu.CompilerParams(dimension_semantics=sem, vmem_limit_bytes=VMEM_LIMIT)


def _const_spec(shape):
    nd = len(shape)
    return pl.BlockSpec(shape, lambda *_: (0,) * nd)


def _rope_table_kernel(pos_ref, inv_ref, sgn_ref, cos_ref, sin_ref):
    ang = pos_ref[...] * inv_ref[...]
    cos_ref[...] = jnp.cos(ang)
    sin_ref[...] = jnp.sin(ang) * sgn_ref[...]


def _rope_tables(pos_b, inv_row, sgn_row, tm):
    T = pos_b.shape[0]
    row = pl.BlockSpec((tm, LANES), lambda i: (i, 0))
    return pl.pallas_call(
        _rope_table_kernel,
        grid=(T // tm,),
        in_specs=[row, _const_spec((1, LANES)), _const_spec((1, LANES))],
        out_specs=[row, row],
        out_shape=[jax.ShapeDtypeStruct((T, LANES), F32)] * 2,
        compiler_params=_cparams(("parallel",)),
        name="rope_tables",
    )(pos_b, inv_row, sgn_row)


def _rope_rows(rot_dim, theta):
    half = rot_dim // 2
    inv = 1.0 / (theta ** (jnp.arange(half, dtype=F32) / half))
    jj = np.arange(LANES) % 64
    inv_row = jnp.where(jj < rot_dim, inv[jj % half], 0.0).astype(F32)
    sgn_row = np.where(jj < half, -1.0, np.where(jj < rot_dim, 1.0, 0.0)).astype(np.float32)
    return inv_row.reshape(1, LANES), jnp.asarray(sgn_row).reshape(1, LANES)


def _rotate_half(x, half):
    jj = lax.broadcasted_iota(jnp.int32, x.shape, 1) % 64
    return jnp.where(jj < half, pltpu.roll(x, LANES - half, 1), pltpu.roll(x, half, 1))


def _sigmoid(x):
    return 1.0 / (1.0 + jnp.exp(-x))


def _in_proj_kernel(h_ref, g_ref, w_ref, gq_ref, gk_ref, gsum_ref, ca_ref, sa_ref, cr_ref, sr_ref, z_ref, vt_ref):
    x = h_ref[...]
    ms = jnp.mean(x * x, axis=-1, keepdims=True)
    hn = (x * lax.rsqrt(ms + EPS) * g_ref[...]).astype(BF16)

    def head_rms(zc, gain):
        sq = zc * zc
        hi = sq.astype(BF16)
        lo = (sq - hi.astype(F32)).astype(BF16)
        outs = []
        for s in range(PROJ_CHUNK // 256):
            sl = slice(s * 256, (s + 1) * 256)
            tot = (jnp.dot(hi[:, sl], gsum_ref[...], preferred_element_type=F32)
                   + jnp.dot(lo[:, sl], gsum_ref[...], preferred_element_type=F32))
            outs.append(zc[:, sl] * lax.rsqrt(tot * (1.0 / 64.0) + EPS))
        return jnp.concatenate(outs, axis=1) * gain

    def rope(y, cos, sin, half):
        outs = []
        for s in range(y.shape[1] // LANES):
            ys = y[:, s * LANES:(s + 1) * LANES]
            outs.append(ys * cos + _rotate_half(ys, half) * sin)
        return jnp.concatenate(outs, axis=1)

    for c in range(IN_W // PROJ_CHUNK):
        c0 = c * PROJ_CHUNK
        zc = jnp.dot(hn, w_ref[:, c0:c0 + PROJ_CHUNK], preferred_element_type=F32)
        if c0 == OFF_QA:
            zc = rope(head_rms(zc, gq_ref[...]), ca_ref[...], sa_ref[...], ROPE_ROT_DIM // 2)
            zc = zc * (DA_HEAD_DIM ** -0.5 * LOG2_E)
        elif c0 == OFF_KA:
            zc = rope(head_rms(zc, gk_ref[...]), ca_ref[...], sa_ref[...], ROPE_ROT_DIM // 2)
        elif c0 == OFF_VA:
            zt = zc.T.astype(BF16)
            extra = lax.broadcasted_iota(jnp.int32, (DA_VT_ROWS - DA_V_DIM, zt.shape[1]), 0)
            for hd in range(DA_HEADS):
                vt_ref[0, hd, 0:DA_V_DIM, :] = zt[hd * DA_V_DIM:(hd + 1) * DA_V_DIM, :]
                vt_ref[0, hd, DA_V_DIM:DA_VT_ROWS, :] = jnp.where(extra == 0, 1.0, 0.0).astype(BF16)
        elif c0 == OFF_QR:
            zc = rope(zc, cr_ref[...], sr_ref[...], RET_QK_DIM // 2)
            col = lax.broadcasted_iota(jnp.int32, zc.shape, 1)
            zc = jnp.where(col >= OFF_KR - OFF_QR, zc * (RET_QK_DIM ** -0.5), zc)
        elif c0 == OFF_GR:
            zc = zc * _sigmoid(zc)
        elif c0 >= OFF_GA:
            zc = _sigmoid(zc)
        z_ref[:, c0:c0 + PROJ_CHUNK] = zc.astype(BF16)


def _in_proj(h, g, w, gq, gk, gsum, tabs, tm):
    T = h.shape[0]
    row = lambda width: pl.BlockSpec((tm, width), lambda i: (i, 0))
    return pl.pallas_call(
        _in_proj_kernel,
        grid=(T // tm,),
        in_specs=[row(D_MODEL), _const_spec((1, D_MODEL)), _const_spec((D_MODEL, IN_W)),
                  _const_spec((1, PROJ_CHUNK)), _const_spec((1, PROJ_CHUNK)), _const_spec((256, 256)),
                  row(LANES), row(LANES), row(LANES), row(LANES)],
        out_specs=[row(IN_W), pl.BlockSpec((1, DA_HEADS, DA_VT_ROWS, tm), lambda i: (i, 0, 0, 0))],
        out_shape=[jax.ShapeDtypeStruct((T, IN_W), BF16),
                   jax.ShapeDtypeStruct((T // tm, DA_HEADS, DA_VT_ROWS, tm), BF16)],
        compiler_params=_cparams(("parallel",)),
        name="in_proj",
    )(h, g, w, gq, gk, gsum, *tabs)


def _diff_attn_kernel(q_ref, k_ref, vt_ref, lam_ref, subln_ref, o_ref, acc_sc, st_sc, *, tq, lambda_init):
    qi = pl.program_id(2)
    q = q_ref[...]
    lane = lax.broadcasted_iota(jnp.int32, q.shape, 1)
    qs = (jnp.where(lane < DA_HEAD_DIM, q, jnp.zeros_like(q)),
          jnp.where(lane >= DA_HEAD_DIM, q, jnp.zeros_like(q)))
    acc_sc[...] = jnp.zeros(acc_sc.shape, F32)

    tqh = tq // 2
    streams = [(c, hf) for hf in range(2) for c in range(2)]
    qsub = [qs[c][hf * tqh:(hf + 1) * tqh, :] for c, hf in streams]

    def scores(ki):
        k = k_ref[pl.ds(pl.multiple_of(ki * tq, tq), tq), :]
        return [lax.dot_general(k, qq, (((1,), (1,)), ((), ())), preferred_element_type=F32) for qq in qsub]

    def stash(sts):
        for s in range(len(streams)):
            st_sc[s] = sts[s]

    def consume(sts, ki, stats, masked):
        vt = vt_ref[ki, 0]
        out = []
        for s, (c, hf) in enumerate(streams):
            m_old = stats[s]
            st = st_sc[s] if sts is None else sts[s]
            if masked:
                key = lax.broadcasted_iota(jnp.int32, st.shape, 0)
                qry = lax.broadcasted_iota(jnp.int32, st.shape, 1) + hf * tqh
                st = jnp.where(key <= qry, st, -jnp.inf)
            m_new = jnp.maximum(m_old, jnp.max(st, axis=0, keepdims=True))
            alpha = jnp.exp2(m_old - m_new)
            pt = jnp.exp2((st - m_new).astype(BF16))
            out.append(m_new)
            cols = slice(hf * tqh, (hf + 1) * tqh)
            acc_sc[c, :, cols] = alpha * acc_sc[c, :, cols] + jnp.dot(vt, pt, preferred_element_type=F32)
        return tuple(out)

    npairs = qi // 2
    stash(scores(0))

    def pair(j, stats):
        b0 = 2 * j
        odd = scores(b0 + 1)
        stats = consume(None, b0, stats, False)
        stash(scores(b0 + 2))
        return consume(odd, b0 + 1, stats, False)

    def leftover(stats):
        stats = consume(None, qi - 1, stats, False)
        stash(scores(qi))
        return stats

    init = (jnp.full((1, tqh), -jnp.inf, F32),) * len(streams)
    stats = lax.fori_loop(0, npairs, pair, init)
    stats = lax.cond(qi - 2 * npairs == 1, leftover, lambda st: st, stats)
    consume(None, qi, stats, True)

    lv = lam_ref[...]
    lam = (jnp.exp(jnp.sum(lv[0:1] * lv[1:2], axis=1, keepdims=True))
           - jnp.exp(jnp.sum(lv[2:3] * lv[3:4], axis=1, keepdims=True)) + lambda_init)
    dv = DA_V_DIM
    ot = (acc_sc[0, 0:dv, :] / acc_sc[0, dv:dv + 1, :]
          - lam * (acc_sc[1, 0:dv, :] / acc_sc[1, dv:dv + 1, :]))
    yt = ot * lax.rsqrt(jnp.mean(ot * ot, axis=0, keepdims=True) + EPS)
    o_ref[...] = (yt.T * subln_ref[...] * (1.0 - lambda_init)).astype(BF16)


def _diff_attn(z, vt, lam_rows, subln, B, S, tq, lambda_init):
    T = B * S
    nq = S // tq
    kern = functools.partial(_diff_attn_kernel, tq=tq, lambda_init=lambda_init)
    return pl.pallas_call(
        kern,
        grid=(B, DA_HEADS, nq),
        in_specs=[pl.BlockSpec((tq, LANES), lambda b, h, i: (b * nq + i, OFF_QA // LANES + h)),
                  pl.BlockSpec((S, LANES), lambda b, h, i: (b, OFF_KA // LANES + h)),
                  pl.BlockSpec((nq, 1, DA_VT_ROWS, tq), lambda b, h, i: (b, h, 0, 0)),
                  _const_spec((4, DA_HEAD_DIM)), _const_spec((1, DA_V_DIM))],
        out_specs=pl.BlockSpec((tq, LANES), lambda b, h, i: (b * nq + i, h)),
        out_shape=jax.ShapeDtypeStruct((T, DA_HEADS * DA_V_DIM), BF16),
        scratch_shapes=[pltpu.VMEM((2, DA_VT_ROWS, tq), F32), pltpu.VMEM((4, tq, tq // 2), F32)],
        compiler_params=_cparams(("parallel", "parallel", "arbitrary")),
        name="diff_attn",
    )(z, z, vt, lam_rows, subln)


def _retention_kernel(q_ref, k_ref, v_ref, g_ref, dec_ref, qdec_ref, kdec_ref, cdec_ref, o_ref, r_sc, *, tc):
    @pl.when(pl.program_id(1) == 0)
    def _():
        r_sc[...] = jnp.zeros(r_sc.shape, F32)

    C = RET_CHUNK
    lane = lax.broadcasted_iota(jnp.int32, (C, LANES), 1)
    rowi = lax.broadcasted_iota(jnp.int32, (C, LANES), 0)
    for n in range(tc // C):
        rows = slice(n * C, (n + 1) * C)
        for j in range(RET_HEADS // 2):
            cols = slice(j * LANES, (j + 1) * LANES)
            qp = q_ref[rows, cols]
            kp = k_ref[rows, cols]
            qd = (qp.astype(F32) * qdec_ref[:, cols]).astype(BF16)
            kdt = (kp.astype(F32) * kdec_ref[:, cols]).T.astype(BF16)
            r_b = r_sc[j].astype(BF16)
            kv = []
            for hh in range(2):
                h = 2 * j + hh
                mine = (lane >= hh * RET_QK_DIM) & (lane < (hh + 1) * RET_QK_DIM)
                vh = v_ref[rows, h * RET_V_DIM:(h + 1) * RET_V_DIM]
                s = lax.dot_general(jnp.where(mine, qp, jnp.zeros_like(qp)), kp, (((1,), (1,)), ((), ())),
                                    preferred_element_type=F32)
                inner = (s * dec_ref[h]).astype(BF16)
                o = (jnp.dot(inner, vh, preferred_element_type=F32)
                     + jnp.dot(jnp.where(mine, qd, jnp.zeros_like(qd)), r_b, preferred_element_type=F32))
                oc = o - jnp.mean(o, axis=-1, keepdims=True)
                y = oc * lax.rsqrt(jnp.mean(oc * oc, axis=-1, keepdims=True) + EPS)
                gs = slice(h * RET_V_DIM, (h + 1) * RET_V_DIM)
                o_ref[rows, gs] = (y * g_ref[rows, gs].astype(F32)).astype(BF16)
                kv.append(jnp.dot(kdt, vh, preferred_element_type=F32))
            r_sc[j] = r_sc[j] * cdec_ref[j] + jnp.where(rowi < RET_QK_DIM, kv[0], kv[1])


def _retention(z, consts, B, S, tc):
    T = B * S
    n = S // tc
    dec, qdec, kdec, cdec = consts
    kern = functools.partial(_retention_kernel, tc=tc)
    blk = lambda width, off: pl.BlockSpec((tc, width), lambda b, i: (b * n + i, off // width))
    return pl.pallas_call(
        kern,
        grid=(B, n),
        in_specs=[blk(256, OFF_QR), blk(256, OFF_KR), blk(512, OFF_VR), blk(512, OFF_GR),
                  _const_spec(dec.shape), _const_spec(qdec.shape), _const_spec(kdec.shape),
                  _const_spec(cdec.shape)],
        out_specs=pl.BlockSpec((tc, 512), lambda b, i: (b * n + i, 0)),
        out_shape=jax.ShapeDtypeStruct((T, RET_HEADS * RET_V_DIM), BF16),
        scratch_shapes=[pltpu.VMEM((RET_HEADS // 2, LANES, LANES), F32)],
        compiler_params=_cparams(("parallel", "arbitrary")),
        name="retention",
    )(z, z, z, z, dec, qdec, kdec, cdec)


def _retention_consts():
    C = RET_CHUNK
    log_gamma = jnp.log1p(-jnp.exp2(-5.0 - jnp.arange(RET_HEADS, dtype=F32)))
    idx = jnp.arange(C, dtype=F32)
    rel = idx[:, None] - idx[None, :]
    dec = jnp.where(rel[None] >= 0, jnp.exp(log_gamma[:, None, None] * jnp.maximum(rel, 0.0)[None]), 0.0)
    q_dec = jnp.exp(log_gamma[None, :] * (idx + 1.0)[:, None])
    k_dec = jnp.exp(log_gamma[None, :] * (C - 1.0 - idx)[:, None])
    qdec = jnp.repeat(q_dec, RET_QK_DIM, axis=1)
    kdec = jnp.repeat(k_dec, RET_QK_DIM, axis=1)
    chunk_decay = jnp.repeat(jnp.exp(log_gamma * C), RET_QK_DIM)
    cdec = jnp.broadcast_to(chunk_decay.reshape(RET_HEADS // 2, LANES, 1), (RET_HEADS // 2, LANES, LANES))
    return dec.astype(F32), qdec.astype(F32), kdec.astype(F32), cdec.astype(F32)


def _merge_kernel(ya_ref, yr_ref, ga_ref, gb_ref, h_ref, wa_ref, wr_ref, wo_ref, gf_ref, h_out, hn_out):
    pa = jnp.dot(ya_ref[...], wa_ref[...], preferred_element_type=F32)
    pr = jnp.dot(yr_ref[...], wr_ref[...], preferred_element_type=F32)
    merged = ga_ref[...].astype(F32) * pa + gb_ref[...].astype(F32) * pr
    hnew = h_ref[...] + jnp.dot(merged.astype(BF16), wo_ref[...], preferred_element_type=F32)
    h_out[...] = hnew
    ms = jnp.mean(hnew * hnew, axis=-1, keepdims=True)
    hn_out[...] = (hnew * lax.rsqrt(ms + EPS) * gf_ref[...]).astype(BF16)


def _merge(ya, yr, z, h, wa, wr, wo, gf, tm):
    T = h.shape[0]
    row = lambda width, cb=0: pl.BlockSpec((tm, width), lambda i: (i, cb))
    return pl.pallas_call(
        _merge_kernel,
        grid=(T // tm,),
        in_specs=[row(512), row(512), row(D_MODEL, OFF_GA // D_MODEL), row(D_MODEL, OFF_GB // D_MODEL),
                  row(D_MODEL), _const_spec(wa.shape), _const_spec(wr.shape), _const_spec(wo.shape),
                  _const_spec((1, D_MODEL))],
        out_specs=[row(D_MODEL), row(D_MODEL)],
        out_shape=[jax.ShapeDtypeStruct((T, D_MODEL), F32), jax.ShapeDtypeStruct((T, D_MODEL), BF16)],
        compiler_params=_cparams(("parallel",)),
        name="merge_out",
    )(ya, yr, z, z, h, wa, wr, wo, gf)


def _ffn_kernel(hn_ref, prev_ref, w1_ref, w3_ref, w2_ref, o_ref, acc_sc):
    x = hn_ref[...]
    for c in range(D_FF // FF_CHUNK):
        cs = slice(c * FF_CHUNK, (c + 1) * FF_CHUNK)
        u = jnp.dot(x, w1_ref[:, cs], preferred_element_type=F32)
        v = jnp.dot(x, w3_ref[:, cs], preferred_element_type=F32)
        a = (u * _sigmoid(u) * v).astype(BF16)
        d = jnp.dot(a, w2_ref[cs, :], preferred_element_type=F32)
        if c == 0:
            acc_sc[...] = d
        else:
            acc_sc[...] += d
    o_ref[...] = prev_ref[...] + acc_sc[...]


def _ffn(hn, prev, w1, w3, w2, tm):
    T = hn.shape[0]
    row = lambda width: pl.BlockSpec((tm, width), lambda i: (i, 0))
    wspec = lambda w: pl.BlockSpec(w.shape, lambda i: (0, 0), pipeline_mode=pl.Buffered(1))
    return pl.pallas_call(
        _ffn_kernel,
        grid=(T // tm,),
        in_specs=[row(D_MODEL), row(D_MODEL), wspec(w1), wspec(w3), wspec(w2)],
        out_specs=row(D_MODEL),
        out_shape=jax.ShapeDtypeStruct((T, D_MODEL), F32),
        scratch_shapes=[pltpu.VMEM((tm, D_MODEL), F32)],
        input_output_aliases={1: 0},
        compiler_params=_cparams(("parallel",)),
        name="ffn",
    )(hn, prev, w1, w3, w2)


R_I1, R_I2, R_W1, R_W2, R_RANK1, R_RANK2 = range(6)


def _router_kernel(h_ref, g_ref, wr_ref, info_ref, cnt_ref, carry_sc):
    @pl.when(pl.program_id(0) == 0)
    def _():
        carry_sc[...] = jnp.zeros(carry_sc.shape, F32)

    x = h_ref[...]
    ms = jnp.mean(x * x, axis=-1, keepdims=True)
    hn = x * lax.rsqrt(ms + EPS) * g_ref[...]
    logits = jnp.dot(hn, wr_ref[...], preferred_element_type=F32, precision=lax.Precision.HIGHEST)
    lane = lax.broadcasted_iota(jnp.int32, logits.shape, 1)
    logits = jnp.where(lane < N_EXPERTS, logits, -jnp.inf)
    m1 = jnp.max(logits, axis=1, keepdims=True)
    i1 = jnp.min(jnp.where(logits == m1, lane, LANES), axis=1, keepdims=True)
    rest = jnp.where(lane == i1, -jnp.inf, logits)
    m2 = jnp.max(rest, axis=1, keepdims=True)
    i2 = jnp.min(jnp.where(rest == m2, lane, LANES), axis=1, keepdims=True)
    e = jnp.exp(m2 - m1)
    w1 = 1.0 / (1.0 + e)
    w2 = e / (1.0 + e)
    sel = jnp.where(lane == i1, 1.0, jnp.where(lane == i2, 1.0, 0.0))
    tm = sel.shape[0]
    rr = lax.broadcasted_iota(jnp.int32, (tm, tm), 0)
    cc = lax.broadcasted_iota(jnp.int32, (tm, tm), 1)
    ltri = jnp.where(cc < rr, 1.0, 0.0).astype(BF16)
    before = jnp.dot(ltri, sel.astype(BF16), preferred_element_type=F32) + carry_sc[0:1, :]
    r1 = jnp.sum(jnp.where(lane == i1, before, 0.0), axis=1, keepdims=True)
    r2 = jnp.sum(jnp.where(lane == i2, before, 0.0), axis=1, keepdims=True)
    info = jnp.where(lane == R_I1, i1.astype(F32), 0.0)
    for ln, val in ((R_I2, i2.astype(F32)), (R_W1, w1), (R_W2, w2), (R_RANK1, r1), (R_RANK2, r2)):
        info = jnp.where(lane == ln, val, info)
    info_ref[...] = info
    carry_sc[...] = carry_sc[...] + jnp.sum(sel, axis=0, keepdims=True)
    cnt_ref[...] = carry_sc[...]


def _router(h, g, wr_pad, tm):
    T = h.shape[0]
    return pl.pallas_call(
        _router_kernel,
        grid=(T // tm,),
        in_specs=[pl.BlockSpec((tm, D_MODEL), lambda i: (i, 0)), _const_spec((1, D_MODEL)),
                  _const_spec((D_MODEL, LANES))],
        out_specs=[pl.BlockSpec((tm, LANES), lambda i: (i, 0)), _const_spec((8, LANES))],
        out_shape=[jax.ShapeDtypeStruct((T, LANES), F32), jax.ShapeDtypeStruct((8, LANES), F32)],
        scratch_shapes=[pltpu.VMEM((8, LANES), F32)],
        compiler_params=_cparams(("arbitrary",)),
        name="router",
    )(h, g, wr_pad)


def _dispatch_kernel(pos_ref, h_ref, xs_init_hbm, xs_hbm, sem, *, tb):
    del xs_init_hbm

    def issue(t, carry):
        for k in range(2):
            p = pos_ref[0, 0, k * tb + t]
            pltpu.make_async_copy(h_ref.at[pl.ds(t, 1)], xs_hbm.at[pl.ds(p, 1)], sem).start()
        return carry

    lax.fori_loop(0, tb, issue, 0, unroll=DMA_UNROLL)

    def drain(t, carry):
        for k in range(2):
            pltpu.make_async_copy(h_ref.at[pl.ds(t, 1)], xs_hbm.at[pl.ds(0, 1)], sem).wait()
        return carry

    lax.fori_loop(0, tb, drain, 0, unroll=DMA_UNROLL)


def _dispatch(pos_blocks, h, xs_init, tb):
    T = h.shape[0]
    return pl.pallas_call(
        functools.partial(_dispatch_kernel, tb=tb),
        grid=(T // tb,),
        in_specs=[pl.BlockSpec((1, 1, 2 * tb), lambda i: (i, 0, 0), memory_space=pltpu.SMEM),
                  pl.BlockSpec((tb, D_MODEL), lambda i: (i, 0)), pl.BlockSpec(memory_space=pl.ANY)],
        out_specs=pl.BlockSpec(memory_space=pl.ANY),
        out_shape=jax.ShapeDtypeStruct(xs_init.shape, F32),
        scratch_shapes=[pltpu.SemaphoreType.DMA(())],
        input_output_aliases={2: 0},
        compiler_params=pltpu.CompilerParams(dimension_semantics=("arbitrary",), has_side_effects=True),
        name="moe_dispatch",
    )(pos_blocks, h, xs_init)


def _expert_kernel(te_ref, nu_ref, xs_ref, g_ref, w1_ref, w3_ref, w2_ref, y_ref, acc_sc):
    del te_ref
    i = pl.program_id(0)

    @pl.when(i < nu_ref[0])
    def _():
        x = xs_ref[...]
        ms = jnp.mean(x * x, axis=-1, keepdims=True)
        hn = (x * lax.rsqrt(ms + EPS) * g_ref[...]).astype(BF16)
        for c in range(D_FF // FF_CHUNK):
            cs = slice(c * FF_CHUNK, (c + 1) * FF_CHUNK)
            u = jnp.dot(hn, w1_ref[0, :, cs], preferred_element_type=F32)
            v = jnp.dot(hn, w3_ref[0, :, cs], preferred_element_type=F32)
            a = (u * _sigmoid(u) * v).astype(BF16)
            d = jnp.dot(a, w2_ref[0, cs, :], preferred_element_type=F32)
            if c == 0:
                acc_sc[...] = d
            else:
                acc_sc[...] += d
        y_ref[...] = acc_sc[...]

    @pl.when(i >= nu_ref[0])
    def _():
        y_ref[...] = jnp.zeros(y_ref.shape, F32)


def _experts(tile_expert, n_used, xs, g, w1, w3, w2, te_rows):
    P = xs.shape[0]
    row = pl.BlockSpec((te_rows, D_MODEL), lambda i, te, nu: (jnp.minimum(i, nu[0] - 1), 0))
    wspec = lambda w: pl.BlockSpec((1,) + w.shape[1:], lambda i, te, nu: (te[i], 0, 0))
    return pl.pallas_call(
        _expert_kernel,
        grid_spec=pltpu.PrefetchScalarGridSpec(
            num_scalar_prefetch=2,
            grid=(P // te_rows,),
            in_specs=[row, pl.BlockSpec((1, D_MODEL), lambda i, te, nu: (0, 0)), wspec(w1), wspec(w3), wspec(w2)],
            out_specs=pl.BlockSpec((te_rows, D_MODEL), lambda i, te, nu: (i, 0)),
            scratch_shapes=[pltpu.VMEM((te_rows, D_MODEL), F32)]),
        out_shape=jax.ShapeDtypeStruct((P, D_MODEL), F32),
        compiler_params=_cparams(("arbitrary",)),
        name="moe_experts",
    )(tile_expert, n_used, xs, g, w1, w3, w2)


def _combine_kernel(pos_ref, info_ref, h_ref, y_hbm, o_ref, g_sc, sem, *, tb):
    def issue(t, carry):
        for k in range(2):
            p = pos_ref[0, 0, k * tb + t]
            pltpu.make_async_copy(y_hbm.at[pl.ds(p, 1)], g_sc.at[k, pl.ds(t, 1)], sem).start()
        return carry

    lax.fori_loop(0, tb, issue, 0, unroll=DMA_UNROLL)

    def drain(t, carry):
        for k in range(2):
            pltpu.make_async_copy(y_hbm.at[pl.ds(0, 1)], g_sc.at[k, pl.ds(t, 1)], sem).wait()
        return carry

    lax.fori_loop(0, tb, drain, 0, unroll=DMA_UNROLL)
    info = info_ref[...]
    o_ref[...] = h_ref[...] + info[:, R_W1:R_W1 + 1] * g_sc[0] + info[:, R_W2:R_W2 + 1] * g_sc[1]


def _combine(pos_blocks, info, h, y, tb):
    T = h.shape[0]
    row = lambda width: pl.BlockSpec((tb, width), lambda i: (i, 0))
    return pl.pallas_call(
        functools.partial(_combine_kernel, tb=tb),
        grid=(T // tb,),
        in_specs=[pl.BlockSpec((1, 1, 2 * tb), lambda i: (i, 0, 0), memory_space=pltpu.SMEM),
                  row(LANES), row(D_MODEL), pl.BlockSpec(memory_space=pl.ANY)],
        out_specs=row(D_MODEL),
        out_shape=jax.ShapeDtypeStruct((T, D_MODEL), F32),
        scratch_shapes=[pltpu.VMEM((2, tb, D_MODEL), F32), pltpu.SemaphoreType.DMA(())],
        compiler_params=_cparams(("arbitrary",)),
        name="moe_combine",
    )(pos_blocks, info, h, y)


def _pos_blocks(pos1, pos2, tb):
    nb = pos1.shape[0] // tb
    return jnp.concatenate([pos1.reshape(nb, 1, tb), pos2.reshape(nb, 1, tb)], axis=2)


def _moe(h, g, w_router, w1, w3, w2, tm):
    T = h.shape[0]
    te_rows = min(512, T)
    wr_pad = jnp.pad(w_router, ((0, 0), (0, LANES - N_EXPERTS)))
    info, cnt = _router(h, g, wr_pad, tm)
    counts = cnt[0, :N_EXPERTS].astype(jnp.int32)
    padded = ((counts + te_rows - 1) // te_rows) * te_rows
    ends = jnp.cumsum(padded)
    base = ends - padded
    i1 = info[:, R_I1].astype(jnp.int32)
    i2 = info[:, R_I2].astype(jnp.int32)
    pos1 = jnp.take(base, i1) + info[:, R_RANK1].astype(jnp.int32)
    pos2 = jnp.take(base, i2) + info[:, R_RANK2].astype(jnp.int32)
    n_tiles = (2 * T) // te_rows + N_EXPERTS
    tile_expert = jnp.minimum(jnp.searchsorted(ends // te_rows, jnp.arange(n_tiles), side="right"),
                              N_EXPERTS - 1).astype(jnp.int32)
    n_used = (ends[-1:] // te_rows).astype(jnp.int32)

    tb_d = min(1024, T)
    xs = _dispatch(_pos_blocks(pos1, pos2, tb_d), h, jnp.zeros((n_tiles * te_rows, D_MODEL), F32), tb_d)
    y = _experts(tile_expert, n_used, xs, g, w1, w3, w2, te_rows)
    tb_c = min(256, T)
    return _combine(_pos_blocks(pos1, pos2, tb_c), info, h, y, tb_c)


def kernel(x, positions, attn_norm, w_in, da_q_norm, da_k_norm, lam_q1, lam_k1, lam_q2, lam_k2, da_subln,
           w_proj_da, w_proj_ret, w_out, ffn_norm, dense_w1, dense_w3, dense_w2, w_router, moe_w1, moe_w3,
           moe_w2):
    B, S, D = x.shape
    assert D == D_MODEL and S % RET_CHUNK == 0
    T = B * S
    depth = attn_norm.shape[0]
    tm = min(512, S)
    tq = min(512, S)
    tc = min(512, S)

    pos_b = jnp.broadcast_to(positions.reshape(T, 1).astype(F32), (T, LANES))
    tabs = (_rope_tables(pos_b, *_rope_rows(ROPE_ROT_DIM, ROPE_THETA), tm)
            + _rope_tables(pos_b, *_rope_rows(RET_QK_DIM, RET_ROPE_THETA), tm))
    ret_consts = _retention_consts()
    gsum = jnp.asarray(np.kron(np.eye(4), np.ones((64, 64))), BF16)

    h = x.reshape(T, D)
    for layer in range(depth):
        lambda_init = 0.8 - 0.6 * math.exp(-0.3 * layer)
        gq = jnp.tile(da_q_norm[layer], PROJ_CHUNK // DA_HEAD_DIM).reshape(1, PROJ_CHUNK)
        gk = jnp.tile(da_k_norm[layer], PROJ_CHUNK // DA_HEAD_DIM).reshape(1, PROJ_CHUNK)
        z, vt = _in_proj(h, attn_norm[layer].reshape(1, D), w_in[layer].astype(BF16), gq, gk, gsum, tabs, tm)
        lam_rows = jnp.stack([lam_q1[layer], lam_k1[layer], lam_q2[layer], lam_k2[layer]])
        ya = _diff_attn(z, vt, lam_rows, da_subln[layer].reshape(1, DA_V_DIM), B, S, tq, lambda_init)
        yr = _retention(z, ret_consts, B, S, tc)
        h, hn = _merge(ya, yr, z, h, w_proj_da[layer].astype(BF16), w_proj_ret[layer].astype(BF16),
                       w_out[layer].astype(BF16), ffn_norm[layer].reshape(1, D), tm)
        j = layer // 2
        if layer % 2 == 0:
            h = _ffn(hn, h, dense_w1[j].astype(BF16), dense_w3[j].astype(BF16), dense_w2[j].astype(BF16), tm)
        else:
            h = _moe(h, ffn_norm[layer].reshape(1, D), w_router[j], moe_w1[j].astype(BF16),
                     moe_w3[j].astype(BF16), moe_w2[j].astype(BF16), tm)
    return h.reshape(B, S, D)
```

```python
import functools
import math

import jax
import jax.numpy as jnp
import numpy as np
from jax import lax
from jax.experimental import pallas as pl
from jax.experimental.pallas import tpu as pltpu

F32 = jnp.float32
BF16 = jnp.bfloat16

D_MODEL = 1024
DA_HEADS = 4
DA_HEAD_DIM = 64
DA_V_DIM = 128
ROPE_THETA = 500000.0
ROPE_ROT_DIM = 16
RET_HEADS = 4
RET_QK_DIM = 64
RET_V_DIM = 128
RET_CHUNK = 128
RET_ROPE_THETA = 10000.0
D_FF = 2816
N_EXPERTS = 8
EPS = 1e-6
LOG2_E = math.log2(math.e)
DA_VT_ROWS = DA_V_DIM + 16
FIXED_REF_MAX_BOUND = 60.0
GROUP = 4

LANES = 128
OFF_QA, OFF_KA, OFF_VA = 0, 512, 1024
OFF_QR, OFF_KR, OFF_VR, OFF_GR = 1536, 1792, 2048, 2560
OFF_GA, OFF_GB = 3072, 4096
IN_W = 5120
PROJ_CHUNK = 512
FF_CHUNK = 256
VMEM_LIMIT = 56 * 1024 * 1024
DMA_UNROLL = 8


def _cparams(sem):
    return pltpu.CompilerParams(dimension_semantics=sem, vmem_limit_bytes=VMEM_LIMIT)


def _const_spec(shape):
    nd = len(shape)
    return pl.BlockSpec(shape, lambda *_: (0,) * nd)


def _rope_table_kernel(pos_ref, inv_ref, sgn_ref, cos_ref, sin_ref):
    ang = pos_ref[...] * inv_ref[...]
    cos_ref[...] = jnp.cos(ang)
    sin_ref[...] = jnp.sin(ang) * sgn_ref[...]


def _rope_tables(pos_b, inv_row, sgn_row, tm):
    T = pos_b.shape[0]
    row = pl.BlockSpec((tm, LANES), lambda i: (i, 0))
    return pl.pallas_call(
        _rope_table_kernel,
        grid=(T // tm,),
        in_specs=[row, _const_spec((1, LANES)), _const_spec((1, LANES))],
        out_specs=[row, row],
        out_shape=[jax.ShapeDtypeStruct((T, LANES), F32)] * 2,
        compiler_params=_cparams(("parallel",)),
        name="rope_tables",
    )(pos_b, inv_row, sgn_row)


def _rope_rows(rot_dim, theta):
    half = rot_dim // 2
    inv = 1.0 / (theta ** (jnp.arange(half, dtype=F32) / half))
    jj = np.arange(LANES) % 64
    inv_row = jnp.where(jj < rot_dim, inv[jj % half], 0.0).astype(F32)
    sgn_row = np.where(jj < half, -1.0, np.where(jj < rot_dim, 1.0, 0.0)).astype(np.float32)
    return inv_row.reshape(1, LANES), jnp.asarray(sgn_row).reshape(1, LANES)


def _rotate_half(x, half):
    jj = lax.broadcasted_iota(jnp.int32, x.shape, 1) % 64
    return jnp.where(jj < half, pltpu.roll(x, LANES - half, 1), pltpu.roll(x, half, 1))


def _sigmoid(x):
    return 1.0 / (1.0 + jnp.exp(-x))


def _in_proj_kernel(h_ref, g_ref, w_ref, gq_ref, gk_ref, gsum_ref, ca_ref, sa_ref, cr_ref, sr_ref, z_ref, vt_ref):
    x = h_ref[...]
    ms = jnp.mean(x * x, axis=-1, keepdims=True)
    hn = (x * lax.rsqrt(ms + EPS) * g_ref[...]).astype(BF16)

    def head_rms(zc, gain):
        sq = zc * zc
        hi = sq.astype(BF16)
        lo = (sq - hi.astype(F32)).astype(BF16)
        outs = []
        for s in range(PROJ_CHUNK // 256):
            sl = slice(s * 256, (s + 1) * 256)
            tot = (jnp.dot(hi[:, sl], gsum_ref[...], preferred_element_type=F32)
                   + jnp.dot(lo[:, sl], gsum_ref[...], preferred_element_type=F32))
            outs.append(zc[:, sl] * lax.rsqrt(tot * (1.0 / 64.0) + EPS))
        return jnp.concatenate(outs, axis=1) * gain

    def rope(y, cos, sin, half):
        outs = []
        for s in range(y.shape[1] // LANES):
            ys = y[:, s * LANES:(s + 1) * LANES]
            outs.append(ys * cos + _rotate_half(ys, half) * sin)
        return jnp.concatenate(outs, axis=1)

    for c in range(IN_W // PROJ_CHUNK):
        c0 = c * PROJ_CHUNK
        zc = jnp.dot(hn, w_ref[:, c0:c0 + PROJ_CHUNK], preferred_element_type=F32)
        if c0 == OFF_QA:
            zc = rope(head_rms(zc, gq_ref[...]), ca_ref[...], sa_ref[...], ROPE_ROT_DIM // 2)
            zc = zc * (DA_HEAD_DIM ** -0.5 * LOG2_E)
        elif c0 == OFF_KA:
            zc = rope(head_rms(zc, gk_ref[...]), ca_ref[...], sa_ref[...], ROPE_ROT_DIM // 2)
        elif c0 == OFF_VA:
            zt = zc.T.astype(BF16)
            extra = lax.broadcasted_iota(jnp.int32, (DA_VT_ROWS - DA_V_DIM, zt.shape[1]), 0)
            for hd in range(DA_HEADS):
                vt_ref[0, hd, 0:DA_V_DIM, :] = zt[hd * DA_V_DIM:(hd + 1) * DA_V_DIM, :]
                vt_ref[0, hd, DA_V_DIM:DA_VT_ROWS, :] = jnp.where(extra == 0, 1.0, 0.0).astype(BF16)
        elif c0 == OFF_QR:
            zc = rope(zc, cr_ref[...], sr_ref[...], RET_QK_DIM // 2)
            col = lax.broadcasted_iota(jnp.int32, zc.shape, 1)
            zc = jnp.where(col >= OFF_KR - OFF_QR, zc * (RET_QK_DIM ** -0.5), zc)
        elif c0 == OFF_GR:
            zc = zc * _sigmoid(zc)
        elif c0 >= OFF_GA:
            zc = _sigmoid(zc)
        z_ref[:, c0:c0 + PROJ_CHUNK] = zc.astype(BF16)


def _in_proj(h, g, w, gq, gk, gsum, tabs, tm):
    T = h.shape[0]
    row = lambda width: pl.BlockSpec((tm, width), lambda i: (i, 0))
    return pl.pallas_call(
        _in_proj_kernel,
        grid=(T // tm,),
        in_specs=[row(D_MODEL), _const_spec((1, D_MODEL)), _const_spec((D_MODEL, IN_W)),
                  _const_spec((1, PROJ_CHUNK)), _const_spec((1, PROJ_CHUNK)), _const_spec((256, 256)),
                  row(LANES), row(LANES), row(LANES), row(LANES)],
        out_specs=[row(IN_W), pl.BlockSpec((1, DA_HEADS, DA_VT_ROWS, tm), lambda i: (i, 0, 0, 0))],
        out_shape=[jax.ShapeDtypeStruct((T, IN_W), BF16),
                   jax.ShapeDtypeStruct((T // tm, DA_HEADS, DA_VT_ROWS, tm), BF16)],
        compiler_params=_cparams(("parallel",)),
        name="in_proj",
    )(h, g, w, gq, gk, gsum, *tabs)


def _diff_attn_kernel(bound_ref, q_ref, k_ref, vt_ref, lam_ref, subln_ref, o_ref, acc_sc, st_sc, *, tq,
                      lambda_init):
    qi = pl.program_id(2)
    q = q_ref[...]
    lane = lax.broadcasted_iota(jnp.int32, q.shape, 1)
    qs = (jnp.where(lane < DA_HEAD_DIM, q, jnp.zeros_like(q)),
          jnp.where(lane >= DA_HEAD_DIM, q, jnp.zeros_like(q)))
    acc_sc[...] = jnp.zeros(acc_sc.shape, F32)

    tqh = tq // 2
    streams = [(c, hf) for hf in range(2) for c in range(2)]
    qsub = [qs[c][hf * tqh:(hf + 1) * tqh, :] for c, hf in streams]

    def scores(ki):
        k = k_ref[pl.ds(pl.multiple_of(ki * tq, tq), tq), :]
        return [lax.dot_general(k, qq, (((1,), (1,)), ((), ())), preferred_element_type=F32) for qq in qsub]

    def stash(sts):
        for s in range(len(streams)):
            st_sc[s] = sts[s]

    bound = bound_ref[0, 0]

    def consume(sts, ki, stats, masked, fixed_ref):
        vt = vt_ref[ki, 0]
        out = []
        for s, (c, hf) in enumerate(streams):
            st = st_sc[s] if sts is None else sts[s]
            if masked:
                key = lax.broadcasted_iota(jnp.int32, st.shape, 0)
                qry = lax.broadcasted_iota(jnp.int32, st.shape, 1) + hf * tqh
                st = jnp.where(key <= qry, st, -jnp.inf)
            cols = slice(hf * tqh, (hf + 1) * tqh)
            if fixed_ref:
                pt = jnp.exp2(st - bound).astype(BF16)
                acc_sc[c, :, cols] += jnp.dot(vt, pt, preferred_element_type=F32)
                out.append(stats[s])
            else:
                m_old = stats[s]
                m_new = jnp.maximum(m_old, jnp.max(st, axis=0, keepdims=True))
                alpha = jnp.exp2(m_old - m_new)
                pt = jnp.exp2((st - m_new).astype(BF16))
                acc_sc[c, :, cols] = alpha * acc_sc[c, :, cols] + jnp.dot(vt, pt, preferred_element_type=F32)
                out.append(m_new)
        return tuple(out)

    def run_fixed_ref():
        none = (None,) * len(streams)

        def group(j, carry):
            blocks = [GROUP * j + u for u in range(GROUP)]
            sts = [scores(b) for b in blocks]
            for b, st in zip(blocks, sts):
                consume(st, b, none, False, True)
            return carry

        def single(b, carry):
            consume(scores(b), b, none, False, True)
            return carry

        ngroups = qi // GROUP
        lax.fori_loop(0, ngroups, group, 0)
        lax.fori_loop(GROUP * ngroups, qi, single, 0)
        consume(scores(qi), qi, none, True, True)

    def run_online():
        npairs = qi // 2
        stash(scores(0))

        def pair(j, stats):
            b0 = 2 * j
            odd = scores(b0 + 1)
            stats = consume(None, b0, stats, False, False)
            stash(scores(b0 + 2))
            return consume(odd, b0 + 1, stats, False, False)

        def leftover(stats):
            stats = consume(None, qi - 1, stats, False, False)
            stash(scores(qi))
            return stats

        init = (jnp.full((1, tqh), -jnp.inf, F32),) * len(streams)
        stats = lax.fori_loop(0, npairs, pair, init)
        stats = lax.cond(qi - 2 * npairs == 1, leftover, lambda st: st, stats)
        consume(None, qi, stats, True, False)

    pl.when(bound <= FIXED_REF_MAX_BOUND)(run_fixed_ref)
    pl.when(bound > FIXED_REF_MAX_BOUND)(run_online)

    lv = lam_ref[...]
    lam = (jnp.exp(jnp.sum(lv[0:1] * lv[1:2], axis=1, keepdims=True))
           - jnp.exp(jnp.sum(lv[2:3] * lv[3:4], axis=1, keepdims=True)) + lambda_init)
    dv = DA_V_DIM
    ot = (acc_sc[0, 0:dv, :] / acc_sc[0, dv:dv + 1, :]
          - lam * (acc_sc[1, 0:dv, :] / acc_sc[1, dv:dv + 1, :]))
    yt = ot * lax.rsqrt(jnp.mean(ot * ot, axis=0, keepdims=True) + EPS)
    o_ref[...] = (yt.T * subln_ref[...] * (1.0 - lambda_init)).astype(BF16)


def _score_bound(gq, gk):
    norm = math.sqrt(DA_HEAD_DIM)
    return (1.01 * norm * norm * DA_HEAD_DIM ** -0.5 * LOG2_E
            * jnp.max(jnp.abs(gq)) * jnp.max(jnp.abs(gk))).astype(F32).reshape(1, 1)


def _diff_attn(z, vt, bound, lam_rows, subln, B, S, tq, lambda_init):
    T = B * S
    nq = S // tq
    kern = functools.partial(_diff_attn_kernel, tq=tq, lambda_init=lambda_init)
    return pl.pallas_call(
        kern,
        grid=(B, DA_HEADS, nq),
        in_specs=[pl.BlockSpec((1, 1), lambda b, h, i: (0, 0), memory_space=pltpu.SMEM),
                  pl.BlockSpec((tq, LANES), lambda b, h, i: (b * nq + i, OFF_QA // LANES + h)),
                  pl.BlockSpec((S, LANES), lambda b, h, i: (b, OFF_KA // LANES + h)),
                  pl.BlockSpec((nq, 1, DA_VT_ROWS, tq), lambda b, h, i: (b, h, 0, 0)),
                  _const_spec((4, DA_HEAD_DIM)), _const_spec((1, DA_V_DIM))],
        out_specs=pl.BlockSpec((tq, LANES), lambda b, h, i: (b * nq + i, h)),
        out_shape=jax.ShapeDtypeStruct((T, DA_HEADS * DA_V_DIM), BF16),
        scratch_shapes=[pltpu.VMEM((2, DA_VT_ROWS, tq), F32), pltpu.VMEM((4, tq, tq // 2), F32)],
        compiler_params=_cparams(("parallel", "parallel", "arbitrary")),
        name="diff_attn",
    )(bound, z, z, vt, lam_rows, subln)


def _retention_kernel(q_ref, k_ref, v_ref, g_ref, dec_ref, qdec_ref, kdec_ref, cdec_ref, o_ref, r_sc, *, tc):
    @pl.when(pl.program_id(1) == 0)
    def _():
        r_sc[...] = jnp.zeros(r_sc.shape, F32)

    C = RET_CHUNK
    lane = lax.broadcasted_iota(jnp.int32, (C, LANES), 1)
    rowi = lax.broadcasted_iota(jnp.int32, (C, LANES), 0)
    for n in range(tc // C):
        rows = slice(n * C, (n + 1) * C)
        for j in range(RET_HEADS // 2):
            cols = slice(j * LANES, (j + 1) * LANES)
            qp = q_ref[rows, cols]
            kp = k_ref[rows, cols]
            qd = (qp.astype(F32) * qdec_ref[:, cols]).astype(BF16)
            kdt = (kp.astype(F32) * kdec_ref[:, cols]).T.astype(BF16)
            r_b = r_sc[j].astype(BF16)
            kv = []
            for hh in range(2):
                h = 2 * j + hh
                mine = (lane >= hh * RET_QK_DIM) & (lane < (hh + 1) * RET_QK_DIM)
                vh = v_ref[rows, h * RET_V_DIM:(h + 1) * RET_V_DIM]
                s = lax.dot_general(jnp.where(mine, qp, jnp.zeros_like(qp)), kp, (((1,), (1,)), ((), ())),
                                    preferred_element_type=F32)
                inner = (s * dec_ref[h]).astype(BF16)
                o = (jnp.dot(inner, vh, preferred_element_type=F32)
                     + jnp.dot(jnp.where(mine, qd, jnp.zeros_like(qd)), r_b, preferred_element_type=F32))
                oc = o - jnp.mean(o, axis=-1, keepdims=True)
                y = oc * lax.rsqrt(jnp.mean(oc * oc, axis=-1, keepdims=True) + EPS)
                gs = slice(h * RET_V_DIM, (h + 1) * RET_V_DIM)
                o_ref[rows, gs] = (y * g_ref[rows, gs].astype(F32)).astype(BF16)
                kv.append(jnp.dot(kdt, vh, preferred_element_type=F32))
            r_sc[j] = r_sc[j] * cdec_ref[j] + jnp.where(rowi < RET_QK_DIM, kv[0], kv[1])


def _retention(z, consts, B, S, tc):
    T = B * S
    n = S // tc
    dec, qdec, kdec, cdec = consts
    kern = functools.partial(_retention_kernel, tc=tc)
    blk = lambda width, off: pl.BlockSpec((tc, width), lambda b, i: (b * n + i, off // width))
    return pl.pallas_call(
        kern,
        grid=(B, n),
        in_specs=[blk(256, OFF_QR), blk(256, OFF_KR), blk(512, OFF_VR), blk(512, OFF_GR),
                  _const_spec(dec.shape), _const_spec(qdec.shape), _const_spec(kdec.shape),
                  _const_spec(cdec.shape)],
        out_specs=pl.BlockSpec((tc, 512), lambda b, i: (b * n + i, 0)),
        out_shape=jax.ShapeDtypeStruct((T, RET_HEADS * RET_V_DIM), BF16),
        scratch_shapes=[pltpu.VMEM((RET_HEADS // 2, LANES, LANES), F32)],
        compiler_params=_cparams(("parallel", "arbitrary")),
        name="retention",
    )(z, z, z, z, dec, qdec, kdec, cdec)


def _retention_consts():
    C = RET_CHUNK
    log_gamma = jnp.log1p(-jnp.exp2(-5.0 - jnp.arange(RET_HEADS, dtype=F32)))
    idx = jnp.arange(C, dtype=F32)
    rel = idx[:, None] - idx[None, :]
    dec = jnp.where(rel[None] >= 0, jnp.exp(log_gamma[:, None, None] * jnp.maximum(rel, 0.0)[None]), 0.0)
    q_dec = jnp.exp(log_gamma[None, :] * (idx + 1.0)[:, None])
    k_dec = jnp.exp(log_gamma[None, :] * (C - 1.0 - idx)[:, None])
    qdec = jnp.repeat(q_dec, RET_QK_DIM, axis=1)
    kdec = jnp.repeat(k_dec, RET_QK_DIM, axis=1)
    chunk_decay = jnp.repeat(jnp.exp(log_gamma * C), RET_QK_DIM)
    cdec = jnp.broadcast_to(chunk_decay.reshape(RET_HEADS // 2, LANES, 1), (RET_HEADS // 2, LANES, LANES))
    return dec.astype(F32), qdec.astype(F32), kdec.astype(F32), cdec.astype(F32)


def _merge_kernel(ya_ref, yr_ref, ga_ref, gb_ref, h_ref, wa_ref, wr_ref, wo_ref, gf_ref, h_out, hn_out):
    pa = jnp.dot(ya_ref[...], wa_ref[...], preferred_element_type=F32)
    pr = jnp.dot(yr_ref[...], wr_ref[...], preferred_element_type=F32)
    merged = ga_ref[...].astype(F32) * pa + gb_ref[...].astype(F32) * pr
    hnew = h_ref[...] + jnp.dot(merged.astype(BF16), wo_ref[...], preferred_element_type=F32)
    h_out[...] = hnew
    ms = jnp.mean(hnew * hnew, axis=-1, keepdims=True)
    hn_out[...] = (hnew * lax.rsqrt(ms + EPS) * gf_ref[...]).astype(BF16)


def _merge(ya, yr, z, h, wa, wr, wo, gf, tm):
    T = h.shape[0]
    row = lambda width, cb=0: pl.BlockSpec((tm, width), lambda i: (i, cb))
    return pl.pallas_call(
        _merge_kernel,
        grid=(T // tm,),
        in_specs=[row(512), row(512), row(D_MODEL, OFF_GA // D_MODEL), row(D_MODEL, OFF_GB // D_MODEL),
                  row(D_MODEL), _const_spec(wa.shape), _const_spec(wr.shape), _const_spec(wo.shape),
                  _const_spec((1, D_MODEL))],
        out_specs=[row(D_MODEL), row(D_MODEL)],
        out_shape=[jax.ShapeDtypeStruct((T, D_MODEL), F32), jax.ShapeDtypeStruct((T, D_MODEL), BF16)],
        compiler_params=_cparams(("parallel",)),
        name="merge_out",
    )(ya, yr, z, z, h, wa, wr, wo, gf)


def _ffn_kernel(hn_ref, prev_ref, w1_ref, w3_ref, w2_ref, o_ref, acc_sc):
    x = hn_ref[...]
    for c in range(D_FF // FF_CHUNK):
        cs = slice(c * FF_CHUNK, (c + 1) * FF_CHUNK)
        u = jnp.dot(x, w1_ref[:, cs], preferred_element_type=F32)
        v = jnp.dot(x, w3_ref[:, cs], preferred_element_type=F32)
        a = (u * _sigmoid(u) * v).astype(BF16)
        d = jnp.dot(a, w2_ref[cs, :], preferred_element_type=F32)
        if c == 0:
            acc_sc[...] = d
        else:
            acc_sc[...] += d
    o_ref[...] = prev_ref[...] + acc_sc[...]


def _ffn(hn, prev, w1, w3, w2, tm):
    T = hn.shape[0]
    row = lambda width: pl.BlockSpec((tm, width), lambda i: (i, 0))
    wspec = lambda w: pl.BlockSpec(w.shape, lambda i: (0, 0), pipeline_mode=pl.Buffered(1))
    return pl.pallas_call(
        _ffn_kernel,
        grid=(T // tm,),
        in_specs=[row(D_MODEL), row(D_MODEL), wspec(w1), wspec(w3), wspec(w2)],
        out_specs=row(D_MODEL),
        out_shape=jax.ShapeDtypeStruct((T, D_MODEL), F32),
        scratch_shapes=[pltpu.VMEM((tm, D_MODEL), F32)],
        input_output_aliases={1: 0},
        compiler_params=_cparams(("parallel",)),
        name="ffn",
    )(hn, prev, w1, w3, w2)


R_I1, R_I2, R_W1, R_W2, R_RANK1, R_RANK2 = range(6)


def _router_kernel(h_ref, g_ref, wr_ref, info_ref, cnt_ref, carry_sc):
    @pl.when(pl.program_id(0) == 0)
    def _():
        carry_sc[...] = jnp.zeros(carry_sc.shape, F32)

    x = h_ref[...]
    ms = jnp.mean(x * x, axis=-1, keepdims=True)
    hn = x * lax.rsqrt(ms + EPS) * g_ref[...]
    logits = jnp.dot(hn, wr_ref[...], preferred_element_type=F32, precision=lax.Precision.HIGHEST)
    lane = lax.broadcasted_iota(jnp.int32, logits.shape, 1)
    logits = jnp.where(lane < N_EXPERTS, logits, -jnp.inf)
    m1 = jnp.max(logits, axis=1, keepdims=True)
    i1 = jnp.min(jnp.where(logits == m1, lane, LANES), axis=1, keepdims=True)
    rest = jnp.where(lane == i1, -jnp.inf, logits)
    m2 = jnp.max(rest, axis=1, keepdims=True)
    i2 = jnp.min(jnp.where(rest == m2, lane, LANES), axis=1, keepdims=True)
    e = jnp.exp(m2 - m1)
    w1 = 1.0 / (1.0 + e)
    w2 = e / (1.0 + e)
    sel = jnp.where(lane == i1, 1.0, jnp.where(lane == i2, 1.0, 0.0))
    tm = sel.shape[0]
    rr = lax.broadcasted_iota(jnp.int32, (tm, tm), 0)
    cc = lax.broadcasted_iota(jnp.int32, (tm, tm), 1)
    ltri = jnp.where(cc < rr, 1.0, 0.0).astype(BF16)
    before = jnp.dot(ltri, sel.astype(BF16), preferred_element_type=F32) + carry_sc[0:1, :]
    r1 = jnp.sum(jnp.where(lane == i1, before, 0.0), axis=1, keepdims=True)
    r2 = jnp.sum(jnp.where(lane == i2, before, 0.0), axis=1, keepdims=True)
    info = jnp.where(lane == R_I1, i1.astype(F32), 0.0)
    for ln, val in ((R_I2, i2.astype(F32)), (R_W1, w1), (R_W2, w2), (R_RANK1, r1), (R_RANK2, r2)):
        info = jnp.where(lane == ln, val, info)
    info_ref[...] = info
    carry_sc[...] = carry_sc[...] + jnp.sum(sel, axis=0, keepdims=True)
    cnt_ref[...] = carry_sc[...]


def _router(h, g, wr_pad, tm):
    T = h.shape[0]
    return pl.pallas_call(
        _router_kernel,
        grid=(T // tm,),
        in_specs=[pl.BlockSpec((tm, D_MODEL), lambda i: (i, 0)), _const_spec((1, D_MODEL)),
                  _const_spec((D_MODEL, LANES))],
        out_specs=[pl.BlockSpec((tm, LANES), lambda i: (i, 0)), _const_spec((8, LANES))],
        out_shape=[jax.ShapeDtypeStruct((T, LANES), F32), jax.ShapeDtypeStruct((8, LANES), F32)],
        scratch_shapes=[pltpu.VMEM((8, LANES), F32)],
        compiler_params=_cparams(("arbitrary",)),
        name="router",
    )(h, g, wr_pad)


def _dispatch_kernel(pos_ref, h_ref, xs_init_hbm, xs_hbm, sem, *, tb):
    del xs_init_hbm

    def issue(t, carry):
        for k in range(2):
            p = pos_ref[0, 0, k * tb + t]
            pltpu.make_async_copy(h_ref.at[pl.ds(t, 1)], xs_hbm.at[pl.ds(p, 1)], sem).start()
        return carry

    lax.fori_loop(0, tb, issue, 0, unroll=DMA_UNROLL)

    def drain(t, carry):
        for k in range(2):
            pltpu.make_async_copy(h_ref.at[pl.ds(t, 1)], xs_hbm.at[pl.ds(0, 1)], sem).wait()
        return carry

    lax.fori_loop(0, tb, drain, 0, unroll=DMA_UNROLL)


def _dispatch(pos_blocks, h, xs_init, tb):
    T = h.shape[0]
    return pl.pallas_call(
        functools.partial(_dispatch_kernel, tb=tb),
        grid=(T // tb,),
        in_specs=[pl.BlockSpec((1, 1, 2 * tb), lambda i: (i, 0, 0), memory_space=pltpu.SMEM),
                  pl.BlockSpec((tb, D_MODEL), lambda i: (i, 0)), pl.BlockSpec(memory_space=pl.ANY)],
        out_specs=pl.BlockSpec(memory_space=pl.ANY),
        out_shape=jax.ShapeDtypeStruct(xs_init.shape, F32),
        scratch_shapes=[pltpu.SemaphoreType.DMA(())],
        input_output_aliases={2: 0},
        compiler_params=pltpu.CompilerParams(dimension_semantics=("arbitrary",), has_side_effects=True),
        name="moe_dispatch",
    )(pos_blocks, h, xs_init)


def _expert_kernel(te_ref, nu_ref, xs_ref, g_ref, w1_ref, w3_ref, w2_ref, y_ref, acc_sc):
    del te_ref
    i = pl.program_id(0)

    @pl.when(i < nu_ref[0])
    def _():
        x = xs_ref[...]
        ms = jnp.mean(x * x, axis=-1, keepdims=True)
        hn = (x * lax.rsqrt(ms + EPS) * g_ref[...]).astype(BF16)
        for c in range(D_FF // FF_CHUNK):
            cs = slice(c * FF_CHUNK, (c + 1) * FF_CHUNK)
            u = jnp.dot(hn, w1_ref[0, :, cs], preferred_element_type=F32)
            v = jnp.dot(hn, w3_ref[0, :, cs], preferred_element_type=F32)
            a = (u * _sigmoid(u) * v).astype(BF16)
            d = jnp.dot(a, w2_ref[0, cs, :], preferred_element_type=F32)
            if c == 0:
                acc_sc[...] = d
            else:
                acc_sc[...] += d
        y_ref[...] = acc_sc[...]

    @pl.when(i >= nu_ref[0])
    def _():
        y_ref[...] = jnp.zeros(y_ref.shape, F32)


def _experts(tile_expert, n_used, xs, g, w1, w3, w2, te_rows):
    P = xs.shape[0]
    row = pl.BlockSpec((te_rows, D_MODEL), lambda i, te, nu: (jnp.minimum(i, nu[0] - 1), 0))
    wspec = lambda w: pl.BlockSpec((1,) + w.shape[1:], lambda i, te, nu: (te[i], 0, 0))
    return pl.pallas_call(
        _expert_kernel,
        grid_spec=pltpu.PrefetchScalarGridSpec(
            num_scalar_prefetch=2,
            grid=(P // te_rows,),
            in_specs=[row, pl.BlockSpec((1, D_MODEL), lambda i, te, nu: (0, 0)), wspec(w1), wspec(w3), wspec(w2)],
            out_specs=pl.BlockSpec((te_rows, D_MODEL), lambda i, te, nu: (i, 0)),
            scratch_shapes=[pltpu.VMEM((te_rows, D_MODEL), F32)]),
        out_shape=jax.ShapeDtypeStruct((P, D_MODEL), F32),
        compiler_params=_cparams(("arbitrary",)),
        name="moe_experts",
    )(tile_expert, n_used, xs, g, w1, w3, w2)


def _combine_kernel(pos_ref, info_ref, h_ref, y_hbm, o_ref, g_sc, sem, *, tb):
    def issue(t, carry):
        for k in range(2):
            p = pos_ref[0, 0, k * tb + t]
            pltpu.make_async_copy(y_hbm.at[pl.ds(p, 1)], g_sc.at[k, pl.ds(t, 1)], sem).start()
        return carry

    lax.fori_loop(0, tb, issue, 0, unroll=DMA_UNROLL)

    def drain(t, carry):
        for k in range(2):
            pltpu.make_async_copy(y_hbm.at[pl.ds(0, 1)], g_sc.at[k, pl.ds(t, 1)], sem).wait()
        return carry

    lax.fori_loop(0, tb, drain, 0, unroll=DMA_UNROLL)
    info = info_ref[...]
    o_ref[...] = h_ref[...] + info[:, R_W1:R_W1 + 1] * g_sc[0] + info[:, R_W2:R_W2 + 1] * g_sc[1]


def _combine(pos_blocks, info, h, y, tb):
    T = h.shape[0]
    row = lambda width: pl.BlockSpec((tb, width), lambda i: (i, 0))
    return pl.pallas_call(
        functools.partial(_combine_kernel, tb=tb),
        grid=(T // tb,),
        in_specs=[pl.BlockSpec((1, 1, 2 * tb), lambda i: (i, 0, 0), memory_space=pltpu.SMEM),
                  row(LANES), row(D_MODEL), pl.BlockSpec(memory_space=pl.ANY)],
        out_specs=row(D_MODEL),
        out_shape=jax.ShapeDtypeStruct((T, D_MODEL), F32),
        scratch_shapes=[pltpu.VMEM((2, tb, D_MODEL), F32), pltpu.SemaphoreType.DMA(())],
        compiler_params=_cparams(("arbitrary",)),
        name="moe_combine",
    )(pos_blocks, info, h, y)


def _pos_blocks(pos1, pos2, tb):
    nb = pos1.shape[0] // tb
    return jnp.concatenate([pos1.reshape(nb, 1, tb), pos2.reshape(nb, 1, tb)], axis=2)


def _moe(h, g, w_router, w1, w3, w2, tm):
    T = h.shape[0]
    te_rows = min(512, T)
    wr_pad = jnp.pad(w_router, ((0, 0), (0, LANES - N_EXPERTS)))
    info, cnt = _router(h, g, wr_pad, tm)
    counts = cnt[0, :N_EXPERTS].astype(jnp.int32)
    padded = ((counts + te_rows - 1) // te_rows) * te_rows
    ends = jnp.cumsum(padded)
    base = ends - padded
    i1 = info[:, R_I1].astype(jnp.int32)
    i2 = info[:, R_I2].astype(jnp.int32)
    pos1 = jnp.take(base, i1) + info[:, R_RANK1].astype(jnp.int32)
    pos2 = jnp.take(base, i2) + info[:, R_RANK2].astype(jnp.int32)
    n_tiles = (2 * T) // te_rows + N_EXPERTS
    tile_expert = jnp.minimum(jnp.searchsorted(ends // te_rows, jnp.arange(n_tiles), side="right"),
                              N_EXPERTS - 1).astype(jnp.int32)
    n_used = (ends[-1:] // te_rows).astype(jnp.int32)

    tb_d = min(1024, T)
    xs = _dispatch(_pos_blocks(pos1, pos2, tb_d), h, jnp.zeros((n_tiles * te_rows, D_MODEL), F32), tb_d)
    y = _experts(tile_expert, n_used, xs, g, w1, w3, w2, te_rows)
    tb_c = min(256, T)
    return _combine(_pos_blocks(pos1, pos2, tb_c), info, h, y, tb_c)


def kernel(x, positions, attn_norm, w_in, da_q_norm, da_k_norm, lam_q1, lam_k1, lam_q2, lam_k2, da_subln,
           w_proj_da, w_proj_ret, w_out, ffn_norm, dense_w1, dense_w3, dense_w2, w_router, moe_w1, moe_w3,
           moe_w2):
    B, S, D = x.shape
    assert D == D_MODEL and S % RET_CHUNK == 0
    T = B * S
    depth = attn_norm.shape[0]
    tm = min(512, S)
    tq = min(512, S)
    tc = min(512, S)

    pos_b = jnp.broadcast_to(positions.reshape(T, 1).astype(F32), (T, LANES))
    tabs = (_rope_tables(pos_b, *_rope_rows(ROPE_ROT_DIM, ROPE_THETA), tm)
            + _rope_tables(pos_b, *_rope_rows(RET_QK_DIM, RET_ROPE_THETA), tm))
    ret_consts = _retention_consts()
    gsum = jnp.asarray(np.kron(np.eye(4), np.ones((64, 64))), BF16)

    h = x.reshape(T, D)
    for layer in range(depth):
        lambda_init = 0.8 - 0.6 * math.exp(-0.3 * layer)
        gq = jnp.tile(da_q_norm[layer], PROJ_CHUNK // DA_HEAD_DIM).reshape(1, PROJ_CHUNK)
        gk = jnp.tile(da_k_norm[layer], PROJ_CHUNK // DA_HEAD_DIM).reshape(1, PROJ_CHUNK)
        z, vt = _in_proj(h, attn_norm[layer].reshape(1, D), w_in[layer].astype(BF16), gq, gk, gsum, tabs, tm)
        lam_rows = jnp.stack([lam_q1[layer], lam_k1[layer], lam_q2[layer], lam_k2[layer]])
        ya = _diff_attn(z, vt, _score_bound(da_q_norm[layer], da_k_norm[layer]), lam_rows,
                        da_subln[layer].reshape(1, DA_V_DIM), B, S, tq, lambda_init)
        yr = _retention(z, ret_consts, B, S, tc)
        h, hn = _merge(ya, yr, z, h, w_proj_da[layer].astype(BF16), w_proj_ret[layer].astype(BF16),
                       w_out[layer].astype(BF16), ffn_norm[layer].reshape(1, D), tm)
        j = layer // 2
        if layer % 2 == 0:
            h = _ffn(hn, h, dense_w1[j].astype(BF16), dense_w3[j].astype(BF16), dense_w2[j].astype(BF16), tm)
        else:
            h = _moe(h, ffn_norm[layer].reshape(1, D), w_router[j], moe_w1[j].astype(BF16),
                     moe_w3[j].astype(BF16), moe_w2[j].astype(BF16), tm)
    return h.reshape(B, S, D)
```

```python
import functools
import math

import jax
import jax.numpy as jnp
import numpy as np
from jax import lax
from jax.experimental import pallas as pl
from jax.experimental.pallas import tpu as pltpu

F32 = jnp.float32
BF16 = jnp.bfloat16

D_MODEL = 1024
DA_HEADS = 4
DA_HEAD_DIM = 64
DA_V_DIM = 128
ROPE_THETA = 500000.0
ROPE_ROT_DIM = 16
RET_HEADS = 4
RET_QK_DIM = 64
RET_V_DIM = 128
RET_CHUNK = 128
RET_ROPE_THETA = 10000.0
D_FF = 2816
N_EXPERTS = 8
EPS = 1e-6
LOG2_E = math.log2(math.e)
DA_VT_ROWS = DA_V_DIM + 16
FIXED_REF_MAX_BOUND = 60.0
GROUP = 4

LANES = 128
OFF_QA, OFF_KA, OFF_VA = 0, 512, 1024
OFF_QR, OFF_KR, OFF_VR, OFF_GR = 1536, 1792, 2048, 2560
OFF_GA, OFF_GB = 3072, 4096
IN_W = 5120
PROJ_CHUNK = 512
FF_CHUNK = 256
VMEM_LIMIT = 56 * 1024 * 1024
DMA_UNROLL = 8


def _cparams(sem):
    return pltpu.CompilerParams(dimension_semantics=sem, vmem_limit_bytes=VMEM_LIMIT)


def _const_spec(shape):
    nd = len(shape)
    return pl.BlockSpec(shape, lambda *_: (0,) * nd)


def _rope_table_kernel(pos_ref, inv_ref, sgn_ref, cos_ref, sin_ref):
    ang = pos_ref[...] * inv_ref[...]
    cos_ref[...] = jnp.cos(ang)
    sin_ref[...] = jnp.sin(ang) * sgn_ref[...]


def _rope_tables(pos_b, inv_row, sgn_row, tm):
    T = pos_b.shape[0]
    row = pl.BlockSpec((tm, LANES), lambda i: (i, 0))
    return pl.pallas_call(
        _rope_table_kernel,
        grid=(T // tm,),
        in_specs=[row, _const_spec((1, LANES)), _const_spec((1, LANES))],
        out_specs=[row, row],
        out_shape=[jax.ShapeDtypeStruct((T, LANES), F32)] * 2,
        compiler_params=_cparams(("parallel",)),
        name="rope_tables",
    )(pos_b, inv_row, sgn_row)


def _rope_rows(rot_dim, theta):
    half = rot_dim // 2
    inv = 1.0 / (theta ** (jnp.arange(half, dtype=F32) / half))
    jj = np.arange(LANES) % 64
    inv_row = jnp.where(jj < rot_dim, inv[jj % half], 0.0).astype(F32)
    sgn_row = np.where(jj < half, -1.0, np.where(jj < rot_dim, 1.0, 0.0)).astype(np.float32)
    return inv_row.reshape(1, LANES), jnp.asarray(sgn_row).reshape(1, LANES)


def _rotate_half(x, half):
    jj = lax.broadcasted_iota(jnp.int32, x.shape, 1) % 64
    return jnp.where(jj < half, pltpu.roll(x, LANES - half, 1), pltpu.roll(x, half, 1))


def _sigmoid(x):
    return 1.0 / (1.0 + jnp.exp(-x))


def _in_proj_kernel(h_ref, g_ref, w_ref, gq_ref, gk_ref, gsum_ref, ca_ref, sa_ref, cr_ref, sr_ref, z_ref, vt_ref):
    x = h_ref[...]
    ms = jnp.mean(x * x, axis=-1, keepdims=True)
    hn = (x * lax.rsqrt(ms + EPS) * g_ref[...]).astype(BF16)

    def head_rms(zc, gain):
        sq = (zc * zc).astype(BF16)
        outs = []
        for s in range(PROJ_CHUNK // 256):
            sl = slice(s * 256, (s + 1) * 256)
            tot = jnp.dot(sq[:, sl], gsum_ref[...], preferred_element_type=F32)
            outs.append(zc[:, sl] * lax.rsqrt(tot * (1.0 / 64.0) + EPS))
        return jnp.concatenate(outs, axis=1) * gain

    def rope(y, cos, sin, half):
        outs = []
        for s in range(y.shape[1] // LANES):
            ys = y[:, s * LANES:(s + 1) * LANES]
            outs.append(ys * cos + _rotate_half(ys, half) * sin)
        return jnp.concatenate(outs, axis=1)

    for c in range(IN_W // PROJ_CHUNK):
        c0 = c * PROJ_CHUNK
        zc = jnp.dot(hn, w_ref[:, c0:c0 + PROJ_CHUNK], preferred_element_type=F32)
        if c0 == OFF_QA:
            zc = rope(head_rms(zc, gq_ref[...]), ca_ref[...], sa_ref[...], ROPE_ROT_DIM // 2)
            zc = zc * (DA_HEAD_DIM ** -0.5 * LOG2_E)
        elif c0 == OFF_KA:
            zc = rope(head_rms(zc, gk_ref[...]), ca_ref[...], sa_ref[...], ROPE_ROT_DIM // 2)
        elif c0 == OFF_VA:
            zt = zc.T.astype(BF16)
            extra = lax.broadcasted_iota(jnp.int32, (DA_VT_ROWS - DA_V_DIM, zt.shape[1]), 0)
            for hd in range(DA_HEADS):
                vt_ref[0, hd, 0:DA_V_DIM, :] = zt[hd * DA_V_DIM:(hd + 1) * DA_V_DIM, :]
                vt_ref[0, hd, DA_V_DIM:DA_VT_ROWS, :] = jnp.where(extra == 0, 1.0, 0.0).astype(BF16)
        elif c0 == OFF_QR:
            zc = rope(zc, cr_ref[...], sr_ref[...], RET_QK_DIM // 2)
            col = lax.broadcasted_iota(jnp.int32, zc.shape, 1)
            zc = jnp.where(col >= OFF_KR - OFF_QR, zc * (RET_QK_DIM ** -0.5), zc)
        elif c0 == OFF_GR:
            zc = zc * _sigmoid(zc)
        elif c0 >= OFF_GA:
            zc = _sigmoid(zc)
        z_ref[:, c0:c0 + PROJ_CHUNK] = zc.astype(BF16)


def _in_proj(h, g, w, gq, gk, gsum, tabs, tm):
    T = h.shape[0]
    row = lambda width: pl.BlockSpec((tm, width), lambda i: (i, 0))
    return pl.pallas_call(
        _in_proj_kernel,
        grid=(T // tm,),
        in_specs=[row(D_MODEL), _const_spec((1, D_MODEL)), _const_spec((D_MODEL, IN_W)),
                  _const_spec((1, PROJ_CHUNK)), _const_spec((1, PROJ_CHUNK)), _const_spec((256, 256)),
                  row(LANES), row(LANES), row(LANES), row(LANES)],
        out_specs=[row(IN_W), pl.BlockSpec((1, DA_HEADS, DA_VT_ROWS, tm), lambda i: (i, 0, 0, 0))],
        out_shape=[jax.ShapeDtypeStruct((T, IN_W), BF16),
                   jax.ShapeDtypeStruct((T // tm, DA_HEADS, DA_VT_ROWS, tm), BF16)],
        compiler_params=_cparams(("parallel",)),
        name="in_proj",
    )(h, g, w, gq, gk, gsum, *tabs)


def _diff_attn_kernel(bound_ref, q_ref, k_ref, vt_ref, lam_ref, subln_ref, o_ref, acc_sc, st_sc, *, tq,
                      lambda_init):
    qi = pl.program_id(2)
    q = q_ref[...]
    lane = lax.broadcasted_iota(jnp.int32, q.shape, 1)
    qs = (jnp.where(lane < DA_HEAD_DIM, q, jnp.zeros_like(q)),
          jnp.where(lane >= DA_HEAD_DIM, q, jnp.zeros_like(q)))
    acc_sc[...] = jnp.zeros(acc_sc.shape, F32)

    tqh = tq // 2
    streams = [(c, hf) for hf in range(2) for c in range(2)]
    qsub = [qs[c][hf * tqh:(hf + 1) * tqh, :] for c, hf in streams]

    def scores(ki):
        k = k_ref[pl.ds(pl.multiple_of(ki * tq, tq), tq), :]
        return [lax.dot_general(k, qq, (((1,), (1,)), ((), ())), preferred_element_type=F32) for qq in qsub]

    def stash(sts):
        for s in range(len(streams)):
            st_sc[s] = sts[s]

    bound = bound_ref[0, 0]

    def consume(sts, ki, stats, masked, fixed_ref):
        vt = vt_ref[ki, 0]
        out = []
        for s, (c, hf) in enumerate(streams):
            st = st_sc[s] if sts is None else sts[s]
            if masked:
                key = lax.broadcasted_iota(jnp.int32, st.shape, 0)
                qry = lax.broadcasted_iota(jnp.int32, st.shape, 1) + hf * tqh
                st = jnp.where(key <= qry, st, -jnp.inf)
            cols = slice(hf * tqh, (hf + 1) * tqh)
            if fixed_ref:
                pt = jnp.exp2(st - bound).astype(BF16)
                acc_sc[c, :, cols] += jnp.dot(vt, pt, preferred_element_type=F32)
                out.append(stats[s])
            else:
                m_old = stats[s]
                m_new = jnp.maximum(m_old, jnp.max(st, axis=0, keepdims=True))
                alpha = jnp.exp2(m_old - m_new)
                pt = jnp.exp2((st - m_new).astype(BF16))
                acc_sc[c, :, cols] = alpha * acc_sc[c, :, cols] + jnp.dot(vt, pt, preferred_element_type=F32)
                out.append(m_new)
        return tuple(out)

    def run_fixed_ref():
        none = (None,) * len(streams)

        def group(j, carry):
            blocks = [GROUP * j + u for u in range(GROUP)]
            sts = [scores(b) for b in blocks]
            for b, st in zip(blocks, sts):
                consume(st, b, none, False, True)
            return carry

        def tail(rest):
            def body():
                blocks = [qi - rest + u for u in range(rest + 1)]
                sts = [scores(b) for b in blocks]
                for u, (b, st) in enumerate(zip(blocks, sts)):
                    consume(st, b, none, u == rest, True)
            return body

        ngroups = qi // GROUP
        lax.fori_loop(0, ngroups, group, 0)
        for rest in range(GROUP):
            pl.when(qi - GROUP * ngroups == rest)(tail(rest))

    def run_online():
        npairs = qi // 2
        stash(scores(0))

        def pair(j, stats):
            b0 = 2 * j
            odd = scores(b0 + 1)
            stats = consume(None, b0, stats, False, False)
            stash(scores(b0 + 2))
            return consume(odd, b0 + 1, stats, False, False)

        def leftover(stats):
            stats = consume(None, qi - 1, stats, False, False)
            stash(scores(qi))
            return stats

        init = (jnp.full((1, tqh), -jnp.inf, F32),) * len(streams)
        stats = lax.fori_loop(0, npairs, pair, init)
        stats = lax.cond(qi - 2 * npairs == 1, leftover, lambda st: st, stats)
        consume(None, qi, stats, True, False)

    pl.when(bound <= FIXED_REF_MAX_BOUND)(run_fixed_ref)
    pl.when(bound > FIXED_REF_MAX_BOUND)(run_online)

    lv = lam_ref[...]
    lam = (jnp.exp(jnp.sum(lv[0:1] * lv[1:2], axis=1, keepdims=True))
           - jnp.exp(jnp.sum(lv[2:3] * lv[3:4], axis=1, keepdims=True)) + lambda_init)
    dv = DA_V_DIM
    ot = (acc_sc[0, 0:dv, :] / acc_sc[0, dv:dv + 1, :]
          - lam * (acc_sc[1, 0:dv, :] / acc_sc[1, dv:dv + 1, :]))
    yt = ot * lax.rsqrt(jnp.mean(ot * ot, axis=0, keepdims=True) + EPS)
    o_ref[...] = (yt.T * subln_ref[...] * (1.0 - lambda_init)).astype(BF16)


def _score_bound(gq, gk):
    norm = math.sqrt(DA_HEAD_DIM)
    return (1.01 * norm * norm * DA_HEAD_DIM ** -0.5 * LOG2_E
            * jnp.max(jnp.abs(gq)) * jnp.max(jnp.abs(gk))).astype(F32).reshape(1, 1)


def _diff_attn(z, vt, bound, lam_rows, subln, B, S, tq, lambda_init):
    T = B * S
    nq = S // tq
    kern = functools.partial(_diff_attn_kernel, tq=tq, lambda_init=lambda_init)
    return pl.pallas_call(
        kern,
        grid=(B, DA_HEADS, nq),
        in_specs=[pl.BlockSpec((1, 1), lambda b, h, i: (0, 0), memory_space=pltpu.SMEM),
                  pl.BlockSpec((tq, LANES), lambda b, h, i: (b * nq + i, OFF_QA // LANES + h)),
                  pl.BlockSpec((S, LANES), lambda b, h, i: (b, OFF_KA // LANES + h)),
                  pl.BlockSpec((nq, 1, DA_VT_ROWS, tq), lambda b, h, i: (b, h, 0, 0)),
                  _const_spec((4, DA_HEAD_DIM)), _const_spec((1, DA_V_DIM))],
        out_specs=pl.BlockSpec((tq, LANES), lambda b, h, i: (b * nq + i, h)),
        out_shape=jax.ShapeDtypeStruct((T, DA_HEADS * DA_V_DIM), BF16),
        scratch_shapes=[pltpu.VMEM((2, DA_VT_ROWS, tq), F32), pltpu.VMEM((4, tq, tq // 2), F32)],
        compiler_params=_cparams(("parallel", "parallel", "arbitrary")),
        name="diff_attn",
    )(bound, z, z, vt, lam_rows, subln)


def _retention_kernel(q_ref, k_ref, v_ref, g_ref, dec_ref, qdec_ref, kdec_ref, cdec_ref, o_ref, r_sc, *, tc):
    @pl.when(pl.program_id(1) == 0)
    def _():
        r_sc[...] = jnp.zeros(r_sc.shape, F32)

    C = RET_CHUNK
    lane = lax.broadcasted_iota(jnp.int32, (C, LANES), 1)
    rowi = lax.broadcasted_iota(jnp.int32, (C, LANES), 0)
    state = [r_sc[j] for j in range(RET_HEADS // 2)]
    for n in range(tc // C):
        rows = slice(n * C, (n + 1) * C)
        for j in range(RET_HEADS // 2):
            cols = slice(j * LANES, (j + 1) * LANES)
            qp = q_ref[rows, cols]
            kp = k_ref[rows, cols]
            qd = (qp.astype(F32) * qdec_ref[:, cols]).astype(BF16)
            kdt = (kp.astype(F32) * kdec_ref[:, cols]).T.astype(BF16)
            r_b = state[j].astype(BF16)
            kv = []
            for hh in range(2):
                h = 2 * j + hh
                mine = (lane >= RET_QK_DIM) if hh else (lane < RET_QK_DIM)
                vh = v_ref[rows, h * RET_V_DIM:(h + 1) * RET_V_DIM]
                s = lax.dot_general(jnp.where(mine, qp, jnp.zeros_like(qp)), kp, (((1,), (1,)), ((), ())),
                                    preferred_element_type=F32)
                inner = (s * dec_ref[h]).astype(BF16)
                o = (jnp.dot(inner, vh, preferred_element_type=F32)
                     + jnp.dot(jnp.where(mine, qd, jnp.zeros_like(qd)), r_b, preferred_element_type=F32))
                oc = o - jnp.mean(o, axis=-1, keepdims=True)
                y = oc * lax.rsqrt(jnp.mean(oc * oc, axis=-1, keepdims=True) + EPS)
                gs = slice(h * RET_V_DIM, (h + 1) * RET_V_DIM)
                o_ref[rows, gs] = (y * g_ref[rows, gs].astype(F32)).astype(BF16)
                kv.append(jnp.dot(kdt, vh, preferred_element_type=F32))
            state[j] = state[j] * cdec_ref[j] + jnp.where(rowi < RET_QK_DIM, kv[0], kv[1])
    for j in range(RET_HEADS // 2):
        r_sc[j] = state[j]


def _retention(z, consts, B, S, tc):
    T = B * S
    n = S // tc
    dec, qdec, kdec, cdec = consts
    kern = functools.partial(_retention_kernel, tc=tc)
    blk = lambda width, off: pl.BlockSpec((tc, width), lambda b, i: (b * n + i, off // width))
    return pl.pallas_call(
        kern,
        grid=(B, n),
        in_specs=[blk(256, OFF_QR), blk(256, OFF_KR), blk(512, OFF_VR), blk(512, OFF_GR),
                  _const_spec(dec.shape), _const_spec(qdec.shape), _const_spec(kdec.shape),
                  _const_spec(cdec.shape)],
        out_specs=pl.BlockSpec((tc, 512), lambda b, i: (b * n + i, 0)),
        out_shape=jax.ShapeDtypeStruct((T, RET_HEADS * RET_V_DIM), BF16),
        scratch_shapes=[pltpu.VMEM((RET_HEADS // 2, LANES, LANES), F32)],
        compiler_params=_cparams(("parallel", "arbitrary")),
        name="retention",
    )(z, z, z, z, dec, qdec, kdec, cdec)


def _retention_consts():
    C = RET_CHUNK
    log_gamma = jnp.log1p(-jnp.exp2(-5.0 - jnp.arange(RET_HEADS, dtype=F32)))
    idx = jnp.arange(C, dtype=F32)
    rel = idx[:, None] - idx[None, :]
    dec = jnp.where(rel[None] >= 0, jnp.exp(log_gamma[:, None, None] * jnp.maximum(rel, 0.0)[None]), 0.0)
    q_dec = jnp.exp(log_gamma[None, :] * (idx + 1.0)[:, None])
    k_dec = jnp.exp(log_gamma[None, :] * (C - 1.0 - idx)[:, None])
    qdec = jnp.repeat(q_dec, RET_QK_DIM, axis=1)
    kdec = jnp.repeat(k_dec, RET_QK_DIM, axis=1)
    chunk_decay = jnp.repeat(jnp.exp(log_gamma * C), RET_QK_DIM)
    cdec = jnp.broadcast_to(chunk_decay.reshape(RET_HEADS // 2, LANES, 1), (RET_HEADS // 2, LANES, LANES))
    return dec.astype(F32), qdec.astype(F32), kdec.astype(F32), cdec.astype(F32)


def _merge_kernel(ya_ref, yr_ref, ga_ref, gb_ref, h_ref, wa_ref, wr_ref, wo_ref, gf_ref, h_out, hn_out):
    pa = jnp.dot(ya_ref[...], wa_ref[...], preferred_element_type=F32)
    pr = jnp.dot(yr_ref[...], wr_ref[...], preferred_element_type=F32)
    merged = ga_ref[...].astype(F32) * pa + gb_ref[...].astype(F32) * pr
    hnew = h_ref[...] + jnp.dot(merged.astype(BF16), wo_ref[...], preferred_element_type=F32)
    h_out[...] = hnew
    ms = jnp.mean(hnew * hnew, axis=-1, keepdims=True)
    hn_out[...] = (hnew * lax.rsqrt(ms + EPS) * gf_ref[...]).astype(BF16)


def _merge(ya, yr, z, h, wa, wr, wo, gf, tm):
    T = h.shape[0]
    row = lambda width, cb=0: pl.BlockSpec((tm, width), lambda i: (i, cb))
    return pl.pallas_call(
        _merge_kernel,
        grid=(T // tm,),
        in_specs=[row(512), row(512), row(D_MODEL, OFF_GA // D_MODEL), row(D_MODEL, OFF_GB // D_MODEL),
                  row(D_MODEL), _const_spec(wa.shape), _const_spec(wr.shape), _const_spec(wo.shape),
                  _const_spec((1, D_MODEL))],
        out_specs=[row(D_MODEL), row(D_MODEL)],
        out_shape=[jax.ShapeDtypeStruct((T, D_MODEL), F32), jax.ShapeDtypeStruct((T, D_MODEL), BF16)],
        compiler_params=_cparams(("parallel",)),
        name="merge_out",
    )(ya, yr, z, z, h, wa, wr, wo, gf)


def _ffn_kernel(hn_ref, prev_ref, w1_ref, w3_ref, w2_ref, o_ref, acc_sc):
    x = hn_ref[...]
    for c in range(D_FF // FF_CHUNK):
        cs = slice(c * FF_CHUNK, (c + 1) * FF_CHUNK)
        u = jnp.dot(x, w1_ref[:, cs], preferred_element_type=F32)
        v = jnp.dot(x, w3_ref[:, cs], preferred_element_type=F32)
        a = (u * _sigmoid(u) * v).astype(BF16)
        d = jnp.dot(a, w2_ref[cs, :], preferred_element_type=F32)
        if c == 0:
            acc_sc[...] = d
        else:
            acc_sc[...] += d
    o_ref[...] = prev_ref[...] + acc_sc[...]


def _ffn(hn, prev, w1, w3, w2, tm):
    T = hn.shape[0]
    row = lambda width: pl.BlockSpec((tm, width), lambda i: (i, 0))
    wspec = lambda w: pl.BlockSpec(w.shape, lambda i: (0, 0), pipeline_mode=pl.Buffered(1))
    return pl.pallas_call(
        _ffn_kernel,
        grid=(T // tm,),
        in_specs=[row(D_MODEL), row(D_MODEL), wspec(w1), wspec(w3), wspec(w2)],
        out_specs=row(D_MODEL),
        out_shape=jax.ShapeDtypeStruct((T, D_MODEL), F32),
        scratch_shapes=[pltpu.VMEM((tm, D_MODEL), F32)],
        input_output_aliases={1: 0},
        compiler_params=_cparams(("parallel",)),
        name="ffn",
    )(hn, prev, w1, w3, w2)


R_I1, R_I2, R_W1, R_W2, R_RANK1, R_RANK2 = range(6)


def _router_kernel(h_ref, g_ref, wr_ref, info_ref, cnt_ref, carry_sc):
    @pl.when(pl.program_id(0) == 0)
    def _():
        carry_sc[...] = jnp.zeros(carry_sc.shape, F32)

    x = h_ref[...]
    ms = jnp.mean(x * x, axis=-1, keepdims=True)
    hn = x * lax.rsqrt(ms + EPS) * g_ref[...]
    w = wr_ref[...]
    hn_hi, w_hi = hn.astype(BF16), w.astype(BF16)
    hn_lo, w_lo = (hn - hn_hi.astype(F32)).astype(BF16), (w - w_hi.astype(F32)).astype(BF16)
    logits = (jnp.dot(hn_hi, w_hi, preferred_element_type=F32) + jnp.dot(hn_hi, w_lo, preferred_element_type=F32)
              + jnp.dot(hn_lo, w_hi, preferred_element_type=F32))
    lane = lax.broadcasted_iota(jnp.int32, logits.shape, 1)
    logits = jnp.where(lane < N_EXPERTS, logits, -jnp.inf)
    m1 = jnp.max(logits, axis=1, keepdims=True)
    i1 = jnp.min(jnp.where(logits == m1, lane, LANES), axis=1, keepdims=True)
    rest = jnp.where(lane == i1, -jnp.inf, logits)
    m2 = jnp.max(rest, axis=1, keepdims=True)
    i2 = jnp.min(jnp.where(rest == m2, lane, LANES), axis=1, keepdims=True)
    e = jnp.exp(m2 - m1)
    w1 = 1.0 / (1.0 + e)
    w2 = e / (1.0 + e)
    sel = jnp.where(lane == i1, 1.0, jnp.where(lane == i2, 1.0, 0.0))
    tm = sel.shape[0]
    rr = lax.broadcasted_iota(jnp.int32, (tm, tm), 0)
    cc = lax.broadcasted_iota(jnp.int32, (tm, tm), 1)
    ltri = jnp.where(cc < rr, 1.0, 0.0).astype(BF16)
    before = jnp.dot(ltri, sel.astype(BF16), preferred_element_type=F32) + carry_sc[0:1, :]
    r1 = jnp.sum(jnp.where(lane == i1, before, 0.0), axis=1, keepdims=True)
    r2 = jnp.sum(jnp.where(lane == i2, before, 0.0), axis=1, keepdims=True)
    info = jnp.where(lane == R_I1, i1.astype(F32), 0.0)
    for ln, val in ((R_I2, i2.astype(F32)), (R_W1, w1), (R_W2, w2), (R_RANK1, r1), (R_RANK2, r2)):
        info = jnp.where(lane == ln, val, info)
    info_ref[...] = info
    carry_sc[...] = carry_sc[...] + jnp.sum(sel, axis=0, keepdims=True)
    cnt_ref[...] = carry_sc[...]


def _router(h, g, wr_pad, tm):
    T = h.shape[0]
    return pl.pallas_call(
        _router_kernel,
        grid=(T // tm,),
        in_specs=[pl.BlockSpec((tm, D_MODEL), lambda i: (i, 0)), _const_spec((1, D_MODEL)),
                  _const_spec((D_MODEL, LANES))],
        out_specs=[pl.BlockSpec((tm, LANES), lambda i: (i, 0)), _const_spec((8, LANES))],
        out_shape=[jax.ShapeDtypeStruct((T, LANES), F32), jax.ShapeDtypeStruct((8, LANES), F32)],
        scratch_shapes=[pltpu.VMEM((8, LANES), F32)],
        compiler_params=_cparams(("arbitrary",)),
        name="router",
    )(h, g, wr_pad)


def _dispatch_kernel(pos_ref, h_ref, xs_init_hbm, xs_hbm, sem, *, tb):
    del xs_init_hbm

    def issue(t, carry):
        for k in range(2):
            p = pos_ref[0, 0, k * tb + t]
            pltpu.make_async_copy(h_ref.at[pl.ds(t, 1)], xs_hbm.at[pl.ds(p, 1)], sem).start()
        return carry

    lax.fori_loop(0, tb, issue, 0, unroll=DMA_UNROLL)

    for k in range(2):
        pltpu.make_async_copy(h_ref, xs_hbm.at[pl.ds(0, tb)], sem).wait()


def _dispatch(pos_blocks, h, xs_init, tb):
    T = h.shape[0]
    return pl.pallas_call(
        functools.partial(_dispatch_kernel, tb=tb),
        grid=(T // tb,),
        in_specs=[pl.BlockSpec((1, 1, 2 * tb), lambda i: (i, 0, 0), memory_space=pltpu.SMEM),
                  pl.BlockSpec((tb, D_MODEL), lambda i: (i, 0)), pl.BlockSpec(memory_space=pl.ANY)],
        out_specs=pl.BlockSpec(memory_space=pl.ANY),
        out_shape=jax.ShapeDtypeStruct(xs_init.shape, F32),
        scratch_shapes=[pltpu.SemaphoreType.DMA(())],
        input_output_aliases={2: 0},
        compiler_params=pltpu.CompilerParams(dimension_semantics=("arbitrary",), has_side_effects=True),
        name="moe_dispatch",
    )(pos_blocks, h, xs_init)


def _expert_kernel(te_ref, nu_ref, xs_ref, g_ref, w1_ref, w3_ref, w2_ref, y_ref, acc_sc):
    del te_ref
    i = pl.program_id(0)

    @pl.when(i < nu_ref[0])
    def _():
        x = xs_ref[...]
        ms = jnp.mean(x * x, axis=-1, keepdims=True)
        hn = (x * lax.rsqrt(ms + EPS) * g_ref[...]).astype(BF16)
        for c in range(D_FF // FF_CHUNK):
            cs = slice(c * FF_CHUNK, (c + 1) * FF_CHUNK)
            u = jnp.dot(hn, w1_ref[0, :, cs], preferred_element_type=F32)
            v = jnp.dot(hn, w3_ref[0, :, cs], preferred_element_type=F32)
            a = (u * _sigmoid(u) * v).astype(BF16)
            d = jnp.dot(a, w2_ref[0, cs, :], preferred_element_type=F32)
            if c == 0:
                acc_sc[...] = d
            else:
                acc_sc[...] += d
        y_ref[...] = acc_sc[...]

    @pl.when(i >= nu_ref[0])
    def _():
        y_ref[...] = jnp.zeros(y_ref.shape, F32)


def _experts(tile_expert, n_used, xs, g, w1, w3, w2, te_rows):
    P = xs.shape[0]
    row = pl.BlockSpec((te_rows, D_MODEL), lambda i, te, nu: (jnp.minimum(i, nu[0] - 1), 0))
    wspec = lambda w: pl.BlockSpec((1,) + w.shape[1:], lambda i, te, nu: (te[i], 0, 0))
    return pl.pallas_call(
        _expert_kernel,
        grid_spec=pltpu.PrefetchScalarGridSpec(
            num_scalar_prefetch=2,
            grid=(P // te_rows,),
            in_specs=[row, pl.BlockSpec((1, D_MODEL), lambda i, te, nu: (0, 0)), wspec(w1), wspec(w3), wspec(w2)],
            out_specs=pl.BlockSpec((te_rows, D_MODEL), lambda i, te, nu: (i, 0)),
            scratch_shapes=[pltpu.VMEM((te_rows, D_MODEL), F32)]),
        out_shape=jax.ShapeDtypeStruct((P, D_MODEL), F32),
        compiler_params=_cparams(("arbitrary",)),
        name="moe_experts",
    )(tile_expert, n_used, xs, g, w1, w3, w2)


def _combine_kernel(pos_ref, info_ref, h_ref, y_hbm, o_ref, g_sc, sem, *, tb):
    def issue(t, carry):
        for k in range(2):
            p = pos_ref[0, 0, k * tb + t]
            pltpu.make_async_copy(y_hbm.at[pl.ds(p, 1)], g_sc.at[k, pl.ds(t, 1)], sem).start()
        return carry

    lax.fori_loop(0, tb, issue, 0, unroll=DMA_UNROLL)

    for k in range(2):
        pltpu.make_async_copy(y_hbm.at[pl.ds(0, tb)], g_sc.at[k], sem).wait()
    info = info_ref[...]
    o_ref[...] = h_ref[...] + info[:, R_W1:R_W1 + 1] * g_sc[0] + info[:, R_W2:R_W2 + 1] * g_sc[1]


def _combine(pos_blocks, info, h, y, tb):
    T = h.shape[0]
    row = lambda width: pl.BlockSpec((tb, width), lambda i: (i, 0))
    return pl.pallas_call(
        functools.partial(_combine_kernel, tb=tb),
        grid=(T // tb,),
        in_specs=[pl.BlockSpec((1, 1, 2 * tb), lambda i: (i, 0, 0), memory_space=pltpu.SMEM),
                  row(LANES), row(D_MODEL), pl.BlockSpec(memory_space=pl.ANY)],
        out_specs=row(D_MODEL),
        out_shape=jax.ShapeDtypeStruct((T, D_MODEL), F32),
        scratch_shapes=[pltpu.VMEM((2, tb, D_MODEL), F32), pltpu.SemaphoreType.DMA(())],
        compiler_params=_cparams(("arbitrary",)),
        name="moe_combine",
    )(pos_blocks, info, h, y)


def _pos_blocks(pos1, pos2, tb):
    nb = pos1.shape[0] // tb
    return jnp.concatenate([pos1.reshape(nb, 1, tb), pos2.reshape(nb, 1, tb)], axis=2)


def _moe(h, g, w_router, w1, w3, w2, tm):
    T = h.shape[0]
    te_rows = min(512, T)
    wr_pad = jnp.pad(w_router, ((0, 0), (0, LANES - N_EXPERTS)))
    info, cnt = _router(h, g, wr_pad, tm)
    counts = cnt[0, :N_EXPERTS].astype(jnp.int32)
    padded = ((counts + te_rows - 1) // te_rows) * te_rows
    ends = jnp.cumsum(padded)
    base = ends - padded
    i1 = info[:, R_I1].astype(jnp.int32)
    i2 = info[:, R_I2].astype(jnp.int32)
    pos1 = jnp.take(base, i1) + info[:, R_RANK1].astype(jnp.int32)
    pos2 = jnp.take(base, i2) + info[:, R_RANK2].astype(jnp.int32)
    n_tiles = (2 * T) // te_rows + N_EXPERTS
    tile_expert = jnp.minimum(jnp.searchsorted(ends // te_rows, jnp.arange(n_tiles), side="right"),
                              N_EXPERTS - 1).astype(jnp.int32)
    n_used = (ends[-1:] // te_rows).astype(jnp.int32)

    tb_d = min(1024, T)
    xs = _dispatch(_pos_blocks(pos1, pos2, tb_d), h, jnp.zeros((n_tiles * te_rows, D_MODEL), F32), tb_d)
    y = _experts(tile_expert, n_used, xs, g, w1, w3, w2, te_rows)
    tb_c = min(256, T)
    return _combine(_pos_blocks(pos1, pos2, tb_c), info, h, y, tb_c)


def kernel(x, positions, attn_norm, w_in, da_q_norm, da_k_norm, lam_q1, lam_k1, lam_q2, lam_k2, da_subln,
           w_proj_da, w_proj_ret, w_out, ffn_norm, dense_w1, dense_w3, dense_w2, w_router, moe_w1, moe_w3,
           moe_w2):
    B, S, D = x.shape
    assert D == D_MODEL and S % RET_CHUNK == 0
    T = B * S
    depth = attn_norm.shape[0]
    tm = min(512, S)
    tq = min(512, S)
    tc = min(512, S)

    pos_b = jnp.broadcast_to(positions.reshape(T, 1).astype(F32), (T, LANES))
    tabs = (_rope_tables(pos_b, *_rope_rows(ROPE_ROT_DIM, ROPE_THETA), tm)
            + _rope_tables(pos_b, *_rope_rows(RET_QK_DIM, RET_ROPE_THETA), tm))
    ret_consts = _retention_consts()
    gsum = jnp.asarray(np.kron(np.eye(4), np.ones((64, 64))), BF16)

    h = x.reshape(T, D)
    for layer in range(depth):
        lambda_init = 0.8 - 0.6 * math.exp(-0.3 * layer)
        gq = jnp.tile(da_q_norm[layer], PROJ_CHUNK // DA_HEAD_DIM).reshape(1, PROJ_CHUNK)
        gk = jnp.tile(da_k_norm[layer], PROJ_CHUNK // DA_HEAD_DIM).reshape(1, PROJ_CHUNK)
        z, vt = _in_proj(h, attn_norm[layer].reshape(1, D), w_in[layer].astype(BF16), gq, gk, gsum, tabs, tm)
        lam_rows = jnp.stack([lam_q1[layer], lam_k1[layer], lam_q2[layer], lam_k2[layer]])
        ya = _diff_attn(z, vt, _score_bound(da_q_norm[layer], da_k_norm[layer]), lam_rows,
                        da_subln[layer].reshape(1, DA_V_DIM), B, S, tq, lambda_init)
        yr = _retention(z, ret_consts, B, S, tc)
        h, hn = _merge(ya, yr, z, h, w_proj_da[layer].astype(BF16), w_proj_ret[layer].astype(BF16),
                       w_out[layer].astype(BF16), ffn_norm[layer].reshape(1, D), tm)
        j = layer // 2
        if layer % 2 == 0:
            h = _ffn(hn, h, dense_w1[j].astype(BF16), dense_w3[j].astype(BF16), dense_w2[j].astype(BF16), tm)
        else:
            h = _moe(h, ffn_norm[layer].reshape(1, D), w_router[j], moe_w1[j].astype(BF16),
                     moe_w3[j].astype(BF16), moe_w2[j].astype(BF16), tm)
    return h.reshape(B, S, D)
```

```python
import functools
import math

import jax
import jax.numpy as jnp
import numpy as np
from jax import lax
from jax.experimental import pallas as pl
from jax.experimental.pallas import tpu as pltpu

F32 = jnp.float32
BF16 = jnp.bfloat16

D_MODEL = 1024
DA_HEADS = 4
DA_HEAD_DIM = 64
DA_V_DIM = 128
ROPE_THETA = 500000.0
ROPE_ROT_DIM = 16
RET_HEADS = 4
RET_QK_DIM = 64
RET_V_DIM = 128
RET_CHUNK = 128
RET_ROPE_THETA = 10000.0
D_FF = 2816
N_EXPERTS = 8
EPS = 1e-6
LOG2_E = math.log2(math.e)
DA_VT_ROWS = DA_V_DIM + 16
FIXED_REF_MAX_BOUND = 60.0
GROUP = 4

LANES = 128
OFF_QA, OFF_KA, OFF_VA = 0, 512, 1024
OFF_QR, OFF_KR, OFF_VR, OFF_GR = 1536, 1792, 2048, 2560
OFF_GA, OFF_GB = 3072, 4096
IN_W = 5120
PROJ_CHUNK = 512
FF_CHUNK = 256
VMEM_LIMIT = 56 * 1024 * 1024
DMA_UNROLL = 8


def _cparams(sem):
    return pltpu.CompilerParams(dimension_semantics=sem, vmem_limit_bytes=VMEM_LIMIT)


def _const_spec(shape):
    nd = len(shape)
    return pl.BlockSpec(shape, lambda *_: (0,) * nd)


def _rope_table_kernel(pos_ref, inv_ref, sgn_ref, cos_ref, sin_ref):
    ang = pos_ref[...] * inv_ref[...]
    cos_ref[...] = jnp.cos(ang)
    sin_ref[...] = jnp.sin(ang) * sgn_ref[...]


def _rope_tables(pos_b, inv_row, sgn_row, tm):
    T = pos_b.shape[0]
    row = pl.BlockSpec((tm, LANES), lambda i: (i, 0))
    return pl.pallas_call(
        _rope_table_kernel,
        grid=(T // tm,),
        in_specs=[row, _const_spec((1, LANES)), _const_spec((1, LANES))],
        out_specs=[row, row],
        out_shape=[jax.ShapeDtypeStruct((T, LANES), F32)] * 2,
        compiler_params=_cparams(("parallel",)),
        name="rope_tables",
    )(pos_b, inv_row, sgn_row)


def _rope_rows(rot_dim, theta):
    half = rot_dim // 2
    inv = 1.0 / (theta ** (jnp.arange(half, dtype=F32) / half))
    jj = np.arange(LANES) % 64
    inv_row = jnp.where(jj < rot_dim, inv[jj % half], 0.0).astype(F32)
    sgn_row = np.where(jj < half, -1.0, np.where(jj < rot_dim, 1.0, 0.0)).astype(np.float32)
    return inv_row.reshape(1, LANES), jnp.asarray(sgn_row).reshape(1, LANES)


def _rotate_half(x, half):
    jj = lax.broadcasted_iota(jnp.int32, x.shape, 1) % 64
    return jnp.where(jj < half, pltpu.roll(x, LANES - half, 1), pltpu.roll(x, half, 1))


def _sigmoid(x):
    return 1.0 / (1.0 + jnp.exp(-x))


def _in_proj_kernel(h_ref, g_ref, w_ref, gq_ref, gk_ref, gsum_ref, ca_ref, sa_ref, cr_ref, sr_ref, z_ref, vt_ref):
    x = h_ref[...]
    ms = jnp.mean(x * x, axis=-1, keepdims=True)
    hn = (x * lax.rsqrt(ms + EPS) * g_ref[...]).astype(BF16)

    def head_rms(zc, gain):
        sq = (zc * zc).astype(BF16)
        outs = []
        for s in range(PROJ_CHUNK // 256):
            sl = slice(s * 256, (s + 1) * 256)
            tot = jnp.dot(sq[:, sl], gsum_ref[...], preferred_element_type=F32)
            outs.append(zc[:, sl] * lax.rsqrt(tot * (1.0 / 64.0) + EPS))
        return jnp.concatenate(outs, axis=1) * gain

    def rope(y, cos, sin, half):
        outs = []
        for s in range(y.shape[1] // LANES):
            ys = y[:, s * LANES:(s + 1) * LANES]
            outs.append(ys * cos + _rotate_half(ys, half) * sin)
        return jnp.concatenate(outs, axis=1)

    for c in range(IN_W // PROJ_CHUNK):
        c0 = c * PROJ_CHUNK
        zc = jnp.dot(hn, w_ref[:, c0:c0 + PROJ_CHUNK], preferred_element_type=F32)
        if c0 == OFF_QA:
            zc = rope(head_rms(zc, gq_ref[...]), ca_ref[...], sa_ref[...], ROPE_ROT_DIM // 2)
            zc = zc * (DA_HEAD_DIM ** -0.5 * LOG2_E)
        elif c0 == OFF_KA:
            zc = rope(head_rms(zc, gk_ref[...]), ca_ref[...], sa_ref[...], ROPE_ROT_DIM // 2)
        elif c0 == OFF_VA:
            zt = zc.T.astype(BF16)
            extra = lax.broadcasted_iota(jnp.int32, (DA_VT_ROWS - DA_V_DIM, zt.shape[1]), 0)
            for hd in range(DA_HEADS):
                vt_ref[0, hd, 0:DA_V_DIM, :] = zt[hd * DA_V_DIM:(hd + 1) * DA_V_DIM, :]
                vt_ref[0, hd, DA_V_DIM:DA_VT_ROWS, :] = jnp.where(extra == 0, 1.0, 0.0).astype(BF16)
        elif c0 == OFF_QR:
            zc = rope(zc, cr_ref[...], sr_ref[...], RET_QK_DIM // 2)
            col = lax.broadcasted_iota(jnp.int32, zc.shape, 1)
            zc = jnp.where(col >= OFF_KR - OFF_QR, zc * (RET_QK_DIM ** -0.5), zc)
        elif c0 == OFF_GR:
            zc = zc * _sigmoid(zc)
        elif c0 >= OFF_GA:
            zc = _sigmoid(zc)
        z_ref[:, c0:c0 + PROJ_CHUNK] = zc.astype(BF16)


def _in_proj(h, g, w, gq, gk, gsum, tabs, tm):
    T = h.shape[0]
    row = lambda width: pl.BlockSpec((tm, width), lambda i: (i, 0))
    return pl.pallas_call(
        _in_proj_kernel,
        grid=(T // tm,),
        in_specs=[row(D_MODEL), _const_spec((1, D_MODEL)), _const_spec((D_MODEL, IN_W)),
                  _const_spec((1, PROJ_CHUNK)), _const_spec((1, PROJ_CHUNK)), _const_spec((256, 256)),
                  row(LANES), row(LANES), row(LANES), row(LANES)],
        out_specs=[row(IN_W), pl.BlockSpec((1, DA_HEADS, DA_VT_ROWS, tm), lambda i: (i, 0, 0, 0))],
        out_shape=[jax.ShapeDtypeStruct((T, IN_W), BF16),
                   jax.ShapeDtypeStruct((T // tm, DA_HEADS, DA_VT_ROWS, tm), BF16)],
        compiler_params=_cparams(("parallel",)),
        name="in_proj",
    )(h, g, w, gq, gk, gsum, *tabs)


def _diff_attn_kernel(bound_ref, q_ref, k_ref, vt_ref, lam_ref, subln_ref, o_ref, acc_sc, st_sc, *, tq,
                      lambda_init):
    qi = pl.program_id(2)
    q = q_ref[...]
    lane = lax.broadcasted_iota(jnp.int32, q.shape, 1)
    qs = (jnp.where(lane < DA_HEAD_DIM, q, jnp.zeros_like(q)),
          jnp.where(lane >= DA_HEAD_DIM, q, jnp.zeros_like(q)))
    acc_sc[...] = jnp.zeros(acc_sc.shape, F32)

    tqh = tq // 2
    streams = [(c, hf) for hf in range(2) for c in range(2)]
    qsub = [qs[c][hf * tqh:(hf + 1) * tqh, :] for c, hf in streams]

    def scores(ki):
        k = k_ref[pl.ds(pl.multiple_of(ki * tq, tq), tq), :]
        return [lax.dot_general(k, qq, (((1,), (1,)), ((), ())), preferred_element_type=F32) for qq in qsub]

    def stash(sts):
        for s in range(len(streams)):
            st_sc[s] = sts[s]

    bound = bound_ref[0, 0]

    def consume(sts, ki, stats, masked, fixed_ref):
        vt = vt_ref[ki, 0]
        out = []
        for s, (c, hf) in enumerate(streams):
            st = st_sc[s] if sts is None else sts[s]
            if masked:
                key = lax.broadcasted_iota(jnp.int32, st.shape, 0)
                qry = lax.broadcasted_iota(jnp.int32, st.shape, 1) + hf * tqh
                st = jnp.where(key <= qry, st, -jnp.inf)
            cols = slice(hf * tqh, (hf + 1) * tqh)
            if fixed_ref:
                pt = jnp.exp2(st - bound).astype(BF16)
                acc_sc[c, :, cols] += jnp.dot(vt, pt, preferred_element_type=F32)
                out.append(stats[s])
            else:
                m_old = stats[s]
                m_new = jnp.maximum(m_old, jnp.max(st, axis=0, keepdims=True))
                alpha = jnp.exp2(m_old - m_new)
                pt = jnp.exp2((st - m_new).astype(BF16))
                acc_sc[c, :, cols] = alpha * acc_sc[c, :, cols] + jnp.dot(vt, pt, preferred_element_type=F32)
                out.append(m_new)
        return tuple(out)

    def run_fixed_ref():
        none = (None,) * len(streams)

        def group(j, carry):
            blocks = [GROUP * j + u for u in range(GROUP)]
            sts = [scores(b) for b in blocks]
            for b, st in zip(blocks, sts):
                consume(st, b, none, False, True)
            return carry

        def tail(rest):
            def body():
                blocks = [qi - rest + u for u in range(rest + 1)]
                sts = [scores(b) for b in blocks]
                for u, (b, st) in enumerate(zip(blocks, sts)):
                    consume(st, b, none, u == rest, True)
            return body

        ngroups = qi // GROUP
        lax.fori_loop(0, ngroups, group, 0)
        for rest in range(GROUP):
            pl.when(qi - GROUP * ngroups == rest)(tail(rest))

    def run_online():
        npairs = qi // 2
        stash(scores(0))

        def pair(j, stats):
            b0 = 2 * j
            odd = scores(b0 + 1)
            stats = consume(None, b0, stats, False, False)
            stash(scores(b0 + 2))
            return consume(odd, b0 + 1, stats, False, False)

        def leftover(stats):
            stats = consume(None, qi - 1, stats, False, False)
            stash(scores(qi))
            return stats

        init = (jnp.full((1, tqh), -jnp.inf, F32),) * len(streams)
        stats = lax.fori_loop(0, npairs, pair, init)
        stats = lax.cond(qi - 2 * npairs == 1, leftover, lambda st: st, stats)
        consume(None, qi, stats, True, False)

    pl.when(bound <= FIXED_REF_MAX_BOUND)(run_fixed_ref)
    pl.when(bound > FIXED_REF_MAX_BOUND)(run_online)

    lv = lam_ref[...]
    lam = (jnp.exp(jnp.sum(lv[0:1] * lv[1:2], axis=1, keepdims=True))
           - jnp.exp(jnp.sum(lv[2:3] * lv[3:4], axis=1, keepdims=True)) + lambda_init)
    dv = DA_V_DIM
    ot = (acc_sc[0, 0:dv, :] / acc_sc[0, dv:dv + 1, :]
          - lam * (acc_sc[1, 0:dv, :] / acc_sc[1, dv:dv + 1, :]))
    yt = ot * lax.rsqrt(jnp.mean(ot * ot, axis=0, keepdims=True) + EPS)
    o_ref[...] = (yt.T * subln_ref[...] * (1.0 - lambda_init)).astype(BF16)


def _score_bound(gq, gk):
    norm = math.sqrt(DA_HEAD_DIM)
    return (1.01 * norm * norm * DA_HEAD_DIM ** -0.5 * LOG2_E
            * jnp.max(jnp.abs(gq)) * jnp.max(jnp.abs(gk))).astype(F32).reshape(1, 1)


def _diff_attn(z, vt, bound, lam_rows, subln, B, S, tq, lambda_init):
    T = B * S
    nq = S // tq
    kern = functools.partial(_diff_attn_kernel, tq=tq, lambda_init=lambda_init)
    return pl.pallas_call(
        kern,
        grid=(B, DA_HEADS, nq),
        in_specs=[pl.BlockSpec((1, 1), lambda b, h, i: (0, 0), memory_space=pltpu.SMEM),
                  pl.BlockSpec((tq, LANES), lambda b, h, i: (b * nq + i, OFF_QA // LANES + h)),
                  pl.BlockSpec((S, LANES), lambda b, h, i: (b, OFF_KA // LANES + h)),
                  pl.BlockSpec((nq, 1, DA_VT_ROWS, tq), lambda b, h, i: (b, h, 0, 0)),
                  _const_spec((4, DA_HEAD_DIM)), _const_spec((1, DA_V_DIM))],
        out_specs=pl.BlockSpec((tq, LANES), lambda b, h, i: (b * nq + i, h)),
        out_shape=jax.ShapeDtypeStruct((T, DA_HEADS * DA_V_DIM), BF16),
        scratch_shapes=[pltpu.VMEM((2, DA_VT_ROWS, tq), F32), pltpu.VMEM((4, tq, tq // 2), F32)],
        compiler_params=_cparams(("parallel", "parallel", "arbitrary")),
        name="diff_attn",
    )(bound, z, z, vt, lam_rows, subln)


def _retention_kernel(q_ref, k_ref, v_ref, g_ref, dec_ref, qdec_ref, kdec_ref, cdec_ref, o_ref, r_sc, o_sc, *,
                      tc):
    @pl.when(pl.program_id(1) == 0)
    def _():
        r_sc[...] = jnp.zeros(r_sc.shape, F32)

    C = RET_CHUNK
    lane = lax.broadcasted_iota(jnp.int32, (C, LANES), 1)
    rowi = lax.broadcasted_iota(jnp.int32, (C, LANES), 0)
    units = [(n, j) for n in range(tc // C) for j in range(RET_HEADS // 2)]
    rows = lambda n: slice(n * C, (n + 1) * C)
    cols = lambda j: slice(j * LANES, (j + 1) * LANES)
    vcol = lambda h: slice(h * RET_V_DIM, (h + 1) * RET_V_DIM)
    own = lambda hh: (lane >= RET_QK_DIM) if hh else (lane < RET_QK_DIM)
    s, kv = {}, {}
    for n, j in units:
        qp = q_ref[rows(n), cols(j)]
        kp = k_ref[rows(n), cols(j)]
        kdt = (kp.astype(F32) * kdec_ref[:, cols(j)]).T.astype(BF16)
        both = []
        for hh in range(2):
            h = 2 * j + hh
            s[n, h] = lax.dot_general(jnp.where(own(hh), qp, jnp.zeros_like(qp)), kp, (((1,), (1,)), ((), ())),
                                      preferred_element_type=F32)
            both.append(jnp.dot(kdt, v_ref[rows(n), vcol(h)], preferred_element_type=F32))
        kv[n, j] = jnp.where(rowi < RET_QK_DIM, both[0], both[1])
    for n, j in units:
        for hh in range(2):
            h = 2 * j + hh
            inner = (s[n, h] * dec_ref[h]).astype(BF16)
            o_sc[rows(n), vcol(h)] = jnp.dot(inner, v_ref[rows(n), vcol(h)], preferred_element_type=F32)
    state = [r_sc[j] for j in range(RET_HEADS // 2)]
    for n, j in units:
        qp = q_ref[rows(n), cols(j)]
        qd = (qp.astype(F32) * qdec_ref[:, cols(j)]).astype(BF16)
        r_b = state[j].astype(BF16)
        for hh in range(2):
            h = 2 * j + hh
            o_sc[rows(n), vcol(h)] += jnp.dot(jnp.where(own(hh), qd, jnp.zeros_like(qd)), r_b,
                                              preferred_element_type=F32)
        state[j] = state[j] * cdec_ref[j] + kv[n, j]
    for j in range(RET_HEADS // 2):
        r_sc[j] = state[j]
    for h in range(RET_HEADS):
        gs = slice(h * RET_V_DIM, (h + 1) * RET_V_DIM)
        o = o_sc[:, gs]
        oc = o - jnp.mean(o, axis=-1, keepdims=True)
        y = oc * lax.rsqrt(jnp.mean(oc * oc, axis=-1, keepdims=True) + EPS)
        o_ref[:, gs] = (y * g_ref[:, gs].astype(F32)).astype(BF16)


def _retention(z, consts, B, S, tc):
    T = B * S
    n = S // tc
    dec, qdec, kdec, cdec = consts
    kern = functools.partial(_retention_kernel, tc=tc)
    blk = lambda width, off: pl.BlockSpec((tc, width), lambda b, i: (b * n + i, off // width))
    return pl.pallas_call(
        kern,
        grid=(B, n),
        in_specs=[blk(256, OFF_QR), blk(256, OFF_KR), blk(512, OFF_VR), blk(512, OFF_GR),
                  _const_spec(dec.shape), _const_spec(qdec.shape), _const_spec(kdec.shape),
                  _const_spec(cdec.shape)],
        out_specs=pl.BlockSpec((tc, 512), lambda b, i: (b * n + i, 0)),
        out_shape=jax.ShapeDtypeStruct((T, RET_HEADS * RET_V_DIM), BF16),
        scratch_shapes=[pltpu.VMEM((RET_HEADS // 2, LANES, LANES), F32),
                        pltpu.VMEM((tc, RET_HEADS * RET_V_DIM), F32)],
        compiler_params=_cparams(("parallel", "arbitrary")),
        name="retention",
    )(z, z, z, z, dec, qdec, kdec, cdec)


def _retention_consts():
    C = RET_CHUNK
    log_gamma = jnp.log1p(-jnp.exp2(-5.0 - jnp.arange(RET_HEADS, dtype=F32)))
    idx = jnp.arange(C, dtype=F32)
    rel = idx[:, None] - idx[None, :]
    dec = jnp.where(rel[None] >= 0, jnp.exp(log_gamma[:, None, None] * jnp.maximum(rel, 0.0)[None]), 0.0)
    q_dec = jnp.exp(log_gamma[None, :] * (idx + 1.0)[:, None])
    k_dec = jnp.exp(log_gamma[None, :] * (C - 1.0 - idx)[:, None])
    qdec = jnp.repeat(q_dec, RET_QK_DIM, axis=1)
    kdec = jnp.repeat(k_dec, RET_QK_DIM, axis=1)
    chunk_decay = jnp.repeat(jnp.exp(log_gamma * C), RET_QK_DIM)
    cdec = jnp.broadcast_to(chunk_decay.reshape(RET_HEADS // 2, LANES, 1), (RET_HEADS // 2, LANES, LANES))
    return dec.astype(F32), qdec.astype(F32), kdec.astype(F32), cdec.astype(F32)


def _merge_kernel(ya_ref, yr_ref, ga_ref, gb_ref, h_ref, wa_ref, wr_ref, wo_ref, gf_ref, h_out, hn_out):
    half = ya_ref.shape[0] // 2
    parts = [slice(0, half), slice(half, 2 * half)]
    branch = [(jnp.dot(ya_ref[r, :], wa_ref[...], preferred_element_type=F32),
               jnp.dot(yr_ref[r, :], wr_ref[...], preferred_element_type=F32)) for r in parts]
    for r, (pa, pr) in zip(parts, branch):
        merged = ga_ref[r, :].astype(F32) * pa + gb_ref[r, :].astype(F32) * pr
        hnew = h_ref[r, :] + jnp.dot(merged.astype(BF16), wo_ref[...], preferred_element_type=F32)
        h_out[r, :] = hnew
        ms = jnp.mean(hnew * hnew, axis=-1, keepdims=True)
        hn_out[r, :] = (hnew * lax.rsqrt(ms + EPS) * gf_ref[...]).astype(BF16)


def _merge(ya, yr, z, h, wa, wr, wo, gf, tm):
    T = h.shape[0]
    row = lambda width, cb=0: pl.BlockSpec((tm, width), lambda i: (i, cb))
    return pl.pallas_call(
        _merge_kernel,
        grid=(T // tm,),
        in_specs=[row(512), row(512), row(D_MODEL, OFF_GA // D_MODEL), row(D_MODEL, OFF_GB // D_MODEL),
                  row(D_MODEL), _const_spec(wa.shape), _const_spec(wr.shape), _const_spec(wo.shape),
                  _const_spec((1, D_MODEL))],
        out_specs=[row(D_MODEL), row(D_MODEL)],
        out_shape=[jax.ShapeDtypeStruct((T, D_MODEL), F32), jax.ShapeDtypeStruct((T, D_MODEL), BF16)],
        compiler_params=_cparams(("parallel",)),
        name="merge_out",
    )(ya, yr, z, z, h, wa, wr, wo, gf)


def _swiglu_into(acc_sc, x, w1_ref, w3_ref, w2_ref):
    chunks = [slice(c * FF_CHUNK, (c + 1) * FF_CHUNK) for c in range(D_FF // FF_CHUNK)]
    up = lambda cs: (jnp.dot(x, w1_ref[:, cs], preferred_element_type=F32),
                     jnp.dot(x, w3_ref[:, cs], preferred_element_type=F32))
    uv = up(chunks[0])
    for c, cs in enumerate(chunks):
        nxt = up(chunks[c + 1]) if c + 1 < len(chunks) else None
        a = (uv[0] * _sigmoid(uv[0]) * uv[1]).astype(BF16)
        d = jnp.dot(a, w2_ref[cs, :], preferred_element_type=F32)
        if c == 0:
            acc_sc[...] = d
        else:
            acc_sc[...] += d
        uv = nxt


def _ffn_kernel(hn_ref, prev_ref, w1_ref, w3_ref, w2_ref, o_ref, acc_sc):
    _swiglu_into(acc_sc, hn_ref[...], w1_ref, w3_ref, w2_ref)
    o_ref[...] = prev_ref[...] + acc_sc[...]


def _ffn(hn, prev, w1, w3, w2, tm):
    T = hn.shape[0]
    row = lambda width: pl.BlockSpec((tm, width), lambda i: (i, 0))
    wspec = lambda w: pl.BlockSpec(w.shape, lambda i: (0, 0), pipeline_mode=pl.Buffered(1))
    return pl.pallas_call(
        _ffn_kernel,
        grid=(T // tm,),
        in_specs=[row(D_MODEL), row(D_MODEL), wspec(w1), wspec(w3), wspec(w2)],
        out_specs=row(D_MODEL),
        out_shape=jax.ShapeDtypeStruct((T, D_MODEL), F32),
        scratch_shapes=[pltpu.VMEM((tm, D_MODEL), F32)],
        input_output_aliases={1: 0},
        compiler_params=_cparams(("parallel",)),
        name="ffn",
    )(hn, prev, w1, w3, w2)


R_I1, R_I2, R_W1, R_W2, R_RANK1, R_RANK2 = range(6)


def _router_kernel(h_ref, g_ref, wr_ref, info_ref, cnt_ref, carry_sc):
    @pl.when(pl.program_id(0) == 0)
    def _():
        carry_sc[...] = jnp.zeros(carry_sc.shape, F32)

    x = h_ref[...]
    ms = jnp.mean(x * x, axis=-1, keepdims=True)
    hn = x * lax.rsqrt(ms + EPS) * g_ref[...]
    w = wr_ref[...]
    hn_hi, w_hi = hn.astype(BF16), w.astype(BF16)
    hn_lo, w_lo = (hn - hn_hi.astype(F32)).astype(BF16), (w - w_hi.astype(F32)).astype(BF16)
    logits = (jnp.dot(hn_hi, w_hi, preferred_element_type=F32) + jnp.dot(hn_hi, w_lo, preferred_element_type=F32)
              + jnp.dot(hn_lo, w_hi, preferred_element_type=F32))
    lane = lax.broadcasted_iota(jnp.int32, logits.shape, 1)
    logits = jnp.where(lane < N_EXPERTS, logits, -jnp.inf)
    m1 = jnp.max(logits, axis=1, keepdims=True)
    i1 = jnp.min(jnp.where(logits == m1, lane, LANES), axis=1, keepdims=True)
    rest = jnp.where(lane == i1, -jnp.inf, logits)
    m2 = jnp.max(rest, axis=1, keepdims=True)
    i2 = jnp.min(jnp.where(rest == m2, lane, LANES), axis=1, keepdims=True)
    e = jnp.exp(m2 - m1)
    w1 = 1.0 / (1.0 + e)
    w2 = e / (1.0 + e)
    sel = jnp.where(lane == i1, 1.0, jnp.where(lane == i2, 1.0, 0.0))
    tm = sel.shape[0]
    rr = lax.broadcasted_iota(jnp.int32, (tm, tm), 0)
    cc = lax.broadcasted_iota(jnp.int32, (tm, tm), 1)
    ltri = jnp.where(cc < rr, 1.0, 0.0).astype(BF16)
    before = jnp.dot(ltri, sel.astype(BF16), preferred_element_type=F32) + carry_sc[0:1, :]
    r1 = jnp.sum(jnp.where(lane == i1, before, 0.0), axis=1, keepdims=True)
    r2 = jnp.sum(jnp.where(lane == i2, before, 0.0), axis=1, keepdims=True)
    info = jnp.where(lane == R_I1, i1.astype(F32), 0.0)
    for ln, val in ((R_I2, i2.astype(F32)), (R_W1, w1), (R_W2, w2), (R_RANK1, r1), (R_RANK2, r2)):
        info = jnp.where(lane == ln, val, info)
    info_ref[...] = info
    carry_sc[...] = carry_sc[...] + jnp.sum(sel, axis=0, keepdims=True)
    cnt_ref[...] = carry_sc[...]


def _router(h, g, wr_pad, tm):
    T = h.shape[0]
    return pl.pallas_call(
        _router_kernel,
        grid=(T // tm,),
        in_specs=[pl.BlockSpec((tm, D_MODEL), lambda i: (i, 0)), _const_spec((1, D_MODEL)),
                  _const_spec((D_MODEL, LANES))],
        out_specs=[pl.BlockSpec((tm, LANES), lambda i: (i, 0)), _const_spec((8, LANES))],
        out_shape=[jax.ShapeDtypeStruct((T, LANES), F32), jax.ShapeDtypeStruct((8, LANES), F32)],
        scratch_shapes=[pltpu.VMEM((8, LANES), F32)],
        compiler_params=_cparams(("arbitrary",)),
        name="router",
    )(h, g, wr_pad)


def _dispatch_kernel(pad_lo_ref, pad_hi_ref, pos_ref, h_ref, xs_hbm, zero_sc, sem, pad_sem, *, tb):
    @pl.when(pl.program_id(0) == 0)
    def _():
        zero_sc[...] = jnp.zeros(zero_sc.shape, F32)
        zero_row = zero_sc.at[pl.ds(0, 1)]
        for e in range(N_EXPERTS):
            def fill(r, carry):
                pltpu.make_async_copy(zero_row, xs_hbm.at[pl.ds(r, 1)], pad_sem).start()
                return carry

            lax.fori_loop(pad_lo_ref[e], pad_hi_ref[e], fill, 0)
        for e in range(N_EXPERTS):
            def drain(r, carry):
                pltpu.make_async_copy(zero_row, xs_hbm.at[pl.ds(r, 1)], pad_sem).wait()
                return carry

            lax.fori_loop(pad_lo_ref[e], pad_hi_ref[e], drain, 0)

    def issue(t, carry):
        for k in range(2):
            p = pos_ref[0, 0, k * tb + t]
            pltpu.make_async_copy(h_ref.at[pl.ds(t, 1)], xs_hbm.at[pl.ds(p, 1)], sem).start(priority=k)
        return carry

    lax.fori_loop(0, tb, issue, 0, unroll=DMA_UNROLL)

    for k in range(2):
        pltpu.make_async_copy(h_ref, xs_hbm.at[pl.ds(0, tb)], sem).wait()


def _dispatch(pad_lo, pad_hi, pos_blocks, h, n_rows, tb):
    T = h.shape[0]
    return pl.pallas_call(
        functools.partial(_dispatch_kernel, tb=tb),
        grid_spec=pltpu.PrefetchScalarGridSpec(
            num_scalar_prefetch=2,
            grid=(T // tb,),
            in_specs=[pl.BlockSpec((1, 1, 2 * tb), lambda i, lo, hi: (i, 0, 0), memory_space=pltpu.SMEM),
                      pl.BlockSpec((tb, D_MODEL), lambda i, lo, hi: (i, 0))],
            out_specs=pl.BlockSpec(memory_space=pl.ANY),
            scratch_shapes=[pltpu.VMEM((8, D_MODEL), F32), pltpu.SemaphoreType.DMA(()),
                            pltpu.SemaphoreType.DMA(())]),
        out_shape=jax.ShapeDtypeStruct((n_rows, D_MODEL), F32),
        compiler_params=pltpu.CompilerParams(dimension_semantics=("arbitrary",), has_side_effects=True),
        name="moe_dispatch",
    )(pad_lo, pad_hi, pos_blocks, h)


def _expert_kernel(te_ref, nu_ref, xs_ref, g_ref, w1_ref, w3_ref, w2_ref, y_ref, acc_sc):
    del te_ref
    i = pl.program_id(0)

    @pl.when(i < nu_ref[0])
    def _():
        x = xs_ref[...]
        ms = jnp.mean(x * x, axis=-1, keepdims=True)
        hn = (x * lax.rsqrt(ms + EPS) * g_ref[...]).astype(BF16)
        _swiglu_into(acc_sc, hn, w1_ref.at[0], w3_ref.at[0], w2_ref.at[0])
        y_ref[...] = acc_sc[...]

    @pl.when(i >= nu_ref[0])
    def _():
        y_ref[...] = jnp.zeros(y_ref.shape, F32)


def _experts(tile_expert, n_used, xs, g, w1, w3, w2, te_rows):
    P = xs.shape[0]
    row = pl.BlockSpec((te_rows, D_MODEL), lambda i, te, nu: (jnp.minimum(i, nu[0] - 1), 0))
    wspec = lambda w: pl.BlockSpec((1,) + w.shape[1:], lambda i, te, nu: (te[i], 0, 0))
    return pl.pallas_call(
        _expert_kernel,
        grid_spec=pltpu.PrefetchScalarGridSpec(
            num_scalar_prefetch=2,
            grid=(P // te_rows,),
            in_specs=[row, pl.BlockSpec((1, D_MODEL), lambda i, te, nu: (0, 0)), wspec(w1), wspec(w3), wspec(w2)],
            out_specs=pl.BlockSpec((te_rows, D_MODEL), lambda i, te, nu: (i, 0)),
            scratch_shapes=[pltpu.VMEM((te_rows, D_MODEL), F32)]),
        out_shape=jax.ShapeDtypeStruct((P, D_MODEL), F32),
        compiler_params=_cparams(("arbitrary",)),
        name="moe_experts",
    )(tile_expert, n_used, xs, g, w1, w3, w2)


def _combine_kernel(pos_ref, info_ref, h_ref, y_hbm, o_ref, g_sc, sem, *, tb):
    def issue(t, carry):
        for k in range(2):
            p = pos_ref[0, 0, k * tb + t]
            pltpu.make_async_copy(y_hbm.at[pl.ds(p, 1)], g_sc.at[k, pl.ds(t, 1)], sem).start(priority=k)
        return carry

    lax.fori_loop(0, tb, issue, 0, unroll=DMA_UNROLL)

    for k in range(2):
        pltpu.make_async_copy(y_hbm.at[pl.ds(0, tb)], g_sc.at[k], sem).wait()
    info = info_ref[...]
    o_ref[...] = h_ref[...] + info[:, R_W1:R_W1 + 1] * g_sc[0] + info[:, R_W2:R_W2 + 1] * g_sc[1]


def _combine(pos_blocks, info, h, y, tb):
    T = h.shape[0]
    row = lambda width: pl.BlockSpec((tb, width), lambda i: (i, 0))
    return pl.pallas_call(
        functools.partial(_combine_kernel, tb=tb),
        grid=(T // tb,),
        in_specs=[pl.BlockSpec((1, 1, 2 * tb), lambda i: (i, 0, 0), memory_space=pltpu.SMEM),
                  row(LANES), row(D_MODEL), pl.BlockSpec(memory_space=pl.ANY)],
        out_specs=row(D_MODEL),
        out_shape=jax.ShapeDtypeStruct((T, D_MODEL), F32),
        scratch_shapes=[pltpu.VMEM((2, tb, D_MODEL), F32), pltpu.SemaphoreType.DMA(())],
        compiler_params=_cparams(("arbitrary",)),
        name="moe_combine",
    )(pos_blocks, info, h, y)


def _pos_blocks(pos1, pos2, tb):
    nb = pos1.shape[0] // tb
    return jnp.concatenate([pos1.reshape(nb, 1, tb), pos2.reshape(nb, 1, tb)], axis=2)


def _moe(h, g, w_router, w1, w3, w2, tm):
    T = h.shape[0]
    te_rows = min(512, T)
    wr_pad = jnp.pad(w_router, ((0, 0), (0, LANES - N_EXPERTS)))
    info, cnt = _router(h, g, wr_pad, tm)
    counts = cnt[0, :N_EXPERTS].astype(jnp.int32)
    padded = ((counts + te_rows - 1) // te_rows) * te_rows
    ends = jnp.cumsum(padded)
    base = ends - padded
    i1 = info[:, R_I1].astype(jnp.int32)
    i2 = info[:, R_I2].astype(jnp.int32)
    pos1 = jnp.take(base, i1) + info[:, R_RANK1].astype(jnp.int32)
    pos2 = jnp.take(base, i2) + info[:, R_RANK2].astype(jnp.int32)
    n_tiles = (2 * T) // te_rows + N_EXPERTS
    tile_expert = jnp.minimum(jnp.searchsorted(ends // te_rows, jnp.arange(n_tiles), side="right"),
                              N_EXPERTS - 1).astype(jnp.int32)
    n_used = (ends[-1:] // te_rows).astype(jnp.int32)

    tb_d = min(1024, T)
    xs = _dispatch(base + counts, ends, _pos_blocks(pos1, pos2, tb_d), h, n_tiles * te_rows, tb_d)
    y = _experts(tile_expert, n_used, xs, g, w1, w3, w2, te_rows)
    tb_c = min(256, T)
    return _combine(_pos_blocks(pos1, pos2, tb_c), info, h, y, tb_c)


def kernel(x, positions, attn_norm, w_in, da_q_norm, da_k_norm, lam_q1, lam_k1, lam_q2, lam_k2, da_subln,
           w_proj_da, w_proj_ret, w_out, ffn_norm, dense_w1, dense_w3, dense_w2, w_router, moe_w1, moe_w3,
           moe_w2):
    B, S, D = x.shape
    assert D == D_MODEL and S % RET_CHUNK == 0
    T = B * S
    depth = attn_norm.shape[0]
    tm = min(512, S)
    tq = min(512, S)
    tc = min(512, S)

    pos_b = jnp.broadcast_to(positions.reshape(T, 1).astype(F32), (T, LANES))
    tabs = (_rope_tables(pos_b, *_rope_rows(ROPE_ROT_DIM, ROPE_THETA), tm)
            + _rope_tables(pos_b, *_rope_rows(RET_QK_DIM, RET_ROPE_THETA), tm))
    ret_consts = _retention_consts()
    gsum = jnp.asarray(np.kron(np.eye(4), np.ones((64, 64))), BF16)

    h = x.reshape(T, D)
    for layer in range(depth):
        lambda_init = 0.8 - 0.6 * math.exp(-0.3 * layer)
        gq = jnp.tile(da_q_norm[layer], PROJ_CHUNK // DA_HEAD_DIM).reshape(1, PROJ_CHUNK)
        gk = jnp.tile(da_k_norm[layer], PROJ_CHUNK // DA_HEAD_DIM).reshape(1, PROJ_CHUNK)
        z, vt = _in_proj(h, attn_norm[layer].reshape(1, D), w_in[layer].astype(BF16), gq, gk, gsum, tabs, tm)
        lam_rows = jnp.stack([lam_q1[layer], lam_k1[layer], lam_q2[layer], lam_k2[layer]])
        ya = _diff_attn(z, vt, _score_bound(da_q_norm[layer], da_k_norm[layer]), lam_rows,
                        da_subln[layer].reshape(1, DA_V_DIM), B, S, tq, lambda_init)
        yr = _retention(z, ret_consts, B, S, tc)
        h, hn = _merge(ya, yr, z, h, w_proj_da[layer].astype(BF16), w_proj_ret[layer].astype(BF16),
                       w_out[layer].astype(BF16), ffn_norm[layer].reshape(1, D), tm)
        j = layer // 2
        if layer % 2 == 0:
            h = _ffn(hn, h, dense_w1[j].astype(BF16), dense_w3[j].astype(BF16), dense_w2[j].astype(BF16), tm)
        else:
            h = _moe(h, ffn_norm[layer].reshape(1, D), w_router[j], moe_w1[j].astype(BF16),
                     moe_w3[j].astype(BF16), moe_w2[j].astype(BF16), tm)
    return h.reshape(B, S, D)
```

```python
import functools
import math

import jax
import jax.numpy as jnp
import numpy as np
from jax import lax
from jax.experimental import pallas as pl
from jax.experimental.pallas import tpu as pltpu

F32 = jnp.float32
BF16 = jnp.bfloat16

D_MODEL = 1024
DA_HEADS = 4
DA_HEAD_DIM = 64
DA_V_DIM = 128
ROPE_THETA = 500000.0
ROPE_ROT_DIM = 16
RET_HEADS = 4
RET_QK_DIM = 64
RET_V_DIM = 128
RET_CHUNK = 128
RET_ROPE_THETA = 10000.0
D_FF = 2816
N_EXPERTS = 8
EPS = 1e-6
LOG2_E = math.log2(math.e)
DA_VT_ROWS = DA_V_DIM + 16
FIXED_REF_MAX_BOUND = 60.0
GROUP = 4

LANES = 128
SUBLANES = 8
OFF_QA, OFF_KA, OFF_VA = 0, 512, 1024
OFF_QR, OFF_KR, OFF_VR, OFF_GR = 1536, 1792, 2048, 2560
OFF_GA, OFF_GB = 3072, 4096
IN_W = 5120
PROJ_CHUNK = 512
FF_CHUNK = 256
VMEM_LIMIT = 56 * 1024 * 1024


def _cparams(sem):
    return pltpu.CompilerParams(dimension_semantics=sem, vmem_limit_bytes=VMEM_LIMIT)


def _const_spec(shape):
    nd = len(shape)
    return pl.BlockSpec(shape, lambda *_: (0,) * nd)


def _rope_table_kernel(pos_ref, inv_ref, sgn_ref, cos_ref, sin_ref):
    ang = pos_ref[...] * inv_ref[...]
    cos_ref[...] = jnp.cos(ang)
    sin_ref[...] = jnp.sin(ang) * sgn_ref[...]


def _rope_tables(pos_b, inv_row, sgn_row, tm):
    T = pos_b.shape[0]
    row = pl.BlockSpec((tm, LANES), lambda i: (i, 0))
    return pl.pallas_call(
        _rope_table_kernel,
        grid=(T // tm,),
        in_specs=[row, _const_spec((1, LANES)), _const_spec((1, LANES))],
        out_specs=[row, row],
        out_shape=[jax.ShapeDtypeStruct((T, LANES), F32)] * 2,
        compiler_params=_cparams(("parallel",)),
        name="rope_tables",
    )(pos_b, inv_row, sgn_row)


def _rope_rows(rot_dim, theta):
    half = rot_dim // 2
    inv = 1.0 / (theta ** (jnp.arange(half, dtype=F32) / half))
    jj = np.arange(LANES) % 64
    inv_row = jnp.where(jj < rot_dim, inv[jj % half], 0.0).astype(F32)
    sgn_row = np.where(jj < half, -1.0, np.where(jj < rot_dim, 1.0, 0.0)).astype(np.float32)
    return inv_row.reshape(1, LANES), jnp.asarray(sgn_row).reshape(1, LANES)


def _rotate_half(x, half):
    jj = lax.broadcasted_iota(jnp.int32, x.shape, 1) % 64
    return jnp.where(jj < half, pltpu.roll(x, LANES - half, 1), pltpu.roll(x, half, 1))


def _sigmoid(x):
    return 1.0 / (1.0 + jnp.exp(-x))


def _in_proj_kernel(h_ref, g_ref, w_ref, gq_ref, gk_ref, gsum_ref, ca_ref, sa_ref, cr_ref, sr_ref, z_ref, vt_ref):
    x = h_ref[...]
    ms = jnp.mean(x * x, axis=-1, keepdims=True)
    hn = (x * lax.rsqrt(ms + EPS) * g_ref[...]).astype(BF16)

    def head_rms(zc, gain):
        sq = (zc * zc).astype(BF16)
        outs = []
        for s in range(PROJ_CHUNK // 256):
            sl = slice(s * 256, (s + 1) * 256)
            tot = jnp.dot(sq[:, sl], gsum_ref[...], preferred_element_type=F32)
            outs.append(zc[:, sl] * lax.rsqrt(tot * (1.0 / 64.0) + EPS))
        return jnp.concatenate(outs, axis=1) * gain

    def rope(y, cos, sin, half):
        outs = []
        for s in range(y.shape[1] // LANES):
            ys = y[:, s * LANES:(s + 1) * LANES]
            outs.append(ys * cos + _rotate_half(ys, half) * sin)
        return jnp.concatenate(outs, axis=1)

    for c in range(IN_W // PROJ_CHUNK):
        c0 = c * PROJ_CHUNK
        zc = jnp.dot(hn, w_ref[:, c0:c0 + PROJ_CHUNK], preferred_element_type=F32)
        if c0 == OFF_QA:
            zc = rope(head_rms(zc, gq_ref[...]), ca_ref[...], sa_ref[...], ROPE_ROT_DIM // 2)
            zc = zc * (DA_HEAD_DIM ** -0.5 * LOG2_E)
        elif c0 == OFF_KA:
            zc = rope(head_rms(zc, gk_ref[...]), ca_ref[...], sa_ref[...], ROPE_ROT_DIM // 2)
        elif c0 == OFF_VA:
            zt = zc.T.astype(BF16)
            extra = lax.broadcasted_iota(jnp.int32, (DA_VT_ROWS - DA_V_DIM, zt.shape[1]), 0)
            for hd in range(DA_HEADS):
                vt_ref[0, hd, 0:DA_V_DIM, :] = zt[hd * DA_V_DIM:(hd + 1) * DA_V_DIM, :]
                vt_ref[0, hd, DA_V_DIM:DA_VT_ROWS, :] = jnp.where(extra == 0, 1.0, 0.0).astype(BF16)
        elif c0 == OFF_QR:
            zc = rope(zc, cr_ref[...], sr_ref[...], RET_QK_DIM // 2)
            col = lax.broadcasted_iota(jnp.int32, zc.shape, 1)
            zc = jnp.where(col >= OFF_KR - OFF_QR, zc * (RET_QK_DIM ** -0.5), zc)
        elif c0 == OFF_GR:
            zc = zc * _sigmoid(zc)
        elif c0 >= OFF_GA:
            zc = _sigmoid(zc)
        z_ref[:, c0:c0 + PROJ_CHUNK] = zc.astype(BF16)


def _in_proj(h, g, w, gq, gk, gsum, tabs, tm):
    T = h.shape[0]
    row = lambda width: pl.BlockSpec((tm, width), lambda i: (i, 0))
    return pl.pallas_call(
        _in_proj_kernel,
        grid=(T // tm,),
        in_specs=[row(D_MODEL), _const_spec((1, D_MODEL)), _const_spec((D_MODEL, IN_W)),
                  _const_spec((1, PROJ_CHUNK)), _const_spec((1, PROJ_CHUNK)), _const_spec((256, 256)),
                  row(LANES), row(LANES), row(LANES), row(LANES)],
        out_specs=[row(IN_W), pl.BlockSpec((1, DA_HEADS, DA_VT_ROWS, tm), lambda i: (i, 0, 0, 0))],
        out_shape=[jax.ShapeDtypeStruct((T, IN_W), BF16),
                   jax.ShapeDtypeStruct((T // tm, DA_HEADS, DA_VT_ROWS, tm), BF16)],
        compiler_params=_cparams(("parallel",)),
        name="in_proj",
    )(h, g, w, gq, gk, gsum, *tabs)


def _diff_attn_kernel(bound_ref, q_ref, k_ref, vt_ref, lam_ref, subln_ref, o_ref, acc_sc, st_sc, *, tq,
                      lambda_init):
    qi = pl.program_id(2)
    q = q_ref[...]
    lane = lax.broadcasted_iota(jnp.int32, q.shape, 1)
    qs = (jnp.where(lane < DA_HEAD_DIM, q, jnp.zeros_like(q)),
          jnp.where(lane >= DA_HEAD_DIM, q, jnp.zeros_like(q)))
    acc_sc[...] = jnp.zeros(acc_sc.shape, F32)

    tqh = tq // 2
    streams = [(c, hf) for hf in range(2) for c in range(2)]
    qsub = [qs[c][hf * tqh:(hf + 1) * tqh, :] for c, hf in streams]

    def scores(ki):
        k = k_ref[pl.ds(pl.multiple_of(ki * tq, tq), tq), :]
        return [lax.dot_general(k, qq, (((1,), (1,)), ((), ())), preferred_element_type=F32) for qq in qsub]

    def stash(sts):
        for s in range(len(streams)):
            st_sc[s] = sts[s]

    bound = bound_ref[0, 0]

    def masked_scores(st, hf):
        key = lax.broadcasted_iota(jnp.int32, st.shape, 0)
        qry = lax.broadcasted_iota(jnp.int32, st.shape, 1) + hf * tqh
        return jnp.where(key <= qry, st, -jnp.inf)

    def consume_online(sts, ki, stats, masked):
        vt = vt_ref[ki, 0]
        out = []
        for s, (c, hf) in enumerate(streams):
            st = st_sc[s] if sts is None else sts[s]
            if masked:
                st = masked_scores(st, hf)
            cols = slice(hf * tqh, (hf + 1) * tqh)
            m_old = stats[s]
            m_new = jnp.maximum(m_old, jnp.max(st, axis=0, keepdims=True))
            alpha = jnp.exp2(m_old - m_new)
            pt = jnp.exp2((st - m_new).astype(BF16))
            acc_sc[c, :, cols] = alpha * acc_sc[c, :, cols] + jnp.dot(vt, pt, preferred_element_type=F32)
            out.append(m_new)
        return tuple(out)

    def run_fixed_ref():
        def accumulate(blocks, last_masked):
            sts = [scores(b) for b in blocks]
            tot = [None] * len(streams)
            den = [None] * len(streams)
            for u, (b, st4) in enumerate(zip(blocks, sts)):
                vt = vt_ref[b, 0, 0:DA_V_DIM, :]
                for s, (c, hf) in enumerate(streams):
                    st = st4[s]
                    if last_masked and u == len(blocks) - 1:
                        st = masked_scores(st, hf)
                    pt = jnp.exp2(st - bound)
                    part = jnp.sum(pt.reshape(tq // SUBLANES, SUBLANES, tqh), axis=0)
                    d = jnp.dot(vt, pt.astype(BF16), preferred_element_type=F32)
                    tot[s] = d if tot[s] is None else tot[s] + d
                    den[s] = part if den[s] is None else den[s] + part
            for s, (c, hf) in enumerate(streams):
                cols = slice(hf * tqh, (hf + 1) * tqh)
                acc_sc[c, 0:DA_V_DIM, cols] += tot[s]
                acc_sc[c, DA_V_DIM:DA_V_DIM + 1, cols] += jnp.sum(den[s], axis=0, keepdims=True)

        def group(j, carry):
            accumulate([GROUP * j + u for u in range(GROUP)], False)
            return carry

        ngroups = qi // GROUP
        lax.fori_loop(0, ngroups, group, 0)
        for rest in range(GROUP):
            pl.when(qi - GROUP * ngroups == rest)(
                functools.partial(accumulate, [qi - rest + u for u in range(rest + 1)], True))

    def run_online():
        npairs = qi // 2
        stash(scores(0))

        def pair(j, stats):
            b0 = 2 * j
            odd = scores(b0 + 1)
            stats = consume_online(None, b0, stats, False)
            stash(scores(b0 + 2))
            return consume_online(odd, b0 + 1, stats, False)

        def leftover(stats):
            stats = consume_online(None, qi - 1, stats, False)
            stash(scores(qi))
            return stats

        init = (jnp.full((1, tqh), -jnp.inf, F32),) * len(streams)
        stats = lax.fori_loop(0, npairs, pair, init)
        stats = lax.cond(qi - 2 * npairs == 1, leftover, lambda st: st, stats)
        consume_online(None, qi, stats, True)

    pl.when(bound <= FIXED_REF_MAX_BOUND)(run_fixed_ref)
    pl.when(bound > FIXED_REF_MAX_BOUND)(run_online)

    lv = lam_ref[...]
    lam = (jnp.exp(jnp.sum(lv[0:1] * lv[1:2], axis=1, keepdims=True))
           - jnp.exp(jnp.sum(lv[2:3] * lv[3:4], axis=1, keepdims=True)) + lambda_init)
    dv = DA_V_DIM
    ot = (acc_sc[0, 0:dv, :] / acc_sc[0, dv:dv + 1, :]
          - lam * (acc_sc[1, 0:dv, :] / acc_sc[1, dv:dv + 1, :]))
    yt = ot * lax.rsqrt(jnp.mean(ot * ot, axis=0, keepdims=True) + EPS)
    o_ref[...] = (yt.T * subln_ref[...] * (1.0 - lambda_init)).astype(BF16)


def _score_bound(gq, gk):
    norm = math.sqrt(DA_HEAD_DIM)
    return (1.01 * norm * norm * DA_HEAD_DIM ** -0.5 * LOG2_E
            * jnp.max(jnp.abs(gq)) * jnp.max(jnp.abs(gk))).astype(F32).reshape(1, 1)


def _diff_attn(z, vt, bound, lam_rows, subln, B, S, tq, lambda_init):
    T = B * S
    nq = S // tq
    kern = functools.partial(_diff_attn_kernel, tq=tq, lambda_init=lambda_init)
    return pl.pallas_call(
        kern,
        grid=(B, DA_HEADS, nq),
        in_specs=[pl.BlockSpec((1, 1), lambda b, h, i: (0, 0), memory_space=pltpu.SMEM),
                  pl.BlockSpec((tq, LANES), lambda b, h, i: (b * nq + i, OFF_QA // LANES + h)),
                  pl.BlockSpec((S, LANES), lambda b, h, i: (b, OFF_KA // LANES + h)),
                  pl.BlockSpec((nq, 1, DA_VT_ROWS, tq), lambda b, h, i: (b, h, 0, 0)),
                  _const_spec((4, DA_HEAD_DIM)), _const_spec((1, DA_V_DIM))],
        out_specs=pl.BlockSpec((tq, LANES), lambda b, h, i: (b * nq + i, h)),
        out_shape=jax.ShapeDtypeStruct((T, DA_HEADS * DA_V_DIM), BF16),
        scratch_shapes=[pltpu.VMEM((2, DA_VT_ROWS, tq), F32), pltpu.VMEM((4, tq, tq // 2), F32)],
        compiler_params=_cparams(("parallel", "parallel", "arbitrary")),
        name="diff_attn",
    )(bound, z, z, vt, lam_rows, subln)


def _retention_kernel(q_ref, k_ref, v_ref, g_ref, dec_ref, qdec_ref, kdec_ref, cdec_ref, o_ref, r_sc, o_sc, *,
                      tc):
    @pl.when(pl.program_id(1) == 0)
    def _():
        r_sc[...] = jnp.zeros(r_sc.shape, F32)

    C = RET_CHUNK
    lane = lax.broadcasted_iota(jnp.int32, (C, LANES), 1)
    rowi = lax.broadcasted_iota(jnp.int32, (C, LANES), 0)
    units = [(n, j) for n in range(tc // C) for j in range(RET_HEADS // 2)]
    rows = lambda n: slice(n * C, (n + 1) * C)
    cols = lambda j: slice(j * LANES, (j + 1) * LANES)
    vcol = lambda h: slice(h * RET_V_DIM, (h + 1) * RET_V_DIM)
    own = lambda hh: (lane >= RET_QK_DIM) if hh else (lane < RET_QK_DIM)
    s, kv = {}, {}
    for n, j in units:
        qp = q_ref[rows(n), cols(j)]
        kp = k_ref[rows(n), cols(j)]
        kdt = (kp.astype(F32) * kdec_ref[:, cols(j)]).T.astype(BF16)
        both = []
        for hh in range(2):
            h = 2 * j + hh
            s[n, h] = lax.dot_general(jnp.where(own(hh), qp, jnp.zeros_like(qp)), kp, (((1,), (1,)), ((), ())),
                                      preferred_element_type=F32)
            both.append(jnp.dot(kdt, v_ref[rows(n), vcol(h)], preferred_element_type=F32))
        kv[n, j] = jnp.where(rowi < RET_QK_DIM, both[0], both[1])
    for n, j in units:
        for hh in range(2):
            h = 2 * j + hh
            inner = (s[n, h] * dec_ref[h]).astype(BF16)
            o_sc[rows(n), vcol(h)] = jnp.dot(inner, v_ref[rows(n), vcol(h)], preferred_element_type=F32)
    state = [r_sc[j] for j in range(RET_HEADS // 2)]
    for n, j in units:
        qp = q_ref[rows(n), cols(j)]
        qd = (qp.astype(F32) * qdec_ref[:, cols(j)]).astype(BF16)
        r_b = state[j].astype(BF16)
        for hh in range(2):
            h = 2 * j + hh
            o_sc[rows(n), vcol(h)] += jnp.dot(jnp.where(own(hh), qd, jnp.zeros_like(qd)), r_b,
                                              preferred_element_type=F32)
        state[j] = state[j] * cdec_ref[j] + kv[n, j]
    for j in range(RET_HEADS // 2):
        r_sc[j] = state[j]
    for h in range(RET_HEADS):
        gs = slice(h * RET_V_DIM, (h + 1) * RET_V_DIM)
        o = o_sc[:, gs]
        oc = o - jnp.mean(o, axis=-1, keepdims=True)
        y = oc * lax.rsqrt(jnp.mean(oc * oc, axis=-1, keepdims=True) + EPS)
        o_ref[:, gs] = (y * g_ref[:, gs].astype(F32)).astype(BF16)


def _retention(z, consts, B, S, tc):
    T = B * S
    n = S // tc
    dec, qdec, kdec, cdec = consts
    kern = functools.partial(_retention_kernel, tc=tc)
    blk = lambda width, off: pl.BlockSpec((tc, width), lambda b, i: (b * n + i, off // width))
    return pl.pallas_call(
        kern,
        grid=(B, n),
        in_specs=[blk(256, OFF_QR), blk(256, OFF_KR), blk(512, OFF_VR), blk(512, OFF_GR),
                  _const_spec(dec.shape), _const_spec(qdec.shape), _const_spec(kdec.shape),
                  _const_spec(cdec.shape)],
        out_specs=pl.BlockSpec((tc, 512), lambda b, i: (b * n + i, 0)),
        out_shape=jax.ShapeDtypeStruct((T, RET_HEADS * RET_V_DIM), BF16),
        scratch_shapes=[pltpu.VMEM((RET_HEADS // 2, LANES, LANES), F32),
                        pltpu.VMEM((tc, RET_HEADS * RET_V_DIM), F32)],
        compiler_params=_cparams(("parallel", "arbitrary")),
        name="retention",
    )(z, z, z, z, dec, qdec, kdec, cdec)


def _retention_consts():
    C = RET_CHUNK
    log_gamma = jnp.log1p(-jnp.exp2(-5.0 - jnp.arange(RET_HEADS, dtype=F32)))
    idx = jnp.arange(C, dtype=F32)
    rel = idx[:, None] - idx[None, :]
    dec = jnp.where(rel[None] >= 0, jnp.exp(log_gamma[:, None, None] * jnp.maximum(rel, 0.0)[None]), 0.0)
    q_dec = jnp.exp(log_gamma[None, :] * (idx + 1.0)[:, None])
    k_dec = jnp.exp(log_gamma[None, :] * (C - 1.0 - idx)[:, None])
    qdec = jnp.repeat(q_dec, RET_QK_DIM, axis=1)
    kdec = jnp.repeat(k_dec, RET_QK_DIM, axis=1)
    chunk_decay = jnp.repeat(jnp.exp(log_gamma * C), RET_QK_DIM)
    cdec = jnp.broadcast_to(chunk_decay.reshape(RET_HEADS // 2, LANES, 1), (RET_HEADS // 2, LANES, LANES))
    return dec.astype(F32), qdec.astype(F32), kdec.astype(F32), cdec.astype(F32)


def _merge_kernel(ya_ref, yr_ref, ga_ref, gb_ref, h_ref, wa_ref, wr_ref, wo_ref, gf_ref, h_out, hn_out):
    half = ya_ref.shape[0] // 2
    parts = [slice(0, half), slice(half, 2 * half)]
    branch = [(jnp.dot(ya_ref[r, :], wa_ref[...], preferred_element_type=F32),
               jnp.dot(yr_ref[r, :], wr_ref[...], preferred_element_type=F32)) for r in parts]
    for r, (pa, pr) in zip(parts, branch):
        merged = ga_ref[r, :].astype(F32) * pa + gb_ref[r, :].astype(F32) * pr
        hnew = h_ref[r, :] + jnp.dot(merged.astype(BF16), wo_ref[...], preferred_element_type=F32)
        h_out[r, :] = hnew
        ms = jnp.mean(hnew * hnew, axis=-1, keepdims=True)
        hn_out[r, :] = (hnew * lax.rsqrt(ms + EPS) * gf_ref[...]).astype(BF16)


def _merge(ya, yr, z, h, wa, wr, wo, gf, tm):
    T = h.shape[0]
    row = lambda width, cb=0: pl.BlockSpec((tm, width), lambda i: (i, cb))
    return pl.pallas_call(
        _merge_kernel,
        grid=(T // tm,),
        in_specs=[row(512), row(512), row(D_MODEL, OFF_GA // D_MODEL), row(D_MODEL, OFF_GB // D_MODEL),
                  row(D_MODEL), _const_spec(wa.shape), _const_spec(wr.shape), _const_spec(wo.shape),
                  _const_spec((1, D_MODEL))],
        out_specs=[row(D_MODEL), row(D_MODEL)],
        out_shape=[jax.ShapeDtypeStruct((T, D_MODEL), F32), jax.ShapeDtypeStruct((T, D_MODEL), BF16)],
        compiler_params=_cparams(("parallel",)),
        name="merge_out",
    )(ya, yr, z, z, h, wa, wr, wo, gf)


def _swiglu_into(acc_sc, x, w1_ref, w3_ref, w2_ref):
    chunks = [slice(c * FF_CHUNK, (c + 1) * FF_CHUNK) for c in range(D_FF // FF_CHUNK)]
    up = lambda cs: (jnp.dot(x, w1_ref[:, cs], preferred_element_type=F32),
                     jnp.dot(x, w3_ref[:, cs], preferred_element_type=F32))
    uv = up(chunks[0])
    for c, cs in enumerate(chunks):
        nxt = up(chunks[c + 1]) if c + 1 < len(chunks) else None
        a = (uv[0] * _sigmoid(uv[0]) * uv[1]).astype(BF16)
        d = jnp.dot(a, w2_ref[cs, :], preferred_element_type=F32)
        if c == 0:
            acc_sc[...] = d
        else:
            acc_sc[...] += d
        uv = nxt


def _ffn_kernel(hn_ref, prev_ref, w1_ref, w3_ref, w2_ref, o_ref, acc_sc):
    _swiglu_into(acc_sc, hn_ref[...], w1_ref, w3_ref, w2_ref)
    o_ref[...] = prev_ref[...] + acc_sc[...]


def _ffn(hn, prev, w1, w3, w2, tm):
    T = hn.shape[0]
    row = lambda width: pl.BlockSpec((tm, width), lambda i: (i, 0))
    wspec = lambda w: pl.BlockSpec(w.shape, lambda i: (0, 0), pipeline_mode=pl.Buffered(1))
    return pl.pallas_call(
        _ffn_kernel,
        grid=(T // tm,),
        in_specs=[row(D_MODEL), row(D_MODEL), wspec(w1), wspec(w3), wspec(w2)],
        out_specs=row(D_MODEL),
        out_shape=jax.ShapeDtypeStruct((T, D_MODEL), F32),
        scratch_shapes=[pltpu.VMEM((tm, D_MODEL), F32)],
        input_output_aliases={1: 0},
        compiler_params=_cparams(("parallel",)),
        name="ffn",
    )(hn, prev, w1, w3, w2)


R_I1, R_I2, R_W1, R_W2, R_RANK1, R_RANK2 = range(6)


def _router_kernel(h_ref, g_ref, wr_ref, info_ref, cnt_ref, carry_sc):
    @pl.when(pl.program_id(0) == 0)
    def _():
        carry_sc[...] = jnp.zeros(carry_sc.shape, F32)

    x = h_ref[...]
    ms = jnp.mean(x * x, axis=-1, keepdims=True)
    hn = x * lax.rsqrt(ms + EPS) * g_ref[...]
    w = wr_ref[...]
    hn_hi, w_hi = hn.astype(BF16), w.astype(BF16)
    hn_lo, w_lo = (hn - hn_hi.astype(F32)).astype(BF16), (w - w_hi.astype(F32)).astype(BF16)
    logits = (jnp.dot(hn_hi, w_hi, preferred_element_type=F32) + jnp.dot(hn_hi, w_lo, preferred_element_type=F32)
              + jnp.dot(hn_lo, w_hi, preferred_element_type=F32))
    lane = lax.broadcasted_iota(jnp.int32, logits.shape, 1)
    logits = jnp.where(lane < N_EXPERTS, logits, -jnp.inf)
    m1 = jnp.max(logits, axis=1, keepdims=True)
    i1 = jnp.min(jnp.where(logits == m1, lane, LANES), axis=1, keepdims=True)
    rest = jnp.where(lane == i1, -jnp.inf, logits)
    m2 = jnp.max(rest, axis=1, keepdims=True)
    i2 = jnp.min(jnp.where(rest == m2, lane, LANES), axis=1, keepdims=True)
    e = jnp.exp(m2 - m1)
    w1 = 1.0 / (1.0 + e)
    w2 = e / (1.0 + e)
    sel = jnp.where(lane == i1, 1.0, jnp.where(lane == i2, 1.0, 0.0))
    tm = sel.shape[0]
    rr = lax.broadcasted_iota(jnp.int32, (tm, tm), 0)
    cc = lax.broadcasted_iota(jnp.int32, (tm, tm), 1)
    ltri = jnp.where(cc < rr, 1.0, 0.0).astype(BF16)
    before = jnp.dot(ltri, sel.astype(BF16), preferred_element_type=F32) + carry_sc[0:1, :]
    r1 = jnp.sum(jnp.where(lane == i1, before, 0.0), axis=1, keepdims=True)
    r2 = jnp.sum(jnp.where(lane == i2, before, 0.0), axis=1, keepdims=True)
    info = jnp.where(lane == R_I1, i1.astype(F32), 0.0)
    for ln, val in ((R_I2, i2.astype(F32)), (R_W1, w1), (R_W2, w2), (R_RANK1, r1), (R_RANK2, r2)):
        info = jnp.where(lane == ln, val, info)
    info_ref[...] = info
    carry_sc[...] = carry_sc[...] + jnp.sum(sel, axis=0, keepdims=True)
    cnt_ref[...] = carry_sc[...]


def _router(h, g, wr_pad, tm):
    T = h.shape[0]
    return pl.pallas_call(
        _router_kernel,
        grid=(T // tm,),
        in_specs=[pl.BlockSpec((tm, D_MODEL), lambda i: (i, 0)), _const_spec((1, D_MODEL)),
                  _const_spec((D_MODEL, LANES))],
        out_specs=[pl.BlockSpec((tm, LANES), lambda i: (i, 0)), _const_spec((8, LANES))],
        out_shape=[jax.ShapeDtypeStruct((T, LANES), F32), jax.ShapeDtypeStruct((8, LANES), F32)],
        scratch_shapes=[pltpu.VMEM((8, LANES), F32)],
        compiler_params=_cparams(("arbitrary",)),
        name="router",
    )(h, g, wr_pad)


def _dispatch_kernel(pad_lo_ref, pad_hi_ref, pos_ref, h_ref, xs_hbm, zero_sc, sem, pad_sem, *, tb):
    @pl.when(pl.program_id(0) == 0)
    def _():
        zero_sc[...] = jnp.zeros(zero_sc.shape, F32)
        zero_row = zero_sc.at[pl.ds(0, 1)]
        for e in range(N_EXPERTS):
            def fill(r, carry):
                pltpu.make_async_copy(zero_row, xs_hbm.at[pl.ds(r, 1)], pad_sem).start()
                return carry

            lax.fori_loop(pad_lo_ref[e], pad_hi_ref[e], fill, 0)
        for e in range(N_EXPERTS):
            def drain(r, carry):
                pltpu.make_async_copy(zero_row, xs_hbm.at[pl.ds(r, 1)], pad_sem).wait()
                return carry

            lax.fori_loop(pad_lo_ref[e], pad_hi_ref[e], drain, 0)

    def issue(i, carry):
        t0 = pl.multiple_of(i * SUBLANES, SUBLANES)
        for u in range(SUBLANES):
            for k in range(2):
                p = pos_ref[0, 0, k * tb + t0 + u]
                pltpu.make_async_copy(h_ref.at[pl.ds(t0 + u, 1)], xs_hbm.at[pl.ds(p, 1)], sem).start(priority=k)
        return carry

    lax.fori_loop(0, tb // SUBLANES, issue, 0)

    for k in range(2):
        pltpu.make_async_copy(h_ref, xs_hbm.at[pl.ds(0, tb)], sem).wait()


def _dispatch(pad_lo, pad_hi, pos_blocks, h, n_rows, tb):
    T = h.shape[0]
    return pl.pallas_call(
        functools.partial(_dispatch_kernel, tb=tb),
        grid_spec=pltpu.PrefetchScalarGridSpec(
            num_scalar_prefetch=2,
            grid=(T // tb,),
            in_specs=[pl.BlockSpec((1, 1, 2 * tb), lambda i, lo, hi: (i, 0, 0), memory_space=pltpu.SMEM),
                      pl.BlockSpec((tb, D_MODEL), lambda i, lo, hi: (i, 0))],
            out_specs=pl.BlockSpec(memory_space=pl.ANY),
            scratch_shapes=[pltpu.VMEM((SUBLANES, D_MODEL), F32), pltpu.SemaphoreType.DMA(()),
                            pltpu.SemaphoreType.DMA(())]),
        out_shape=jax.ShapeDtypeStruct((n_rows, D_MODEL), F32),
        compiler_params=pltpu.CompilerParams(dimension_semantics=("arbitrary",), has_side_effects=True),
        name="moe_dispatch",
    )(pad_lo, pad_hi, pos_blocks, h)


def _expert_kernel(te_ref, nu_ref, xs_ref, g_ref, w1_ref, w3_ref, w2_ref, y_ref, acc_sc):
    del te_ref
    i = pl.program_id(0)

    @pl.when(i < nu_ref[0])
    def _():
        x = xs_ref[...]
        ms = jnp.mean(x * x, axis=-1, keepdims=True)
        hn = (x * lax.rsqrt(ms + EPS) * g_ref[...]).astype(BF16)
        _swiglu_into(acc_sc, hn, w1_ref.at[0], w3_ref.at[0], w2_ref.at[0])
        y_ref[...] = acc_sc[...]

    @pl.when(i >= nu_ref[0])
    def _():
        y_ref[...] = jnp.zeros(y_ref.shape, F32)


def _experts(tile_expert, n_used, xs, g, w1, w3, w2, te_rows):
    P = xs.shape[0]
    row = pl.BlockSpec((te_rows, D_MODEL), lambda i, te, nu: (jnp.minimum(i, nu[0] - 1), 0))
    wspec = lambda w: pl.BlockSpec((1,) + w.shape[1:], lambda i, te, nu: (te[i], 0, 0))
    return pl.pallas_call(
        _expert_kernel,
        grid_spec=pltpu.PrefetchScalarGridSpec(
            num_scalar_prefetch=2,
            grid=(P // te_rows,),
            in_specs=[row, pl.BlockSpec((1, D_MODEL), lambda i, te, nu: (0, 0)), wspec(w1), wspec(w3), wspec(w2)],
            out_specs=pl.BlockSpec((te_rows, D_MODEL), lambda i, te, nu: (i, 0)),
            scratch_shapes=[pltpu.VMEM((te_rows, D_MODEL), F32)]),
        out_shape=jax.ShapeDtypeStruct((P, D_MODEL), F32),
        compiler_params=_cparams(("arbitrary",)),
        name="moe_experts",
    )(tile_expert, n_used, xs, g, w1, w3, w2)


def _combine_kernel(pos_ref, info_ref, h_ref, y_hbm, o_ref, g_sc, sem, *, tb):
    def issue(i, carry):
        t0 = pl.multiple_of(i * SUBLANES, SUBLANES)
        for u in range(SUBLANES):
            for k in range(2):
                p = pos_ref[0, 0, k * tb + t0 + u]
                pltpu.make_async_copy(y_hbm.at[pl.ds(p, 1)], g_sc.at[k, pl.ds(t0 + u, 1)], sem).start(priority=k)
        return carry

    lax.fori_loop(0, tb // SUBLANES, issue, 0)

    for k in range(2):
        pltpu.make_async_copy(y_hbm.at[pl.ds(0, tb)], g_sc.at[k], sem).wait()
    info = info_ref[...]
    o_ref[...] = h_ref[...] + info[:, R_W1:R_W1 + 1] * g_sc[0] + info[:, R_W2:R_W2 + 1] * g_sc[1]


def _combine(pos_blocks, info, h, y, tb):
    T = h.shape[0]
    row = lambda width: pl.BlockSpec((tb, width), lambda i: (i, 0))
    return pl.pallas_call(
        functools.partial(_combine_kernel, tb=tb),
        grid=(T // tb,),
        in_specs=[pl.BlockSpec((1, 1, 2 * tb), lambda i: (i, 0, 0), memory_space=pltpu.SMEM),
                  row(LANES), row(D_MODEL), pl.BlockSpec(memory_space=pl.ANY)],
        out_specs=row(D_MODEL),
        out_shape=jax.ShapeDtypeStruct((T, D_MODEL), F32),
        scratch_shapes=[pltpu.VMEM((2, tb, D_MODEL), F32), pltpu.SemaphoreType.DMA(())],
        compiler_params=_cparams(("arbitrary",)),
        name="moe_combine",
    )(pos_blocks, info, h, y)


def _pos_blocks(pos1, pos2, tb):
    nb = pos1.shape[0] // tb
    return jnp.concatenate([pos1.reshape(nb, 1, tb), pos2.reshape(nb, 1, tb)], axis=2)


def _moe(h, g, w_router, w1, w3, w2, tm):
    T = h.shape[0]
    te_rows = min(512, T)
    wr_pad = jnp.pad(w_router, ((0, 0), (0, LANES - N_EXPERTS)))
    info, cnt = _router(h, g, wr_pad, tm)
    counts = cnt[0, :N_EXPERTS].astype(jnp.int32)
    padded = ((counts + te_rows - 1) // te_rows) * te_rows
    ends = jnp.cumsum(padded)
    base = ends - padded
    i1 = info[:, R_I1].astype(jnp.int32)
    i2 = info[:, R_I2].astype(jnp.int32)
    pos1 = jnp.take(base, i1) + info[:, R_RANK1].astype(jnp.int32)
    pos2 = jnp.take(base, i2) + info[:, R_RANK2].astype(jnp.int32)
    n_tiles = (2 * T) // te_rows + N_EXPERTS
    tile_expert = jnp.minimum(jnp.searchsorted(ends // te_rows, jnp.arange(n_tiles), side="right"),
                              N_EXPERTS - 1).astype(jnp.int32)
    n_used = (ends[-1:] // te_rows).astype(jnp.int32)

    tb_d = min(1024, T)
    xs = _dispatch(base + counts, ends, _pos_blocks(pos1, pos2, tb_d), h, n_tiles * te_rows, tb_d)
    y = _experts(tile_expert, n_used, xs, g, w1, w3, w2, te_rows)
    tb_c = min(256, T)
    return _combine(_pos_blocks(pos1, pos2, tb_c), info, h, y, tb_c)


def kernel(x, positions, attn_norm, w_in, da_q_norm, da_k_norm, lam_q1, lam_k1, lam_q2, lam_k2, da_subln,
           w_proj_da, w_proj_ret, w_out, ffn_norm, dense_w1, dense_w3, dense_w2, w_router, moe_w1, moe_w3,
           moe_w2):
    B, S, D = x.shape
    assert D == D_MODEL and S % RET_CHUNK == 0
    T = B * S
    depth = attn_norm.shape[0]
    tm = min(512, S)
    tq = min(512, S)
    tc = min(512, S)

    pos_b = jnp.broadcast_to(positions.reshape(T, 1).astype(F32), (T, LANES))
    tabs = (_rope_tables(pos_b, *_rope_rows(ROPE_ROT_DIM, ROPE_THETA), tm)
            + _rope_tables(pos_b, *_rope_rows(RET_QK_DIM, RET_ROPE_THETA), tm))
    ret_consts = _retention_consts()
    gsum = jnp.asarray(np.kron(np.eye(4), np.ones((64, 64))), BF16)

    h = x.reshape(T, D)
    for layer in range(depth):
        lambda_init = 0.8 - 0.6 * math.exp(-0.3 * layer)
        gq = jnp.tile(da_q_norm[layer], PROJ_CHUNK // DA_HEAD_DIM).reshape(1, PROJ_CHUNK)
        gk = jnp.tile(da_k_norm[layer], PROJ_CHUNK // DA_HEAD_DIM).reshape(1, PROJ_CHUNK)
        z, vt = _in_proj(h, attn_norm[layer].reshape(1, D), w_in[layer].astype(BF16), gq, gk, gsum, tabs, tm)
        lam_rows = jnp.stack([lam_q1[layer], lam_k1[layer], lam_q2[layer], lam_k2[layer]])
        ya = _diff_attn(z, vt, _score_bound(da_q_norm[layer], da_k_norm[layer]), lam_rows,
                        da_subln[layer].reshape(1, DA_V_DIM), B, S, tq, lambda_init)
        yr = _retention(z, ret_consts, B, S, tc)
        h, hn = _merge(ya, yr, z, h, w_proj_da[layer].astype(BF16), w_proj_ret[layer].astype(BF16),
                       w_out[layer].astype(BF16), ffn_norm[layer].reshape(1, D), tm)
        j = layer // 2
        if layer % 2 == 0:
            h = _ffn(hn, h, dense_w1[j].astype(BF16), dense_w3[j].astype(BF16), dense_w2[j].astype(BF16), tm)
        else:
            h = _moe(h, ffn_norm[layer].reshape(1, D), w_router[j], moe_w1[j].astype(BF16),
                     moe_w3[j].astype(BF16), moe_w2[j].astype(BF16), tm)
    return h.reshape(B, S, D)
```

```python
import functools
import math

import jax
import jax.numpy as jnp
import numpy as np
from jax import lax
from jax.experimental import pallas as pl
from jax.experimental.pallas import tpu as pltpu

F32 = jnp.float32
BF16 = jnp.bfloat16

D_MODEL = 1024
DA_HEADS = 4
DA_HEAD_DIM = 64
DA_V_DIM = 128
ROPE_THETA = 500000.0
ROPE_ROT_DIM = 16
RET_HEADS = 4
RET_QK_DIM = 64
RET_V_DIM = 128
RET_CHUNK = 128
RET_ROPE_THETA = 10000.0
D_FF = 2816
N_EXPERTS = 8
EPS = 1e-6
LOG2_E = math.log2(math.e)
DA_VT_ROWS = DA_V_DIM + 16
FIXED_REF_MAX_BOUND = 60.0
GROUP = 4

LANES = 128
SUBLANES = 8
OFF_QA, OFF_KA, OFF_VA = 0, 512, 1024
OFF_QR, OFF_KR, OFF_VR, OFF_GR = 1536, 1792, 2048, 2560
OFF_GA, OFF_GB = 3072, 4096
IN_W = 5120
PROJ_CHUNK = 512
FF_CHUNK = 256
VMEM_LIMIT = 56 * 1024 * 1024


def _cparams(sem):
    return pltpu.CompilerParams(dimension_semantics=sem, vmem_limit_bytes=VMEM_LIMIT)


def _const_spec(shape):
    nd = len(shape)
    return pl.BlockSpec(shape, lambda *_: (0,) * nd)


def _rope_table_kernel(pos_ref, inv_ref, sgn_ref, cos_ref, sin_ref):
    ang = pos_ref[...] * inv_ref[...]
    cos_ref[...] = jnp.cos(ang)
    sin_ref[...] = jnp.sin(ang) * sgn_ref[...]


def _rope_tables(pos_b, inv_row, sgn_row, tm):
    T = pos_b.shape[0]
    row = pl.BlockSpec((tm, LANES), lambda i: (i, 0))
    return pl.pallas_call(
        _rope_table_kernel,
        grid=(T // tm,),
        in_specs=[row, _const_spec((1, LANES)), _const_spec((1, LANES))],
        out_specs=[row, row],
        out_shape=[jax.ShapeDtypeStruct((T, LANES), F32)] * 2,
        compiler_params=_cparams(("parallel",)),
        name="rope_tables",
    )(pos_b, inv_row, sgn_row)


def _rope_rows(rot_dim, theta):
    half = rot_dim // 2
    inv = 1.0 / (theta ** (jnp.arange(half, dtype=F32) / half))
    jj = np.arange(LANES) % 64
    inv_row = jnp.where(jj < rot_dim, inv[jj % half], 0.0).astype(F32)
    sgn_row = np.where(jj < half, -1.0, np.where(jj < rot_dim, 1.0, 0.0)).astype(np.float32)
    return inv_row.reshape(1, LANES), jnp.asarray(sgn_row).reshape(1, LANES)


def _rotate_half(x, half):
    jj = lax.broadcasted_iota(jnp.int32, x.shape, 1) % 64
    return jnp.where(jj < half, pltpu.roll(x, LANES - half, 1), pltpu.roll(x, half, 1))


def _sigmoid(x):
    return 1.0 / (1.0 + jnp.exp(-x))


def _in_proj_kernel(h_ref, g_ref, w_ref, gq_ref, gk_ref, gsum_ref, ca_ref, sa_ref, cr_ref, sr_ref, z_ref, vt_ref):
    half = h_ref.shape[0] // 2
    for r in (slice(0, half), slice(half, 2 * half)):
        _in_proj_rows(r, h_ref, g_ref, w_ref, gq_ref, gk_ref, gsum_ref, ca_ref, sa_ref, cr_ref, sr_ref, z_ref,
                      vt_ref)


def _in_proj_rows(r, h_ref, g_ref, w_ref, gq_ref, gk_ref, gsum_ref, ca_ref, sa_ref, cr_ref, sr_ref, z_ref, vt_ref):
    x = h_ref[r, :]
    ms = jnp.mean(x * x, axis=-1, keepdims=True)
    hn = (x * lax.rsqrt(ms + EPS) * g_ref[...]).astype(BF16)
    ca, sa, cr, sr = ca_ref[r, :], sa_ref[r, :], cr_ref[r, :], sr_ref[r, :]

    def head_rms(zc, gain):
        sq = (zc * zc).astype(BF16)
        outs = []
        for s in range(PROJ_CHUNK // 256):
            sl = slice(s * 256, (s + 1) * 256)
            tot = jnp.dot(sq[:, sl], gsum_ref[...], preferred_element_type=F32)
            outs.append(zc[:, sl] * lax.rsqrt(tot * (1.0 / 64.0) + EPS))
        return jnp.concatenate(outs, axis=1) * gain

    def rope(y, cos, sin, half):
        outs = []
        for s in range(y.shape[1] // LANES):
            ys = y[:, s * LANES:(s + 1) * LANES]
            outs.append(ys * cos + _rotate_half(ys, half) * sin)
        return jnp.concatenate(outs, axis=1)

    for c in range(IN_W // PROJ_CHUNK):
        c0 = c * PROJ_CHUNK
        zc = jnp.dot(hn, w_ref[:, c0:c0 + PROJ_CHUNK], preferred_element_type=F32)
        if c0 == OFF_QA:
            zc = rope(head_rms(zc, gq_ref[...]), ca, sa, ROPE_ROT_DIM // 2)
            zc = zc * (DA_HEAD_DIM ** -0.5 * LOG2_E)
        elif c0 == OFF_KA:
            zc = rope(head_rms(zc, gk_ref[...]), ca, sa, ROPE_ROT_DIM // 2)
        elif c0 == OFF_VA:
            zt = zc.T.astype(BF16)
            extra = lax.broadcasted_iota(jnp.int32, (DA_VT_ROWS - DA_V_DIM, zt.shape[1]), 0)
            for hd in range(DA_HEADS):
                vt_ref[0, hd, 0:DA_V_DIM, r] = zt[hd * DA_V_DIM:(hd + 1) * DA_V_DIM, :]
                vt_ref[0, hd, DA_V_DIM:DA_VT_ROWS, r] = jnp.where(extra == 0, 1.0, 0.0).astype(BF16)
        elif c0 == OFF_QR:
            zc = rope(zc, cr, sr, RET_QK_DIM // 2)
            col = lax.broadcasted_iota(jnp.int32, zc.shape, 1)
            zc = jnp.where(col >= OFF_KR - OFF_QR, zc * (RET_QK_DIM ** -0.5), zc)
        elif c0 == OFF_GR:
            zc = zc * _sigmoid(zc)
        elif c0 >= OFF_GA:
            zc = _sigmoid(zc)
        z_ref[r, c0:c0 + PROJ_CHUNK] = zc.astype(BF16)


def _in_proj(h, g, w, gq, gk, gsum, tabs, tm):
    T = h.shape[0]
    row = lambda width: pl.BlockSpec((tm, width), lambda i: (i, 0))
    return pl.pallas_call(
        _in_proj_kernel,
        grid=(T // tm,),
        in_specs=[row(D_MODEL), _const_spec((1, D_MODEL)), _const_spec((D_MODEL, IN_W)),
                  _const_spec((1, PROJ_CHUNK)), _const_spec((1, PROJ_CHUNK)), _const_spec((256, 256)),
                  row(LANES), row(LANES), row(LANES), row(LANES)],
        out_specs=[row(IN_W), pl.BlockSpec((1, DA_HEADS, DA_VT_ROWS, tm), lambda i: (i, 0, 0, 0))],
        out_shape=[jax.ShapeDtypeStruct((T, IN_W), BF16),
                   jax.ShapeDtypeStruct((T // tm, DA_HEADS, DA_VT_ROWS, tm), BF16)],
        compiler_params=_cparams(("parallel",)),
        name="in_proj",
    )(h, g, w, gq, gk, gsum, *tabs)


def _diff_attn_kernel(bound_ref, q_ref, k_ref, vt_ref, lam_ref, subln_ref, o_ref, acc_sc, st_sc, *, tq,
                      lambda_init):
    qi = pl.program_id(2)
    q = q_ref[...]
    lane = lax.broadcasted_iota(jnp.int32, q.shape, 1)
    qs = (jnp.where(lane < DA_HEAD_DIM, q, jnp.zeros_like(q)),
          jnp.where(lane >= DA_HEAD_DIM, q, jnp.zeros_like(q)))
    acc_sc[...] = jnp.zeros(acc_sc.shape, F32)

    tqh = tq // 2
    streams = [(c, hf) for hf in range(2) for c in range(2)]
    qsub = [qs[c][hf * tqh:(hf + 1) * tqh, :] for c, hf in streams]

    def scores(ki):
        k = k_ref[pl.ds(pl.multiple_of(ki * tq, tq), tq), :]
        return [lax.dot_general(k, qq, (((1,), (1,)), ((), ())), preferred_element_type=F32) for qq in qsub]

    def stash(sts):
        for s in range(len(streams)):
            st_sc[s] = sts[s]

    bound = bound_ref[0, 0]

    def masked_scores(st, hf):
        key = lax.broadcasted_iota(jnp.int32, st.shape, 0)
        qry = lax.broadcasted_iota(jnp.int32, st.shape, 1) + hf * tqh
        return jnp.where(key <= qry, st, -jnp.inf)

    def consume_online(sts, ki, stats, masked):
        vt = vt_ref[ki, 0]
        out = []
        for s, (c, hf) in enumerate(streams):
            st = st_sc[s] if sts is None else sts[s]
            if masked:
                st = masked_scores(st, hf)
            cols = slice(hf * tqh, (hf + 1) * tqh)
            m_old = stats[s]
            m_new = jnp.maximum(m_old, jnp.max(st, axis=0, keepdims=True))
            alpha = jnp.exp2(m_old - m_new)
            pt = jnp.exp2((st - m_new).astype(BF16))
            acc_sc[c, :, cols] = alpha * acc_sc[c, :, cols] + jnp.dot(vt, pt, preferred_element_type=F32)
            out.append(m_new)
        return tuple(out)

    def run_fixed_ref():
        def accumulate(blocks, last_masked):
            sts = [scores(b) for b in blocks]
            tot = [None] * len(streams)
            den = [None] * len(streams)
            for u, (b, st4) in enumerate(zip(blocks, sts)):
                vt = vt_ref[b, 0, 0:DA_V_DIM, :]
                for s, (c, hf) in enumerate(streams):
                    st = st4[s]
                    if last_masked and u == len(blocks) - 1:
                        st = masked_scores(st, hf)
                    pt = jnp.exp2(st - bound)
                    part = jnp.sum(pt.reshape(tq // SUBLANES, SUBLANES, tqh), axis=0)
                    d = jnp.dot(vt, pt.astype(BF16), preferred_element_type=F32)
                    tot[s] = d if tot[s] is None else tot[s] + d
                    den[s] = part if den[s] is None else den[s] + part
            for s, (c, hf) in enumerate(streams):
                cols = slice(hf * tqh, (hf + 1) * tqh)
                acc_sc[c, 0:DA_V_DIM, cols] += tot[s]
                acc_sc[c, DA_V_DIM:DA_V_DIM + 1, cols] += jnp.sum(den[s], axis=0, keepdims=True)

        def group(j, carry):
            accumulate([GROUP * j + u for u in range(GROUP)], False)
            return carry

        ngroups = qi // GROUP
        lax.fori_loop(0, ngroups, group, 0)
        for rest in range(GROUP):
            pl.when(qi - GROUP * ngroups == rest)(
                functools.partial(accumulate, [qi - rest + u for u in range(rest + 1)], True))

    def run_online():
        npairs = qi // 2
        stash(scores(0))

        def pair(j, stats):
            b0 = 2 * j
            odd = scores(b0 + 1)
            stats = consume_online(None, b0, stats, False)
            stash(scores(b0 + 2))
            return consume_online(odd, b0 + 1, stats, False)

        def leftover(stats):
            stats = consume_online(None, qi - 1, stats, False)
            stash(scores(qi))
            return stats

        init = (jnp.full((1, tqh), -jnp.inf, F32),) * len(streams)
        stats = lax.fori_loop(0, npairs, pair, init)
        stats = lax.cond(qi - 2 * npairs == 1, leftover, lambda st: st, stats)
        consume_online(None, qi, stats, True)

    pl.when(bound <= FIXED_REF_MAX_BOUND)(run_fixed_ref)
    pl.when(bound > FIXED_REF_MAX_BOUND)(run_online)

    lv = lam_ref[...]
    lam = (jnp.exp(jnp.sum(lv[0:1] * lv[1:2], axis=1, keepdims=True))
           - jnp.exp(jnp.sum(lv[2:3] * lv[3:4], axis=1, keepdims=True)) + lambda_init)
    dv = DA_V_DIM
    ot = (acc_sc[0, 0:dv, :] / acc_sc[0, dv:dv + 1, :]
          - lam * (acc_sc[1, 0:dv, :] / acc_sc[1, dv:dv + 1, :]))
    yt = ot * lax.rsqrt(jnp.mean(ot * ot, axis=0, keepdims=True) + EPS)
    o_ref[...] = (yt.T * subln_ref[...] * (1.0 - lambda_init)).astype(BF16)


def _score_bound(gq, gk):
    norm = math.sqrt(DA_HEAD_DIM)
    return (1.01 * norm * norm * DA_HEAD_DIM ** -0.5 * LOG2_E
            * jnp.max(jnp.abs(gq)) * jnp.max(jnp.abs(gk))).astype(F32).reshape(1, 1)


def _diff_attn(z, vt, bound, lam_rows, subln, B, S, tq, lambda_init):
    T = B * S
    nq = S // tq
    kern = functools.partial(_diff_attn_kernel, tq=tq, lambda_init=lambda_init)
    return pl.pallas_call(
        kern,
        grid=(B, DA_HEADS, nq),
        in_specs=[pl.BlockSpec((1, 1), lambda b, h, i: (0, 0), memory_space=pltpu.SMEM),
                  pl.BlockSpec((tq, LANES), lambda b, h, i: (b * nq + i, OFF_QA // LANES + h)),
                  pl.BlockSpec((S, LANES), lambda b, h, i: (b, OFF_KA // LANES + h)),
                  pl.BlockSpec((nq, 1, DA_VT_ROWS, tq), lambda b, h, i: (b, h, 0, 0)),
                  _const_spec((4, DA_HEAD_DIM)), _const_spec((1, DA_V_DIM))],
        out_specs=pl.BlockSpec((tq, LANES), lambda b, h, i: (b * nq + i, h)),
        out_shape=jax.ShapeDtypeStruct((T, DA_HEADS * DA_V_DIM), BF16),
        scratch_shapes=[pltpu.VMEM((2, DA_VT_ROWS, tq), F32), pltpu.VMEM((4, tq, tq // 2), F32)],
        compiler_params=_cparams(("parallel", "parallel", "arbitrary")),
        name="diff_attn",
    )(bound, z, z, vt, lam_rows, subln)


def _retention_kernel(q_ref, k_ref, v_ref, g_ref, dec_ref, qdec_ref, kdec_ref, cdec_ref, o_ref, r_sc, o_sc, *,
                      tc):
    @pl.when(pl.program_id(1) == 0)
    def _():
        r_sc[...] = jnp.zeros(r_sc.shape, F32)

    C = RET_CHUNK
    lane = lax.broadcasted_iota(jnp.int32, (C, LANES), 1)
    rowi = lax.broadcasted_iota(jnp.int32, (C, LANES), 0)
    units = [(n, j) for n in range(tc // C) for j in range(RET_HEADS // 2)]
    rows = lambda n: slice(n * C, (n + 1) * C)
    cols = lambda j: slice(j * LANES, (j + 1) * LANES)
    vcol = lambda h: slice(h * RET_V_DIM, (h + 1) * RET_V_DIM)
    own = lambda hh: (lane >= RET_QK_DIM) if hh else (lane < RET_QK_DIM)
    s, kv = {}, {}
    for n, j in units:
        qp = q_ref[rows(n), cols(j)]
        kp = k_ref[rows(n), cols(j)]
        kdt = (kp.astype(F32) * kdec_ref[:, cols(j)]).T.astype(BF16)
        both = []
        for hh in range(2):
            h = 2 * j + hh
            s[n, h] = lax.dot_general(jnp.where(own(hh), qp, jnp.zeros_like(qp)), kp, (((1,), (1,)), ((), ())),
                                      preferred_element_type=F32)
            both.append(jnp.dot(kdt, v_ref[rows(n), vcol(h)], preferred_element_type=F32))
        kv[n, j] = jnp.where(rowi < RET_QK_DIM, both[0], both[1])
    for n, j in units:
        for hh in range(2):
            h = 2 * j + hh
            inner = (s[n, h] * dec_ref[h]).astype(BF16)
            o_sc[rows(n), vcol(h)] = jnp.dot(inner, v_ref[rows(n), vcol(h)], preferred_element_type=F32)
    state = [r_sc[j] for j in range(RET_HEADS // 2)]
    for n, j in units:
        qp = q_ref[rows(n), cols(j)]
        qd = (qp.astype(F32) * qdec_ref[:, cols(j)]).astype(BF16)
        r_b = state[j].astype(BF16)
        for hh in range(2):
            h = 2 * j + hh
            o_sc[rows(n), vcol(h)] += jnp.dot(jnp.where(own(hh), qd, jnp.zeros_like(qd)), r_b,
                                              preferred_element_type=F32)
        state[j] = state[j] * cdec_ref[j] + kv[n, j]
    for j in range(RET_HEADS // 2):
        r_sc[j] = state[j]
    for h in range(RET_HEADS):
        gs = slice(h * RET_V_DIM, (h + 1) * RET_V_DIM)
        o = o_sc[:, gs]
        oc = o - jnp.mean(o, axis=-1, keepdims=True)
        y = oc * lax.rsqrt(jnp.mean(oc * oc, axis=-1, keepdims=True) + EPS)
        o_ref[:, gs] = (y * g_ref[:, gs].astype(F32)).astype(BF16)


def _retention(z, consts, B, S, tc):
    T = B * S
    n = S // tc
    dec, qdec, kdec, cdec = consts
    kern = functools.partial(_retention_kernel, tc=tc)
    blk = lambda width, off: pl.BlockSpec((tc, width), lambda b, i: (b * n + i, off // width))
    return pl.pallas_call(
        kern,
        grid=(B, n),
        in_specs=[blk(256, OFF_QR), blk(256, OFF_KR), blk(512, OFF_VR), blk(512, OFF_GR),
                  _const_spec(dec.shape), _const_spec(qdec.shape), _const_spec(kdec.shape),
                  _const_spec(cdec.shape)],
        out_specs=pl.BlockSpec((tc, 512), lambda b, i: (b * n + i, 0)),
        out_shape=jax.ShapeDtypeStruct((T, RET_HEADS * RET_V_DIM), BF16),
        scratch_shapes=[pltpu.VMEM((RET_HEADS // 2, LANES, LANES), F32),
                        pltpu.VMEM((tc, RET_HEADS * RET_V_DIM), F32)],
        compiler_params=_cparams(("parallel", "arbitrary")),
        name="retention",
    )(z, z, z, z, dec, qdec, kdec, cdec)


def _retention_consts():
    C = RET_CHUNK
    log_gamma = jnp.log1p(-jnp.exp2(-5.0 - jnp.arange(RET_HEADS, dtype=F32)))
    idx = jnp.arange(C, dtype=F32)
    rel = idx[:, None] - idx[None, :]
    dec = jnp.where(rel[None] >= 0, jnp.exp(log_gamma[:, None, None] * jnp.maximum(rel, 0.0)[None]), 0.0)
    q_dec = jnp.exp(log_gamma[None, :] * (idx + 1.0)[:, None])
    k_dec = jnp.exp(log_gamma[None, :] * (C - 1.0 - idx)[:, None])
    qdec = jnp.repeat(q_dec, RET_QK_DIM, axis=1)
    kdec = jnp.repeat(k_dec, RET_QK_DIM, axis=1)
    chunk_decay = jnp.repeat(jnp.exp(log_gamma * C), RET_QK_DIM)
    cdec = jnp.broadcast_to(chunk_decay.reshape(RET_HEADS // 2, LANES, 1), (RET_HEADS // 2, LANES, LANES))
    return dec.astype(F32), qdec.astype(F32), kdec.astype(F32), cdec.astype(F32)


R_I1, R_I2, R_W1, R_W2, R_RANK1, R_RANK2 = range(6)


def _route(hn, wr_ref, rows, info_ref, carry_sc):
    w = wr_ref[...]
    hn_hi, w_hi = hn.astype(BF16), w.astype(BF16)
    hn_lo, w_lo = (hn - hn_hi.astype(F32)).astype(BF16), (w - w_hi.astype(F32)).astype(BF16)
    logits = (jnp.dot(hn_hi, w_hi, preferred_element_type=F32) + jnp.dot(hn_hi, w_lo, preferred_element_type=F32)
              + jnp.dot(hn_lo, w_hi, preferred_element_type=F32))
    lane = lax.broadcasted_iota(jnp.int32, logits.shape, 1)
    logits = jnp.where(lane < N_EXPERTS, logits, -jnp.inf)
    m1 = jnp.max(logits, axis=1, keepdims=True)
    i1 = jnp.min(jnp.where(logits == m1, lane, LANES), axis=1, keepdims=True)
    rest = jnp.where(lane == i1, -jnp.inf, logits)
    m2 = jnp.max(rest, axis=1, keepdims=True)
    i2 = jnp.min(jnp.where(rest == m2, lane, LANES), axis=1, keepdims=True)
    e = jnp.exp(m2 - m1)
    w1 = 1.0 / (1.0 + e)
    w2 = e / (1.0 + e)
    sel = jnp.where(lane == i1, 1.0, jnp.where(lane == i2, 1.0, 0.0))
    n = sel.shape[0]
    rr = lax.broadcasted_iota(jnp.int32, (n, n), 0)
    cc = lax.broadcasted_iota(jnp.int32, (n, n), 1)
    ltri = jnp.where(cc < rr, 1.0, 0.0).astype(BF16)
    before = jnp.dot(ltri, sel.astype(BF16), preferred_element_type=F32) + carry_sc[0:1, :]
    r1 = jnp.sum(jnp.where(lane == i1, before, 0.0), axis=1, keepdims=True)
    r2 = jnp.sum(jnp.where(lane == i2, before, 0.0), axis=1, keepdims=True)
    info = jnp.where(lane == R_I1, i1.astype(F32), 0.0)
    for ln, val in ((R_I2, i2.astype(F32)), (R_W1, w1), (R_W2, w2), (R_RANK1, r1), (R_RANK2, r2)):
        info = jnp.where(lane == ln, val, info)
    info_ref[rows, :] = info
    carry_sc[...] = carry_sc[...] + jnp.sum(sel, axis=0, keepdims=True)


def _merge_kernel(*refs, route):
    if route:
        (ya_ref, yr_ref, ga_ref, gb_ref, h_ref, wa_ref, wr_ref, wo_ref, gf_ref, wrt_ref,
         h_out, info_out, cnt_out, carry_sc) = refs

        @pl.when(pl.program_id(0) == 0)
        def _():
            carry_sc[...] = jnp.zeros(carry_sc.shape, F32)
    else:
        ya_ref, yr_ref, ga_ref, gb_ref, h_ref, wa_ref, wr_ref, wo_ref, gf_ref, h_out, hn_out = refs
    half = ya_ref.shape[0] // 2
    parts = [slice(0, half), slice(half, 2 * half)]
    branch = [(jnp.dot(ya_ref[r, :], wa_ref[...], preferred_element_type=F32),
               jnp.dot(yr_ref[r, :], wr_ref[...], preferred_element_type=F32)) for r in parts]
    for r, (pa, pr) in zip(parts, branch):
        merged = ga_ref[r, :].astype(F32) * pa + gb_ref[r, :].astype(F32) * pr
        hnew = h_ref[r, :] + jnp.dot(merged.astype(BF16), wo_ref[...], preferred_element_type=F32)
        h_out[r, :] = hnew
        ms = jnp.mean(hnew * hnew, axis=-1, keepdims=True)
        hn = hnew * lax.rsqrt(ms + EPS) * gf_ref[...]
        if route:
            _route(hn, wrt_ref, r, info_out, carry_sc)
        else:
            hn_out[r, :] = hn.astype(BF16)
    if route:
        cnt_out[...] = carry_sc[...]


def _merge(ya, yr, z, h, wa, wr, wo, gf, tm, w_router=None):
    T = h.shape[0]
    route = w_router is not None
    row = lambda width, cb=0: pl.BlockSpec((tm, width), lambda i: (i, cb))
    in_specs = [row(512), row(512), row(D_MODEL, OFF_GA // D_MODEL), row(D_MODEL, OFF_GB // D_MODEL),
                row(D_MODEL), _const_spec(wa.shape), _const_spec(wr.shape), _const_spec(wo.shape),
                _const_spec((1, D_MODEL))]
    args = [ya, yr, z, z, h, wa, wr, wo, gf]
    if route:
        in_specs.append(_const_spec((D_MODEL, LANES)))
        args.append(jnp.pad(w_router, ((0, 0), (0, LANES - N_EXPERTS))))
        out_specs = [row(D_MODEL), row(LANES), _const_spec((SUBLANES, LANES))]
        out_shape = [jax.ShapeDtypeStruct((T, D_MODEL), F32), jax.ShapeDtypeStruct((T, LANES), F32),
                     jax.ShapeDtypeStruct((SUBLANES, LANES), F32)]
        scratch = [pltpu.VMEM((SUBLANES, LANES), F32)]
    else:
        out_specs = [row(D_MODEL), row(D_MODEL)]
        out_shape = [jax.ShapeDtypeStruct((T, D_MODEL), F32), jax.ShapeDtypeStruct((T, D_MODEL), BF16)]
        scratch = []
    return pl.pallas_call(
        functools.partial(_merge_kernel, route=route),
        grid=(T // tm,),
        in_specs=in_specs,
        out_specs=out_specs,
        out_shape=out_shape,
        scratch_shapes=scratch,
        compiler_params=_cparams(("arbitrary",) if route else ("parallel",)),
        name="merge_route" if route else "merge_out",
    )(*args)


def _swiglu_into(acc_sc, x, w1_ref, w3_ref, w2_ref):
    chunks = [slice(c * FF_CHUNK, (c + 1) * FF_CHUNK) for c in range(D_FF // FF_CHUNK)]
    up = lambda cs: (jnp.dot(x, w1_ref[:, cs], preferred_element_type=F32),
                     jnp.dot(x, w3_ref[:, cs], preferred_element_type=F32))
    uv = up(chunks[0])
    for c, cs in enumerate(chunks):
        nxt = up(chunks[c + 1]) if c + 1 < len(chunks) else None
        a = (uv[0] * _sigmoid(uv[0]) * uv[1]).astype(BF16)
        d = jnp.dot(a, w2_ref[cs, :], preferred_element_type=F32)
        if c == 0:
            acc_sc[...] = d
        else:
            acc_sc[...] += d
        uv = nxt


def _ffn_kernel(hn_ref, prev_ref, w1_ref, w3_ref, w2_ref, o_ref, acc_sc):
    _swiglu_into(acc_sc, hn_ref[...], w1_ref, w3_ref, w2_ref)
    o_ref[...] = prev_ref[...] + acc_sc[...]


def _ffn(hn, prev, w1, w3, w2, tm):
    T = hn.shape[0]
    row = lambda width: pl.BlockSpec((tm, width), lambda i: (i, 0))
    wspec = lambda w: pl.BlockSpec(w.shape, lambda i: (0, 0), pipeline_mode=pl.Buffered(1))
    return pl.pallas_call(
        _ffn_kernel,
        grid=(T // tm,),
        in_specs=[row(D_MODEL), row(D_MODEL), wspec(w1), wspec(w3), wspec(w2)],
        out_specs=row(D_MODEL),
        out_shape=jax.ShapeDtypeStruct((T, D_MODEL), F32),
        scratch_shapes=[pltpu.VMEM((tm, D_MODEL), F32)],
        input_output_aliases={1: 0},
        compiler_params=_cparams(("parallel",)),
        name="ffn",
    )(hn, prev, w1, w3, w2)


def _dispatch_kernel(pad_lo_ref, pad_hi_ref, pos_ref, h_ref, xs_hbm, zero_sc, sem, pad_sem, *, tb):
    @pl.when(pl.program_id(0) == 0)
    def _():
        zero_sc[...] = jnp.zeros(zero_sc.shape, F32)
        zero_row = zero_sc.at[pl.ds(0, 1)]
        for e in range(N_EXPERTS):
            def fill(r, carry):
                pltpu.make_async_copy(zero_row, xs_hbm.at[pl.ds(r, 1)], pad_sem).start()
                return carry

            lax.fori_loop(pad_lo_ref[e], pad_hi_ref[e], fill, 0)
        for e in range(N_EXPERTS):
            def drain(r, carry):
                pltpu.make_async_copy(zero_row, xs_hbm.at[pl.ds(r, 1)], pad_sem).wait()
                return carry

            lax.fori_loop(pad_lo_ref[e], pad_hi_ref[e], drain, 0)

    def issue(i, carry):
        t0 = pl.multiple_of(i * SUBLANES, SUBLANES)
        for u in range(SUBLANES):
            for k in range(2):
                p = pos_ref[0, 0, k * tb + t0 + u]
                pltpu.make_async_copy(h_ref.at[pl.ds(t0 + u, 1)], xs_hbm.at[pl.ds(p, 1)], sem).start(priority=k)
        return carry

    lax.fori_loop(0, tb // SUBLANES, issue, 0)

    for k in range(2):
        pltpu.make_async_copy(h_ref, xs_hbm.at[pl.ds(0, tb)], sem).wait()


def _dispatch(pad_lo, pad_hi, pos_blocks, h, n_rows, tb):
    T = h.shape[0]
    return pl.pallas_call(
        functools.partial(_dispatch_kernel, tb=tb),
        grid_spec=pltpu.PrefetchScalarGridSpec(
            num_scalar_prefetch=2,
            grid=(T // tb,),
            in_specs=[pl.BlockSpec((1, 1, 2 * tb), lambda i, lo, hi: (i, 0, 0), memory_space=pltpu.SMEM),
                      pl.BlockSpec((tb, D_MODEL), lambda i, lo, hi: (i, 0))],
            out_specs=pl.BlockSpec(memory_space=pl.ANY),
            scratch_shapes=[pltpu.VMEM((SUBLANES, D_MODEL), F32), pltpu.SemaphoreType.DMA(()),
                            pltpu.SemaphoreType.DMA(())]),
        out_shape=jax.ShapeDtypeStruct((n_rows, D_MODEL), F32),
        compiler_params=pltpu.CompilerParams(dimension_semantics=("arbitrary",), has_side_effects=True),
        name="moe_dispatch",
    )(pad_lo, pad_hi, pos_blocks, h)


def _expert_kernel(te_ref, nu_ref, xs_ref, g_ref, w1_ref, w3_ref, w2_ref, y_ref, acc_sc):
    del te_ref
    i = pl.program_id(0)

    @pl.when(i < nu_ref[0])
    def _():
        x = xs_ref[...]
        ms = jnp.mean(x * x, axis=-1, keepdims=True)
        hn = (x * lax.rsqrt(ms + EPS) * g_ref[...]).astype(BF16)
        _swiglu_into(acc_sc, hn, w1_ref.at[0], w3_ref.at[0], w2_ref.at[0])
        y_ref[...] = acc_sc[...]

    @pl.when(i >= nu_ref[0])
    def _():
        y_ref[...] = jnp.zeros(y_ref.shape, F32)


def _experts(tile_expert, n_used, xs, g, w1, w3, w2, te_rows):
    P = xs.shape[0]
    row = pl.BlockSpec((te_rows, D_MODEL), lambda i, te, nu: (jnp.minimum(i, nu[0] - 1), 0))
    wspec = lambda w: pl.BlockSpec((1,) + w.shape[1:], lambda i, te, nu: (te[i], 0, 0))
    return pl.pallas_call(
        _expert_kernel,
        grid_spec=pltpu.PrefetchScalarGridSpec(
            num_scalar_prefetch=2,
            grid=(P // te_rows,),
            in_specs=[row, pl.BlockSpec((1, D_MODEL), lambda i, te, nu: (0, 0)), wspec(w1), wspec(w3), wspec(w2)],
            out_specs=pl.BlockSpec((te_rows, D_MODEL), lambda i, te, nu: (i, 0)),
            scratch_shapes=[pltpu.VMEM((te_rows, D_MODEL), F32)]),
        out_shape=jax.ShapeDtypeStruct((P, D_MODEL), F32),
        compiler_params=_cparams(("arbitrary",)),
        name="moe_experts",
    )(tile_expert, n_used, xs, g, w1, w3, w2)


def _combine_kernel(pos_ref, info_ref, h_ref, y_hbm, o_ref, g_sc, sem, *, tb):
    def issue(i, carry):
        t0 = pl.multiple_of(i * SUBLANES, SUBLANES)
        for u in range(SUBLANES):
            for k in range(2):
                p = pos_ref[0, 0, k * tb + t0 + u]
                pltpu.make_async_copy(y_hbm.at[pl.ds(p, 1)], g_sc.at[k, pl.ds(t0 + u, 1)], sem).start(priority=k)
        return carry

    lax.fori_loop(0, tb // SUBLANES, issue, 0)

    for k in range(2):
        pltpu.make_async_copy(y_hbm.at[pl.ds(0, tb)], g_sc.at[k], sem).wait()
    info = info_ref[...]
    o_ref[...] = h_ref[...] + info[:, R_W1:R_W1 + 1] * g_sc[0] + info[:, R_W2:R_W2 + 1] * g_sc[1]


def _combine(pos_blocks, info, h, y, tb):
    T = h.shape[0]
    row = lambda width: pl.BlockSpec((tb, width), lambda i: (i, 0))
    return pl.pallas_call(
        functools.partial(_combine_kernel, tb=tb),
        grid=(T // tb,),
        in_specs=[pl.BlockSpec((1, 1, 2 * tb), lambda i: (i, 0, 0), memory_space=pltpu.SMEM),
                  row(LANES), row(D_MODEL), pl.BlockSpec(memory_space=pl.ANY)],
        out_specs=row(D_MODEL),
        out_shape=jax.ShapeDtypeStruct((T, D_MODEL), F32),
        scratch_shapes=[pltpu.VMEM((2, tb, D_MODEL), F32), pltpu.SemaphoreType.DMA(())],
        compiler_params=_cparams(("arbitrary",)),
        name="moe_combine",
    )(pos_blocks, info, h, y)


def _pos_blocks(pos1, pos2, tb):
    nb = pos1.shape[0] // tb
    return jnp.concatenate([pos1.reshape(nb, 1, tb), pos2.reshape(nb, 1, tb)], axis=2)


def _moe(h, info, cnt, g, w1, w3, w2):
    T = h.shape[0]
    te_rows = min(512, T)
    counts = cnt[0, :N_EXPERTS].astype(jnp.int32)
    padded = ((counts + te_rows - 1) // te_rows) * te_rows
    ends = jnp.cumsum(padded)
    base = ends - padded
    i1 = info[:, R_I1].astype(jnp.int32)
    i2 = info[:, R_I2].astype(jnp.int32)
    pos1 = jnp.take(base, i1) + info[:, R_RANK1].astype(jnp.int32)
    pos2 = jnp.take(base, i2) + info[:, R_RANK2].astype(jnp.int32)
    n_tiles = (2 * T) // te_rows + N_EXPERTS
    tile_expert = jnp.minimum(jnp.searchsorted(ends // te_rows, jnp.arange(n_tiles), side="right"),
                              N_EXPERTS - 1).astype(jnp.int32)
    n_used = (ends[-1:] // te_rows).astype(jnp.int32)

    tb_d = min(1024, T)
    xs = _dispatch(base + counts, ends, _pos_blocks(pos1, pos2, tb_d), h, n_tiles * te_rows, tb_d)
    y = _experts(tile_expert, n_used, xs, g, w1, w3, w2, te_rows)
    tb_c = min(256, T)
    return _combine(_pos_blocks(pos1, pos2, tb_c), info, h, y, tb_c)


def kernel(x, positions, attn_norm, w_in, da_q_norm, da_k_norm, lam_q1, lam_k1, lam_q2, lam_k2, da_subln,
           w_proj_da, w_proj_ret, w_out, ffn_norm, dense_w1, dense_w3, dense_w2, w_router, moe_w1, moe_w3,
           moe_w2):
    B, S, D = x.shape
    assert D == D_MODEL and S % RET_CHUNK == 0
    T = B * S
    depth = attn_norm.shape[0]
    tm = min(512, S)
    tq = min(512, S)
    tc = min(512, S)

    pos_b = jnp.broadcast_to(positions.reshape(T, 1).astype(F32), (T, LANES))
    tabs = (_rope_tables(pos_b, *_rope_rows(ROPE_ROT_DIM, ROPE_THETA), tm)
            + _rope_tables(pos_b, *_rope_rows(RET_QK_DIM, RET_ROPE_THETA), tm))
    ret_consts = _retention_consts()
    gsum = jnp.asarray(np.kron(np.eye(4), np.ones((64, 64))), BF16)

    h = x.reshape(T, D)
    for layer in range(depth):
        lambda_init = 0.8 - 0.6 * math.exp(-0.3 * layer)
        gq = jnp.tile(da_q_norm[layer], PROJ_CHUNK // DA_HEAD_DIM).reshape(1, PROJ_CHUNK)
        gk = jnp.tile(da_k_norm[layer], PROJ_CHUNK // DA_HEAD_DIM).reshape(1, PROJ_CHUNK)
        z, vt = _in_proj(h, attn_norm[layer].reshape(1, D), w_in[layer].astype(BF16), gq, gk, gsum, tabs, tm)
        lam_rows = jnp.stack([lam_q1[layer], lam_k1[layer], lam_q2[layer], lam_k2[layer]])
        ya = _diff_attn(z, vt, _score_bound(da_q_norm[layer], da_k_norm[layer]), lam_rows,
                        da_subln[layer].reshape(1, DA_V_DIM), B, S, tq, lambda_init)
        yr = _retention(z, ret_consts, B, S, tc)
        j = layer // 2
        merge_args = (ya, yr, z, h, w_proj_da[layer].astype(BF16), w_proj_ret[layer].astype(BF16),
                      w_out[layer].astype(BF16), ffn_norm[layer].reshape(1, D), tm)
        if layer % 2 == 0:
            h, hn = _merge(*merge_args)
            h = _ffn(hn, h, dense_w1[j].astype(BF16), dense_w3[j].astype(BF16), dense_w2[j].astype(BF16), tm)
        else:
            h, info, cnt = _merge(*merge_args, w_router=w_router[j])
            h = _moe(h, info, cnt, ffn_norm[layer].reshape(1, D), moe_w1[j].astype(BF16),
                     moe_w3[j].astype(BF16), moe_w2[j].astype(BF16))
    return h.reshape(B, S, D)
```

```python
import functools
import math

import jax
import jax.numpy as jnp
import numpy as np
from jax import lax
from jax.experimental import pallas as pl
from jax.experimental.pallas import tpu as pltpu

F32 = jnp.float32
BF16 = jnp.bfloat16

D_MODEL = 1024
DA_HEADS = 4
DA_HEAD_DIM = 64
DA_V_DIM = 128
ROPE_THETA = 500000.0
ROPE_ROT_DIM = 16
RET_HEADS = 4
RET_QK_DIM = 64
RET_V_DIM = 128
RET_CHUNK = 128
RET_ROPE_THETA = 10000.0
D_FF = 2816
N_EXPERTS = 8
EPS = 1e-6
LOG2_E = math.log2(math.e)
DA_VT_ROWS = DA_V_DIM + 16
FIXED_REF_MAX_BOUND = 60.0
GROUP = 4

LANES = 128
SUBLANES = 8
OFF_QA, OFF_KA, OFF_VA = 0, 512, 1024
OFF_QR, OFF_KR, OFF_VR, OFF_GR = 1536, 1792, 2048, 2560
OFF_GA, OFF_GB = 3072, 4096
IN_W = 5120
PROJ_CHUNK = 512
FF_CHUNK = 256
VMEM_LIMIT = 56 * 1024 * 1024


def _cparams(sem):
    return pltpu.CompilerParams(dimension_semantics=sem, vmem_limit_bytes=VMEM_LIMIT)


def _const_spec(shape):
    nd = len(shape)
    return pl.BlockSpec(shape, lambda *_: (0,) * nd)


def _rope_table_kernel(pos_ref, inv_ref, sgn_ref, cos_ref, sin_ref):
    ang = pos_ref[...] * inv_ref[...]
    cos_ref[...] = jnp.cos(ang)
    sin_ref[...] = jnp.sin(ang) * sgn_ref[...]


def _rope_tables(pos_b, inv_row, sgn_row, tm):
    T = pos_b.shape[0]
    row = pl.BlockSpec((tm, LANES), lambda i: (i, 0))
    return pl.pallas_call(
        _rope_table_kernel,
        grid=(T // tm,),
        in_specs=[row, _const_spec((1, LANES)), _const_spec((1, LANES))],
        out_specs=[row, row],
        out_shape=[jax.ShapeDtypeStruct((T, LANES), F32)] * 2,
        compiler_params=_cparams(("parallel",)),
        name="rope_tables",
    )(pos_b, inv_row, sgn_row)


def _rope_rows(rot_dim, theta):
    half = rot_dim // 2
    inv = 1.0 / (theta ** (jnp.arange(half, dtype=F32) / half))
    jj = np.arange(LANES) % 64
    inv_row = jnp.where(jj < rot_dim, inv[jj % half], 0.0).astype(F32)
    sgn_row = np.where(jj < half, -1.0, np.where(jj < rot_dim, 1.0, 0.0)).astype(np.float32)
    return inv_row.reshape(1, LANES), jnp.asarray(sgn_row).reshape(1, LANES)


def _rotate_half(x, half):
    jj = lax.broadcasted_iota(jnp.int32, x.shape, 1) % 64
    return jnp.where(jj < half, pltpu.roll(x, LANES - half, 1), pltpu.roll(x, half, 1))


def _sigmoid(x):
    return 1.0 / (1.0 + jnp.exp(-x))


def _in_proj_kernel(h_ref, g_ref, w_ref, gq_ref, gk_ref, gsum_ref, ca_ref, sa_ref, cr_ref, sr_ref, z_ref, vt_ref):
    half = h_ref.shape[0] // 2
    for r in (slice(0, half), slice(half, 2 * half)):
        _in_proj_rows(r, h_ref, g_ref, w_ref, gq_ref, gk_ref, gsum_ref, ca_ref, sa_ref, cr_ref, sr_ref, z_ref,
                      vt_ref)


def _in_proj_rows(r, h_ref, g_ref, w_ref, gq_ref, gk_ref, gsum_ref, ca_ref, sa_ref, cr_ref, sr_ref, z_ref, vt_ref):
    x = h_ref[r, :]
    ms = jnp.mean(x * x, axis=-1, keepdims=True)
    hn = (x * lax.rsqrt(ms + EPS) * g_ref[...]).astype(BF16)
    ca, sa, cr, sr = ca_ref[r, :], sa_ref[r, :], cr_ref[r, :], sr_ref[r, :]

    def head_rms(zc, gain):
        sq = (zc * zc).astype(BF16)
        outs = []
        for s in range(PROJ_CHUNK // 256):
            sl = slice(s * 256, (s + 1) * 256)
            tot = jnp.dot(sq[:, sl], gsum_ref[...], preferred_element_type=F32)
            outs.append(zc[:, sl] * lax.rsqrt(tot * (1.0 / 64.0) + EPS))
        return jnp.concatenate(outs, axis=1) * gain

    def rope(y, cos, sin, half):
        outs = []
        for s in range(y.shape[1] // LANES):
            ys = y[:, s * LANES:(s + 1) * LANES]
            outs.append(ys * cos + _rotate_half(ys, half) * sin)
        return jnp.concatenate(outs, axis=1)

    for c in range(IN_W // PROJ_CHUNK):
        c0 = c * PROJ_CHUNK
        zc = jnp.dot(hn, w_ref[:, c0:c0 + PROJ_CHUNK], preferred_element_type=F32)
        if c0 == OFF_QA:
            zc = rope(head_rms(zc, gq_ref[...]), ca, sa, ROPE_ROT_DIM // 2)
            zc = zc * (DA_HEAD_DIM ** -0.5 * LOG2_E)
        elif c0 == OFF_KA:
            zc = rope(head_rms(zc, gk_ref[...]), ca, sa, ROPE_ROT_DIM // 2)
        elif c0 == OFF_VA:
            zt = zc.T.astype(BF16)
            extra = lax.broadcasted_iota(jnp.int32, (DA_VT_ROWS - DA_V_DIM, zt.shape[1]), 0)
            for hd in range(DA_HEADS):
                vt_ref[0, hd, 0:DA_V_DIM, r] = zt[hd * DA_V_DIM:(hd + 1) * DA_V_DIM, :]
                vt_ref[0, hd, DA_V_DIM:DA_VT_ROWS, r] = jnp.where(extra == 0, 1.0, 0.0).astype(BF16)
        elif c0 == OFF_QR:
            zc = rope(zc, cr, sr, RET_QK_DIM // 2)
            col = lax.broadcasted_iota(jnp.int32, zc.shape, 1)
            zc = jnp.where(col >= OFF_KR - OFF_QR, zc * (RET_QK_DIM ** -0.5), zc)
        elif c0 == OFF_GR:
            zc = zc * _sigmoid(zc)
        elif c0 >= OFF_GA:
            zc = _sigmoid(zc)
        z_ref[r, c0:c0 + PROJ_CHUNK] = zc.astype(BF16)


def _in_proj(h, g, w, gq, gk, gsum, tabs, tm):
    T = h.shape[0]
    row = lambda width: pl.BlockSpec((tm, width), lambda i: (i, 0))
    return pl.pallas_call(
        _in_proj_kernel,
        grid=(T // tm,),
        in_specs=[row(D_MODEL), _const_spec((1, D_MODEL)), _const_spec((D_MODEL, IN_W)),
                  _const_spec((1, PROJ_CHUNK)), _const_spec((1, PROJ_CHUNK)), _const_spec((256, 256)),
                  row(LANES), row(LANES), row(LANES), row(LANES)],
        out_specs=[row(IN_W), pl.BlockSpec((1, DA_HEADS, DA_VT_ROWS, tm), lambda i: (i, 0, 0, 0))],
        out_shape=[jax.ShapeDtypeStruct((T, IN_W), BF16),
                   jax.ShapeDtypeStruct((T // tm, DA_HEADS, DA_VT_ROWS, tm), BF16)],
        compiler_params=_cparams(("parallel",)),
        name="in_proj",
    )(h, g, w, gq, gk, gsum, *tabs)


def _diff_attn_kernel(bound_ref, q_ref, k_ref, vt_ref, lam_ref, subln_ref, o_ref, acc_sc, st_sc, *, tq,
                      lambda_init):
    qi = pl.program_id(2)
    q = q_ref[...]
    lane = lax.broadcasted_iota(jnp.int32, q.shape, 1)
    qs = (jnp.where(lane < DA_HEAD_DIM, q, jnp.zeros_like(q)),
          jnp.where(lane >= DA_HEAD_DIM, q, jnp.zeros_like(q)))
    acc_sc[...] = jnp.zeros(acc_sc.shape, F32)

    tqh = tq // 2
    streams = [(c, hf) for hf in range(2) for c in range(2)]
    qsub = [qs[c][hf * tqh:(hf + 1) * tqh, :] for c, hf in streams]

    def scores(ki, diagonal=False):
        k = k_ref[pl.ds(pl.multiple_of(ki * tq, tq), tq), :]
        return [lax.dot_general(k[0:tqh] if diagonal and hf == 0 else k, qq, (((1,), (1,)), ((), ())),
                                preferred_element_type=F32) for qq, (c, hf) in zip(qsub, streams)]

    def stash(sts):
        for s in range(len(streams)):
            st_sc[s] = sts[s]

    bound = bound_ref[0, 0]

    def masked_scores(st, hf):
        key = lax.broadcasted_iota(jnp.int32, st.shape, 0)
        qry = lax.broadcasted_iota(jnp.int32, st.shape, 1) + hf * tqh
        return jnp.where(key <= qry, st, -jnp.inf)

    def consume_online(sts, ki, stats, masked):
        vt = vt_ref[ki, 0]
        out = []
        for s, (c, hf) in enumerate(streams):
            st = st_sc[s] if sts is None else sts[s]
            if masked:
                st = masked_scores(st, hf)
            cols = slice(hf * tqh, (hf + 1) * tqh)
            m_old = stats[s]
            m_new = jnp.maximum(m_old, jnp.max(st, axis=0, keepdims=True))
            alpha = jnp.exp2(m_old - m_new)
            pt = jnp.exp2((st - m_new).astype(BF16))
            acc_sc[c, :, cols] = alpha * acc_sc[c, :, cols] + jnp.dot(vt, pt, preferred_element_type=F32)
            out.append(m_new)
        return tuple(out)

    def run_fixed_ref():
        def accumulate(blocks, last_masked):
            diag = [last_masked and u == len(blocks) - 1 for u in range(len(blocks))]
            sts = [scores(b, dg) for b, dg in zip(blocks, diag)]
            tot = [None] * len(streams)
            den = [None] * len(streams)
            for b, st4, dg in zip(blocks, sts, diag):
                vt = vt_ref[b, 0, 0:DA_V_DIM, :]
                for s, (c, hf) in enumerate(streams):
                    st = masked_scores(st4[s], hf) if dg else st4[s]
                    keys = st.shape[0]
                    pt = jnp.exp2(st - bound)
                    part = jnp.sum(pt.reshape(keys // SUBLANES, SUBLANES, tqh), axis=0)
                    d = jnp.dot(vt[:, 0:keys], pt.astype(BF16), preferred_element_type=F32)
                    tot[s] = d if tot[s] is None else tot[s] + d
                    den[s] = part if den[s] is None else den[s] + part
            for s, (c, hf) in enumerate(streams):
                cols = slice(hf * tqh, (hf + 1) * tqh)
                acc_sc[c, 0:DA_V_DIM, cols] += tot[s]
                acc_sc[c, DA_V_DIM:DA_V_DIM + 1, cols] += jnp.sum(den[s], axis=0, keepdims=True)

        def group(j, carry):
            accumulate([GROUP * j + u for u in range(GROUP)], False)
            return carry

        ngroups = qi // GROUP
        lax.fori_loop(0, ngroups, group, 0)
        for rest in range(GROUP):
            pl.when(qi - GROUP * ngroups == rest)(
                functools.partial(accumulate, [qi - rest + u for u in range(rest + 1)], True))

    def run_online():
        npairs = qi // 2
        stash(scores(0))

        def pair(j, stats):
            b0 = 2 * j
            odd = scores(b0 + 1)
            stats = consume_online(None, b0, stats, False)
            stash(scores(b0 + 2))
            return consume_online(odd, b0 + 1, stats, False)

        def leftover(stats):
            stats = consume_online(None, qi - 1, stats, False)
            stash(scores(qi))
            return stats

        init = (jnp.full((1, tqh), -jnp.inf, F32),) * len(streams)
        stats = lax.fori_loop(0, npairs, pair, init)
        stats = lax.cond(qi - 2 * npairs == 1, leftover, lambda st: st, stats)
        consume_online(None, qi, stats, True)

    pl.when(bound <= FIXED_REF_MAX_BOUND)(run_fixed_ref)
    pl.when(bound > FIXED_REF_MAX_BOUND)(run_online)

    lv = lam_ref[...]
    lam = (jnp.exp(jnp.sum(lv[0:1] * lv[1:2], axis=1, keepdims=True))
           - jnp.exp(jnp.sum(lv[2:3] * lv[3:4], axis=1, keepdims=True)) + lambda_init)
    dv = DA_V_DIM
    ot = (acc_sc[0, 0:dv, :] / acc_sc[0, dv:dv + 1, :]
          - lam * (acc_sc[1, 0:dv, :] / acc_sc[1, dv:dv + 1, :]))
    yt = ot * lax.rsqrt(jnp.mean(ot * ot, axis=0, keepdims=True) + EPS)
    o_ref[...] = (yt.T * subln_ref[...] * (1.0 - lambda_init)).astype(BF16)


def _score_bound(gq, gk):
    norm = math.sqrt(DA_HEAD_DIM)
    return (1.01 * norm * norm * DA_HEAD_DIM ** -0.5 * LOG2_E
            * jnp.max(jnp.abs(gq)) * jnp.max(jnp.abs(gk))).astype(F32).reshape(1, 1)


def _diff_attn(z, vt, bound, lam_rows, subln, B, S, tq, lambda_init):
    T = B * S
    nq = S // tq
    kern = functools.partial(_diff_attn_kernel, tq=tq, lambda_init=lambda_init)
    return pl.pallas_call(
        kern,
        grid=(B, DA_HEADS, nq),
        in_specs=[pl.BlockSpec((1, 1), lambda b, h, i: (0, 0), memory_space=pltpu.SMEM),
                  pl.BlockSpec((tq, LANES), lambda b, h, i: (b * nq + i, OFF_QA // LANES + h)),
                  pl.BlockSpec((S, LANES), lambda b, h, i: (b, OFF_KA // LANES + h)),
                  pl.BlockSpec((nq, 1, DA_VT_ROWS, tq), lambda b, h, i: (b, h, 0, 0)),
                  _const_spec((4, DA_HEAD_DIM)), _const_spec((1, DA_V_DIM))],
        out_specs=pl.BlockSpec((tq, LANES), lambda b, h, i: (b * nq + i, h)),
        out_shape=jax.ShapeDtypeStruct((T, DA_HEADS * DA_V_DIM), BF16),
        scratch_shapes=[pltpu.VMEM((2, DA_VT_ROWS, tq), F32), pltpu.VMEM((4, tq, tq // 2), F32)],
        compiler_params=_cparams(("parallel", "parallel", "arbitrary")),
        name="diff_attn",
    )(bound, z, z, vt, lam_rows, subln)


def _retention_kernel(q_ref, k_ref, v_ref, g_ref, dec_ref, qdec_ref, kdec_ref, cdec_ref, o_ref, r_sc, o_sc, *,
                      tc):
    @pl.when(pl.program_id(1) == 0)
    def _():
        r_sc[...] = jnp.zeros(r_sc.shape, F32)

    C = RET_CHUNK
    lane = lax.broadcasted_iota(jnp.int32, (C, LANES), 1)
    rowi = lax.broadcasted_iota(jnp.int32, (C, LANES), 0)
    units = [(n, j) for n in range(tc // C) for j in range(RET_HEADS // 2)]
    rows = lambda n: slice(n * C, (n + 1) * C)
    cols = lambda j: slice(j * LANES, (j + 1) * LANES)
    vcol = lambda h: slice(h * RET_V_DIM, (h + 1) * RET_V_DIM)
    own = lambda hh: (lane >= RET_QK_DIM) if hh else (lane < RET_QK_DIM)
    s, kv = {}, {}
    for n, j in units:
        qp = q_ref[rows(n), cols(j)]
        kp = k_ref[rows(n), cols(j)]
        kdt = (kp.astype(F32) * kdec_ref[:, cols(j)]).T.astype(BF16)
        both = []
        for hh in range(2):
            h = 2 * j + hh
            s[n, h] = lax.dot_general(jnp.where(own(hh), qp, jnp.zeros_like(qp)), kp, (((1,), (1,)), ((), ())),
                                      preferred_element_type=F32)
            both.append(jnp.dot(kdt, v_ref[rows(n), vcol(h)], preferred_element_type=F32))
        kv[n, j] = jnp.where(rowi < RET_QK_DIM, both[0], both[1])
    for n, j in units:
        for hh in range(2):
            h = 2 * j + hh
            inner = (s[n, h] * dec_ref[h]).astype(BF16)
            o_sc[rows(n), vcol(h)] = jnp.dot(inner, v_ref[rows(n), vcol(h)], preferred_element_type=F32)
    state = [r_sc[j] for j in range(RET_HEADS // 2)]
    for n, j in units:
        qp = q_ref[rows(n), cols(j)]
        qd = (qp.astype(F32) * qdec_ref[:, cols(j)]).astype(BF16)
        r_b = state[j].astype(BF16)
        for hh in range(2):
            h = 2 * j + hh
            o_sc[rows(n), vcol(h)] += jnp.dot(jnp.where(own(hh), qd, jnp.zeros_like(qd)), r_b,
                                              preferred_element_type=F32)
        state[j] = state[j] * cdec_ref[j] + kv[n, j]
    for j in range(RET_HEADS // 2):
        r_sc[j] = state[j]
    for h in range(RET_HEADS):
        gs = slice(h * RET_V_DIM, (h + 1) * RET_V_DIM)
        o = o_sc[:, gs]
        oc = o - jnp.mean(o, axis=-1, keepdims=True)
        y = oc * lax.rsqrt(jnp.mean(oc * oc, axis=-1, keepdims=True) + EPS)
        o_ref[:, gs] = (y * g_ref[:, gs].astype(F32)).astype(BF16)


def _retention(z, consts, B, S, tc):
    T = B * S
    n = S // tc
    dec, qdec, kdec, cdec = consts
    kern = functools.partial(_retention_kernel, tc=tc)
    blk = lambda width, off: pl.BlockSpec((tc, width), lambda b, i: (b * n + i, off // width))
    return pl.pallas_call(
        kern,
        grid=(B, n),
        in_specs=[blk(256, OFF_QR), blk(256, OFF_KR), blk(512, OFF_VR), blk(512, OFF_GR),
                  _const_spec(dec.shape), _const_spec(qdec.shape), _const_spec(kdec.shape),
                  _const_spec(cdec.shape)],
        out_specs=pl.BlockSpec((tc, 512), lambda b, i: (b * n + i, 0)),
        out_shape=jax.ShapeDtypeStruct((T, RET_HEADS * RET_V_DIM), BF16),
        scratch_shapes=[pltpu.VMEM((RET_HEADS // 2, LANES, LANES), F32),
                        pltpu.VMEM((tc, RET_HEADS * RET_V_DIM), F32)],
        compiler_params=_cparams(("parallel", "arbitrary")),
        name="retention",
    )(z, z, z, z, dec, qdec, kdec, cdec)


def _retention_consts():
    C = RET_CHUNK
    log_gamma = jnp.log1p(-jnp.exp2(-5.0 - jnp.arange(RET_HEADS, dtype=F32)))
    idx = jnp.arange(C, dtype=F32)
    rel = idx[:, None] - idx[None, :]
    dec = jnp.where(rel[None] >= 0, jnp.exp(log_gamma[:, None, None] * jnp.maximum(rel, 0.0)[None]), 0.0)
    q_dec = jnp.exp(log_gamma[None, :] * (idx + 1.0)[:, None])
    k_dec = jnp.exp(log_gamma[None, :] * (C - 1.0 - idx)[:, None])
    qdec = jnp.repeat(q_dec, RET_QK_DIM, axis=1)
    kdec = jnp.repeat(k_dec, RET_QK_DIM, axis=1)
    chunk_decay = jnp.repeat(jnp.exp(log_gamma * C), RET_QK_DIM)
    cdec = jnp.broadcast_to(chunk_decay.reshape(RET_HEADS // 2, LANES, 1), (RET_HEADS // 2, LANES, LANES))
    return dec.astype(F32), qdec.astype(F32), kdec.astype(F32), cdec.astype(F32)


R_I1, R_I2, R_W1, R_W2, R_RANK1, R_RANK2 = range(6)


def _route(hn, wr_ref, rows, info_ref, carry_sc):
    w = wr_ref[...]
    hn_hi, w_hi = hn.astype(BF16), w.astype(BF16)
    hn_lo, w_lo = (hn - hn_hi.astype(F32)).astype(BF16), (w - w_hi.astype(F32)).astype(BF16)
    logits = (jnp.dot(hn_hi, w_hi, preferred_element_type=F32) + jnp.dot(hn_hi, w_lo, preferred_element_type=F32)
              + jnp.dot(hn_lo, w_hi, preferred_element_type=F32))
    lane = lax.broadcasted_iota(jnp.int32, logits.shape, 1)
    logits = jnp.where(lane < N_EXPERTS, logits, -jnp.inf)
    m1 = jnp.max(logits, axis=1, keepdims=True)
    i1 = jnp.min(jnp.where(logits == m1, lane, LANES), axis=1, keepdims=True)
    rest = jnp.where(lane == i1, -jnp.inf, logits)
    m2 = jnp.max(rest, axis=1, keepdims=True)
    i2 = jnp.min(jnp.where(rest == m2, lane, LANES), axis=1, keepdims=True)
    e = jnp.exp(m2 - m1)
    w1 = 1.0 / (1.0 + e)
    w2 = e / (1.0 + e)
    sel = jnp.where(lane == i1, 1.0, jnp.where(lane == i2, 1.0, 0.0))
    n = sel.shape[0]
    rr = lax.broadcasted_iota(jnp.int32, (n, n), 0)
    cc = lax.broadcasted_iota(jnp.int32, (n, n), 1)
    ltri = jnp.where(cc < rr, 1.0, 0.0).astype(BF16)
    before = jnp.dot(ltri, sel.astype(BF16), preferred_element_type=F32) + carry_sc[0:1, :]
    r1 = jnp.sum(jnp.where(lane == i1, before, 0.0), axis=1, keepdims=True)
    r2 = jnp.sum(jnp.where(lane == i2, before, 0.0), axis=1, keepdims=True)
    info = jnp.where(lane == R_I1, i1.astype(F32), 0.0)
    for ln, val in ((R_I2, i2.astype(F32)), (R_W1, w1), (R_W2, w2), (R_RANK1, r1), (R_RANK2, r2)):
        info = jnp.where(lane == ln, val, info)
    info_ref[rows, :] = info
    carry_sc[...] = carry_sc[...] + jnp.sum(sel, axis=0, keepdims=True)


def _merge_kernel(*refs, route):
    if route:
        (ya_ref, yr_ref, ga_ref, gb_ref, h_ref, wa_ref, wr_ref, wo_ref, gf_ref, wrt_ref,
         h_out, info_out, cnt_out, carry_sc) = refs

        @pl.when(pl.program_id(0) == 0)
        def _():
            carry_sc[...] = jnp.zeros(carry_sc.shape, F32)
    else:
        ya_ref, yr_ref, ga_ref, gb_ref, h_ref, wa_ref, wr_ref, wo_ref, gf_ref, h_out, hn_out = refs
    half = ya_ref.shape[0] // 2
    parts = [slice(0, half), slice(half, 2 * half)]
    branch = [(jnp.dot(ya_ref[r, :], wa_ref[...], preferred_element_type=F32),
               jnp.dot(yr_ref[r, :], wr_ref[...], preferred_element_type=F32)) for r in parts]
    for r, (pa, pr) in zip(parts, branch):
        merged = ga_ref[r, :].astype(F32) * pa + gb_ref[r, :].astype(F32) * pr
        hnew = h_ref[r, :] + jnp.dot(merged.astype(BF16), wo_ref[...], preferred_element_type=F32)
        h_out[r, :] = hnew
        ms = jnp.mean(hnew * hnew, axis=-1, keepdims=True)
        hn = hnew * lax.rsqrt(ms + EPS) * gf_ref[...]
        if route:
            _route(hn, wrt_ref, r, info_out, carry_sc)
        else:
            hn_out[r, :] = hn.astype(BF16)
    if route:
        cnt_out[...] = carry_sc[...]


def _merge(ya, yr, z, h, wa, wr, wo, gf, tm, w_router=None):
    T = h.shape[0]
    route = w_router is not None
    row = lambda width, cb=0: pl.BlockSpec((tm, width), lambda i: (i, cb))
    in_specs = [row(512), row(512), row(D_MODEL, OFF_GA // D_MODEL), row(D_MODEL, OFF_GB // D_MODEL),
                row(D_MODEL), _const_spec(wa.shape), _const_spec(wr.shape), _const_spec(wo.shape),
                _const_spec((1, D_MODEL))]
    args = [ya, yr, z, z, h, wa, wr, wo, gf]
    if route:
        in_specs.append(_const_spec((D_MODEL, LANES)))
        args.append(jnp.pad(w_router, ((0, 0), (0, LANES - N_EXPERTS))))
        out_specs = [row(D_MODEL), row(LANES), _const_spec((SUBLANES, LANES))]
        out_shape = [jax.ShapeDtypeStruct((T, D_MODEL), F32), jax.ShapeDtypeStruct((T, LANES), F32),
                     jax.ShapeDtypeStruct((SUBLANES, LANES), F32)]
        scratch = [pltpu.VMEM((SUBLANES, LANES), F32)]
    else:
        out_specs = [row(D_MODEL), row(D_MODEL)]
        out_shape = [jax.ShapeDtypeStruct((T, D_MODEL), F32), jax.ShapeDtypeStruct((T, D_MODEL), BF16)]
        scratch = []
    return pl.pallas_call(
        functools.partial(_merge_kernel, route=route),
        grid=(T // tm,),
        in_specs=in_specs,
        out_specs=out_specs,
        out_shape=out_shape,
        scratch_shapes=scratch,
        compiler_params=_cparams(("arbitrary",) if route else ("parallel",)),
        name="merge_route" if route else "merge_out",
    )(*args)


def _swiglu_into(acc_sc, x, w1_ref, w3_ref, w2_ref):
    chunks = [slice(c * FF_CHUNK, (c + 1) * FF_CHUNK) for c in range(D_FF // FF_CHUNK)]
    up = lambda cs: (jnp.dot(x, w1_ref[:, cs], preferred_element_type=F32),
                     jnp.dot(x, w3_ref[:, cs], preferred_element_type=F32))
    uv = up(chunks[0])
    for c, cs in enumerate(chunks):
        nxt = up(chunks[c + 1]) if c + 1 < len(chunks) else None
        a = (uv[0] * _sigmoid(uv[0]) * uv[1]).astype(BF16)
        d = jnp.dot(a, w2_ref[cs, :], preferred_element_type=F32)
        if c == 0:
            acc_sc[...] = d
        else:
            acc_sc[...] += d
        uv = nxt


def _ffn_kernel(hn_ref, prev_ref, w1_ref, w3_ref, w2_ref, o_ref, acc_sc):
    _swiglu_into(acc_sc, hn_ref[...], w1_ref, w3_ref, w2_ref)
    o_ref[...] = prev_ref[...] + acc_sc[...]


def _ffn(hn, prev, w1, w3, w2, tm):
    T = hn.shape[0]
    row = lambda width: pl.BlockSpec((tm, width), lambda i: (i, 0))
    wspec = lambda w: pl.BlockSpec(w.shape, lambda i: (0, 0), pipeline_mode=pl.Buffered(1))
    return pl.pallas_call(
        _ffn_kernel,
        grid=(T // tm,),
        in_specs=[row(D_MODEL), row(D_MODEL), wspec(w1), wspec(w3), wspec(w2)],
        out_specs=row(D_MODEL),
        out_shape=jax.ShapeDtypeStruct((T, D_MODEL), F32),
        scratch_shapes=[pltpu.VMEM((tm, D_MODEL), F32)],
        input_output_aliases={1: 0},
        compiler_params=_cparams(("parallel",)),
        name="ffn",
    )(hn, prev, w1, w3, w2)


def _dispatch_kernel(pad_lo_ref, pad_hi_ref, pos_ref, h_ref, xs_hbm, zero_sc, sem, pad_sem, *, tb):
    @pl.when(pl.program_id(0) == 0)
    def _():
        zero_sc[...] = jnp.zeros(zero_sc.shape, F32)
        zero_row = zero_sc.at[pl.ds(0, 1)]
        for e in range(N_EXPERTS):
            def fill(r, carry):
                pltpu.make_async_copy(zero_row, xs_hbm.at[pl.ds(r, 1)], pad_sem).start()
                return carry

            lax.fori_loop(pad_lo_ref[e], pad_hi_ref[e], fill, 0)
        for e in range(N_EXPERTS):
            def drain(r, carry):
                pltpu.make_async_copy(zero_row, xs_hbm.at[pl.ds(r, 1)], pad_sem).wait()
                return carry

            lax.fori_loop(pad_lo_ref[e], pad_hi_ref[e], drain, 0)

    def issue(i, carry):
        t0 = pl.multiple_of(i * SUBLANES, SUBLANES)
        for u in range(SUBLANES):
            for k in range(2):
                p = pos_ref[0, 0, k * tb + t0 + u]
                pltpu.make_async_copy(h_ref.at[pl.ds(t0 + u, 1)], xs_hbm.at[pl.ds(p, 1)], sem).start(priority=k)
        return carry

    lax.fori_loop(0, tb // SUBLANES, issue, 0)

    for k in range(2):
        pltpu.make_async_copy(h_ref, xs_hbm.at[pl.ds(0, tb)], sem).wait()


def _dispatch(pad_lo, pad_hi, pos_blocks, h, n_rows, tb):
    T = h.shape[0]
    return pl.pallas_call(
        functools.partial(_dispatch_kernel, tb=tb),
        grid_spec=pltpu.PrefetchScalarGridSpec(
            num_scalar_prefetch=2,
            grid=(T // tb,),
            in_specs=[pl.BlockSpec((1, 1, 2 * tb), lambda i, lo, hi: (i, 0, 0), memory_space=pltpu.SMEM),
                      pl.BlockSpec((tb, D_MODEL), lambda i, lo, hi: (i, 0))],
            out_specs=pl.BlockSpec(memory_space=pl.ANY),
            scratch_shapes=[pltpu.VMEM((SUBLANES, D_MODEL), F32), pltpu.SemaphoreType.DMA(()),
                            pltpu.SemaphoreType.DMA(())]),
        out_shape=jax.ShapeDtypeStruct((n_rows, D_MODEL), F32),
        compiler_params=pltpu.CompilerParams(dimension_semantics=("arbitrary",), has_side_effects=True),
        name="moe_dispatch",
    )(pad_lo, pad_hi, pos_blocks, h)


def _expert_kernel(te_ref, nu_ref, xs_ref, g_ref, w1_ref, w3_ref, w2_ref, y_ref, acc_sc):
    del te_ref
    i = pl.program_id(0)

    @pl.when(i < nu_ref[0])
    def _():
        x = xs_ref[...]
        ms = jnp.mean(x * x, axis=-1, keepdims=True)
        hn = (x * lax.rsqrt(ms + EPS) * g_ref[...]).astype(BF16)
        _swiglu_into(acc_sc, hn, w1_ref.at[0], w3_ref.at[0], w2_ref.at[0])
        y_ref[...] = acc_sc[...]

    @pl.when(i >= nu_ref[0])
    def _():
        y_ref[...] = jnp.zeros(y_ref.shape, F32)


def _experts(tile_expert, n_used, xs, g, w1, w3, w2, te_rows):
    P = xs.shape[0]
    row = pl.BlockSpec((te_rows, D_MODEL), lambda i, te, nu: (jnp.minimum(i, nu[0] - 1), 0))
    wspec = lambda w: pl.BlockSpec((1,) + w.shape[1:], lambda i, te, nu: (te[i], 0, 0))
    return pl.pallas_call(
        _expert_kernel,
        grid_spec=pltpu.PrefetchScalarGridSpec(
            num_scalar_prefetch=2,
            grid=(P // te_rows,),
            in_specs=[row, pl.BlockSpec((1, D_MODEL), lambda i, te, nu: (0, 0)), wspec(w1), wspec(w3), wspec(w2)],
            out_specs=pl.BlockSpec((te_rows, D_MODEL), lambda i, te, nu: (i, 0)),
            scratch_shapes=[pltpu.VMEM((te_rows, D_MODEL), F32)]),
        out_shape=jax.ShapeDtypeStruct((P, D_MODEL), F32),
        compiler_params=_cparams(("arbitrary",)),
        name="moe_experts",
    )(tile_expert, n_used, xs, g, w1, w3, w2)


def _combine_kernel(pos_ref, info_ref, h_ref, y_hbm, o_ref, g_sc, sem, *, tb):
    def issue(i, carry):
        t0 = pl.multiple_of(i * SUBLANES, SUBLANES)
        for u in range(SUBLANES):
            for k in range(2):
                p = pos_ref[0, 0, k * tb + t0 + u]
                pltpu.make_async_copy(y_hbm.at[pl.ds(p, 1)], g_sc.at[k, pl.ds(t0 + u, 1)], sem).start(priority=k)
        return carry

    lax.fori_loop(0, tb // SUBLANES, issue, 0)

    for k in range(2):
        pltpu.make_async_copy(y_hbm.at[pl.ds(0, tb)], g_sc.at[k], sem).wait()
    info = info_ref[...]
    o_ref[...] = h_ref[...] + info[:, R_W1:R_W1 + 1] * g_sc[0] + info[:, R_W2:R_W2 + 1] * g_sc[1]


def _combine(pos_blocks, info, h, y, tb):
    T = h.shape[0]
    row = lambda width: pl.BlockSpec((tb, width), lambda i: (i, 0))
    return pl.pallas_call(
        functools.partial(_combine_kernel, tb=tb),
        grid=(T // tb,),
        in_specs=[pl.BlockSpec((1, 1, 2 * tb), lambda i: (i, 0, 0), memory_space=pltpu.SMEM),
                  row(LANES), row(D_MODEL), pl.BlockSpec(memory_space=pl.ANY)],
        out_specs=row(D_MODEL),
        out_shape=jax.ShapeDtypeStruct((T, D_MODEL), F32),
        scratch_shapes=[pltpu.VMEM((2, tb, D_MODEL), F32), pltpu.SemaphoreType.DMA(())],
        compiler_params=_cparams(("arbitrary",)),
        name="moe_combine",
    )(pos_blocks, info, h, y)


def _pos_blocks(pos1, pos2, tb):
    nb = pos1.shape[0] // tb
    return jnp.concatenate([pos1.reshape(nb, 1, tb), pos2.reshape(nb, 1, tb)], axis=2)


def _moe(h, info, cnt, g, w1, w3, w2):
    T = h.shape[0]
    te_rows = min(512, T)
    counts = cnt[0, :N_EXPERTS].astype(jnp.int32)
    padded = ((counts + te_rows - 1) // te_rows) * te_rows
    ends = jnp.cumsum(padded)
    base = ends - padded
    i1 = info[:, R_I1].astype(jnp.int32)
    i2 = info[:, R_I2].astype(jnp.int32)
    pos1 = jnp.take(base, i1) + info[:, R_RANK1].astype(jnp.int32)
    pos2 = jnp.take(base, i2) + info[:, R_RANK2].astype(jnp.int32)
    n_tiles = (2 * T) // te_rows + N_EXPERTS
    tile_expert = jnp.minimum(jnp.searchsorted(ends // te_rows, jnp.arange(n_tiles), side="right"),
                              N_EXPERTS - 1).astype(jnp.int32)
    n_used = (ends[-1:] // te_rows).astype(jnp.int32)

    tb_d = min(1024, T)
    xs = _dispatch(base + counts, ends, _pos_blocks(pos1, pos2, tb_d), h, n_tiles * te_rows, tb_d)
    y = _experts(tile_expert, n_used, xs, g, w1, w3, w2, te_rows)
    tb_c = min(512, T)
    return _combine(_pos_blocks(pos1, pos2, tb_c), info, h, y, tb_c)


def kernel(x, positions, attn_norm, w_in, da_q_norm, da_k_norm, lam_q1, lam_k1, lam_q2, lam_k2, da_subln,
           w_proj_da, w_proj_ret, w_out, ffn_norm, dense_w1, dense_w3, dense_w2, w_router, moe_w1, moe_w3,
           moe_w2):
    B, S, D = x.shape
    assert D == D_MODEL and S % RET_CHUNK == 0
    T = B * S
    depth = attn_norm.shape[0]
    tm = min(512, S)
    tq = min(512, S)
    tc = min(512, S)

    pos_b = jnp.broadcast_to(positions.reshape(T, 1).astype(F32), (T, LANES))
    tabs = (_rope_tables(pos_b, *_rope_rows(ROPE_ROT_DIM, ROPE_THETA), tm)
            + _rope_tables(pos_b, *_rope_rows(RET_QK_DIM, RET_ROPE_THETA), tm))
    ret_consts = _retention_consts()
    gsum = jnp.asarray(np.kron(np.eye(4), np.ones((64, 64))), BF16)

    h = x.reshape(T, D)
    for layer in range(depth):
        lambda_init = 0.8 - 0.6 * math.exp(-0.3 * layer)
        gq = jnp.tile(da_q_norm[layer], PROJ_CHUNK // DA_HEAD_DIM).reshape(1, PROJ_CHUNK)
        gk = jnp.tile(da_k_norm[layer], PROJ_CHUNK // DA_HEAD_DIM).reshape(1, PROJ_CHUNK)
        z, vt = _in_proj(h, attn_norm[layer].reshape(1, D), w_in[layer].astype(BF16), gq, gk, gsum, tabs, tm)
        lam_rows = jnp.stack([lam_q1[layer], lam_k1[layer], lam_q2[layer], lam_k2[layer]])
        ya = _diff_attn(z, vt, _score_bound(da_q_norm[layer], da_k_norm[layer]), lam_rows,
                        da_subln[layer].reshape(1, DA_V_DIM), B, S, tq, lambda_init)
        yr = _retention(z, ret_consts, B, S, tc)
        j = layer // 2
        merge_args = (ya, yr, z, h, w_proj_da[layer].astype(BF16), w_proj_ret[layer].astype(BF16),
                      w_out[layer].astype(BF16), ffn_norm[layer].reshape(1, D), tm)
        if layer % 2 == 0:
            h, hn = _merge(*merge_args)
            h = _ffn(hn, h, dense_w1[j].astype(BF16), dense_w3[j].astype(BF16), dense_w2[j].astype(BF16), tm)
        else:
            h, info, cnt = _merge(*merge_args, w_router=w_router[j])
            h = _moe(h, info, cnt, ffn_norm[layer].reshape(1, D), moe_w1[j].astype(BF16),
                     moe_w3[j].astype(BF16), moe_w2[j].astype(BF16))
    return h.reshape(B, S, D)
```

```python
import functools
import math

import jax
import jax.numpy as jnp
import numpy as np
from jax import lax
from jax.experimental import pallas as pl
from jax.experimental.pallas import tpu as pltpu

F32 = jnp.float32
BF16 = jnp.bfloat16

D_MODEL = 1024
DA_HEADS = 4
DA_HEAD_DIM = 64
DA_V_DIM = 128
ROPE_THETA = 500000.0
ROPE_ROT_DIM = 16
RET_HEADS = 4
RET_QK_DIM = 64
RET_V_DIM = 128
RET_CHUNK = 128
RET_ROPE_THETA = 10000.0
D_FF = 2816
N_EXPERTS = 8
EPS = 1e-6
LOG2_E = math.log2(math.e)
DA_VT_ROWS = DA_V_DIM + 16
FIXED_REF_MAX_BOUND = 60.0
GROUP = 4

LANES = 128
SUBLANES = 8
OFF_QA, OFF_KA, OFF_VA = 0, 512, 1024
OFF_QR, OFF_KR, OFF_VR, OFF_GR = 1536, 1792, 2048, 2560
OFF_GA, OFF_GB = 3072, 4096
IN_W = 5120
PROJ_CHUNK = 512
FF_CHUNK = 256
VMEM_LIMIT = 56 * 1024 * 1024


def _cparams(sem):
    return pltpu.CompilerParams(dimension_semantics=sem, vmem_limit_bytes=VMEM_LIMIT)


def _const_spec(shape):
    nd = len(shape)
    return pl.BlockSpec(shape, lambda *_: (0,) * nd)


def _rope_table_kernel(pos_ref, inv_ref, sgn_ref, cos_ref, sin_ref):
    ang = pos_ref[...] * inv_ref[...]
    cos_ref[...] = jnp.cos(ang)
    sin_ref[...] = jnp.sin(ang) * sgn_ref[...]


def _rope_tables(pos_b, inv_row, sgn_row, tm):
    T = pos_b.shape[0]
    row = pl.BlockSpec((tm, LANES), lambda i: (i, 0))
    return pl.pallas_call(
        _rope_table_kernel,
        grid=(T // tm,),
        in_specs=[row, _const_spec((1, LANES)), _const_spec((1, LANES))],
        out_specs=[row, row],
        out_shape=[jax.ShapeDtypeStruct((T, LANES), F32)] * 2,
        compiler_params=_cparams(("parallel",)),
        name="rope_tables",
    )(pos_b, inv_row, sgn_row)


def _rope_rows(rot_dim, theta):
    half = rot_dim // 2
    inv = 1.0 / (theta ** (jnp.arange(half, dtype=F32) / half))
    jj = np.arange(LANES) % 64
    inv_row = jnp.where(jj < rot_dim, inv[jj % half], 0.0).astype(F32)
    sgn_row = np.where(jj < half, -1.0, np.where(jj < rot_dim, 1.0, 0.0)).astype(np.float32)
    return inv_row.reshape(1, LANES), jnp.asarray(sgn_row).reshape(1, LANES)


def _rotate_half(x, half):
    jj = lax.broadcasted_iota(jnp.int32, x.shape, 1) % 64
    return jnp.where(jj < half, pltpu.roll(x, LANES - half, 1), pltpu.roll(x, half, 1))


def _sigmoid(x):
    return 1.0 / (1.0 + jnp.exp(-x))


def _in_proj_kernel(h_ref, g_ref, w_ref, gq_ref, gk_ref, gsum_ref, ca_ref, sa_ref, cr_ref, sr_ref, z_ref, vt_ref):
    half = h_ref.shape[0] // 2
    for r in (slice(0, half), slice(half, 2 * half)):
        _in_proj_rows(r, h_ref, g_ref, w_ref, gq_ref, gk_ref, gsum_ref, ca_ref, sa_ref, cr_ref, sr_ref, z_ref,
                      vt_ref)


def _in_proj_rows(r, h_ref, g_ref, w_ref, gq_ref, gk_ref, gsum_ref, ca_ref, sa_ref, cr_ref, sr_ref, z_ref, vt_ref):
    x = h_ref[r, :]
    ms = jnp.mean(x * x, axis=-1, keepdims=True)
    hn = (x * lax.rsqrt(ms + EPS) * g_ref[...]).astype(BF16)
    ca, sa, cr, sr = ca_ref[r, :], sa_ref[r, :], cr_ref[r, :], sr_ref[r, :]

    def head_rms(zc, gain):
        sq = (zc * zc).astype(BF16)
        outs = []
        for s in range(PROJ_CHUNK // 256):
            sl = slice(s * 256, (s + 1) * 256)
            tot = jnp.dot(sq[:, sl], gsum_ref[...], preferred_element_type=F32)
            outs.append(zc[:, sl] * lax.rsqrt(tot * (1.0 / 64.0) + EPS))
        return jnp.concatenate(outs, axis=1) * gain

    def rope(y, cos, sin, half):
        outs = []
        for s in range(y.shape[1] // LANES):
            ys = y[:, s * LANES:(s + 1) * LANES]
            outs.append(ys * cos + _rotate_half(ys, half) * sin)
        return jnp.concatenate(outs, axis=1)

    for c in range(IN_W // PROJ_CHUNK):
        c0 = c * PROJ_CHUNK
        zc = jnp.dot(hn, w_ref[:, c0:c0 + PROJ_CHUNK], preferred_element_type=F32)
        if c0 == OFF_QA:
            zc = rope(head_rms(zc, gq_ref[...]), ca, sa, ROPE_ROT_DIM // 2)
            zc = zc * (DA_HEAD_DIM ** -0.5 * LOG2_E)
        elif c0 == OFF_KA:
            zc = rope(head_rms(zc, gk_ref[...]), ca, sa, ROPE_ROT_DIM // 2)
        elif c0 == OFF_VA:
            zt = zc.T.astype(BF16)
            extra = lax.broadcasted_iota(jnp.int32, (DA_VT_ROWS - DA_V_DIM, zt.shape[1]), 0)
            for hd in range(DA_HEADS):
                vt_ref[0, hd, 0:DA_V_DIM, r] = zt[hd * DA_V_DIM:(hd + 1) * DA_V_DIM, :]
                vt_ref[0, hd, DA_V_DIM:DA_VT_ROWS, r] = jnp.where(extra == 0, 1.0, 0.0).astype(BF16)
        elif c0 == OFF_QR:
            zc = rope(zc, cr, sr, RET_QK_DIM // 2)
            col = lax.broadcasted_iota(jnp.int32, zc.shape, 1)
            zc = jnp.where(col >= OFF_KR - OFF_QR, zc * (RET_QK_DIM ** -0.5), zc)
        elif c0 == OFF_GR:
            zc = zc * _sigmoid(zc)
        elif c0 >= OFF_GA:
            zc = _sigmoid(zc)
        z_ref[r, c0:c0 + PROJ_CHUNK] = zc.astype(BF16)


def _in_proj(h, g, w, gq, gk, gsum, tabs, tm):
    T = h.shape[0]
    row = lambda width: pl.BlockSpec((tm, width), lambda i: (i, 0))
    return pl.pallas_call(
        _in_proj_kernel,
        grid=(T // tm,),
        in_specs=[row(D_MODEL), _const_spec((1, D_MODEL)), _const_spec((D_MODEL, IN_W)),
                  _const_spec((1, PROJ_CHUNK)), _const_spec((1, PROJ_CHUNK)), _const_spec((256, 256)),
                  row(LANES), row(LANES), row(LANES), row(LANES)],
        out_specs=[row(IN_W), pl.BlockSpec((1, DA_HEADS, DA_VT_ROWS, tm), lambda i: (i, 0, 0, 0))],
        out_shape=[jax.ShapeDtypeStruct((T, IN_W), BF16),
                   jax.ShapeDtypeStruct((T // tm, DA_HEADS, DA_VT_ROWS, tm), BF16)],
        compiler_params=_cparams(("parallel",)),
        name="in_proj",
    )(h, g, w, gq, gk, gsum, *tabs)


def _diff_attn_kernel(bound_ref, q_ref, k_ref, vt_ref, lam_ref, subln_ref, o_ref, acc_sc, st_sc, *, tq,
                      lambda_init):
    qi = pl.program_id(2)
    q = q_ref[...]
    lane = lax.broadcasted_iota(jnp.int32, q.shape, 1)
    qs = (jnp.where(lane < DA_HEAD_DIM, q, jnp.zeros_like(q)),
          jnp.where(lane >= DA_HEAD_DIM, q, jnp.zeros_like(q)))
    acc_sc[...] = jnp.zeros(acc_sc.shape, F32)

    tqh = tq // 2
    streams = [(c, hf) for hf in range(2) for c in range(2)]
    qsub = [qs[c][hf * tqh:(hf + 1) * tqh, :] for c, hf in streams]

    def scores(ki, diagonal=False):
        k = k_ref[pl.ds(pl.multiple_of(ki * tq, tq), tq), :]
        return [lax.dot_general(k[0:tqh] if diagonal and hf == 0 else k, qq, (((1,), (1,)), ((), ())),
                                preferred_element_type=F32) for qq, (c, hf) in zip(qsub, streams)]

    def stash(sts):
        for s in range(len(streams)):
            st_sc[s] = sts[s]

    bound = bound_ref[0, 0]

    def masked_scores(st, hf):
        key = lax.broadcasted_iota(jnp.int32, st.shape, 0)
        qry = lax.broadcasted_iota(jnp.int32, st.shape, 1) + hf * tqh
        return jnp.where(key <= qry, st, -jnp.inf)

    def consume_online(sts, ki, stats, masked):
        vt = vt_ref[ki, 0]
        out = []
        for s, (c, hf) in enumerate(streams):
            st = st_sc[s] if sts is None else sts[s]
            if masked:
                st = masked_scores(st, hf)
            cols = slice(hf * tqh, (hf + 1) * tqh)
            m_old = stats[s]
            m_new = jnp.maximum(m_old, jnp.max(st, axis=0, keepdims=True))
            alpha = jnp.exp2(m_old - m_new)
            pt = jnp.exp2((st - m_new).astype(BF16))
            acc_sc[c, :, cols] = alpha * acc_sc[c, :, cols] + jnp.dot(vt, pt, preferred_element_type=F32)
            out.append(m_new)
        return tuple(out)

    def run_fixed_ref():
        def accumulate(blocks, last_masked):
            diag = [last_masked and u == len(blocks) - 1 for u in range(len(blocks))]
            sts = [scores(b, dg) for b, dg in zip(blocks, diag)]
            tot = [None] * len(streams)
            den = [None] * len(streams)
            for b, st4, dg in zip(blocks, sts, diag):
                vt = vt_ref[b, 0, 0:DA_V_DIM, :]
                for s, (c, hf) in enumerate(streams):
                    st = masked_scores(st4[s], hf) if dg else st4[s]
                    keys = st.shape[0]
                    pt = jnp.exp2(st - bound)
                    part = jnp.sum(pt.reshape(keys // SUBLANES, SUBLANES, tqh), axis=0)
                    d = jnp.dot(vt[:, 0:keys], pt.astype(BF16), preferred_element_type=F32)
                    tot[s] = d if tot[s] is None else tot[s] + d
                    den[s] = part if den[s] is None else den[s] + part
            for s, (c, hf) in enumerate(streams):
                cols = slice(hf * tqh, (hf + 1) * tqh)
                acc_sc[c, 0:DA_V_DIM, cols] += tot[s]
                acc_sc[c, DA_V_DIM:DA_V_DIM + 1, cols] += jnp.sum(den[s], axis=0, keepdims=True)

        def group(j, carry):
            accumulate([GROUP * j + u for u in range(GROUP)], False)
            return carry

        ngroups = qi // GROUP
        lax.fori_loop(0, ngroups, group, 0)
        for rest in range(GROUP):
            pl.when(qi - GROUP * ngroups == rest)(
                functools.partial(accumulate, [qi - rest + u for u in range(rest + 1)], True))

    def run_online():
        npairs = qi // 2
        stash(scores(0))

        def pair(j, stats):
            b0 = 2 * j
            odd = scores(b0 + 1)
            stats = consume_online(None, b0, stats, False)
            stash(scores(b0 + 2))
            return consume_online(odd, b0 + 1, stats, False)

        def leftover(stats):
            stats = consume_online(None, qi - 1, stats, False)
            stash(scores(qi))
            return stats

        init = (jnp.full((1, tqh), -jnp.inf, F32),) * len(streams)
        stats = lax.fori_loop(0, npairs, pair, init)
        stats = lax.cond(qi - 2 * npairs == 1, leftover, lambda st: st, stats)
        consume_online(None, qi, stats, True)

    pl.when(bound <= FIXED_REF_MAX_BOUND)(run_fixed_ref)
    pl.when(bound > FIXED_REF_MAX_BOUND)(run_online)

    lv = lam_ref[...]
    lam = (jnp.exp(jnp.sum(lv[0:1] * lv[1:2], axis=1, keepdims=True))
           - jnp.exp(jnp.sum(lv[2:3] * lv[3:4], axis=1, keepdims=True)) + lambda_init)
    dv = DA_V_DIM
    ot = (acc_sc[0, 0:dv, :] / acc_sc[0, dv:dv + 1, :]
          - lam * (acc_sc[1, 0:dv, :] / acc_sc[1, dv:dv + 1, :]))
    yt = ot * lax.rsqrt(jnp.mean(ot * ot, axis=0, keepdims=True) + EPS)
    o_ref[...] = (yt.T * subln_ref[...] * (1.0 - lambda_init)).astype(BF16)


def _score_bound(gq, gk):
    norm = math.sqrt(DA_HEAD_DIM)
    return (1.01 * norm * norm * DA_HEAD_DIM ** -0.5 * LOG2_E
            * jnp.max(jnp.abs(gq)) * jnp.max(jnp.abs(gk))).astype(F32).reshape(1, 1)


def _diff_attn(z, vt, bound, lam_rows, subln, B, S, tq, lambda_init):
    T = B * S
    nq = S // tq
    kern = functools.partial(_diff_attn_kernel, tq=tq, lambda_init=lambda_init)
    return pl.pallas_call(
        kern,
        grid=(B, DA_HEADS, nq),
        in_specs=[pl.BlockSpec((1, 1), lambda b, h, i: (0, 0), memory_space=pltpu.SMEM),
                  pl.BlockSpec((tq, LANES), lambda b, h, i: (b * nq + i, OFF_QA // LANES + h)),
                  pl.BlockSpec((S, LANES), lambda b, h, i: (b, OFF_KA // LANES + h)),
                  pl.BlockSpec((nq, 1, DA_VT_ROWS, tq), lambda b, h, i: (b, h, 0, 0)),
                  _const_spec((4, DA_HEAD_DIM)), _const_spec((1, DA_V_DIM))],
        out_specs=pl.BlockSpec((tq, LANES), lambda b, h, i: (b * nq + i, h)),
        out_shape=jax.ShapeDtypeStruct((T, DA_HEADS * DA_V_DIM), BF16),
        scratch_shapes=[pltpu.VMEM((2, DA_VT_ROWS, tq), F32), pltpu.VMEM((4, tq, tq // 2), F32)],
        compiler_params=_cparams(("parallel", "parallel", "arbitrary")),
        name="diff_attn",
    )(bound, z, z, vt, lam_rows, subln)


def _retention_kernel(q_ref, k_ref, v_ref, g_ref, dec_ref, qdec_ref, kdec_ref, cdec_ref, o_ref, r_sc, o_sc, *,
                      tc):
    @pl.when(pl.program_id(1) == 0)
    def _():
        r_sc[...] = jnp.zeros(r_sc.shape, F32)

    C = RET_CHUNK
    lane = lax.broadcasted_iota(jnp.int32, (C, LANES), 1)
    rowi = lax.broadcasted_iota(jnp.int32, (C, LANES), 0)
    units = [(n, j) for n in range(tc // C) for j in range(RET_HEADS // 2)]
    rows = lambda n: slice(n * C, (n + 1) * C)
    cols = lambda j: slice(j * LANES, (j + 1) * LANES)
    vcol = lambda h: slice(h * RET_V_DIM, (h + 1) * RET_V_DIM)
    own = lambda hh: (lane >= RET_QK_DIM) if hh else (lane < RET_QK_DIM)
    s, kv = {}, {}
    for n, j in units:
        qp = q_ref[rows(n), cols(j)]
        kp = k_ref[rows(n), cols(j)]
        kdt = (kp.astype(F32) * kdec_ref[:, cols(j)]).T.astype(BF16)
        both = []
        for hh in range(2):
            h = 2 * j + hh
            s[n, h] = lax.dot_general(jnp.where(own(hh), qp, jnp.zeros_like(qp)), kp, (((1,), (1,)), ((), ())),
                                      preferred_element_type=F32)
            both.append(jnp.dot(kdt, v_ref[rows(n), vcol(h)], preferred_element_type=F32))
        kv[n, j] = jnp.where(rowi < RET_QK_DIM, both[0], both[1])
    for n, j in units:
        for hh in range(2):
            h = 2 * j + hh
            inner = (s[n, h] * dec_ref[h]).astype(BF16)
            o_sc[rows(n), vcol(h)] = jnp.dot(inner, v_ref[rows(n), vcol(h)], preferred_element_type=F32)
    state = [r_sc[j] for j in range(RET_HEADS // 2)]
    for n, j in units:
        qp = q_ref[rows(n), cols(j)]
        qd = (qp.astype(F32) * qdec_ref[:, cols(j)]).astype(BF16)
        r_b = state[j].astype(BF16)
        for hh in range(2):
            h = 2 * j + hh
            o_sc[rows(n), vcol(h)] += jnp.dot(jnp.where(own(hh), qd, jnp.zeros_like(qd)), r_b,
                                              preferred_element_type=F32)
        state[j] = state[j] * cdec_ref[j] + kv[n, j]
    for j in range(RET_HEADS // 2):
        r_sc[j] = state[j]
    for h in range(RET_HEADS):
        gs = slice(h * RET_V_DIM, (h + 1) * RET_V_DIM)
        o = o_sc[:, gs]
        oc = o - jnp.mean(o, axis=-1, keepdims=True)
        y = oc * lax.rsqrt(jnp.mean(oc * oc, axis=-1, keepdims=True) + EPS)
        o_ref[:, gs] = (y * g_ref[:, gs].astype(F32)).astype(BF16)


def _retention(z, consts, B, S, tc):
    T = B * S
    n = S // tc
    dec, qdec, kdec, cdec = consts
    kern = functools.partial(_retention_kernel, tc=tc)
    blk = lambda width, off: pl.BlockSpec((tc, width), lambda b, i: (b * n + i, off // width))
    return pl.pallas_call(
        kern,
        grid=(B, n),
        in_specs=[blk(256, OFF_QR), blk(256, OFF_KR), blk(512, OFF_VR), blk(512, OFF_GR),
                  _const_spec(dec.shape), _const_spec(qdec.shape), _const_spec(kdec.shape),
                  _const_spec(cdec.shape)],
        out_specs=pl.BlockSpec((tc, 512), lambda b, i: (b * n + i, 0)),
        out_shape=jax.ShapeDtypeStruct((T, RET_HEADS * RET_V_DIM), BF16),
        scratch_shapes=[pltpu.VMEM((RET_HEADS // 2, LANES, LANES), F32),
                        pltpu.VMEM((tc, RET_HEADS * RET_V_DIM), F32)],
        compiler_params=_cparams(("parallel", "arbitrary")),
        name="retention",
    )(z, z, z, z, dec, qdec, kdec, cdec)


def _retention_consts():
    C = RET_CHUNK
    log_gamma = jnp.log1p(-jnp.exp2(-5.0 - jnp.arange(RET_HEADS, dtype=F32)))
    idx = jnp.arange(C, dtype=F32)
    rel = idx[:, None] - idx[None, :]
    dec = jnp.where(rel[None] >= 0, jnp.exp(log_gamma[:, None, None] * jnp.maximum(rel, 0.0)[None]), 0.0)
    q_dec = jnp.exp(log_gamma[None, :] * (idx + 1.0)[:, None])
    k_dec = jnp.exp(log_gamma[None, :] * (C - 1.0 - idx)[:, None])
    qdec = jnp.repeat(q_dec, RET_QK_DIM, axis=1)
    kdec = jnp.repeat(k_dec, RET_QK_DIM, axis=1)
    chunk_decay = jnp.repeat(jnp.exp(log_gamma * C), RET_QK_DIM)
    cdec = jnp.broadcast_to(chunk_decay.reshape(RET_HEADS // 2, LANES, 1), (RET_HEADS // 2, LANES, LANES))
    return dec.astype(F32), qdec.astype(F32), kdec.astype(F32), cdec.astype(F32)


R_I1, R_I2, R_W1, R_W2, R_RANK1, R_RANK2 = range(6)


def _route(hn, wr_ref, rows, info_ref, carry_sc):
    w = wr_ref[...]
    hn_hi, w_hi = hn.astype(BF16), w.astype(BF16)
    hn_lo, w_lo = (hn - hn_hi.astype(F32)).astype(BF16), (w - w_hi.astype(F32)).astype(BF16)
    logits = (jnp.dot(hn_hi, w_hi, preferred_element_type=F32) + jnp.dot(hn_hi, w_lo, preferred_element_type=F32)
              + jnp.dot(hn_lo, w_hi, preferred_element_type=F32))
    lane = lax.broadcasted_iota(jnp.int32, logits.shape, 1)
    logits = jnp.where(lane < N_EXPERTS, logits, -jnp.inf)
    m1 = jnp.max(logits, axis=1, keepdims=True)
    i1 = jnp.min(jnp.where(logits == m1, lane, LANES), axis=1, keepdims=True)
    rest = jnp.where(lane == i1, -jnp.inf, logits)
    m2 = jnp.max(rest, axis=1, keepdims=True)
    i2 = jnp.min(jnp.where(rest == m2, lane, LANES), axis=1, keepdims=True)
    e = jnp.exp(m2 - m1)
    w1 = 1.0 / (1.0 + e)
    w2 = e / (1.0 + e)
    sel = jnp.where(lane == i1, 1.0, jnp.where(lane == i2, 1.0, 0.0))
    n = sel.shape[0]
    rr = lax.broadcasted_iota(jnp.int32, (n, n), 0)
    cc = lax.broadcasted_iota(jnp.int32, (n, n), 1)
    ltri = jnp.where(cc < rr, 1.0, 0.0).astype(BF16)
    before = jnp.dot(ltri, sel.astype(BF16), preferred_element_type=F32) + carry_sc[0:1, :]
    r1 = jnp.sum(jnp.where(lane == i1, before, 0.0), axis=1, keepdims=True)
    r2 = jnp.sum(jnp.where(lane == i2, before, 0.0), axis=1, keepdims=True)
    info = jnp.where(lane == R_I1, i1.astype(F32), 0.0)
    for ln, val in ((R_I2, i2.astype(F32)), (R_W1, w1), (R_W2, w2), (R_RANK1, r1), (R_RANK2, r2)):
        info = jnp.where(lane == ln, val, info)
    info_ref[rows, :] = info
    carry_sc[...] = carry_sc[...] + jnp.sum(sel, axis=0, keepdims=True)


def _merge_kernel(*refs, route):
    if route:
        (ya_ref, yr_ref, ga_ref, gb_ref, h_ref, wa_ref, wr_ref, wo_ref, gf_ref, wrt_ref,
         h_out, info_out, cnt_out, carry_sc) = refs

        @pl.when(pl.program_id(0) == 0)
        def _():
            carry_sc[...] = jnp.zeros(carry_sc.shape, F32)
    else:
        ya_ref, yr_ref, ga_ref, gb_ref, h_ref, wa_ref, wr_ref, wo_ref, gf_ref, h_out, hn_out = refs
    half = ya_ref.shape[0] // 2
    parts = [slice(0, half), slice(half, 2 * half)]
    branch = [(jnp.dot(ya_ref[r, :], wa_ref[...], preferred_element_type=F32),
               jnp.dot(yr_ref[r, :], wr_ref[...], preferred_element_type=F32)) for r in parts]
    for r, (pa, pr) in zip(parts, branch):
        merged = ga_ref[r, :].astype(F32) * pa + gb_ref[r, :].astype(F32) * pr
        hnew = h_ref[r, :] + jnp.dot(merged.astype(BF16), wo_ref[...], preferred_element_type=F32)
        h_out[r, :] = hnew
        ms = jnp.mean(hnew * hnew, axis=-1, keepdims=True)
        hn = hnew * lax.rsqrt(ms + EPS) * gf_ref[...]
        if route:
            _route(hn, wrt_ref, r, info_out, carry_sc)
        else:
            hn_out[r, :] = hn.astype(BF16)
    if route:
        cnt_out[...] = carry_sc[...]


def _merge(ya, yr, z, h, wa, wr, wo, gf, tm, w_router=None):
    T = h.shape[0]
    route = w_router is not None
    row = lambda width, cb=0: pl.BlockSpec((tm, width), lambda i: (i, cb))
    in_specs = [row(512), row(512), row(D_MODEL, OFF_GA // D_MODEL), row(D_MODEL, OFF_GB // D_MODEL),
                row(D_MODEL), _const_spec(wa.shape), _const_spec(wr.shape), _const_spec(wo.shape),
                _const_spec((1, D_MODEL))]
    args = [ya, yr, z, z, h, wa, wr, wo, gf]
    if route:
        in_specs.append(_const_spec((D_MODEL, LANES)))
        args.append(jnp.pad(w_router, ((0, 0), (0, LANES - N_EXPERTS))))
        out_specs = [row(D_MODEL), row(LANES), _const_spec((SUBLANES, LANES))]
        out_shape = [jax.ShapeDtypeStruct((T, D_MODEL), F32), jax.ShapeDtypeStruct((T, LANES), F32),
                     jax.ShapeDtypeStruct((SUBLANES, LANES), F32)]
        scratch = [pltpu.VMEM((SUBLANES, LANES), F32)]
    else:
        out_specs = [row(D_MODEL), row(D_MODEL)]
        out_shape = [jax.ShapeDtypeStruct((T, D_MODEL), F32), jax.ShapeDtypeStruct((T, D_MODEL), BF16)]
        scratch = []
    return pl.pallas_call(
        functools.partial(_merge_kernel, route=route),
        grid=(T // tm,),
        in_specs=in_specs,
        out_specs=out_specs,
        out_shape=out_shape,
        scratch_shapes=scratch,
        compiler_params=_cparams(("arbitrary",) if route else ("parallel",)),
        name="merge_route" if route else "merge_out",
    )(*args)


def _swiglu_into(acc_sc, x, w1_ref, w3_ref, w2_ref):
    chunks = [slice(c * FF_CHUNK, (c + 1) * FF_CHUNK) for c in range(D_FF // FF_CHUNK)]
    up = lambda cs: (jnp.dot(x, w1_ref[:, cs], preferred_element_type=F32),
                     jnp.dot(x, w3_ref[:, cs], preferred_element_type=F32))
    uv = up(chunks[0])
    for c, cs in enumerate(chunks):
        nxt = up(chunks[c + 1]) if c + 1 < len(chunks) else None
        a = (uv[0] * _sigmoid(uv[0]) * uv[1]).astype(BF16)
        d = jnp.dot(a, w2_ref[cs, :], preferred_element_type=F32)
        if c == 0:
            acc_sc[...] = d
        else:
            acc_sc[...] += d
        uv = nxt


def _ffn_kernel(hn_ref, prev_ref, w1_ref, w3_ref, w2_ref, o_ref, acc_sc):
    _swiglu_into(acc_sc, hn_ref[...], w1_ref, w3_ref, w2_ref)
    o_ref[...] = prev_ref[...] + acc_sc[...]


def _ffn(hn, prev, w1, w3, w2, tm):
    T = hn.shape[0]
    row = lambda width: pl.BlockSpec((tm, width), lambda i: (i, 0))
    wspec = lambda w: pl.BlockSpec(w.shape, lambda i: (0, 0), pipeline_mode=pl.Buffered(1))
    return pl.pallas_call(
        _ffn_kernel,
        grid=(T // tm,),
        in_specs=[row(D_MODEL), row(D_MODEL), wspec(w1), wspec(w3), wspec(w2)],
        out_specs=row(D_MODEL),
        out_shape=jax.ShapeDtypeStruct((T, D_MODEL), F32),
        scratch_shapes=[pltpu.VMEM((tm, D_MODEL), F32)],
        input_output_aliases={1: 0},
        compiler_params=_cparams(("parallel",)),
        name="ffn",
    )(hn, prev, w1, w3, w2)


def _dispatch_kernel(pad_lo_ref, pad_hi_ref, pos_ref, h_ref, xs_hbm, zero_sc, sem, pad_sem, *, tb):
    @pl.when(pl.program_id(0) == 0)
    def _():
        zero_sc[...] = jnp.zeros(zero_sc.shape, F32)
        zero_row = zero_sc.at[pl.ds(0, 1)]
        for e in range(N_EXPERTS):
            def fill(r, carry):
                pltpu.make_async_copy(zero_row, xs_hbm.at[pl.ds(r, 1)], pad_sem).start()
                return carry

            lax.fori_loop(pad_lo_ref[e], pad_hi_ref[e], fill, 0)
        for e in range(N_EXPERTS):
            def drain(r, carry):
                pltpu.make_async_copy(zero_row, xs_hbm.at[pl.ds(r, 1)], pad_sem).wait()
                return carry

            lax.fori_loop(pad_lo_ref[e], pad_hi_ref[e], drain, 0)

    def issue(i, carry):
        t0 = pl.multiple_of(i * SUBLANES, SUBLANES)
        for u in range(SUBLANES):
            for k in range(2):
                p = pos_ref[0, 0, k * tb + t0 + u]
                pltpu.make_async_copy(h_ref.at[pl.ds(t0 + u, 1)], xs_hbm.at[pl.ds(p, 1)], sem).start(priority=k)
        return carry

    lax.fori_loop(0, tb // SUBLANES, issue, 0)

    for k in range(2):
        pltpu.make_async_copy(h_ref, xs_hbm.at[pl.ds(0, tb)], sem).wait()


def _dispatch(pad_lo, pad_hi, pos_blocks, h, n_rows, tb):
    T = h.shape[0]
    return pl.pallas_call(
        functools.partial(_dispatch_kernel, tb=tb),
        grid_spec=pltpu.PrefetchScalarGridSpec(
            num_scalar_prefetch=2,
            grid=(T // tb,),
            in_specs=[pl.BlockSpec((1, 1, 2 * tb), lambda i, lo, hi: (i, 0, 0), memory_space=pltpu.SMEM),
                      pl.BlockSpec((tb, D_MODEL), lambda i, lo, hi: (i, 0))],
            out_specs=pl.BlockSpec(memory_space=pl.ANY),
            scratch_shapes=[pltpu.VMEM((SUBLANES, D_MODEL), F32), pltpu.SemaphoreType.DMA(()),
                            pltpu.SemaphoreType.DMA(())]),
        out_shape=jax.ShapeDtypeStruct((n_rows, D_MODEL), F32),
        compiler_params=pltpu.CompilerParams(dimension_semantics=("arbitrary",), has_side_effects=True),
        name="moe_dispatch",
    )(pad_lo, pad_hi, pos_blocks, h)


def _expert_kernel(te_ref, nu_ref, xs_ref, g_ref, w1_ref, w3_ref, w2_ref, y_ref, acc_sc):
    del te_ref
    i = pl.program_id(0)

    @pl.when(i < nu_ref[0])
    def _():
        x = xs_ref[...]
        ms = jnp.mean(x * x, axis=-1, keepdims=True)
        hn = (x * lax.rsqrt(ms + EPS) * g_ref[...]).astype(BF16)
        _swiglu_into(acc_sc, hn, w1_ref.at[0], w3_ref.at[0], w2_ref.at[0])
        y_ref[...] = acc_sc[...]

    @pl.when(i >= nu_ref[0])
    def _():
        y_ref[...] = jnp.zeros(y_ref.shape, F32)


def _experts(tile_expert, n_used, xs, g, w1, w3, w2, te_rows):
    P = xs.shape[0]
    row = pl.BlockSpec((te_rows, D_MODEL), lambda i, te, nu: (jnp.minimum(i, nu[0] - 1), 0))
    wspec = lambda w: pl.BlockSpec((1,) + w.shape[1:], lambda i, te, nu: (te[i], 0, 0))
    return pl.pallas_call(
        _expert_kernel,
        grid_spec=pltpu.PrefetchScalarGridSpec(
            num_scalar_prefetch=2,
            grid=(P // te_rows,),
            in_specs=[row, pl.BlockSpec((1, D_MODEL), lambda i, te, nu: (0, 0)), wspec(w1), wspec(w3), wspec(w2)],
            out_specs=pl.BlockSpec((te_rows, D_MODEL), lambda i, te, nu: (i, 0)),
            scratch_shapes=[pltpu.VMEM((te_rows, D_MODEL), F32)]),
        out_shape=jax.ShapeDtypeStruct((P, D_MODEL), F32),
        compiler_params=_cparams(("arbitrary",)),
        name="moe_experts",
    )(tile_expert, n_used, xs, g, w1, w3, w2)


def _combine_kernel(pos_ref, info_ref, h_ref, y_hbm, o_ref, g_sc, sem, *, tb):
    def issue(i, carry):
        t0 = pl.multiple_of(i * SUBLANES, SUBLANES)
        for u in range(SUBLANES):
            for k in range(2):
                p = pos_ref[0, 0, k * tb + t0 + u]
                pltpu.make_async_copy(y_hbm.at[pl.ds(p, 1)], g_sc.at[k, pl.ds(t0 + u, 1)], sem).start(priority=k)
        return carry

    lax.fori_loop(0, tb // SUBLANES, issue, 0)

    for k in range(2):
        pltpu.make_async_copy(y_hbm.at[pl.ds(0, tb)], g_sc.at[k], sem).wait()
    info = info_ref[...]
    o_ref[...] = h_ref[...] + info[:, R_W1:R_W1 + 1] * g_sc[0] + info[:, R_W2:R_W2 + 1] * g_sc[1]


def _combine(pos_blocks, info, h, y, tb):
    T = h.shape[0]
    row = lambda width: pl.BlockSpec((tb, width), lambda i: (i, 0))
    return pl.pallas_call(
        functools.partial(_combine_kernel, tb=tb),
        grid=(T // tb,),
        in_specs=[pl.BlockSpec((1, 1, 2 * tb), lambda i: (i, 0, 0), memory_space=pltpu.SMEM),
                  row(LANES), row(D_MODEL), pl.BlockSpec(memory_space=pl.ANY)],
        out_specs=row(D_MODEL),
        out_shape=jax.ShapeDtypeStruct((T, D_MODEL), F32),
        scratch_shapes=[pltpu.VMEM((2, tb, D_MODEL), F32), pltpu.SemaphoreType.DMA(())],
        compiler_params=_cparams(("arbitrary",)),
        name="moe_combine",
    )(pos_blocks, info, h, y)


def _pos_blocks(pos1, pos2, tb):
    nb = pos1.shape[0] // tb
    return jnp.concatenate([pos1.reshape(nb, 1, tb), pos2.reshape(nb, 1, tb)], axis=2)


def _moe(h, info, cnt, g, w1, w3, w2):
    T = h.shape[0]
    te_rows = min(512, T)
    counts = cnt[0, :N_EXPERTS].astype(jnp.int32)
    padded = ((counts + te_rows - 1) // te_rows) * te_rows
    ends = jnp.cumsum(padded)
    base = ends - padded
    i1 = info[:, R_I1].astype(jnp.int32)
    i2 = info[:, R_I2].astype(jnp.int32)
    pos1 = jnp.take(base, i1) + info[:, R_RANK1].astype(jnp.int32)
    pos2 = jnp.take(base, i2) + info[:, R_RANK2].astype(jnp.int32)
    n_tiles = (2 * T) // te_rows + N_EXPERTS
    tile_expert = jnp.minimum(jnp.searchsorted(ends // te_rows, jnp.arange(n_tiles), side="right"),
                              N_EXPERTS - 1).astype(jnp.int32)
    n_used = (ends[-1:] // te_rows).astype(jnp.int32)

    tb_d = min(2048, T)
    xs = _dispatch(base + counts, ends, _pos_blocks(pos1, pos2, tb_d), h, n_tiles * te_rows, tb_d)
    y = _experts(tile_expert, n_used, xs, g, w1, w3, w2, te_rows)
    tb_c = min(1024, T)
    return _combine(_pos_blocks(pos1, pos2, tb_c), info, h, y, tb_c)


def kernel(x, positions, attn_norm, w_in, da_q_norm, da_k_norm, lam_q1, lam_k1, lam_q2, lam_k2, da_subln,
           w_proj_da, w_proj_ret, w_out, ffn_norm, dense_w1, dense_w3, dense_w2, w_router, moe_w1, moe_w3,
           moe_w2):
    B, S, D = x.shape
    assert D == D_MODEL and S % RET_CHUNK == 0
    T = B * S
    depth = attn_norm.shape[0]
    tm = min(512, S)
    tq = min(512, S)
    tc = min(512, S)

    pos_b = jnp.broadcast_to(positions.reshape(T, 1).astype(F32), (T, LANES))
    tabs = (_rope_tables(pos_b, *_rope_rows(ROPE_ROT_DIM, ROPE_THETA), tm)
            + _rope_tables(pos_b, *_rope_rows(RET_QK_DIM, RET_ROPE_THETA), tm))
    ret_consts = _retention_consts()
    gsum = jnp.asarray(np.kron(np.eye(4), np.ones((64, 64))), BF16)

    h = x.reshape(T, D)
    for layer in range(depth):
        lambda_init = 0.8 - 0.6 * math.exp(-0.3 * layer)
        gq = jnp.tile(da_q_norm[layer], PROJ_CHUNK // DA_HEAD_DIM).reshape(1, PROJ_CHUNK)
        gk = jnp.tile(da_k_norm[layer], PROJ_CHUNK // DA_HEAD_DIM).reshape(1, PROJ_CHUNK)
        z, vt = _in_proj(h, attn_norm[layer].reshape(1, D), w_in[layer].astype(BF16), gq, gk, gsum, tabs, tm)
        lam_rows = jnp.stack([lam_q1[layer], lam_k1[layer], lam_q2[layer], lam_k2[layer]])
        ya = _diff_attn(z, vt, _score_bound(da_q_norm[layer], da_k_norm[layer]), lam_rows,
                        da_subln[layer].reshape(1, DA_V_DIM), B, S, tq, lambda_init)
        yr = _retention(z, ret_consts, B, S, tc)
        j = layer // 2
        merge_args = (ya, yr, z, h, w_proj_da[layer].astype(BF16), w_proj_ret[layer].astype(BF16),
                      w_out[layer].astype(BF16), ffn_norm[layer].reshape(1, D), tm)
        if layer % 2 == 0:
            h, hn = _merge(*merge_args)
            h = _ffn(hn, h, dense_w1[j].astype(BF16), dense_w3[j].astype(BF16), dense_w2[j].astype(BF16), tm)
        else:
            h, info, cnt = _merge(*merge_args, w_router=w_router[j])
            h = _moe(h, info, cnt, ffn_norm[layer].reshape(1, D), moe_w1[j].astype(BF16),
                     moe_w3[j].astype(BF16), moe_w2[j].astype(BF16))
    return h.reshape(B, S, D)
```

```python
import functools
import math

import jax
import jax.numpy as jnp
import numpy as np
from jax import lax
from jax.experimental import pallas as pl
from jax.experimental.pallas import tpu as pltpu

F32 = jnp.float32
BF16 = jnp.bfloat16

D_MODEL = 1024
DA_HEADS = 4
DA_HEAD_DIM = 64
DA_V_DIM = 128
ROPE_THETA = 500000.0
ROPE_ROT_DIM = 16
RET_HEADS = 4
RET_QK_DIM = 64
RET_V_DIM = 128
RET_CHUNK = 128
RET_ROPE_THETA = 10000.0
D_FF = 2816
N_EXPERTS = 8
EPS = 1e-6
LOG2_E = math.log2(math.e)
DA_VT_ROWS = DA_V_DIM + 16
FIXED_REF_MAX_BOUND = 60.0
GROUP = 4

LANES = 128
SUBLANES = 8
MXU_DIM = 256
HEAD = 64
assert DA_HEAD_DIM == RET_QK_DIM == HEAD and LANES == 2 * HEAD
DA_W = DA_HEADS * DA_V_DIM
RET_W = RET_HEADS * RET_V_DIM
RET_QK_W = RET_HEADS * RET_QK_DIM
ROW_BLOCK = 512
EXPERT_TILE = 512
DISPATCH_BLOCK = 2048
COMBINE_BLOCK = 1024
OFF_QA, OFF_KA, OFF_VA = 0, 512, 1024
OFF_QR, OFF_KR, OFF_VR, OFF_GR = 1536, 1792, 2048, 2560
OFF_GA, OFF_GB = 3072, 4096
IN_W = 5120
PROJ_CHUNK = 512
FF_CHUNK = 256
VMEM_LIMIT = 56 * 1024 * 1024


def _cparams(sem):
    return pltpu.CompilerParams(dimension_semantics=sem, vmem_limit_bytes=VMEM_LIMIT)


def _const_spec(shape):
    nd = len(shape)
    return pl.BlockSpec(shape, lambda *_: (0,) * nd)


def _rope_table_kernel(pos_ref, inv_ref, sgn_ref, cos_ref, sin_ref):
    ang = pos_ref[...] * inv_ref[...]
    cos_ref[...] = jnp.cos(ang)
    sin_ref[...] = jnp.sin(ang) * sgn_ref[...]


def _rope_tables(pos_b, inv_row, sgn_row, tm):
    T = pos_b.shape[0]
    row = pl.BlockSpec((tm, LANES), lambda i: (i, 0))
    return pl.pallas_call(
        _rope_table_kernel,
        grid=(T // tm,),
        in_specs=[row, _const_spec((1, LANES)), _const_spec((1, LANES))],
        out_specs=[row, row],
        out_shape=[jax.ShapeDtypeStruct((T, LANES), F32)] * 2,
        compiler_params=_cparams(("parallel",)),
        name="rope_tables",
    )(pos_b, inv_row, sgn_row)


def _rope_rows(rot_dim, theta):
    half = rot_dim // 2
    inv = 1.0 / (theta ** (jnp.arange(half, dtype=F32) / half))
    jj = np.arange(LANES) % HEAD
    inv_row = jnp.where(jj < rot_dim, inv[jj % half], 0.0).astype(F32)
    sgn_row = np.where(jj < half, -1.0, np.where(jj < rot_dim, 1.0, 0.0)).astype(np.float32)
    return inv_row.reshape(1, LANES), jnp.asarray(sgn_row).reshape(1, LANES)


def _rotate_half(x, half):
    jj = lax.broadcasted_iota(jnp.int32, x.shape, 1) % HEAD
    return jnp.where(jj < half, pltpu.roll(x, LANES - half, 1), pltpu.roll(x, half, 1))


def _sigmoid(x):
    return 1.0 / (1.0 + jnp.exp(-x))


def _in_proj_kernel(h_ref, g_ref, w_ref, gq_ref, gk_ref, gsum_ref, ca_ref, sa_ref, cr_ref, sr_ref, z_ref, vt_ref):
    half = h_ref.shape[0] // 2
    for r in (slice(0, half), slice(half, 2 * half)):
        _in_proj_rows(r, h_ref, g_ref, w_ref, gq_ref, gk_ref, gsum_ref, ca_ref, sa_ref, cr_ref, sr_ref, z_ref,
                      vt_ref)


def _in_proj_rows(r, h_ref, g_ref, w_ref, gq_ref, gk_ref, gsum_ref, ca_ref, sa_ref, cr_ref, sr_ref, z_ref, vt_ref):
    x = h_ref[r, :]
    ms = jnp.mean(x * x, axis=-1, keepdims=True)
    hn = (x * lax.rsqrt(ms + EPS) * g_ref[...]).astype(BF16)
    ca, sa, cr, sr = ca_ref[r, :], sa_ref[r, :], cr_ref[r, :], sr_ref[r, :]

    def head_rms(zc, gain):
        sq = (zc * zc).astype(BF16)
        outs = []
        for s in range(PROJ_CHUNK // MXU_DIM):
            sl = slice(s * MXU_DIM, (s + 1) * MXU_DIM)
            tot = jnp.dot(sq[:, sl], gsum_ref[...], preferred_element_type=F32)
            outs.append(zc[:, sl] * lax.rsqrt(tot * (1.0 / HEAD) + EPS))
        return jnp.concatenate(outs, axis=1) * gain

    def rope(y, cos, sin, half):
        outs = []
        for s in range(y.shape[1] // LANES):
            ys = y[:, s * LANES:(s + 1) * LANES]
            outs.append(ys * cos + _rotate_half(ys, half) * sin)
        return jnp.concatenate(outs, axis=1)

    for c in range(IN_W // PROJ_CHUNK):
        c0 = c * PROJ_CHUNK
        zc = jnp.dot(hn, w_ref[:, c0:c0 + PROJ_CHUNK], preferred_element_type=F32)
        if c0 == OFF_QA:
            zc = rope(head_rms(zc, gq_ref[...]), ca, sa, ROPE_ROT_DIM // 2)
            zc = zc * (DA_HEAD_DIM ** -0.5 * LOG2_E)
        elif c0 == OFF_KA:
            zc = rope(head_rms(zc, gk_ref[...]), ca, sa, ROPE_ROT_DIM // 2)
        elif c0 == OFF_VA:
            zt = zc.T.astype(BF16)
            extra = lax.broadcasted_iota(jnp.int32, (DA_VT_ROWS - DA_V_DIM, zt.shape[1]), 0)
            for hd in range(DA_HEADS):
                vt_ref[0, hd, 0:DA_V_DIM, r] = zt[hd * DA_V_DIM:(hd + 1) * DA_V_DIM, :]
                vt_ref[0, hd, DA_V_DIM:DA_VT_ROWS, r] = jnp.where(extra == 0, 1.0, 0.0).astype(BF16)
        elif c0 == OFF_QR:
            zc = rope(zc, cr, sr, RET_QK_DIM // 2)
            col = lax.broadcasted_iota(jnp.int32, zc.shape, 1)
            zc = jnp.where(col >= OFF_KR - OFF_QR, zc * (RET_QK_DIM ** -0.5), zc)
        elif c0 == OFF_GR:
            zc = zc * _sigmoid(zc)
        elif c0 >= OFF_GA:
            zc = _sigmoid(zc)
        z_ref[r, c0:c0 + PROJ_CHUNK] = zc.astype(BF16)


def _in_proj(h, g, w, gq, gk, gsum, tabs, tm):
    T = h.shape[0]
    row = lambda width: pl.BlockSpec((tm, width), lambda i: (i, 0))
    return pl.pallas_call(
        _in_proj_kernel,
        grid=(T // tm,),
        in_specs=[row(D_MODEL), _const_spec((1, D_MODEL)), _const_spec((D_MODEL, IN_W)),
                  _const_spec((1, PROJ_CHUNK)), _const_spec((1, PROJ_CHUNK)), _const_spec((MXU_DIM, MXU_DIM)),
                  row(LANES), row(LANES), row(LANES), row(LANES)],
        out_specs=[row(IN_W), pl.BlockSpec((1, DA_HEADS, DA_VT_ROWS, tm), lambda i: (i, 0, 0, 0))],
        out_shape=[jax.ShapeDtypeStruct((T, IN_W), BF16),
                   jax.ShapeDtypeStruct((T // tm, DA_HEADS, DA_VT_ROWS, tm), BF16)],
        compiler_params=_cparams(("parallel",)),
        name="in_proj",
    )(h, g, w, gq, gk, gsum, *tabs)


def _diff_attn_kernel(bound_ref, q_ref, k_ref, vt_ref, lam_ref, subln_ref, o_ref, acc_sc, st_sc, *, tq,
                      lambda_init):
    qi = pl.program_id(2)
    q = q_ref[...]
    lane = lax.broadcasted_iota(jnp.int32, q.shape, 1)
    qs = (jnp.where(lane < DA_HEAD_DIM, q, jnp.zeros_like(q)),
          jnp.where(lane >= DA_HEAD_DIM, q, jnp.zeros_like(q)))
    acc_sc[...] = jnp.zeros(acc_sc.shape, F32)

    tqh = tq // 2
    streams = [(c, hf) for hf in range(2) for c in range(2)]
    qsub = [qs[c][hf * tqh:(hf + 1) * tqh, :] for c, hf in streams]

    def scores(ki, diagonal=False):
        k = k_ref[pl.ds(pl.multiple_of(ki * tq, tq), tq), :]
        return [lax.dot_general(k[0:tqh] if diagonal and hf == 0 else k, qq, (((1,), (1,)), ((), ())),
                                preferred_element_type=F32) for qq, (c, hf) in zip(qsub, streams)]

    def stash(sts):
        for s in range(len(streams)):
            st_sc[s] = sts[s]

    bound = bound_ref[0, 0]

    def masked_scores(st, hf):
        key = lax.broadcasted_iota(jnp.int32, st.shape, 0)
        qry = lax.broadcasted_iota(jnp.int32, st.shape, 1) + hf * tqh
        return jnp.where(key <= qry, st, -jnp.inf)

    def consume_online(sts, ki, stats, masked):
        vt = vt_ref[ki, 0]
        out = []
        for s, (c, hf) in enumerate(streams):
            st = st_sc[s] if sts is None else sts[s]
            if masked:
                st = masked_scores(st, hf)
            cols = slice(hf * tqh, (hf + 1) * tqh)
            m_old = stats[s]
            m_new = jnp.maximum(m_old, jnp.max(st, axis=0, keepdims=True))
            alpha = jnp.exp2(m_old - m_new)
            pt = jnp.exp2((st - m_new).astype(BF16))
            acc_sc[c, :, cols] = alpha * acc_sc[c, :, cols] + jnp.dot(vt, pt, preferred_element_type=F32)
            out.append(m_new)
        return tuple(out)

    def run_fixed_ref():
        def accumulate(blocks, last_masked):
            diag = [last_masked and u == len(blocks) - 1 for u in range(len(blocks))]
            sts = [scores(b, dg) for b, dg in zip(blocks, diag)]
            tot = [None] * len(streams)
            den = [None] * len(streams)
            for b, st4, dg in zip(blocks, sts, diag):
                vt = vt_ref[b, 0, 0:DA_V_DIM, :]
                for s, (c, hf) in enumerate(streams):
                    st = masked_scores(st4[s], hf) if dg else st4[s]
                    keys = st.shape[0]
                    pt = jnp.exp2(st - bound)
                    part = jnp.sum(pt.reshape(keys // SUBLANES, SUBLANES, tqh), axis=0)
                    d = jnp.dot(vt[:, 0:keys], pt.astype(BF16), preferred_element_type=F32)
                    tot[s] = d if tot[s] is None else tot[s] + d
                    den[s] = part if den[s] is None else den[s] + part
            for s, (c, hf) in enumerate(streams):
                cols = slice(hf * tqh, (hf + 1) * tqh)
                acc_sc[c, 0:DA_V_DIM, cols] += tot[s]
                acc_sc[c, DA_V_DIM:DA_V_DIM + 1, cols] += jnp.sum(den[s], axis=0, keepdims=True)

        def group(j, carry):
            accumulate([GROUP * j + u for u in range(GROUP)], False)
            return carry

        ngroups = qi // GROUP
        lax.fori_loop(0, ngroups, group, 0)
        for rest in range(GROUP):
            pl.when(qi - GROUP * ngroups == rest)(
                functools.partial(accumulate, [qi - rest + u for u in range(rest + 1)], True))

    def run_online():
        npairs = qi // 2
        stash(scores(0))

        def pair(j, stats):
            b0 = 2 * j
            odd = scores(b0 + 1)
            stats = consume_online(None, b0, stats, False)
            stash(scores(b0 + 2))
            return consume_online(odd, b0 + 1, stats, False)

        def leftover(stats):
            stats = consume_online(None, qi - 1, stats, False)
            stash(scores(qi))
            return stats

        init = (jnp.full((1, tqh), -jnp.inf, F32),) * len(streams)
        stats = lax.fori_loop(0, npairs, pair, init)
        stats = lax.cond(qi - 2 * npairs == 1, leftover, lambda st: st, stats)
        consume_online(None, qi, stats, True)

    pl.when(bound <= FIXED_REF_MAX_BOUND)(run_fixed_ref)
    pl.when(bound > FIXED_REF_MAX_BOUND)(run_online)

    lv = lam_ref[...]
    lam = (jnp.exp(jnp.sum(lv[0:1] * lv[1:2], axis=1, keepdims=True))
           - jnp.exp(jnp.sum(lv[2:3] * lv[3:4], axis=1, keepdims=True)) + lambda_init)
    dv = DA_V_DIM
    ot = (acc_sc[0, 0:dv, :] / acc_sc[0, dv:dv + 1, :]
          - lam * (acc_sc[1, 0:dv, :] / acc_sc[1, dv:dv + 1, :]))
    yt = ot * lax.rsqrt(jnp.mean(ot * ot, axis=0, keepdims=True) + EPS)
    o_ref[...] = (yt.T * subln_ref[...] * (1.0 - lambda_init)).astype(BF16)


def _score_bound(gq, gk):
    norm = math.sqrt(DA_HEAD_DIM)
    return (1.01 * norm * norm * DA_HEAD_DIM ** -0.5 * LOG2_E
            * jnp.max(jnp.abs(gq)) * jnp.max(jnp.abs(gk))).astype(F32).reshape(1, 1)


def _diff_attn(z, vt, bound, lam_rows, subln, B, S, tq, lambda_init):
    T = B * S
    nq = S // tq
    kern = functools.partial(_diff_attn_kernel, tq=tq, lambda_init=lambda_init)
    return pl.pallas_call(
        kern,
        grid=(B, DA_HEADS, nq),
        in_specs=[pl.BlockSpec((1, 1), lambda b, h, i: (0, 0), memory_space=pltpu.SMEM),
                  pl.BlockSpec((tq, LANES), lambda b, h, i: (b * nq + i, OFF_QA // LANES + h)),
                  pl.BlockSpec((S, LANES), lambda b, h, i: (b, OFF_KA // LANES + h)),
                  pl.BlockSpec((nq, 1, DA_VT_ROWS, tq), lambda b, h, i: (b, h, 0, 0)),
                  _const_spec((4, DA_HEAD_DIM)), _const_spec((1, DA_V_DIM))],
        out_specs=pl.BlockSpec((tq, LANES), lambda b, h, i: (b * nq + i, h)),
        out_shape=jax.ShapeDtypeStruct((T, DA_HEADS * DA_V_DIM), BF16),
        scratch_shapes=[pltpu.VMEM((2, DA_VT_ROWS, tq), F32), pltpu.VMEM((4, tq, tq // 2), F32)],
        compiler_params=_cparams(("parallel", "parallel", "arbitrary")),
        name="diff_attn",
    )(bound, z, z, vt, lam_rows, subln)


def _retention_kernel(q_ref, k_ref, v_ref, g_ref, dec_ref, qdec_ref, kdec_ref, cdec_ref, o_ref, r_sc, o_sc, *,
                      tc):
    @pl.when(pl.program_id(1) == 0)
    def _():
        r_sc[...] = jnp.zeros(r_sc.shape, F32)

    C = RET_CHUNK
    lane = lax.broadcasted_iota(jnp.int32, (C, LANES), 1)
    rowi = lax.broadcasted_iota(jnp.int32, (C, LANES), 0)
    units = [(n, j) for n in range(tc // C) for j in range(RET_HEADS // 2)]
    rows = lambda n: slice(n * C, (n + 1) * C)
    cols = lambda j: slice(j * LANES, (j + 1) * LANES)
    vcol = lambda h: slice(h * RET_V_DIM, (h + 1) * RET_V_DIM)
    own = lambda hh: (lane >= RET_QK_DIM) if hh else (lane < RET_QK_DIM)
    s, kv = {}, {}
    for n, j in units:
        qp = q_ref[rows(n), cols(j)]
        kp = k_ref[rows(n), cols(j)]
        kdt = (kp.astype(F32) * kdec_ref[:, cols(j)]).T.astype(BF16)
        both = []
        for hh in range(2):
            h = 2 * j + hh
            s[n, h] = lax.dot_general(jnp.where(own(hh), qp, jnp.zeros_like(qp)), kp, (((1,), (1,)), ((), ())),
                                      preferred_element_type=F32)
            both.append(jnp.dot(kdt, v_ref[rows(n), vcol(h)], preferred_element_type=F32))
        kv[n, j] = jnp.where(rowi < RET_QK_DIM, both[0], both[1])
    for n, j in units:
        for hh in range(2):
            h = 2 * j + hh
            inner = (s[n, h] * dec_ref[h]).astype(BF16)
            o_sc[rows(n), vcol(h)] = jnp.dot(inner, v_ref[rows(n), vcol(h)], preferred_element_type=F32)
    state = [r_sc[j] for j in range(RET_HEADS // 2)]
    for n, j in units:
        qp = q_ref[rows(n), cols(j)]
        qd = (qp.astype(F32) * qdec_ref[:, cols(j)]).astype(BF16)
        r_b = state[j].astype(BF16)
        for hh in range(2):
            h = 2 * j + hh
            o_sc[rows(n), vcol(h)] += jnp.dot(jnp.where(own(hh), qd, jnp.zeros_like(qd)), r_b,
                                              preferred_element_type=F32)
        state[j] = state[j] * cdec_ref[j] + kv[n, j]
    for j in range(RET_HEADS // 2):
        r_sc[j] = state[j]
    for h in range(RET_HEADS):
        gs = slice(h * RET_V_DIM, (h + 1) * RET_V_DIM)
        o = o_sc[:, gs]
        oc = o - jnp.mean(o, axis=-1, keepdims=True)
        y = oc * lax.rsqrt(jnp.mean(oc * oc, axis=-1, keepdims=True) + EPS)
        o_ref[:, gs] = (y * g_ref[:, gs].astype(F32)).astype(BF16)


def _retention(z, consts, B, S, tc):
    T = B * S
    n = S // tc
    dec, qdec, kdec, cdec = consts
    kern = functools.partial(_retention_kernel, tc=tc)
    blk = lambda width, off: pl.BlockSpec((tc, width), lambda b, i: (b * n + i, off // width))
    return pl.pallas_call(
        kern,
        grid=(B, n),
        in_specs=[blk(RET_QK_W, OFF_QR), blk(RET_QK_W, OFF_KR), blk(RET_W, OFF_VR), blk(RET_W, OFF_GR),
                  _const_spec(dec.shape), _const_spec(qdec.shape), _const_spec(kdec.shape),
                  _const_spec(cdec.shape)],
        out_specs=pl.BlockSpec((tc, RET_W), lambda b, i: (b * n + i, 0)),
        out_shape=jax.ShapeDtypeStruct((T, RET_HEADS * RET_V_DIM), BF16),
        scratch_shapes=[pltpu.VMEM((RET_HEADS // 2, LANES, LANES), F32),
                        pltpu.VMEM((tc, RET_HEADS * RET_V_DIM), F32)],
        compiler_params=_cparams(("parallel", "arbitrary")),
        name="retention",
    )(z, z, z, z, dec, qdec, kdec, cdec)


def _retention_consts():
    C = RET_CHUNK
    log_gamma = jnp.log1p(-jnp.exp2(-5.0 - jnp.arange(RET_HEADS, dtype=F32)))
    idx = jnp.arange(C, dtype=F32)
    rel = idx[:, None] - idx[None, :]
    dec = jnp.where(rel[None] >= 0, jnp.exp(log_gamma[:, None, None] * jnp.maximum(rel, 0.0)[None]), 0.0)
    q_dec = jnp.exp(log_gamma[None, :] * (idx + 1.0)[:, None])
    k_dec = jnp.exp(log_gamma[None, :] * (C - 1.0 - idx)[:, None])
    qdec = jnp.repeat(q_dec, RET_QK_DIM, axis=1)
    kdec = jnp.repeat(k_dec, RET_QK_DIM, axis=1)
    chunk_decay = jnp.repeat(jnp.exp(log_gamma * C), RET_QK_DIM)
    cdec = jnp.broadcast_to(chunk_decay.reshape(RET_HEADS // 2, LANES, 1), (RET_HEADS // 2, LANES, LANES))
    return dec.astype(F32), qdec.astype(F32), kdec.astype(F32), cdec.astype(F32)


R_I1, R_I2, R_W1, R_W2, R_RANK1, R_RANK2 = range(6)


def _route(hn, wr_ref, rows, info_ref, carry_sc):
    w = wr_ref[...]
    hn_hi, w_hi = hn.astype(BF16), w.astype(BF16)
    hn_lo, w_lo = (hn - hn_hi.astype(F32)).astype(BF16), (w - w_hi.astype(F32)).astype(BF16)
    logits = (jnp.dot(hn_hi, w_hi, preferred_element_type=F32) + jnp.dot(hn_hi, w_lo, preferred_element_type=F32)
              + jnp.dot(hn_lo, w_hi, preferred_element_type=F32))
    lane = lax.broadcasted_iota(jnp.int32, logits.shape, 1)
    logits = jnp.where(lane < N_EXPERTS, logits, -jnp.inf)
    m1 = jnp.max(logits, axis=1, keepdims=True)
    i1 = jnp.min(jnp.where(logits == m1, lane, LANES), axis=1, keepdims=True)
    rest = jnp.where(lane == i1, -jnp.inf, logits)
    m2 = jnp.max(rest, axis=1, keepdims=True)
    i2 = jnp.min(jnp.where(rest == m2, lane, LANES), axis=1, keepdims=True)
    e = jnp.exp(m2 - m1)
    w1 = 1.0 / (1.0 + e)
    w2 = e / (1.0 + e)
    sel = jnp.where(lane == i1, 1.0, jnp.where(lane == i2, 1.0, 0.0))
    n = sel.shape[0]
    rr = lax.broadcasted_iota(jnp.int32, (n, n), 0)
    cc = lax.broadcasted_iota(jnp.int32, (n, n), 1)
    ltri = jnp.where(cc < rr, 1.0, 0.0).astype(BF16)
    before = jnp.dot(ltri, sel.astype(BF16), preferred_element_type=F32) + carry_sc[0:1, :]
    r1 = jnp.sum(jnp.where(lane == i1, before, 0.0), axis=1, keepdims=True)
    r2 = jnp.sum(jnp.where(lane == i2, before, 0.0), axis=1, keepdims=True)
    info = jnp.where(lane == R_I1, i1.astype(F32), 0.0)
    for ln, val in ((R_I2, i2.astype(F32)), (R_W1, w1), (R_W2, w2), (R_RANK1, r1), (R_RANK2, r2)):
        info = jnp.where(lane == ln, val, info)
    info_ref[rows, :] = info
    carry_sc[...] = carry_sc[...] + jnp.sum(sel, axis=0, keepdims=True)


def _merge_kernel(*refs, route):
    if route:
        (ya_ref, yr_ref, ga_ref, gb_ref, h_ref, wa_ref, wr_ref, wo_ref, gf_ref, wrt_ref,
         h_out, info_out, cnt_out, carry_sc) = refs

        @pl.when(pl.program_id(0) == 0)
        def _():
            carry_sc[...] = jnp.zeros(carry_sc.shape, F32)
    else:
        ya_ref, yr_ref, ga_ref, gb_ref, h_ref, wa_ref, wr_ref, wo_ref, gf_ref, h_out, hn_out = refs
    half = ya_ref.shape[0] // 2
    parts = [slice(0, half), slice(half, 2 * half)]
    branch = [(jnp.dot(ya_ref[r, :], wa_ref[...], preferred_element_type=F32),
               jnp.dot(yr_ref[r, :], wr_ref[...], preferred_element_type=F32)) for r in parts]
    for r, (pa, pr) in zip(parts, branch):
        merged = ga_ref[r, :].astype(F32) * pa + gb_ref[r, :].astype(F32) * pr
        hnew = h_ref[r, :] + jnp.dot(merged.astype(BF16), wo_ref[...], preferred_element_type=F32)
        h_out[r, :] = hnew
        ms = jnp.mean(hnew * hnew, axis=-1, keepdims=True)
        hn = hnew * lax.rsqrt(ms + EPS) * gf_ref[...]
        if route:
            _route(hn, wrt_ref, r, info_out, carry_sc)
        else:
            hn_out[r, :] = hn.astype(BF16)
    if route:
        cnt_out[...] = carry_sc[...]


def _merge(ya, yr, z, h, wa, wr, wo, gf, tm, w_router=None):
    T = h.shape[0]
    route = w_router is not None
    row = lambda width, cb=0: pl.BlockSpec((tm, width), lambda i: (i, cb))
    in_specs = [row(DA_W), row(RET_W), row(D_MODEL, OFF_GA // D_MODEL), row(D_MODEL, OFF_GB // D_MODEL),
                row(D_MODEL), _const_spec(wa.shape), _const_spec(wr.shape), _const_spec(wo.shape),
                _const_spec((1, D_MODEL))]
    args = [ya, yr, z, z, h, wa, wr, wo, gf]
    if route:
        in_specs.append(_const_spec((D_MODEL, LANES)))
        args.append(jnp.pad(w_router, ((0, 0), (0, LANES - N_EXPERTS))))
        out_specs = [row(D_MODEL), row(LANES), _const_spec((SUBLANES, LANES))]
        out_shape = [jax.ShapeDtypeStruct((T, D_MODEL), F32), jax.ShapeDtypeStruct((T, LANES), F32),
                     jax.ShapeDtypeStruct((SUBLANES, LANES), F32)]
        scratch = [pltpu.VMEM((SUBLANES, LANES), F32)]
    else:
        out_specs = [row(D_MODEL), row(D_MODEL)]
        out_shape = [jax.ShapeDtypeStruct((T, D_MODEL), F32), jax.ShapeDtypeStruct((T, D_MODEL), BF16)]
        scratch = []
    return pl.pallas_call(
        functools.partial(_merge_kernel, route=route),
        grid=(T // tm,),
        in_specs=in_specs,
        out_specs=out_specs,
        out_shape=out_shape,
        scratch_shapes=scratch,
        compiler_params=_cparams(("arbitrary",) if route else ("parallel",)),
        name="merge_route" if route else "merge_out",
    )(*args)


def _swiglu_into(acc_sc, x, w1_ref, w3_ref, w2_ref):
    chunks = [slice(c * FF_CHUNK, (c + 1) * FF_CHUNK) for c in range(D_FF // FF_CHUNK)]
    up = lambda cs: (jnp.dot(x, w1_ref[:, cs], preferred_element_type=F32),
                     jnp.dot(x, w3_ref[:, cs], preferred_element_type=F32))
    uv = up(chunks[0])
    for c, cs in enumerate(chunks):
        nxt = up(chunks[c + 1]) if c + 1 < len(chunks) else None
        a = (uv[0] * _sigmoid(uv[0]) * uv[1]).astype(BF16)
        d = jnp.dot(a, w2_ref[cs, :], preferred_element_type=F32)
        if c == 0:
            acc_sc[...] = d
        else:
            acc_sc[...] += d
        uv = nxt


def _ffn_kernel(hn_ref, prev_ref, w1_ref, w3_ref, w2_ref, o_ref, acc_sc):
    _swiglu_into(acc_sc, hn_ref[...], w1_ref, w3_ref, w2_ref)
    o_ref[...] = prev_ref[...] + acc_sc[...]


def _ffn(hn, prev, w1, w3, w2, tm):
    T = hn.shape[0]
    row = lambda width: pl.BlockSpec((tm, width), lambda i: (i, 0))
    wspec = lambda w: pl.BlockSpec(w.shape, lambda i: (0, 0), pipeline_mode=pl.Buffered(1))
    return pl.pallas_call(
        _ffn_kernel,
        grid=(T // tm,),
        in_specs=[row(D_MODEL), row(D_MODEL), wspec(w1), wspec(w3), wspec(w2)],
        out_specs=row(D_MODEL),
        out_shape=jax.ShapeDtypeStruct((T, D_MODEL), F32),
        scratch_shapes=[pltpu.VMEM((tm, D_MODEL), F32)],
        input_output_aliases={1: 0},
        compiler_params=_cparams(("parallel",)),
        name="ffn",
    )(hn, prev, w1, w3, w2)


def _dispatch_kernel(pad_lo_ref, pad_hi_ref, pos_ref, h_ref, xs_hbm, zero_sc, sem, pad_sem, *, tb):
    @pl.when(pl.program_id(0) == 0)
    def _():
        zero_sc[...] = jnp.zeros(zero_sc.shape, F32)
        zero_row = zero_sc.at[pl.ds(0, 1)]
        for e in range(N_EXPERTS):
            def fill(r, carry):
                pltpu.make_async_copy(zero_row, xs_hbm.at[pl.ds(r, 1)], pad_sem).start()
                return carry

            lax.fori_loop(pad_lo_ref[e], pad_hi_ref[e], fill, 0)
        for e in range(N_EXPERTS):
            def drain(r, carry):
                pltpu.make_async_copy(zero_row, xs_hbm.at[pl.ds(r, 1)], pad_sem).wait()
                return carry

            lax.fori_loop(pad_lo_ref[e], pad_hi_ref[e], drain, 0)

    def issue(i, carry):
        t0 = pl.multiple_of(i * SUBLANES, SUBLANES)
        for u in range(SUBLANES):
            for k in range(2):
                p = pos_ref[0, 0, k * tb + t0 + u]
                pltpu.make_async_copy(h_ref.at[pl.ds(t0 + u, 1)], xs_hbm.at[pl.ds(p, 1)], sem).start(priority=k)
        return carry

    lax.fori_loop(0, tb // SUBLANES, issue, 0)

    for k in range(2):
        pltpu.make_async_copy(h_ref, xs_hbm.at[pl.ds(0, tb)], sem).wait()


def _dispatch(pad_lo, pad_hi, pos_blocks, h, n_rows, tb):
    T = h.shape[0]
    return pl.pallas_call(
        functools.partial(_dispatch_kernel, tb=tb),
        grid_spec=pltpu.PrefetchScalarGridSpec(
            num_scalar_prefetch=2,
            grid=(T // tb,),
            in_specs=[pl.BlockSpec((1, 1, 2 * tb), lambda i, lo, hi: (i, 0, 0), memory_space=pltpu.SMEM),
                      pl.BlockSpec((tb, D_MODEL), lambda i, lo, hi: (i, 0))],
            out_specs=pl.BlockSpec(memory_space=pl.ANY),
            scratch_shapes=[pltpu.VMEM((SUBLANES, D_MODEL), F32), pltpu.SemaphoreType.DMA(()),
                            pltpu.SemaphoreType.DMA(())]),
        out_shape=jax.ShapeDtypeStruct((n_rows, D_MODEL), F32),
        compiler_params=pltpu.CompilerParams(dimension_semantics=("arbitrary",), has_side_effects=True),
        name="moe_dispatch",
    )(pad_lo, pad_hi, pos_blocks, h)


def _expert_kernel(te_ref, nu_ref, xs_ref, g_ref, w1_ref, w3_ref, w2_ref, y_ref, acc_sc):
    del te_ref
    i = pl.program_id(0)

    @pl.when(i < nu_ref[0])
    def _():
        x = xs_ref[...]
        ms = jnp.mean(x * x, axis=-1, keepdims=True)
        hn = (x * lax.rsqrt(ms + EPS) * g_ref[...]).astype(BF16)
        _swiglu_into(acc_sc, hn, w1_ref.at[0], w3_ref.at[0], w2_ref.at[0])
        y_ref[...] = acc_sc[...]

    @pl.when(i >= nu_ref[0])
    def _():
        y_ref[...] = jnp.zeros(y_ref.shape, F32)


def _experts(tile_expert, n_used, xs, g, w1, w3, w2, te_rows):
    P = xs.shape[0]
    row = pl.BlockSpec((te_rows, D_MODEL), lambda i, te, nu: (jnp.minimum(i, nu[0] - 1), 0))
    wspec = lambda w: pl.BlockSpec((1,) + w.shape[1:], lambda i, te, nu: (te[i], 0, 0))
    return pl.pallas_call(
        _expert_kernel,
        grid_spec=pltpu.PrefetchScalarGridSpec(
            num_scalar_prefetch=2,
            grid=(P // te_rows,),
            in_specs=[row, pl.BlockSpec((1, D_MODEL), lambda i, te, nu: (0, 0)), wspec(w1), wspec(w3), wspec(w2)],
            out_specs=pl.BlockSpec((te_rows, D_MODEL), lambda i, te, nu: (i, 0)),
            scratch_shapes=[pltpu.VMEM((te_rows, D_MODEL), F32)]),
        out_shape=jax.ShapeDtypeStruct((P, D_MODEL), F32),
        compiler_params=_cparams(("arbitrary",)),
        name="moe_experts",
    )(tile_expert, n_used, xs, g, w1, w3, w2)


def _combine_kernel(pos_ref, info_ref, h_ref, y_hbm, o_ref, g_sc, sem, *, tb):
    def issue(i, carry):
        t0 = pl.multiple_of(i * SUBLANES, SUBLANES)
        for u in range(SUBLANES):
            for k in range(2):
                p = pos_ref[0, 0, k * tb + t0 + u]
                pltpu.make_async_copy(y_hbm.at[pl.ds(p, 1)], g_sc.at[k, pl.ds(t0 + u, 1)], sem).start(priority=k)
        return carry

    lax.fori_loop(0, tb // SUBLANES, issue, 0)

    for k in range(2):
        pltpu.make_async_copy(y_hbm.at[pl.ds(0, tb)], g_sc.at[k], sem).wait()
    info = info_ref[...]
    o_ref[...] = h_ref[...] + info[:, R_W1:R_W1 + 1] * g_sc[0] + info[:, R_W2:R_W2 + 1] * g_sc[1]


def _combine(pos_blocks, info, h, y, tb):
    T = h.shape[0]
    row = lambda width: pl.BlockSpec((tb, width), lambda i: (i, 0))
    return pl.pallas_call(
        functools.partial(_combine_kernel, tb=tb),
        grid=(T // tb,),
        in_specs=[pl.BlockSpec((1, 1, 2 * tb), lambda i: (i, 0, 0), memory_space=pltpu.SMEM),
                  row(LANES), row(D_MODEL), pl.BlockSpec(memory_space=pl.ANY)],
        out_specs=row(D_MODEL),
        out_shape=jax.ShapeDtypeStruct((T, D_MODEL), F32),
        scratch_shapes=[pltpu.VMEM((2, tb, D_MODEL), F32), pltpu.SemaphoreType.DMA(())],
        compiler_params=_cparams(("arbitrary",)),
        name="moe_combine",
    )(pos_blocks, info, h, y)


def _pos_blocks(pos1, pos2, tb):
    nb = pos1.shape[0] // tb
    return jnp.concatenate([pos1.reshape(nb, 1, tb), pos2.reshape(nb, 1, tb)], axis=2)


def _moe(h, info, cnt, g, w1, w3, w2):
    T = h.shape[0]
    te_rows = min(EXPERT_TILE, T)
    counts = cnt[0, :N_EXPERTS].astype(jnp.int32)
    padded = ((counts + te_rows - 1) // te_rows) * te_rows
    ends = jnp.cumsum(padded)
    base = ends - padded
    i1 = info[:, R_I1].astype(jnp.int32)
    i2 = info[:, R_I2].astype(jnp.int32)
    pos1 = jnp.take(base, i1) + info[:, R_RANK1].astype(jnp.int32)
    pos2 = jnp.take(base, i2) + info[:, R_RANK2].astype(jnp.int32)
    n_tiles = (2 * T) // te_rows + N_EXPERTS
    tile_expert = jnp.minimum(jnp.searchsorted(ends // te_rows, jnp.arange(n_tiles), side="right"),
                              N_EXPERTS - 1).astype(jnp.int32)
    n_used = (ends[-1:] // te_rows).astype(jnp.int32)

    tb_d = min(DISPATCH_BLOCK, T)
    xs = _dispatch(base + counts, ends, _pos_blocks(pos1, pos2, tb_d), h, n_tiles * te_rows, tb_d)
    y = _experts(tile_expert, n_used, xs, g, w1, w3, w2, te_rows)
    tb_c = min(COMBINE_BLOCK, T)
    return _combine(_pos_blocks(pos1, pos2, tb_c), info, h, y, tb_c)


def kernel(x, positions, attn_norm, w_in, da_q_norm, da_k_norm, lam_q1, lam_k1, lam_q2, lam_k2, da_subln,
           w_proj_da, w_proj_ret, w_out, ffn_norm, dense_w1, dense_w3, dense_w2, w_router, moe_w1, moe_w3,
           moe_w2):
    B, S, D = x.shape
    T = B * S
    depth = attn_norm.shape[0]
    tm = tq = tc = min(ROW_BLOCK, S)
    assert D == D_MODEL and S % tm == 0 and tm % (2 * MXU_DIM) == 0 and tm % RET_CHUNK == 0, (x.shape, tm)

    pos_b = jnp.broadcast_to(positions.reshape(T, 1).astype(F32), (T, LANES))
    tabs = (_rope_tables(pos_b, *_rope_rows(ROPE_ROT_DIM, ROPE_THETA), tm)
            + _rope_tables(pos_b, *_rope_rows(RET_QK_DIM, RET_ROPE_THETA), tm))
    ret_consts = _retention_consts()
    gsum = jnp.asarray(np.kron(np.eye(MXU_DIM // HEAD), np.ones((HEAD, HEAD))), BF16)

    h = x.reshape(T, D)
    for layer in range(depth):
        lambda_init = 0.8 - 0.6 * math.exp(-0.3 * layer)
        gq = jnp.tile(da_q_norm[layer], PROJ_CHUNK // DA_HEAD_DIM).reshape(1, PROJ_CHUNK)
        gk = jnp.tile(da_k_norm[layer], PROJ_CHUNK // DA_HEAD_DIM).reshape(1, PROJ_CHUNK)
        z, vt = _in_proj(h, attn_norm[layer].reshape(1, D), w_in[layer].astype(BF16), gq, gk, gsum, tabs, tm)
        lam_rows = jnp.stack([lam_q1[layer], lam_k1[layer], lam_q2[layer], lam_k2[layer]])
        ya = _diff_attn(z, vt, _score_bound(da_q_norm[layer], da_k_norm[layer]), lam_rows,
                        da_subln[layer].reshape(1, DA_V_DIM), B, S, tq, lambda_init)
        yr = _retention(z, ret_consts, B, S, tc)
        j = layer // 2
        merge_args = (ya, yr, z, h, w_proj_da[layer].astype(BF16), w_proj_ret[layer].astype(BF16),
                      w_out[layer].astype(BF16), ffn_norm[layer].reshape(1, D), tm)
        if layer % 2 == 0:
            h, hn = _merge(*merge_args)
            h = _ffn(hn, h, dense_w1[j].astype(BF16), dense_w3[j].astype(BF16), dense_w2[j].astype(BF16), tm)
        else:
            h, info, cnt = _merge(*merge_args, w_router=w_router[j])
            h = _moe(h, info, cnt, ffn_norm[layer].reshape(1, D), moe_w1[j].astype(BF16),
                     moe_w3[j].astype(BF16), moe_w2[j].astype(BF16))
    return h.reshape(B, S, D)
```

```python
import functools
import math

import jax
import jax.numpy as jnp
import numpy as np
from jax import lax
from jax.experimental import pallas as pl
from jax.experimental.pallas import tpu as pltpu

F32 = jnp.float32
BF16 = jnp.bfloat16

D_MODEL = 1024
DA_HEADS = 4
DA_HEAD_DIM = 64
DA_V_DIM = 128
ROPE_THETA = 500000.0
ROPE_ROT_DIM = 16
RET_HEADS = 4
RET_QK_DIM = 64
RET_V_DIM = 128
RET_CHUNK = 128
RET_ROPE_THETA = 10000.0
D_FF = 2816
N_EXPERTS = 8
EPS = 1e-6
LOG2_E = math.log2(math.e)
DA_VT_ROWS = DA_V_DIM + 16
FIXED_REF_MAX_BOUND = 60.0
GROUP = 4

LANES = 128
SUBLANES = 8
MXU_DIM = 256
HEAD = 64
assert DA_HEAD_DIM == RET_QK_DIM == HEAD and LANES == 2 * HEAD
DA_W = DA_HEADS * DA_V_DIM
RET_W = RET_HEADS * RET_V_DIM
RET_QK_W = RET_HEADS * RET_QK_DIM
ROW_BLOCK = 512
EXPERT_TILE = 512
DISPATCH_BLOCK = 2048
COMBINE_BLOCK = 1024
OFF_QA, OFF_KA, OFF_VA = 0, 512, 1024
OFF_QR, OFF_KR, OFF_VR, OFF_GR = 1536, 1792, 2048, 2560
OFF_GA, OFF_GB = 3072, 4096
IN_W = 5120
PROJ_CHUNK = 512
FF_CHUNK = 256
VMEM_LIMIT = 56 * 1024 * 1024


def _cparams(sem):
    return pltpu.CompilerParams(dimension_semantics=sem, vmem_limit_bytes=VMEM_LIMIT)


def _const_spec(shape):
    nd = len(shape)
    return pl.BlockSpec(shape, lambda *_: (0,) * nd)


def _rope_table_kernel(pos_ref, inv_ref, sgn_ref, cos_ref, sin_ref):
    ang = pos_ref[...] * inv_ref[...]
    cos_ref[...] = jnp.cos(ang)
    sin_ref[...] = jnp.sin(ang) * sgn_ref[...]


def _rope_tables(pos_b, inv_row, sgn_row, tm):
    T = pos_b.shape[0]
    row = pl.BlockSpec((tm, LANES), lambda i: (i, 0))
    return pl.pallas_call(
        _rope_table_kernel,
        grid=(T // tm,),
        in_specs=[row, _const_spec((1, LANES)), _const_spec((1, LANES))],
        out_specs=[row, row],
        out_shape=[jax.ShapeDtypeStruct((T, LANES), F32)] * 2,
        compiler_params=_cparams(("parallel",)),
        name="rope_tables",
    )(pos_b, inv_row, sgn_row)


def _rope_rows(rot_dim, theta):
    half = rot_dim // 2
    inv = 1.0 / (theta ** (jnp.arange(half, dtype=F32) / half))
    jj = np.arange(LANES) % HEAD
    inv_row = jnp.where(jj < rot_dim, inv[jj % half], 0.0).astype(F32)
    sgn_row = np.where(jj < half, -1.0, np.where(jj < rot_dim, 1.0, 0.0)).astype(np.float32)
    return inv_row.reshape(1, LANES), jnp.asarray(sgn_row).reshape(1, LANES)


def _rotate_half(x, half):
    jj = lax.broadcasted_iota(jnp.int32, x.shape, 1) % HEAD
    return jnp.where(jj < half, pltpu.roll(x, LANES - half, 1), pltpu.roll(x, half, 1))


def _sigmoid(x):
    return 1.0 / (1.0 + jnp.exp(-x))


def _in_proj_kernel(h_ref, g_ref, w_ref, gq_ref, gk_ref, gsum_ref, ca_ref, sa_ref, cr_ref, sr_ref, z_ref, vt_ref):
    half = h_ref.shape[0] // 2
    for r in (slice(0, half), slice(half, 2 * half)):
        _in_proj_rows(r, h_ref, g_ref, w_ref, gq_ref, gk_ref, gsum_ref, ca_ref, sa_ref, cr_ref, sr_ref, z_ref,
                      vt_ref)


def _in_proj_rows(r, h_ref, g_ref, w_ref, gq_ref, gk_ref, gsum_ref, ca_ref, sa_ref, cr_ref, sr_ref, z_ref, vt_ref):
    x = h_ref[r, :]
    ms = jnp.mean(x * x, axis=-1, keepdims=True)
    hn = (x * lax.rsqrt(ms + EPS) * g_ref[...]).astype(BF16)
    ca, sa, cr, sr = ca_ref[r, :], sa_ref[r, :], cr_ref[r, :], sr_ref[r, :]

    def head_rms(zc, gain):
        sq = (zc * zc).astype(BF16)
        outs = []
        for s in range(PROJ_CHUNK // MXU_DIM):
            sl = slice(s * MXU_DIM, (s + 1) * MXU_DIM)
            tot = jnp.dot(sq[:, sl], gsum_ref[...], preferred_element_type=F32)
            outs.append(zc[:, sl] * lax.rsqrt(tot * (1.0 / HEAD) + EPS))
        return jnp.concatenate(outs, axis=1) * gain

    def rope(y, cos, sin, half):
        outs = []
        for s in range(y.shape[1] // LANES):
            ys = y[:, s * LANES:(s + 1) * LANES]
            outs.append(ys * cos + _rotate_half(ys, half) * sin)
        return jnp.concatenate(outs, axis=1)

    for c in range(IN_W // PROJ_CHUNK):
        c0 = c * PROJ_CHUNK
        zc = jnp.dot(hn, w_ref[:, c0:c0 + PROJ_CHUNK], preferred_element_type=F32)
        if c0 == OFF_QA:
            zc = rope(head_rms(zc, gq_ref[...]), ca, sa, ROPE_ROT_DIM // 2)
            zc = zc * (DA_HEAD_DIM ** -0.5 * LOG2_E)
        elif c0 == OFF_KA:
            zc = rope(head_rms(zc, gk_ref[...]), ca, sa, ROPE_ROT_DIM // 2)
        elif c0 == OFF_VA:
            zt = zc.T.astype(BF16)
            extra = lax.broadcasted_iota(jnp.int32, (DA_VT_ROWS - DA_V_DIM, zt.shape[1]), 0)
            for hd in range(DA_HEADS):
                vt_ref[0, hd, 0:DA_V_DIM, r] = zt[hd * DA_V_DIM:(hd + 1) * DA_V_DIM, :]
                vt_ref[0, hd, DA_V_DIM:DA_VT_ROWS, r] = jnp.where(extra == 0, 1.0, 0.0).astype(BF16)
        elif c0 == OFF_QR:
            zc = rope(zc, cr, sr, RET_QK_DIM // 2)
            col = lax.broadcasted_iota(jnp.int32, zc.shape, 1)
            zc = jnp.where(col >= OFF_KR - OFF_QR, zc * (RET_QK_DIM ** -0.5), zc)
        elif c0 == OFF_GR:
            zc = zc * _sigmoid(zc)
        elif c0 >= OFF_GA:
            zc = _sigmoid(zc)
        z_ref[r, c0:c0 + PROJ_CHUNK] = zc.astype(BF16)


def _in_proj(h, g, w, gq, gk, gsum, tabs, tm):
    T = h.shape[0]
    row = lambda width: pl.BlockSpec((tm, width), lambda i: (i, 0))
    return pl.pallas_call(
        _in_proj_kernel,
        grid=(T // tm,),
        in_specs=[row(D_MODEL), _const_spec((1, D_MODEL)), _const_spec((D_MODEL, IN_W)),
                  _const_spec((1, PROJ_CHUNK)), _const_spec((1, PROJ_CHUNK)), _const_spec((MXU_DIM, MXU_DIM)),
                  row(LANES), row(LANES), row(LANES), row(LANES)],
        out_specs=[row(IN_W), pl.BlockSpec((1, DA_HEADS, DA_VT_ROWS, tm), lambda i: (i, 0, 0, 0))],
        out_shape=[jax.ShapeDtypeStruct((T, IN_W), BF16),
                   jax.ShapeDtypeStruct((T // tm, DA_HEADS, DA_VT_ROWS, tm), BF16)],
        compiler_params=_cparams(("parallel",)),
        name="in_proj",
    )(h, g, w, gq, gk, gsum, *tabs)


def _diff_attn_kernel(bound_ref, q_ref, k_ref, vt_ref, lam_ref, subln_ref, o_ref, acc_sc, st_sc, *, tq,
                      lambda_init):
    qi = pl.program_id(2)
    q = q_ref[...]
    lane = lax.broadcasted_iota(jnp.int32, q.shape, 1)
    qs = (jnp.where(lane < DA_HEAD_DIM, q, jnp.zeros_like(q)),
          jnp.where(lane >= DA_HEAD_DIM, q, jnp.zeros_like(q)))
    acc_sc[...] = jnp.zeros(acc_sc.shape, F32)

    tqh = tq // 2
    streams = [(c, hf) for hf in range(2) for c in range(2)]
    qsub = [qs[c][hf * tqh:(hf + 1) * tqh, :] for c, hf in streams]

    def scores(ki, diagonal=False):
        k = k_ref[pl.ds(pl.multiple_of(ki * tq, tq), tq), :]
        return [lax.dot_general(k[0:tqh] if diagonal and hf == 0 else k, qq, (((1,), (1,)), ((), ())),
                                preferred_element_type=F32) for qq, (c, hf) in zip(qsub, streams)]

    def stash(sts):
        for s in range(len(streams)):
            st_sc[s] = sts[s]

    bound = bound_ref[0, 0]

    def masked_scores(st, hf):
        key = lax.broadcasted_iota(jnp.int32, st.shape, 0)
        qry = lax.broadcasted_iota(jnp.int32, st.shape, 1) + hf * tqh
        return jnp.where(key <= qry, st, -jnp.inf)

    def consume_online(sts, ki, stats, masked):
        vt = vt_ref[ki, 0]
        out = []
        for s, (c, hf) in enumerate(streams):
            st = st_sc[s] if sts is None else sts[s]
            if masked:
                st = masked_scores(st, hf)
            cols = slice(hf * tqh, (hf + 1) * tqh)
            m_old = stats[s]
            m_new = jnp.maximum(m_old, jnp.max(st, axis=0, keepdims=True))
            alpha = jnp.exp2(m_old - m_new)
            pt = jnp.exp2((st - m_new).astype(BF16))
            acc_sc[c, :, cols] = alpha * acc_sc[c, :, cols] + jnp.dot(vt, pt, preferred_element_type=F32)
            out.append(m_new)
        return tuple(out)

    def run_fixed_ref():
        def accumulate(blocks, last_masked):
            diag = [last_masked and u == len(blocks) - 1 for u in range(len(blocks))]
            sts = [scores(b, dg) for b, dg in zip(blocks, diag)]
            tot = [None] * len(streams)
            den = [None] * len(streams)
            for b, st4, dg in zip(blocks, sts, diag):
                vt = vt_ref[b, 0, 0:DA_V_DIM, :]
                for s, (c, hf) in enumerate(streams):
                    st = masked_scores(st4[s], hf) if dg else st4[s]
                    keys = st.shape[0]
                    pt = jnp.exp2(st - bound)
                    part = jnp.sum(pt.reshape(keys // SUBLANES, SUBLANES, tqh), axis=0)
                    d = jnp.dot(vt[:, 0:keys], pt.astype(BF16), preferred_element_type=F32)
                    tot[s] = d if tot[s] is None else tot[s] + d
                    den[s] = part if den[s] is None else den[s] + part
            for s, (c, hf) in enumerate(streams):
                cols = slice(hf * tqh, (hf + 1) * tqh)
                acc_sc[c, 0:DA_V_DIM, cols] += tot[s]
                acc_sc[c, DA_V_DIM:DA_V_DIM + 1, cols] += jnp.sum(den[s], axis=0, keepdims=True)

        def group(j, carry):
            accumulate([GROUP * j + u for u in range(GROUP)], False)
            return carry

        ngroups = qi // GROUP
        lax.fori_loop(0, ngroups, group, 0)
        for rest in range(GROUP):
            pl.when(qi - GROUP * ngroups == rest)(
                functools.partial(accumulate, [qi - rest + u for u in range(rest + 1)], True))

    def run_online():
        npairs = qi // 2
        stash(scores(0))

        def pair(j, stats):
            b0 = 2 * j
            odd = scores(b0 + 1)
            stats = consume_online(None, b0, stats, False)
            stash(scores(b0 + 2))
            return consume_online(odd, b0 + 1, stats, False)

        def leftover(stats):
            stats = consume_online(None, qi - 1, stats, False)
            stash(scores(qi))
            return stats

        init = (jnp.full((1, tqh), -jnp.inf, F32),) * len(streams)
        stats = lax.fori_loop(0, npairs, pair, init)
        stats = lax.cond(qi - 2 * npairs == 1, leftover, lambda st: st, stats)
        consume_online(None, qi, stats, True)

    pl.when(bound <= FIXED_REF_MAX_BOUND)(run_fixed_ref)
    pl.when(bound > FIXED_REF_MAX_BOUND)(run_online)

    lv = lam_ref[...]
    lam = (jnp.exp(jnp.sum(lv[0:1] * lv[1:2], axis=1, keepdims=True))
           - jnp.exp(jnp.sum(lv[2:3] * lv[3:4], axis=1, keepdims=True)) + lambda_init)
    dv = DA_V_DIM
    ot = (acc_sc[0, 0:dv, :] / acc_sc[0, dv:dv + 1, :]
          - lam * (acc_sc[1, 0:dv, :] / acc_sc[1, dv:dv + 1, :]))
    yt = ot * lax.rsqrt(jnp.mean(ot * ot, axis=0, keepdims=True) + EPS)
    o_ref[...] = (yt.T * subln_ref[...] * (1.0 - lambda_init)).astype(BF16)


def _score_bound(gq, gk):
    norm = math.sqrt(DA_HEAD_DIM)
    return (1.01 * norm * norm * DA_HEAD_DIM ** -0.5 * LOG2_E
            * jnp.max(jnp.abs(gq)) * jnp.max(jnp.abs(gk))).astype(F32).reshape(1, 1)


def _diff_attn(z, vt, bound, lam_rows, subln, B, S, tq, lambda_init):
    T = B * S
    nq = S // tq
    kern = functools.partial(_diff_attn_kernel, tq=tq, lambda_init=lambda_init)
    return pl.pallas_call(
        kern,
        grid=(B, DA_HEADS, nq),
        in_specs=[pl.BlockSpec((1, 1), lambda b, h, i: (0, 0), memory_space=pltpu.SMEM),
                  pl.BlockSpec((tq, LANES), lambda b, h, i: (b * nq + i, OFF_QA // LANES + h)),
                  pl.BlockSpec((S, LANES), lambda b, h, i: (b, OFF_KA // LANES + h)),
                  pl.BlockSpec((nq, 1, DA_VT_ROWS, tq), lambda b, h, i: (b, h, 0, 0)),
                  _const_spec((4, DA_HEAD_DIM)), _const_spec((1, DA_V_DIM))],
        out_specs=pl.BlockSpec((tq, LANES), lambda b, h, i: (b * nq + i, h)),
        out_shape=jax.ShapeDtypeStruct((T, DA_HEADS * DA_V_DIM), BF16),
        scratch_shapes=[pltpu.VMEM((2, DA_VT_ROWS, tq), F32), pltpu.VMEM((4, tq, tq // 2), F32)],
        compiler_params=_cparams(("parallel", "parallel", "arbitrary")),
        name="diff_attn",
    )(bound, z, z, vt, lam_rows, subln)


def _retention_kernel(q_ref, k_ref, v_ref, g_ref, dec_ref, qdec_ref, kdec_ref, cdec_ref, o_ref, r_sc, o_sc, *,
                      tc):
    @pl.when(pl.program_id(1) == 0)
    def _():
        r_sc[...] = jnp.zeros(r_sc.shape, F32)

    C = RET_CHUNK
    lane = lax.broadcasted_iota(jnp.int32, (C, LANES), 1)
    rowi = lax.broadcasted_iota(jnp.int32, (C, LANES), 0)
    units = [(n, j) for n in range(tc // C) for j in range(RET_HEADS // 2)]
    rows = lambda n: slice(n * C, (n + 1) * C)
    cols = lambda j: slice(j * LANES, (j + 1) * LANES)
    vcol = lambda h: slice(h * RET_V_DIM, (h + 1) * RET_V_DIM)
    own = lambda hh: (lane >= RET_QK_DIM) if hh else (lane < RET_QK_DIM)
    s, kv = {}, {}
    for n, j in units:
        qp = q_ref[rows(n), cols(j)]
        kp = k_ref[rows(n), cols(j)]
        kdt = (kp.astype(F32) * kdec_ref[:, cols(j)]).T.astype(BF16)
        both = []
        for hh in range(2):
            h = 2 * j + hh
            s[n, h] = lax.dot_general(jnp.where(own(hh), qp, jnp.zeros_like(qp)), kp, (((1,), (1,)), ((), ())),
                                      preferred_element_type=F32)
            both.append(jnp.dot(kdt, v_ref[rows(n), vcol(h)], preferred_element_type=F32))
        kv[n, j] = jnp.where(rowi < RET_QK_DIM, both[0], both[1])
    for n, j in units:
        for hh in range(2):
            h = 2 * j + hh
            inner = (s[n, h] * dec_ref[h]).astype(BF16)
            o_sc[rows(n), vcol(h)] = jnp.dot(inner, v_ref[rows(n), vcol(h)], preferred_element_type=F32)
    state = [r_sc[j] for j in range(RET_HEADS // 2)]
    for n, j in units:
        qp = q_ref[rows(n), cols(j)]
        qd = (qp.astype(F32) * qdec_ref[:, cols(j)]).astype(BF16)
        r_b = state[j].astype(BF16)
        for hh in range(2):
            h = 2 * j + hh
            o_sc[rows(n), vcol(h)] += jnp.dot(jnp.where(own(hh), qd, jnp.zeros_like(qd)), r_b,
                                              preferred_element_type=F32)
        state[j] = state[j] * cdec_ref[j] + kv[n, j]
    for j in range(RET_HEADS // 2):
        r_sc[j] = state[j]
    for h in range(RET_HEADS):
        gs = slice(h * RET_V_DIM, (h + 1) * RET_V_DIM)
        o = o_sc[:, gs]
        oc = o - jnp.mean(o, axis=-1, keepdims=True)
        y = oc * lax.rsqrt(jnp.mean(oc * oc, axis=-1, keepdims=True) + EPS)
        o_ref[:, gs] = (y * g_ref[:, gs].astype(F32)).astype(BF16)


def _retention(z, consts, B, S, tc):
    T = B * S
    n = S // tc
    dec, qdec, kdec, cdec = consts
    kern = functools.partial(_retention_kernel, tc=tc)
    blk = lambda width, off: pl.BlockSpec((tc, width), lambda b, i: (b * n + i, off // width))
    return pl.pallas_call(
        kern,
        grid=(B, n),
        in_specs=[blk(RET_QK_W, OFF_QR), blk(RET_QK_W, OFF_KR), blk(RET_W, OFF_VR), blk(RET_W, OFF_GR),
                  _const_spec(dec.shape), _const_spec(qdec.shape), _const_spec(kdec.shape),
                  _const_spec(cdec.shape)],
        out_specs=pl.BlockSpec((tc, RET_W), lambda b, i: (b * n + i, 0)),
        out_shape=jax.ShapeDtypeStruct((T, RET_HEADS * RET_V_DIM), BF16),
        scratch_shapes=[pltpu.VMEM((RET_HEADS // 2, LANES, LANES), F32),
                        pltpu.VMEM((tc, RET_HEADS * RET_V_DIM), F32)],
        compiler_params=_cparams(("parallel", "arbitrary")),
        name="retention",
    )(z, z, z, z, dec, qdec, kdec, cdec)


def _retention_consts():
    C = RET_CHUNK
    log_gamma = jnp.log1p(-jnp.exp2(-5.0 - jnp.arange(RET_HEADS, dtype=F32)))
    idx = jnp.arange(C, dtype=F32)
    rel = idx[:, None] - idx[None, :]
    dec = jnp.where(rel[None] >= 0, jnp.exp(log_gamma[:, None, None] * jnp.maximum(rel, 0.0)[None]), 0.0)
    q_dec = jnp.exp(log_gamma[None, :] * (idx + 1.0)[:, None])
    k_dec = jnp.exp(log_gamma[None, :] * (C - 1.0 - idx)[:, None])
    qdec = jnp.repeat(q_dec, RET_QK_DIM, axis=1)
    kdec = jnp.repeat(k_dec, RET_QK_DIM, axis=1)
    chunk_decay = jnp.repeat(jnp.exp(log_gamma * C), RET_QK_DIM)
    cdec = jnp.broadcast_to(chunk_decay.reshape(RET_HEADS // 2, LANES, 1), (RET_HEADS // 2, LANES, LANES))
    return dec.astype(F32), qdec.astype(F32), kdec.astype(F32), cdec.astype(F32)


R_I1, R_I2, R_W1, R_W2, R_RANK1, R_RANK2 = range(6)


def _route(hn, wr_ref, rows, info_ref, carry_sc):
    w = wr_ref[...]
    hn_hi, w_hi = hn.astype(BF16), w.astype(BF16)
    hn_lo, w_lo = (hn - hn_hi.astype(F32)).astype(BF16), (w - w_hi.astype(F32)).astype(BF16)
    logits = (jnp.dot(hn_hi, w_hi, preferred_element_type=F32) + jnp.dot(hn_hi, w_lo, preferred_element_type=F32)
              + jnp.dot(hn_lo, w_hi, preferred_element_type=F32))
    lane = lax.broadcasted_iota(jnp.int32, logits.shape, 1)
    logits = jnp.where(lane < N_EXPERTS, logits, -jnp.inf)
    m1 = jnp.max(logits, axis=1, keepdims=True)
    i1 = jnp.min(jnp.where(logits == m1, lane, LANES), axis=1, keepdims=True)
    rest = jnp.where(lane == i1, -jnp.inf, logits)
    m2 = jnp.max(rest, axis=1, keepdims=True)
    i2 = jnp.min(jnp.where(rest == m2, lane, LANES), axis=1, keepdims=True)
    e = jnp.exp(m2 - m1)
    w1 = 1.0 / (1.0 + e)
    w2 = e / (1.0 + e)
    sel = jnp.where(lane == i1, 1.0, jnp.where(lane == i2, 1.0, 0.0))
    n = sel.shape[0]
    rr = lax.broadcasted_iota(jnp.int32, (n, n), 0)
    cc = lax.broadcasted_iota(jnp.int32, (n, n), 1)
    ltri = jnp.where(cc < rr, 1.0, 0.0).astype(BF16)
    before = jnp.dot(ltri, sel.astype(BF16), preferred_element_type=F32) + carry_sc[0:1, :]
    r1 = jnp.sum(jnp.where(lane == i1, before, 0.0), axis=1, keepdims=True)
    r2 = jnp.sum(jnp.where(lane == i2, before, 0.0), axis=1, keepdims=True)
    info = jnp.where(lane == R_I1, i1.astype(F32), 0.0)
    for ln, val in ((R_I2, i2.astype(F32)), (R_W1, w1), (R_W2, w2), (R_RANK1, r1), (R_RANK2, r2)):
        info = jnp.where(lane == ln, val, info)
    info_ref[rows, :] = info
    carry_sc[...] = carry_sc[...] + jnp.sum(sel, axis=0, keepdims=True)


def _merge_kernel(*refs, route):
    if route:
        (ya_ref, yr_ref, ga_ref, gb_ref, h_ref, wa_ref, wr_ref, wo_ref, gf_ref, wrt_ref,
         h_out, info_out, cnt_out, carry_sc) = refs

        @pl.when(pl.program_id(0) == 0)
        def _():
            carry_sc[...] = jnp.zeros(carry_sc.shape, F32)
    else:
        ya_ref, yr_ref, ga_ref, gb_ref, h_ref, wa_ref, wr_ref, wo_ref, gf_ref, h_out, hn_out = refs
    half = ya_ref.shape[0] // 2
    parts = [slice(0, half), slice(half, 2 * half)]
    branch = [(jnp.dot(ya_ref[r, :], wa_ref[...], preferred_element_type=F32),
               jnp.dot(yr_ref[r, :], wr_ref[...], preferred_element_type=F32)) for r in parts]
    for r, (pa, pr) in zip(parts, branch):
        merged = ga_ref[r, :].astype(F32) * pa + gb_ref[r, :].astype(F32) * pr
        hnew = h_ref[r, :] + jnp.dot(merged.astype(BF16), wo_ref[...], preferred_element_type=F32)
        h_out[r, :] = hnew
        ms = jnp.mean(hnew * hnew, axis=-1, keepdims=True)
        hn = hnew * lax.rsqrt(ms + EPS) * gf_ref[...]
        if route:
            _route(hn, wrt_ref, r, info_out, carry_sc)
        else:
            hn_out[r, :] = hn.astype(BF16)
    if route:
        cnt_out[...] = carry_sc[...]


def _merge(ya, yr, z, h, wa, wr, wo, gf, tm, w_router=None):
    T = h.shape[0]
    route = w_router is not None
    row = lambda width, cb=0: pl.BlockSpec((tm, width), lambda i: (i, cb))
    in_specs = [row(DA_W), row(RET_W), row(D_MODEL, OFF_GA // D_MODEL), row(D_MODEL, OFF_GB // D_MODEL),
                row(D_MODEL), _const_spec(wa.shape), _const_spec(wr.shape), _const_spec(wo.shape),
                _const_spec((1, D_MODEL))]
    args = [ya, yr, z, z, h, wa, wr, wo, gf]
    if route:
        in_specs.append(_const_spec((D_MODEL, LANES)))
        args.append(jnp.pad(w_router, ((0, 0), (0, LANES - N_EXPERTS))))
        out_specs = [row(D_MODEL), row(LANES), _const_spec((SUBLANES, LANES))]
        out_shape = [jax.ShapeDtypeStruct((T, D_MODEL), F32), jax.ShapeDtypeStruct((T, LANES), F32),
                     jax.ShapeDtypeStruct((SUBLANES, LANES), F32)]
        scratch = [pltpu.VMEM((SUBLANES, LANES), F32)]
    else:
        out_specs = [row(D_MODEL), row(D_MODEL)]
        out_shape = [jax.ShapeDtypeStruct((T, D_MODEL), F32), jax.ShapeDtypeStruct((T, D_MODEL), BF16)]
        scratch = []
    return pl.pallas_call(
        functools.partial(_merge_kernel, route=route),
        grid=(T // tm,),
        in_specs=in_specs,
        out_specs=out_specs,
        out_shape=out_shape,
        scratch_shapes=scratch,
        compiler_params=_cparams(("arbitrary",) if route else ("parallel",)),
        name="merge_route" if route else "merge_out",
    )(*args)


def _swiglu_into(acc_sc, x, w1_ref, w3_ref, w2_ref):
    chunks = [slice(c * FF_CHUNK, (c + 1) * FF_CHUNK) for c in range(D_FF // FF_CHUNK)]
    up = lambda cs: (jnp.dot(x, w1_ref[:, cs], preferred_element_type=F32),
                     jnp.dot(x, w3_ref[:, cs], preferred_element_type=F32))
    uv = up(chunks[0])
    for c, cs in enumerate(chunks):
        nxt = up(chunks[c + 1]) if c + 1 < len(chunks) else None
        a = (uv[0] * _sigmoid(uv[0]) * uv[1]).astype(BF16)
        d = jnp.dot(a, w2_ref[cs, :], preferred_element_type=F32)
        if c == 0:
            acc_sc[...] = d
        else:
            acc_sc[...] += d
        uv = nxt


def _ffn_kernel(hn_ref, prev_ref, w1_ref, w3_ref, w2_ref, o_ref, acc_sc):
    _swiglu_into(acc_sc, hn_ref[...], w1_ref, w3_ref, w2_ref)
    o_ref[...] = prev_ref[...] + acc_sc[...]


def _ffn(hn, prev, w1, w3, w2, tm):
    T = hn.shape[0]
    row = lambda width: pl.BlockSpec((tm, width), lambda i: (i, 0))
    wspec = lambda w: pl.BlockSpec(w.shape, lambda i: (0, 0), pipeline_mode=pl.Buffered(1))
    return pl.pallas_call(
        _ffn_kernel,
        grid=(T // tm,),
        in_specs=[row(D_MODEL), row(D_MODEL), wspec(w1), wspec(w3), wspec(w2)],
        out_specs=row(D_MODEL),
        out_shape=jax.ShapeDtypeStruct((T, D_MODEL), F32),
        scratch_shapes=[pltpu.VMEM((tm, D_MODEL), F32)],
        input_output_aliases={1: 0},
        compiler_params=_cparams(("parallel",)),
        name="ffn",
    )(hn, prev, w1, w3, w2)


def _dispatch_kernel(pad_lo_ref, pad_hi_ref, pos_ref, h_ref, xs_hbm, zero_sc, sem, pad_sem, *, tb):
    @pl.when(pl.program_id(0) == 0)
    def _():
        zero_sc[...] = jnp.zeros(zero_sc.shape, F32)
        zero_row = zero_sc.at[pl.ds(0, 1)]
        for e in range(N_EXPERTS):
            def fill(r, carry):
                pltpu.make_async_copy(zero_row, xs_hbm.at[pl.ds(r, 1)], pad_sem).start()
                return carry

            lax.fori_loop(pad_lo_ref[e], pad_hi_ref[e], fill, 0)
        for e in range(N_EXPERTS):
            def drain(r, carry):
                pltpu.make_async_copy(zero_row, xs_hbm.at[pl.ds(r, 1)], pad_sem).wait()
                return carry

            lax.fori_loop(pad_lo_ref[e], pad_hi_ref[e], drain, 0)

    def issue(i, carry):
        t0 = pl.multiple_of(i * SUBLANES, SUBLANES)
        for u in range(SUBLANES):
            for k in range(2):
                p = pos_ref[0, 0, k * tb + t0 + u]
                pltpu.make_async_copy(h_ref.at[pl.ds(t0 + u, 1)], xs_hbm.at[pl.ds(p, 1)], sem).start(priority=k)
        return carry

    lax.fori_loop(0, tb // SUBLANES, issue, 0)

    for k in range(2):
        pltpu.make_async_copy(h_ref, xs_hbm.at[pl.ds(0, tb)], sem).wait()


def _dispatch(pad_lo, pad_hi, pos_blocks, h, n_rows, tb):
    T = h.shape[0]
    return pl.pallas_call(
        functools.partial(_dispatch_kernel, tb=tb),
        grid_spec=pltpu.PrefetchScalarGridSpec(
            num_scalar_prefetch=2,
            grid=(T // tb,),
            in_specs=[pl.BlockSpec((1, 1, 2 * tb), lambda i, lo, hi: (i, 0, 0), memory_space=pltpu.SMEM),
                      pl.BlockSpec((tb, D_MODEL), lambda i, lo, hi: (i, 0))],
            out_specs=pl.BlockSpec(memory_space=pl.ANY),
            scratch_shapes=[pltpu.VMEM((SUBLANES, D_MODEL), F32), pltpu.SemaphoreType.DMA(()),
                            pltpu.SemaphoreType.DMA(())]),
        out_shape=jax.ShapeDtypeStruct((n_rows, D_MODEL), F32),
        compiler_params=pltpu.CompilerParams(dimension_semantics=("arbitrary",), has_side_effects=True),
        name="moe_dispatch",
    )(pad_lo, pad_hi, pos_blocks, h)


def _expert_kernel(te_ref, nu_ref, xs_ref, g_ref, w1_ref, w3_ref, w2_ref, y_ref, acc_sc):
    del te_ref
    i = pl.program_id(0)

    @pl.when(i < nu_ref[0])
    def _():
        half = xs_ref.shape[0] // 2
        for r in (slice(0, half), slice(half, 2 * half)):
            x = xs_ref[r, :]
            ms = jnp.mean(x * x, axis=-1, keepdims=True)
            hn = (x * lax.rsqrt(ms + EPS) * g_ref[...]).astype(BF16)
            _swiglu_into(acc_sc.at[r, :], hn, w1_ref.at[0], w3_ref.at[0], w2_ref.at[0])
            y_ref[r, :] = acc_sc[r, :]

    @pl.when(i >= nu_ref[0])
    def _():
        y_ref[...] = jnp.zeros(y_ref.shape, F32)


def _experts(tile_expert, n_used, xs, g, w1, w3, w2, te_rows):
    P = xs.shape[0]
    row = pl.BlockSpec((te_rows, D_MODEL), lambda i, te, nu: (jnp.minimum(i, nu[0] - 1), 0))
    wspec = lambda w: pl.BlockSpec((1,) + w.shape[1:], lambda i, te, nu: (te[i], 0, 0))
    return pl.pallas_call(
        _expert_kernel,
        grid_spec=pltpu.PrefetchScalarGridSpec(
            num_scalar_prefetch=2,
            grid=(P // te_rows,),
            in_specs=[row, pl.BlockSpec((1, D_MODEL), lambda i, te, nu: (0, 0)), wspec(w1), wspec(w3), wspec(w2)],
            out_specs=pl.BlockSpec((te_rows, D_MODEL), lambda i, te, nu: (i, 0)),
            scratch_shapes=[pltpu.VMEM((te_rows, D_MODEL), F32)]),
        out_shape=jax.ShapeDtypeStruct((P, D_MODEL), F32),
        compiler_params=_cparams(("arbitrary",)),
        name="moe_experts",
    )(tile_expert, n_used, xs, g, w1, w3, w2)


def _combine_kernel(pos_ref, info_ref, h_ref, y_hbm, o_ref, g_sc, sem, *, tb):
    def issue(i, carry):
        t0 = pl.multiple_of(i * SUBLANES, SUBLANES)
        for u in range(SUBLANES):
            for k in range(2):
                p = pos_ref[0, 0, k * tb + t0 + u]
                pltpu.make_async_copy(y_hbm.at[pl.ds(p, 1)], g_sc.at[k, pl.ds(t0 + u, 1)], sem).start(priority=k)
        return carry

    lax.fori_loop(0, tb // SUBLANES, issue, 0)

    for k in range(2):
        pltpu.make_async_copy(y_hbm.at[pl.ds(0, tb)], g_sc.at[k], sem).wait()
    info = info_ref[...]
    o_ref[...] = h_ref[...] + info[:, R_W1:R_W1 + 1] * g_sc[0] + info[:, R_W2:R_W2 + 1] * g_sc[1]


def _combine(pos_blocks, info, h, y, tb):
    T = h.shape[0]
    row = lambda width: pl.BlockSpec((tb, width), lambda i: (i, 0))
    return pl.pallas_call(
        functools.partial(_combine_kernel, tb=tb),
        grid=(T // tb,),
        in_specs=[pl.BlockSpec((1, 1, 2 * tb), lambda i: (i, 0, 0), memory_space=pltpu.SMEM),
                  row(LANES), row(D_MODEL), pl.BlockSpec(memory_space=pl.ANY)],
        out_specs=row(D_MODEL),
        out_shape=jax.ShapeDtypeStruct((T, D_MODEL), F32),
        scratch_shapes=[pltpu.VMEM((2, tb, D_MODEL), F32), pltpu.SemaphoreType.DMA(())],
        compiler_params=_cparams(("arbitrary",)),
        name="moe_combine",
    )(pos_blocks, info, h, y)


def _pos_blocks(pos1, pos2, tb):
    nb = pos1.shape[0] // tb
    return jnp.concatenate([pos1.reshape(nb, 1, tb), pos2.reshape(nb, 1, tb)], axis=2)


def _moe(h, info, cnt, g, w1, w3, w2):
    T = h.shape[0]
    te_rows = min(EXPERT_TILE, T)
    counts = cnt[0, :N_EXPERTS].astype(jnp.int32)
    padded = ((counts + te_rows - 1) // te_rows) * te_rows
    ends = jnp.cumsum(padded)
    base = ends - padded
    i1 = info[:, R_I1].astype(jnp.int32)
    i2 = info[:, R_I2].astype(jnp.int32)
    pos1 = jnp.take(base, i1) + info[:, R_RANK1].astype(jnp.int32)
    pos2 = jnp.take(base, i2) + info[:, R_RANK2].astype(jnp.int32)
    n_tiles = (2 * T) // te_rows + N_EXPERTS
    tile_expert = jnp.minimum(jnp.searchsorted(ends // te_rows, jnp.arange(n_tiles), side="right"),
                              N_EXPERTS - 1).astype(jnp.int32)
    n_used = (ends[-1:] // te_rows).astype(jnp.int32)

    tb_d = min(DISPATCH_BLOCK, T)
    xs = _dispatch(base + counts, ends, _pos_blocks(pos1, pos2, tb_d), h, n_tiles * te_rows, tb_d)
    y = _experts(tile_expert, n_used, xs, g, w1, w3, w2, te_rows)
    tb_c = min(COMBINE_BLOCK, T)
    return _combine(_pos_blocks(pos1, pos2, tb_c), info, h, y, tb_c)


def kernel(x, positions, attn_norm, w_in, da_q_norm, da_k_norm, lam_q1, lam_k1, lam_q2, lam_k2, da_subln,
           w_proj_da, w_proj_ret, w_out, ffn_norm, dense_w1, dense_w3, dense_w2, w_router, moe_w1, moe_w3,
           moe_w2):
    B, S, D = x.shape
    T = B * S
    depth = attn_norm.shape[0]
    tm = tq = tc = min(ROW_BLOCK, S)
    assert D == D_MODEL and S % tm == 0 and tm % (2 * MXU_DIM) == 0 and tm % RET_CHUNK == 0, (x.shape, tm)

    def rope_tables(pos):
        pos_b = jnp.broadcast_to(pos.reshape(-1, 1).astype(F32), (pos.size, LANES))
        return (_rope_tables(pos_b, *_rope_rows(ROPE_ROT_DIM, ROPE_THETA), tm)
                + _rope_tables(pos_b, *_rope_rows(RET_QK_DIM, RET_ROPE_THETA), tm))

    tabs = lax.cond(jnp.all(positions == positions[:1]),
                    lambda: tuple(jnp.tile(t, (B, 1)) for t in rope_tables(positions[0])),
                    lambda: tuple(rope_tables(positions.reshape(T))))
    ret_consts = _retention_consts()
    gsum = jnp.asarray(np.kron(np.eye(MXU_DIM // HEAD), np.ones((HEAD, HEAD))), BF16)

    h = x.reshape(T, D)
    for layer in range(depth):
        lambda_init = 0.8 - 0.6 * math.exp(-0.3 * layer)
        gq = jnp.tile(da_q_norm[layer], PROJ_CHUNK // DA_HEAD_DIM).reshape(1, PROJ_CHUNK)
        gk = jnp.tile(da_k_norm[layer], PROJ_CHUNK // DA_HEAD_DIM).reshape(1, PROJ_CHUNK)
        z, vt = _in_proj(h, attn_norm[layer].reshape(1, D), w_in[layer].astype(BF16), gq, gk, gsum, tabs, tm)
        lam_rows = jnp.stack([lam_q1[layer], lam_k1[layer], lam_q2[layer], lam_k2[layer]])
        ya = _diff_attn(z, vt, _score_bound(da_q_norm[layer], da_k_norm[layer]), lam_rows,
                        da_subln[layer].reshape(1, DA_V_DIM), B, S, tq, lambda_init)
        yr = _retention(z, ret_consts, B, S, tc)
        j = layer // 2
        merge_args = (ya, yr, z, h, w_proj_da[layer].astype(BF16), w_proj_ret[layer].astype(BF16),
                      w_out[layer].astype(BF16), ffn_norm[layer].reshape(1, D), tm)
        if layer % 2 == 0:
            h, hn = _merge(*merge_args)
            h = _ffn(hn, h, dense_w1[j].astype(BF16), dense_w3[j].astype(BF16), dense_w2[j].astype(BF16), tm)
        else:
            h, info, cnt = _merge(*merge_args, w_router=w_router[j])
            h = _moe(h, info, cnt, ffn_norm[layer].reshape(1, D), moe_w1[j].astype(BF16),
                     moe_w3[j].astype(BF16), moe_w2[j].astype(BF16))
    return h.reshape(B, S, D)
```

```python
import functools
import math

import jax
import jax.numpy as jnp
import numpy as np
from jax import lax
from jax.experimental import pallas as pl
from jax.experimental.pallas import tpu as pltpu

F32 = jnp.float32
BF16 = jnp.bfloat16

D_MODEL = 1024
DA_HEADS = 4
DA_HEAD_DIM = 64
DA_V_DIM = 128
ROPE_THETA = 500000.0
ROPE_ROT_DIM = 16
RET_HEADS = 4
RET_QK_DIM = 64
RET_V_DIM = 128
RET_CHUNK = 128
RET_ROPE_THETA = 10000.0
D_FF = 2816
N_EXPERTS = 8
EPS = 1e-6
LOG2_E = math.log2(math.e)
DA_VT_ROWS = DA_V_DIM + 16
FIXED_REF_MAX_BOUND = 60.0
GROUPS = (8, 4)

LANES = 128
SUBLANES = 8
MXU_DIM = 256
HEAD = 64
assert DA_HEAD_DIM == RET_QK_DIM == HEAD and LANES == 2 * HEAD
DA_W = DA_HEADS * DA_V_DIM
RET_W = RET_HEADS * RET_V_DIM
RET_QK_W = RET_HEADS * RET_QK_DIM
ROW_BLOCK = 512
EXPERT_TILE = 512
DISPATCH_BLOCK = 2048
COMBINE_BLOCK = 1024
OFF_QA, OFF_KA, OFF_VA = 0, 512, 1024
OFF_QR, OFF_KR, OFF_VR, OFF_GR = 1536, 1792, 2048, 2560
OFF_GA, OFF_GB = 3072, 4096
IN_W = 5120
PROJ_CHUNK = 512
FF_CHUNK = 256
VMEM_LIMIT = 56 * 1024 * 1024


def _cparams(sem):
    return pltpu.CompilerParams(dimension_semantics=sem, vmem_limit_bytes=VMEM_LIMIT)


def _const_spec(shape):
    nd = len(shape)
    return pl.BlockSpec(shape, lambda *_: (0,) * nd)


def _rope_table_kernel(pos_ref, inv_ref, sgn_ref, cos_ref, sin_ref):
    ang = pos_ref[...] * inv_ref[...]
    cos_ref[...] = jnp.cos(ang)
    sin_ref[...] = jnp.sin(ang) * sgn_ref[...]


def _rope_tables(pos_b, inv_row, sgn_row, tm):
    T = pos_b.shape[0]
    row = pl.BlockSpec((tm, LANES), lambda i: (i, 0))
    return pl.pallas_call(
        _rope_table_kernel,
        grid=(T // tm,),
        in_specs=[row, _const_spec((1, LANES)), _const_spec((1, LANES))],
        out_specs=[row, row],
        out_shape=[jax.ShapeDtypeStruct((T, LANES), F32)] * 2,
        compiler_params=_cparams(("parallel",)),
        name="rope_tables",
    )(pos_b, inv_row, sgn_row)


def _rope_rows(rot_dim, theta):
    half = rot_dim // 2
    inv = 1.0 / (theta ** (jnp.arange(half, dtype=F32) / half))
    jj = np.arange(LANES) % HEAD
    inv_row = jnp.where(jj < rot_dim, inv[jj % half], 0.0).astype(F32)
    sgn_row = np.where(jj < half, -1.0, np.where(jj < rot_dim, 1.0, 0.0)).astype(np.float32)
    return inv_row.reshape(1, LANES), jnp.asarray(sgn_row).reshape(1, LANES)


def _rotate_half(x, half):
    jj = lax.broadcasted_iota(jnp.int32, x.shape, 1) % HEAD
    return jnp.where(jj < half, pltpu.roll(x, LANES - half, 1), pltpu.roll(x, half, 1))


def _sigmoid(x):
    return 1.0 / (1.0 + jnp.exp(-x))


def _in_proj_kernel(h_ref, g_ref, w_ref, gq_ref, gk_ref, gsum_ref, ca_ref, sa_ref, cr_ref, sr_ref, z_ref, vt_ref):
    half = h_ref.shape[0] // 2
    for r in (slice(0, half), slice(half, 2 * half)):
        _in_proj_rows(r, h_ref, g_ref, w_ref, gq_ref, gk_ref, gsum_ref, ca_ref, sa_ref, cr_ref, sr_ref, z_ref,
                      vt_ref)


def _in_proj_rows(r, h_ref, g_ref, w_ref, gq_ref, gk_ref, gsum_ref, ca_ref, sa_ref, cr_ref, sr_ref, z_ref, vt_ref):
    x = h_ref[r, :]
    ms = jnp.mean(x * x, axis=-1, keepdims=True)
    hn = (x * lax.rsqrt(ms + EPS) * g_ref[...]).astype(BF16)
    ca, sa, cr, sr = ca_ref[r, :], sa_ref[r, :], cr_ref[r, :], sr_ref[r, :]

    def head_rms(zc, gain):
        sq = (zc * zc).astype(BF16)
        outs = []
        for s in range(PROJ_CHUNK // MXU_DIM):
            sl = slice(s * MXU_DIM, (s + 1) * MXU_DIM)
            tot = jnp.dot(sq[:, sl], gsum_ref[...], preferred_element_type=F32)
            outs.append(zc[:, sl] * lax.rsqrt(tot * (1.0 / HEAD) + EPS))
        return jnp.concatenate(outs, axis=1) * gain

    def rope(y, cos, sin, half):
        outs = []
        for s in range(y.shape[1] // LANES):
            ys = y[:, s * LANES:(s + 1) * LANES]
            outs.append(ys * cos + _rotate_half(ys, half) * sin)
        return jnp.concatenate(outs, axis=1)

    for c in range(IN_W // PROJ_CHUNK):
        c0 = c * PROJ_CHUNK
        zc = jnp.dot(hn, w_ref[:, c0:c0 + PROJ_CHUNK], preferred_element_type=F32)
        if c0 == OFF_QA:
            zc = rope(head_rms(zc, gq_ref[...]), ca, sa, ROPE_ROT_DIM // 2)
            zc = zc * (DA_HEAD_DIM ** -0.5 * LOG2_E)
        elif c0 == OFF_KA:
            zc = rope(head_rms(zc, gk_ref[...]), ca, sa, ROPE_ROT_DIM // 2)
        elif c0 == OFF_VA:
            zt = zc.T.astype(BF16)
            extra = lax.broadcasted_iota(jnp.int32, (DA_VT_ROWS - DA_V_DIM, zt.shape[1]), 0)
            for hd in range(DA_HEADS):
                vt_ref[0, hd, 0:DA_V_DIM, r] = zt[hd * DA_V_DIM:(hd + 1) * DA_V_DIM, :]
                vt_ref[0, hd, DA_V_DIM:DA_VT_ROWS, r] = jnp.where(extra == 0, 1.0, 0.0).astype(BF16)
        elif c0 == OFF_QR:
            zc = rope(zc, cr, sr, RET_QK_DIM // 2)
            col = lax.broadcasted_iota(jnp.int32, zc.shape, 1)
            zc = jnp.where(col >= OFF_KR - OFF_QR, zc * (RET_QK_DIM ** -0.5), zc)
        elif c0 == OFF_GR:
            zc = zc * _sigmoid(zc)
        elif c0 >= OFF_GA:
            zc = _sigmoid(zc)
        z_ref[r, c0:c0 + PROJ_CHUNK] = zc.astype(BF16)


def _in_proj(h, g, w, gq, gk, gsum, tabs, tm):
    T = h.shape[0]
    row = lambda width: pl.BlockSpec((tm, width), lambda i: (i, 0))
    return pl.pallas_call(
        _in_proj_kernel,
        grid=(T // tm,),
        in_specs=[row(D_MODEL), _const_spec((1, D_MODEL)), _const_spec((D_MODEL, IN_W)),
                  _const_spec((1, PROJ_CHUNK)), _const_spec((1, PROJ_CHUNK)), _const_spec((MXU_DIM, MXU_DIM)),
                  row(LANES), row(LANES), row(LANES), row(LANES)],
        out_specs=[row(IN_W), pl.BlockSpec((1, DA_HEADS, DA_VT_ROWS, tm), lambda i: (i, 0, 0, 0))],
        out_shape=[jax.ShapeDtypeStruct((T, IN_W), BF16),
                   jax.ShapeDtypeStruct((T // tm, DA_HEADS, DA_VT_ROWS, tm), BF16)],
        compiler_params=_cparams(("parallel",)),
        name="in_proj",
    )(h, g, w, gq, gk, gsum, *tabs)


def _diff_attn_kernel(bound_ref, q_ref, k_ref, vt_ref, lam_ref, subln_ref, o_ref, acc_sc, st_sc, *, tq,
                      lambda_init):
    qi = pl.program_id(2)
    q = q_ref[...]
    lane = lax.broadcasted_iota(jnp.int32, q.shape, 1)
    qs = (jnp.where(lane < DA_HEAD_DIM, q, jnp.zeros_like(q)),
          jnp.where(lane >= DA_HEAD_DIM, q, jnp.zeros_like(q)))
    acc_sc[...] = jnp.zeros(acc_sc.shape, F32)

    tqh = tq // 2
    streams = [(c, hf) for hf in range(2) for c in range(2)]
    qsub = [qs[c][hf * tqh:(hf + 1) * tqh, :] for c, hf in streams]

    def scores(ki, diagonal=False):
        k = k_ref[pl.ds(pl.multiple_of(ki * tq, tq), tq), :]
        return [lax.dot_general(k[0:tqh] if diagonal and hf == 0 else k, qq, (((1,), (1,)), ((), ())),
                                preferred_element_type=F32) for qq, (c, hf) in zip(qsub, streams)]

    def stash(sts):
        for s in range(len(streams)):
            st_sc[s] = sts[s]

    bound = bound_ref[0, 0]

    def masked_scores(st, hf):
        key = lax.broadcasted_iota(jnp.int32, st.shape, 0)
        qry = lax.broadcasted_iota(jnp.int32, st.shape, 1) + hf * tqh
        return jnp.where(key <= qry, st, -jnp.inf)

    def consume_online(sts, ki, stats, masked):
        vt = vt_ref[ki, 0]
        out = []
        for s, (c, hf) in enumerate(streams):
            st = st_sc[s] if sts is None else sts[s]
            if masked:
                st = masked_scores(st, hf)
            cols = slice(hf * tqh, (hf + 1) * tqh)
            m_old = stats[s]
            m_new = jnp.maximum(m_old, jnp.max(st, axis=0, keepdims=True))
            alpha = jnp.exp2(m_old - m_new)
            pt = jnp.exp2((st - m_new).astype(BF16))
            acc_sc[c, :, cols] = alpha * acc_sc[c, :, cols] + jnp.dot(vt, pt, preferred_element_type=F32)
            out.append(m_new)
        return tuple(out)

    def run_fixed_ref():
        def accumulate(blocks, last_masked):
            diag = [last_masked and u == len(blocks) - 1 for u in range(len(blocks))]
            sts = [scores(b, dg) for b, dg in zip(blocks, diag)]
            tot = [None] * len(streams)
            den = [None] * len(streams)
            for b, st4, dg in zip(blocks, sts, diag):
                vt = vt_ref[b, 0, 0:DA_V_DIM, :]
                for s, (c, hf) in enumerate(streams):
                    st = masked_scores(st4[s], hf) if dg else st4[s]
                    keys = st.shape[0]
                    pt = jnp.exp2(st - bound)
                    part = jnp.sum(pt.reshape(keys // SUBLANES, SUBLANES, tqh), axis=0)
                    d = jnp.dot(vt[:, 0:keys], pt.astype(BF16), preferred_element_type=F32)
                    tot[s] = d if tot[s] is None else tot[s] + d
                    den[s] = part if den[s] is None else den[s] + part
            for s, (c, hf) in enumerate(streams):
                cols = slice(hf * tqh, (hf + 1) * tqh)
                acc_sc[c, 0:DA_V_DIM, cols] += tot[s]
                acc_sc[c, DA_V_DIM:DA_V_DIM + 1, cols] += jnp.sum(den[s], axis=0, keepdims=True)

        done = 0
        for g in GROUPS:
            def trip(j, carry, g=g, done=done):
                accumulate([done + g * j + u for u in range(g)], False)
                return carry

            ntrips = (qi - done) // g
            lax.fori_loop(0, ntrips, trip, 0)
            done = done + g * ntrips
        for rest in range(GROUPS[-1]):
            pl.when(qi - done == rest)(
                functools.partial(accumulate, [qi - rest + u for u in range(rest + 1)], True))

    def run_online():
        npairs = qi // 2
        stash(scores(0))

        def pair(j, stats):
            b0 = 2 * j
            odd = scores(b0 + 1)
            stats = consume_online(None, b0, stats, False)
            stash(scores(b0 + 2))
            return consume_online(odd, b0 + 1, stats, False)

        def leftover(stats):
            stats = consume_online(None, qi - 1, stats, False)
            stash(scores(qi))
            return stats

        init = (jnp.full((1, tqh), -jnp.inf, F32),) * len(streams)
        stats = lax.fori_loop(0, npairs, pair, init)
        stats = lax.cond(qi - 2 * npairs == 1, leftover, lambda st: st, stats)
        consume_online(None, qi, stats, True)

    pl.when(bound <= FIXED_REF_MAX_BOUND)(run_fixed_ref)
    pl.when(bound > FIXED_REF_MAX_BOUND)(run_online)

    lv = lam_ref[...]
    lam = (jnp.exp(jnp.sum(lv[0:1] * lv[1:2], axis=1, keepdims=True))
           - jnp.exp(jnp.sum(lv[2:3] * lv[3:4], axis=1, keepdims=True)) + lambda_init)
    dv = DA_V_DIM
    ot = (acc_sc[0, 0:dv, :] / acc_sc[0, dv:dv + 1, :]
          - lam * (acc_sc[1, 0:dv, :] / acc_sc[1, dv:dv + 1, :]))
    yt = ot * lax.rsqrt(jnp.mean(ot * ot, axis=0, keepdims=True) + EPS)
    o_ref[...] = (yt.T * subln_ref[...] * (1.0 - lambda_init)).astype(BF16)


def _score_bound(gq, gk):
    norm = math.sqrt(DA_HEAD_DIM)
    return (1.01 * norm * norm * DA_HEAD_DIM ** -0.5 * LOG2_E
            * jnp.max(jnp.abs(gq)) * jnp.max(jnp.abs(gk))).astype(F32).reshape(1, 1)


def _diff_attn(z, vt, bound, lam_rows, subln, B, S, tq, lambda_init):
    T = B * S
    nq = S // tq
    kern = functools.partial(_diff_attn_kernel, tq=tq, lambda_init=lambda_init)
    return pl.pallas_call(
        kern,
        grid=(B, DA_HEADS, nq),
        in_specs=[pl.BlockSpec((1, 1), lambda b, h, i: (0, 0), memory_space=pltpu.SMEM),
                  pl.BlockSpec((tq, LANES), lambda b, h, i: (b * nq + i, OFF_QA // LANES + h)),
                  pl.BlockSpec((S, LANES), lambda b, h, i: (b, OFF_KA // LANES + h)),
                  pl.BlockSpec((nq, 1, DA_VT_ROWS, tq), lambda b, h, i: (b, h, 0, 0)),
                  _const_spec((4, DA_HEAD_DIM)), _const_spec((1, DA_V_DIM))],
        out_specs=pl.BlockSpec((tq, LANES), lambda b, h, i: (b * nq + i, h)),
        out_shape=jax.ShapeDtypeStruct((T, DA_HEADS * DA_V_DIM), BF16),
        scratch_shapes=[pltpu.VMEM((2, DA_VT_ROWS, tq), F32), pltpu.VMEM((4, tq, tq // 2), F32)],
        compiler_params=_cparams(("parallel", "parallel", "arbitrary")),
        name="diff_attn",
    )(bound, z, z, vt, lam_rows, subln)


def _retention_kernel(q_ref, k_ref, v_ref, g_ref, dec_ref, qdec_ref, kdec_ref, cdec_ref, o_ref, r_sc, o_sc, *,
                      tc):
    @pl.when(pl.program_id(1) == 0)
    def _():
        r_sc[...] = jnp.zeros(r_sc.shape, F32)

    C = RET_CHUNK
    lane = lax.broadcasted_iota(jnp.int32, (C, LANES), 1)
    rowi = lax.broadcasted_iota(jnp.int32, (C, LANES), 0)
    units = [(n, j) for n in range(tc // C) for j in range(RET_HEADS // 2)]
    rows = lambda n: slice(n * C, (n + 1) * C)
    cols = lambda j: slice(j * LANES, (j + 1) * LANES)
    vcol = lambda h: slice(h * RET_V_DIM, (h + 1) * RET_V_DIM)
    own = lambda hh: (lane >= RET_QK_DIM) if hh else (lane < RET_QK_DIM)
    s, kv = {}, {}
    for n, j in units:
        qp = q_ref[rows(n), cols(j)]
        kp = k_ref[rows(n), cols(j)]
        kdt = (kp.astype(F32) * kdec_ref[:, cols(j)]).T.astype(BF16)
        both = []
        for hh in range(2):
            h = 2 * j + hh
            s[n, h] = lax.dot_general(jnp.where(own(hh), qp, jnp.zeros_like(qp)), kp, (((1,), (1,)), ((), ())),
                                      preferred_element_type=F32)
            both.append(jnp.dot(kdt, v_ref[rows(n), vcol(h)], preferred_element_type=F32))
        kv[n, j] = jnp.where(rowi < RET_QK_DIM, both[0], both[1])
    for n, j in units:
        for hh in range(2):
            h = 2 * j + hh
            inner = (s[n, h] * dec_ref[h]).astype(BF16)
            o_sc[rows(n), vcol(h)] = jnp.dot(inner, v_ref[rows(n), vcol(h)], preferred_element_type=F32)
    state = [r_sc[j] for j in range(RET_HEADS // 2)]
    for n, j in units:
        qp = q_ref[rows(n), cols(j)]
        qd = (qp.astype(F32) * qdec_ref[:, cols(j)]).astype(BF16)
        r_b = state[j].astype(BF16)
        for hh in range(2):
            h = 2 * j + hh
            o_sc[rows(n), vcol(h)] += jnp.dot(jnp.where(own(hh), qd, jnp.zeros_like(qd)), r_b,
                                              preferred_element_type=F32)
        state[j] = state[j] * cdec_ref[j] + kv[n, j]
    for j in range(RET_HEADS // 2):
        r_sc[j] = state[j]
    for h in range(RET_HEADS):
        gs = slice(h * RET_V_DIM, (h + 1) * RET_V_DIM)
        o = o_sc[:, gs]
        oc = o - jnp.mean(o, axis=-1, keepdims=True)
        y = oc * lax.rsqrt(jnp.mean(oc * oc, axis=-1, keepdims=True) + EPS)
        o_ref[:, gs] = (y * g_ref[:, gs].astype(F32)).astype(BF16)


def _retention(z, consts, B, S, tc):
    T = B * S
    n = S // tc
    dec, qdec, kdec, cdec = consts
    kern = functools.partial(_retention_kernel, tc=tc)
    blk = lambda width, off: pl.BlockSpec((tc, width), lambda b, i: (b * n + i, off // width))
    return pl.pallas_call(
        kern,
        grid=(B, n),
        in_specs=[blk(RET_QK_W, OFF_QR), blk(RET_QK_W, OFF_KR), blk(RET_W, OFF_VR), blk(RET_W, OFF_GR),
                  _const_spec(dec.shape), _const_spec(qdec.shape), _const_spec(kdec.shape),
                  _const_spec(cdec.shape)],
        out_specs=pl.BlockSpec((tc, RET_W), lambda b, i: (b * n + i, 0)),
        out_shape=jax.ShapeDtypeStruct((T, RET_HEADS * RET_V_DIM), BF16),
        scratch_shapes=[pltpu.VMEM((RET_HEADS // 2, LANES, LANES), F32),
                        pltpu.VMEM((tc, RET_HEADS * RET_V_DIM), F32)],
        compiler_params=_cparams(("parallel", "arbitrary")),
        name="retention",
    )(z, z, z, z, dec, qdec, kdec, cdec)


def _retention_consts():
    C = RET_CHUNK
    log_gamma = jnp.log1p(-jnp.exp2(-5.0 - jnp.arange(RET_HEADS, dtype=F32)))
    idx = jnp.arange(C, dtype=F32)
    rel = idx[:, None] - idx[None, :]
    dec = jnp.where(rel[None] >= 0, jnp.exp(log_gamma[:, None, None] * jnp.maximum(rel, 0.0)[None]), 0.0)
    q_dec = jnp.exp(log_gamma[None, :] * (idx + 1.0)[:, None])
    k_dec = jnp.exp(log_gamma[None, :] * (C - 1.0 - idx)[:, None])
    qdec = jnp.repeat(q_dec, RET_QK_DIM, axis=1)
    kdec = jnp.repeat(k_dec, RET_QK_DIM, axis=1)
    chunk_decay = jnp.repeat(jnp.exp(log_gamma * C), RET_QK_DIM)
    cdec = jnp.broadcast_to(chunk_decay.reshape(RET_HEADS // 2, LANES, 1), (RET_HEADS // 2, LANES, LANES))
    return dec.astype(F32), qdec.astype(F32), kdec.astype(F32), cdec.astype(F32)


R_I1, R_I2, R_W1, R_W2, R_RANK1, R_RANK2 = range(6)


def _route(hn, wr_ref, rows, info_ref, carry_sc):
    w = wr_ref[...]
    hn_hi, w_hi = hn.astype(BF16), w.astype(BF16)
    hn_lo, w_lo = (hn - hn_hi.astype(F32)).astype(BF16), (w - w_hi.astype(F32)).astype(BF16)
    logits = (jnp.dot(hn_hi, w_hi, preferred_element_type=F32) + jnp.dot(hn_hi, w_lo, preferred_element_type=F32)
              + jnp.dot(hn_lo, w_hi, preferred_element_type=F32))
    lane = lax.broadcasted_iota(jnp.int32, logits.shape, 1)
    logits = jnp.where(lane < N_EXPERTS, logits, -jnp.inf)
    m1 = jnp.max(logits, axis=1, keepdims=True)
    i1 = jnp.min(jnp.where(logits == m1, lane, LANES), axis=1, keepdims=True)
    rest = jnp.where(lane == i1, -jnp.inf, logits)
    m2 = jnp.max(rest, axis=1, keepdims=True)
    i2 = jnp.min(jnp.where(rest == m2, lane, LANES), axis=1, keepdims=True)
    e = jnp.exp(m2 - m1)
    w1 = 1.0 / (1.0 + e)
    w2 = e / (1.0 + e)
    sel = jnp.where(lane == i1, 1.0, jnp.where(lane == i2, 1.0, 0.0))
    n = sel.shape[0]
    rr = lax.broadcasted_iota(jnp.int32, (n, n), 0)
    cc = lax.broadcasted_iota(jnp.int32, (n, n), 1)
    ltri = jnp.where(cc < rr, 1.0, 0.0).astype(BF16)
    before = jnp.dot(ltri, sel.astype(BF16), preferred_element_type=F32) + carry_sc[0:1, :]
    r1 = jnp.sum(jnp.where(lane == i1, before, 0.0), axis=1, keepdims=True)
    r2 = jnp.sum(jnp.where(lane == i2, before, 0.0), axis=1, keepdims=True)
    info = jnp.where(lane == R_I1, i1.astype(F32), 0.0)
    for ln, val in ((R_I2, i2.astype(F32)), (R_W1, w1), (R_W2, w2), (R_RANK1, r1), (R_RANK2, r2)):
        info = jnp.where(lane == ln, val, info)
    info_ref[rows, :] = info
    carry_sc[...] = carry_sc[...] + jnp.sum(sel, axis=0, keepdims=True)


def _merge_kernel(*refs, route):
    if route:
        (ya_ref, yr_ref, ga_ref, gb_ref, h_ref, wa_ref, wr_ref, wo_ref, gf_ref, wrt_ref,
         h_out, info_out, cnt_out, carry_sc) = refs

        @pl.when(pl.program_id(0) == 0)
        def _():
            carry_sc[...] = jnp.zeros(carry_sc.shape, F32)
    else:
        ya_ref, yr_ref, ga_ref, gb_ref, h_ref, wa_ref, wr_ref, wo_ref, gf_ref, h_out, hn_out = refs
    half = ya_ref.shape[0] // 2
    parts = [slice(0, half), slice(half, 2 * half)]
    branch = [(jnp.dot(ya_ref[r, :], wa_ref[...], preferred_element_type=F32),
               jnp.dot(yr_ref[r, :], wr_ref[...], preferred_element_type=F32)) for r in parts]
    for r, (pa, pr) in zip(parts, branch):
        merged = ga_ref[r, :].astype(F32) * pa + gb_ref[r, :].astype(F32) * pr
        hnew = h_ref[r, :] + jnp.dot(merged.astype(BF16), wo_ref[...], preferred_element_type=F32)
        h_out[r, :] = hnew
        ms = jnp.mean(hnew * hnew, axis=-1, keepdims=True)
        hn = hnew * lax.rsqrt(ms + EPS) * gf_ref[...]
        if route:
            _route(hn, wrt_ref, r, info_out, carry_sc)
        else:
            hn_out[r, :] = hn.astype(BF16)
    if route:
        cnt_out[...] = carry_sc[...]


def _merge(ya, yr, z, h, wa, wr, wo, gf, tm, w_router=None):
    T = h.shape[0]
    route = w_router is not None
    row = lambda width, cb=0: pl.BlockSpec((tm, width), lambda i: (i, cb))
    in_specs = [row(DA_W), row(RET_W), row(D_MODEL, OFF_GA // D_MODEL), row(D_MODEL, OFF_GB // D_MODEL),
                row(D_MODEL), _const_spec(wa.shape), _const_spec(wr.shape), _const_spec(wo.shape),
                _const_spec((1, D_MODEL))]
    args = [ya, yr, z, z, h, wa, wr, wo, gf]
    if route:
        in_specs.append(_const_spec((D_MODEL, LANES)))
        args.append(jnp.pad(w_router, ((0, 0), (0, LANES - N_EXPERTS))))
        out_specs = [row(D_MODEL), row(LANES), _const_spec((SUBLANES, LANES))]
        out_shape = [jax.ShapeDtypeStruct((T, D_MODEL), F32), jax.ShapeDtypeStruct((T, LANES), F32),
                     jax.ShapeDtypeStruct((SUBLANES, LANES), F32)]
        scratch = [pltpu.VMEM((SUBLANES, LANES), F32)]
    else:
        out_specs = [row(D_MODEL), row(D_MODEL)]
        out_shape = [jax.ShapeDtypeStruct((T, D_MODEL), F32), jax.ShapeDtypeStruct((T, D_MODEL), BF16)]
        scratch = []
    return pl.pallas_call(
        functools.partial(_merge_kernel, route=route),
        grid=(T // tm,),
        in_specs=in_specs,
        out_specs=out_specs,
        out_shape=out_shape,
        scratch_shapes=scratch,
        compiler_params=_cparams(("arbitrary",) if route else ("parallel",)),
        name="merge_route" if route else "merge_out",
    )(*args)


def _swiglu_into(acc_sc, x, w1_ref, w3_ref, w2_ref):
    chunks = [slice(c * FF_CHUNK, (c + 1) * FF_CHUNK) for c in range(D_FF // FF_CHUNK)]
    up = lambda cs: (jnp.dot(x, w1_ref[:, cs], preferred_element_type=F32),
                     jnp.dot(x, w3_ref[:, cs], preferred_element_type=F32))
    uv = up(chunks[0])
    for c, cs in enumerate(chunks):
        nxt = up(chunks[c + 1]) if c + 1 < len(chunks) else None
        a = (uv[0] * _sigmoid(uv[0]) * uv[1]).astype(BF16)
        d = jnp.dot(a, w2_ref[cs, :], preferred_element_type=F32)
        if c == 0:
            acc_sc[...] = d
        else:
            acc_sc[...] += d
        uv = nxt


def _ffn_kernel(hn_ref, prev_ref, w1_ref, w3_ref, w2_ref, o_ref, acc_sc):
    _swiglu_into(acc_sc, hn_ref[...], w1_ref, w3_ref, w2_ref)
    o_ref[...] = prev_ref[...] + acc_sc[...]


def _ffn(hn, prev, w1, w3, w2, tm):
    T = hn.shape[0]
    row = lambda width: pl.BlockSpec((tm, width), lambda i: (i, 0))
    wspec = lambda w: pl.BlockSpec(w.shape, lambda i: (0, 0), pipeline_mode=pl.Buffered(1))
    return pl.pallas_call(
        _ffn_kernel,
        grid=(T // tm,),
        in_specs=[row(D_MODEL), row(D_MODEL), wspec(w1), wspec(w3), wspec(w2)],
        out_specs=row(D_MODEL),
        out_shape=jax.ShapeDtypeStruct((T, D_MODEL), F32),
        scratch_shapes=[pltpu.VMEM((tm, D_MODEL), F32)],
        input_output_aliases={1: 0},
        compiler_params=_cparams(("parallel",)),
        name="ffn",
    )(hn, prev, w1, w3, w2)


def _dispatch_kernel(pad_lo_ref, pad_hi_ref, pos_ref, h_ref, xs_hbm, zero_sc, sem, pad_sem, *, tb):
    @pl.when(pl.program_id(0) == 0)
    def _():
        zero_sc[...] = jnp.zeros(zero_sc.shape, F32)
        zero_row = zero_sc.at[pl.ds(0, 1)]
        for e in range(N_EXPERTS):
            def fill(r, carry):
                pltpu.make_async_copy(zero_row, xs_hbm.at[pl.ds(r, 1)], pad_sem).start()
                return carry

            lax.fori_loop(pad_lo_ref[e], pad_hi_ref[e], fill, 0)
        for e in range(N_EXPERTS):
            def drain(r, carry):
                pltpu.make_async_copy(zero_row, xs_hbm.at[pl.ds(r, 1)], pad_sem).wait()
                return carry

            lax.fori_loop(pad_lo_ref[e], pad_hi_ref[e], drain, 0)

    def issue(i, carry):
        t0 = pl.multiple_of(i * SUBLANES, SUBLANES)
        for u in range(SUBLANES):
            for k in range(2):
                p = pos_ref[0, 0, k * tb + t0 + u]
                pltpu.make_async_copy(h_ref.at[pl.ds(t0 + u, 1)], xs_hbm.at[pl.ds(p, 1)], sem).start(priority=k)
        return carry

    lax.fori_loop(0, tb // SUBLANES, issue, 0)

    for k in range(2):
        pltpu.make_async_copy(h_ref, xs_hbm.at[pl.ds(0, tb)], sem).wait()


def _dispatch(pad_lo, pad_hi, pos_blocks, h, n_rows, tb):
    T = h.shape[0]
    return pl.pallas_call(
        functools.partial(_dispatch_kernel, tb=tb),
        grid_spec=pltpu.PrefetchScalarGridSpec(
            num_scalar_prefetch=2,
            grid=(T // tb,),
            in_specs=[pl.BlockSpec((1, 1, 2 * tb), lambda i, lo, hi: (i, 0, 0), memory_space=pltpu.SMEM),
                      pl.BlockSpec((tb, D_MODEL), lambda i, lo, hi: (i, 0))],
            out_specs=pl.BlockSpec(memory_space=pl.ANY),
            scratch_shapes=[pltpu.VMEM((SUBLANES, D_MODEL), F32), pltpu.SemaphoreType.DMA(()),
                            pltpu.SemaphoreType.DMA(())]),
        out_shape=jax.ShapeDtypeStruct((n_rows, D_MODEL), F32),
        compiler_params=pltpu.CompilerParams(dimension_semantics=("arbitrary",), has_side_effects=True),
        name="moe_dispatch",
    )(pad_lo, pad_hi, pos_blocks, h)


def _expert_kernel(te_ref, nu_ref, xs_ref, g_ref, w1_ref, w3_ref, w2_ref, y_ref, acc_sc):
    del te_ref
    i = pl.program_id(0)

    @pl.when(i < nu_ref[0])
    def _():
        half = xs_ref.shape[0] // 2
        for r in (slice(0, half), slice(half, 2 * half)):
            x = xs_ref[r, :]
            ms = jnp.mean(x * x, axis=-1, keepdims=True)
            hn = (x * lax.rsqrt(ms + EPS) * g_ref[...]).astype(BF16)
            _swiglu_into(acc_sc.at[r, :], hn, w1_ref.at[0], w3_ref.at[0], w2_ref.at[0])
            y_ref[r, :] = acc_sc[r, :]

    @pl.when(i >= nu_ref[0])
    def _():
        y_ref[...] = jnp.zeros(y_ref.shape, F32)


def _experts(tile_expert, n_used, xs, g, w1, w3, w2, te_rows):
    P = xs.shape[0]
    row = pl.BlockSpec((te_rows, D_MODEL), lambda i, te, nu: (jnp.minimum(i, nu[0] - 1), 0))
    wspec = lambda w: pl.BlockSpec((1,) + w.shape[1:], lambda i, te, nu: (te[i], 0, 0))
    return pl.pallas_call(
        _expert_kernel,
        grid_spec=pltpu.PrefetchScalarGridSpec(
            num_scalar_prefetch=2,
            grid=(P // te_rows,),
            in_specs=[row, pl.BlockSpec((1, D_MODEL), lambda i, te, nu: (0, 0)), wspec(w1), wspec(w3), wspec(w2)],
            out_specs=pl.BlockSpec((te_rows, D_MODEL), lambda i, te, nu: (i, 0)),
            scratch_shapes=[pltpu.VMEM((te_rows, D_MODEL), F32)]),
        out_shape=jax.ShapeDtypeStruct((P, D_MODEL), F32),
        compiler_params=_cparams(("arbitrary",)),
        name="moe_experts",
    )(tile_expert, n_used, xs, g, w1, w3, w2)


def _combine_kernel(pos_ref, info_ref, h_ref, y_hbm, o_ref, g_sc, sem, *, tb):
    def issue(i, carry):
        t0 = pl.multiple_of(i * SUBLANES, SUBLANES)
        for u in range(SUBLANES):
            for k in range(2):
                p = pos_ref[0, 0, k * tb + t0 + u]
                pltpu.make_async_copy(y_hbm.at[pl.ds(p, 1)], g_sc.at[k, pl.ds(t0 + u, 1)], sem).start(priority=k)
        return carry

    lax.fori_loop(0, tb // SUBLANES, issue, 0)

    for k in range(2):
        pltpu.make_async_copy(y_hbm.at[pl.ds(0, tb)], g_sc.at[k], sem).wait()
    info = info_ref[...]
    o_ref[...] = h_ref[...] + info[:, R_W1:R_W1 + 1] * g_sc[0] + info[:, R_W2:R_W2 + 1] * g_sc[1]


def _combine(pos_blocks, info, h, y, tb):
    T = h.shape[0]
    row = lambda width: pl.BlockSpec((tb, width), lambda i: (i, 0))
    return pl.pallas_call(
        functools.partial(_combine_kernel, tb=tb),
        grid=(T // tb,),
        in_specs=[pl.BlockSpec((1, 1, 2 * tb), lambda i: (i, 0, 0), memory_space=pltpu.SMEM),
                  row(LANES), row(D_MODEL), pl.BlockSpec(memory_space=pl.ANY)],
        out_specs=row(D_MODEL),
        out_shape=jax.ShapeDtypeStruct((T, D_MODEL), F32),
        scratch_shapes=[pltpu.VMEM((2, tb, D_MODEL), F32), pltpu.SemaphoreType.DMA(())],
        compiler_params=_cparams(("arbitrary",)),
        name="moe_combine",
    )(pos_blocks, info, h, y)


def _pos_blocks(pos1, pos2, tb):
    nb = pos1.shape[0] // tb
    return jnp.concatenate([pos1.reshape(nb, 1, tb), pos2.reshape(nb, 1, tb)], axis=2)


def _moe(h, info, cnt, g, w1, w3, w2):
    T = h.shape[0]
    te_rows = min(EXPERT_TILE, T)
    counts = cnt[0, :N_EXPERTS].astype(jnp.int32)
    padded = ((counts + te_rows - 1) // te_rows) * te_rows
    ends = jnp.cumsum(padded)
    base = ends - padded
    i1 = info[:, R_I1].astype(jnp.int32)
    i2 = info[:, R_I2].astype(jnp.int32)
    pos1 = jnp.take(base, i1) + info[:, R_RANK1].astype(jnp.int32)
    pos2 = jnp.take(base, i2) + info[:, R_RANK2].astype(jnp.int32)
    n_tiles = (2 * T) // te_rows + N_EXPERTS
    tile_expert = jnp.minimum(jnp.searchsorted(ends // te_rows, jnp.arange(n_tiles), side="right"),
                              N_EXPERTS - 1).astype(jnp.int32)
    n_used = (ends[-1:] // te_rows).astype(jnp.int32)

    tb_d = min(DISPATCH_BLOCK, T)
    xs = _dispatch(base + counts, ends, _pos_blocks(pos1, pos2, tb_d), h, n_tiles * te_rows, tb_d)
    y = _experts(tile_expert, n_used, xs, g, w1, w3, w2, te_rows)
    tb_c = min(COMBINE_BLOCK, T)
    return _combine(_pos_blocks(pos1, pos2, tb_c), info, h, y, tb_c)


def kernel(x, positions, attn_norm, w_in, da_q_norm, da_k_norm, lam_q1, lam_k1, lam_q2, lam_k2, da_subln,
           w_proj_da, w_proj_ret, w_out, ffn_norm, dense_w1, dense_w3, dense_w2, w_router, moe_w1, moe_w3,
           moe_w2):
    B, S, D = x.shape
    T = B * S
    depth = attn_norm.shape[0]
    tm = tq = tc = min(ROW_BLOCK, S)
    assert D == D_MODEL and S % tm == 0 and tm % (2 * MXU_DIM) == 0 and tm % RET_CHUNK == 0, (x.shape, tm)

    def rope_tables(pos):
        pos_b = jnp.broadcast_to(pos.reshape(-1, 1).astype(F32), (pos.size, LANES))
        return (_rope_tables(pos_b, *_rope_rows(ROPE_ROT_DIM, ROPE_THETA), tm)
                + _rope_tables(pos_b, *_rope_rows(RET_QK_DIM, RET_ROPE_THETA), tm))

    tabs = lax.cond(jnp.all(positions == positions[:1]),
                    lambda: tuple(jnp.tile(t, (B, 1)) for t in rope_tables(positions[0])),
                    lambda: tuple(rope_tables(positions.reshape(T))))
    ret_consts = _retention_consts()
    gsum = jnp.asarray(np.kron(np.eye(MXU_DIM // HEAD), np.ones((HEAD, HEAD))), BF16)

    h = x.reshape(T, D)
    for layer in range(depth):
        lambda_init = 0.8 - 0.6 * math.exp(-0.3 * layer)
        gq = jnp.tile(da_q_norm[layer], PROJ_CHUNK // DA_HEAD_DIM).reshape(1, PROJ_CHUNK)
        gk = jnp.tile(da_k_norm[layer], PROJ_CHUNK // DA_HEAD_DIM).reshape(1, PROJ_CHUNK)
        z, vt = _in_proj(h, attn_norm[layer].reshape(1, D), w_in[layer].astype(BF16), gq, gk, gsum, tabs, tm)
        lam_rows = jnp.stack([lam_q1[layer], lam_k1[layer], lam_q2[layer], lam_k2[layer]])
        ya = _diff_attn(z, vt, _score_bound(da_q_norm[layer], da_k_norm[layer]), lam_rows,
                        da_subln[layer].reshape(1, DA_V_DIM), B, S, tq, lambda_init)
        yr = _retention(z, ret_consts, B, S, tc)
        j = layer // 2
        merge_args = (ya, yr, z, h, w_proj_da[layer].astype(BF16), w_proj_ret[layer].astype(BF16),
                      w_out[layer].astype(BF16), ffn_norm[layer].reshape(1, D), tm)
        if layer % 2 == 0:
            h, hn = _merge(*merge_args)
            h = _ffn(hn, h, dense_w1[j].astype(BF16), dense_w3[j].astype(BF16), dense_w2[j].astype(BF16), tm)
        else:
            h, info, cnt = _merge(*merge_args, w_router=w_router[j])
            h = _moe(h, info, cnt, ffn_norm[layer].reshape(1, D), moe_w1[j].astype(BF16),
                     moe_w3[j].astype(BF16), moe_w2[j].astype(BF16))
    return h.reshape(B, S, D)
```

```python
import functools
import math

import jax
import jax.numpy as jnp
import numpy as np
from jax import lax
from jax.experimental import pallas as pl
from jax.experimental.pallas import tpu as pltpu

F32 = jnp.float32
BF16 = jnp.bfloat16

D_MODEL = 1024
DA_HEADS = 4
DA_HEAD_DIM = 64
DA_V_DIM = 128
ROPE_THETA = 500000.0
ROPE_ROT_DIM = 16
RET_HEADS = 4
RET_QK_DIM = 64
RET_V_DIM = 128
RET_CHUNK = 128
RET_ROPE_THETA = 10000.0
D_FF = 2816
N_EXPERTS = 8
EPS = 1e-6
LOG2_E = math.log2(math.e)
DA_VT_ROWS = DA_V_DIM + 16
FIXED_REF_MAX_BOUND = 60.0
GROUPS = (8, 4)

LANES = 128
SUBLANES = 8
MXU_DIM = 256
HEAD = 64
assert DA_HEAD_DIM == RET_QK_DIM == HEAD and LANES == 2 * HEAD
DA_W = DA_HEADS * DA_V_DIM
RET_W = RET_HEADS * RET_V_DIM
RET_QK_W = RET_HEADS * RET_QK_DIM
ROW_BLOCK = 512
EXPERT_TILE = 512
DISPATCH_BLOCK = 2048
COMBINE_BLOCK = 1024
OFF_QA, OFF_KA, OFF_VA = 0, 512, 1024
OFF_QR, OFF_KR, OFF_VR, OFF_GR = 1536, 1792, 2048, 2560
OFF_GA, OFF_GB = 3072, 4096
IN_W = 5120
PROJ_CHUNK = 512
FF_CHUNK = 256
VMEM_LIMIT = 56 * 1024 * 1024


def _cparams(sem):
    return pltpu.CompilerParams(dimension_semantics=sem, vmem_limit_bytes=VMEM_LIMIT)


def _const_spec(shape):
    nd = len(shape)
    return pl.BlockSpec(shape, lambda *_: (0,) * nd)


def _rope_table_kernel(pos_ref, inv_ref, sgn_ref, cos_ref, sin_ref):
    ang = pos_ref[...] * inv_ref[...]
    cos_ref[...] = jnp.cos(ang)
    sin_ref[...] = jnp.sin(ang) * sgn_ref[...]


def _rope_tables(pos_b, inv_row, sgn_row, tm):
    T = pos_b.shape[0]
    row = pl.BlockSpec((tm, LANES), lambda i: (i, 0))
    return pl.pallas_call(
        _rope_table_kernel,
        grid=(T // tm,),
        in_specs=[row, _const_spec((1, LANES)), _const_spec((1, LANES))],
        out_specs=[row, row],
        out_shape=[jax.ShapeDtypeStruct((T, LANES), F32)] * 2,
        compiler_params=_cparams(("parallel",)),
        name="rope_tables",
    )(pos_b, inv_row, sgn_row)


def _rope_rows(rot_dim, theta):
    half = rot_dim // 2
    inv = 1.0 / (theta ** (jnp.arange(half, dtype=F32) / half))
    jj = np.arange(LANES) % HEAD
    inv_row = jnp.where(jj < rot_dim, inv[jj % half], 0.0).astype(F32)
    sgn_row = np.where(jj < half, -1.0, np.where(jj < rot_dim, 1.0, 0.0)).astype(np.float32)
    return inv_row.reshape(1, LANES), jnp.asarray(sgn_row).reshape(1, LANES)


def _rotate_half(x, half):
    jj = lax.broadcasted_iota(jnp.int32, x.shape, 1) % HEAD
    return jnp.where(jj < half, pltpu.roll(x, LANES - half, 1), pltpu.roll(x, half, 1))


def _sigmoid(x):
    return 1.0 / (1.0 + jnp.exp(-x))


def _in_proj_kernel(h_ref, g_ref, w_ref, gq_ref, gk_ref, gsum_ref, ca_ref, sa_ref, cr_ref, sr_ref, z_ref, vt_ref):
    half = h_ref.shape[0] // 2
    for r in (slice(0, half), slice(half, 2 * half)):
        _in_proj_rows(r, h_ref, g_ref, w_ref, gq_ref, gk_ref, gsum_ref, ca_ref, sa_ref, cr_ref, sr_ref, z_ref,
                      vt_ref)


def _in_proj_rows(r, h_ref, g_ref, w_ref, gq_ref, gk_ref, gsum_ref, ca_ref, sa_ref, cr_ref, sr_ref, z_ref, vt_ref):
    x = h_ref[r, :]
    ms = jnp.mean(x * x, axis=-1, keepdims=True)
    hn = (x * lax.rsqrt(ms + EPS) * g_ref[...]).astype(BF16)
    ca, sa, cr, sr = ca_ref[r, :], sa_ref[r, :], cr_ref[r, :], sr_ref[r, :]

    def head_rms(zc, gain):
        sq = (zc * zc).astype(BF16)
        outs = []
        for s in range(PROJ_CHUNK // MXU_DIM):
            sl = slice(s * MXU_DIM, (s + 1) * MXU_DIM)
            tot = jnp.dot(sq[:, sl], gsum_ref[...], preferred_element_type=F32)
            outs.append(zc[:, sl] * lax.rsqrt(tot * (1.0 / HEAD) + EPS))
        return jnp.concatenate(outs, axis=1) * gain

    def rope(y, cos, sin, half):
        outs = []
        for s in range(y.shape[1] // LANES):
            ys = y[:, s * LANES:(s + 1) * LANES]
            outs.append(ys * cos + _rotate_half(ys, half) * sin)
        return jnp.concatenate(outs, axis=1)

    for c in range(IN_W // PROJ_CHUNK):
        c0 = c * PROJ_CHUNK
        zc = jnp.dot(hn, w_ref[:, c0:c0 + PROJ_CHUNK], preferred_element_type=F32)
        if c0 == OFF_QA:
            zc = rope(head_rms(zc, gq_ref[...]), ca, sa, ROPE_ROT_DIM // 2)
            zc = zc * (DA_HEAD_DIM ** -0.5 * LOG2_E)
        elif c0 == OFF_KA:
            zc = rope(head_rms(zc, gk_ref[...]), ca, sa, ROPE_ROT_DIM // 2)
        elif c0 == OFF_VA:
            zt = zc.T.astype(BF16)
            extra = lax.broadcasted_iota(jnp.int32, (DA_VT_ROWS - DA_V_DIM, zt.shape[1]), 0)
            for hd in range(DA_HEADS):
                vt_ref[0, hd, 0:DA_V_DIM, r] = zt[hd * DA_V_DIM:(hd + 1) * DA_V_DIM, :]
                vt_ref[0, hd, DA_V_DIM:DA_VT_ROWS, r] = jnp.where(extra == 0, 1.0, 0.0).astype(BF16)
        elif c0 == OFF_QR:
            zc = rope(zc, cr, sr, RET_QK_DIM // 2)
            col = lax.broadcasted_iota(jnp.int32, zc.shape, 1)
            zc = jnp.where(col >= OFF_KR - OFF_QR, zc * (RET_QK_DIM ** -0.5), zc)
        elif c0 == OFF_GR:
            zc = zc * _sigmoid(zc)
        elif c0 >= OFF_GA:
            zc = _sigmoid(zc)
        z_ref[r, c0:c0 + PROJ_CHUNK] = zc.astype(BF16)


def _in_proj(h, g, w, gq, gk, gsum, tabs, tm):
    T = h.shape[0]
    row = lambda width: pl.BlockSpec((tm, width), lambda i: (i, 0))
    return pl.pallas_call(
        _in_proj_kernel,
        grid=(T // tm,),
        in_specs=[row(D_MODEL), _const_spec((1, D_MODEL)), _const_spec((D_MODEL, IN_W)),
                  _const_spec((1, PROJ_CHUNK)), _const_spec((1, PROJ_CHUNK)), _const_spec((MXU_DIM, MXU_DIM)),
                  row(LANES), row(LANES), row(LANES), row(LANES)],
        out_specs=[row(IN_W), pl.BlockSpec((1, DA_HEADS, DA_VT_ROWS, tm), lambda i: (i, 0, 0, 0))],
        out_shape=[jax.ShapeDtypeStruct((T, IN_W), BF16),
                   jax.ShapeDtypeStruct((T // tm, DA_HEADS, DA_VT_ROWS, tm), BF16)],
        compiler_params=_cparams(("parallel",)),
        name="in_proj",
    )(h, g, w, gq, gk, gsum, *tabs)


def _diff_attn_kernel(bound_ref, q_ref, k_ref, vt_ref, lam_ref, subln_ref, o_ref, acc_sc, st_sc, *, tq,
                      lambda_init):
    qi = pl.program_id(2)
    q = q_ref[...]
    lane = lax.broadcasted_iota(jnp.int32, q.shape, 1)
    qs = (jnp.where(lane < DA_HEAD_DIM, q, jnp.zeros_like(q)),
          jnp.where(lane >= DA_HEAD_DIM, q, jnp.zeros_like(q)))
    acc_sc[...] = jnp.zeros(acc_sc.shape, F32)

    tqh = tq // 2
    streams = [(c, hf) for hf in range(2) for c in range(2)]
    qsub = [qs[c][hf * tqh:(hf + 1) * tqh, :] for c, hf in streams]

    def scores(ki, diagonal=False):
        k = k_ref[pl.ds(pl.multiple_of(ki * tq, tq), tq), :]
        return [lax.dot_general(k[0:tqh] if diagonal and hf == 0 else k, qq, (((1,), (1,)), ((), ())),
                                preferred_element_type=F32) for qq, (c, hf) in zip(qsub, streams)]

    def stash(sts):
        for s in range(len(streams)):
            st_sc[s] = sts[s]

    bound = bound_ref[0, 0]

    def masked_scores(st, hf):
        key = lax.broadcasted_iota(jnp.int32, st.shape, 0)
        qry = lax.broadcasted_iota(jnp.int32, st.shape, 1) + hf * tqh
        return jnp.where(key <= qry, st, -jnp.inf)

    def consume_online(sts, ki, stats, masked):
        vt = vt_ref[ki, 0]
        out = []
        for s, (c, hf) in enumerate(streams):
            st = st_sc[s] if sts is None else sts[s]
            if masked:
                st = masked_scores(st, hf)
            cols = slice(hf * tqh, (hf + 1) * tqh)
            m_old = stats[s]
            m_new = jnp.maximum(m_old, jnp.max(st, axis=0, keepdims=True))
            alpha = jnp.exp2(m_old - m_new)
            pt = jnp.exp2((st - m_new).astype(BF16))
            acc_sc[c, :, cols] = alpha * acc_sc[c, :, cols] + jnp.dot(vt, pt, preferred_element_type=F32)
            out.append(m_new)
        return tuple(out)

    def run_fixed_ref():
        def accumulate(blocks, last_masked):
            diag = [last_masked and u == len(blocks) - 1 for u in range(len(blocks))]
            sts = [scores(b, dg) for b, dg in zip(blocks, diag)]
            tot = [None] * len(streams)
            den = [None] * len(streams)
            for b, st4, dg in zip(blocks, sts, diag):
                vt = vt_ref[b, 0, 0:DA_V_DIM, :]
                for s, (c, hf) in enumerate(streams):
                    st = masked_scores(st4[s], hf) if dg else st4[s]
                    keys = st.shape[0]
                    pt = jnp.exp2(st - bound)
                    part = jnp.sum(pt.reshape(keys // SUBLANES, SUBLANES, tqh), axis=0)
                    d = jnp.dot(vt[:, 0:keys], pt.astype(BF16), preferred_element_type=F32)
                    tot[s] = d if tot[s] is None else tot[s] + d
                    den[s] = part if den[s] is None else den[s] + part
            for s, (c, hf) in enumerate(streams):
                cols = slice(hf * tqh, (hf + 1) * tqh)
                acc_sc[c, 0:DA_V_DIM, cols] += tot[s]
                acc_sc[c, DA_V_DIM:DA_V_DIM + 1, cols] += jnp.sum(den[s], axis=0, keepdims=True)

        done = 0
        for g in GROUPS:
            def trip(j, carry, g=g, done=done):
                accumulate([done + g * j + u for u in range(g)], False)
                return carry

            ntrips = (qi - done) // g
            lax.fori_loop(0, ntrips, trip, 0)
            done = done + g * ntrips
        for rest in range(GROUPS[-1]):
            pl.when(qi - done == rest)(
                functools.partial(accumulate, [qi - rest + u for u in range(rest + 1)], True))

    def run_online():
        npairs = qi // 2
        stash(scores(0))

        def pair(j, stats):
            b0 = 2 * j
            odd = scores(b0 + 1)
            stats = consume_online(None, b0, stats, False)
            stash(scores(b0 + 2))
            return consume_online(odd, b0 + 1, stats, False)

        def leftover(stats):
            stats = consume_online(None, qi - 1, stats, False)
            stash(scores(qi))
            return stats

        init = (jnp.full((1, tqh), -jnp.inf, F32),) * len(streams)
        stats = lax.fori_loop(0, npairs, pair, init)
        stats = lax.cond(qi - 2 * npairs == 1, leftover, lambda st: st, stats)
        consume_online(None, qi, stats, True)

    pl.when(bound <= FIXED_REF_MAX_BOUND)(run_fixed_ref)
    pl.when(bound > FIXED_REF_MAX_BOUND)(run_online)

    lv = lam_ref[...]
    lam = (jnp.exp(jnp.sum(lv[0:1] * lv[1:2], axis=1, keepdims=True))
           - jnp.exp(jnp.sum(lv[2:3] * lv[3:4], axis=1, keepdims=True)) + lambda_init)
    dv = DA_V_DIM
    ot = (acc_sc[0, 0:dv, :] / acc_sc[0, dv:dv + 1, :]
          - lam * (acc_sc[1, 0:dv, :] / acc_sc[1, dv:dv + 1, :]))
    yt = ot * lax.rsqrt(jnp.mean(ot * ot, axis=0, keepdims=True) + EPS)
    o_ref[...] = (yt.T * subln_ref[...] * (1.0 - lambda_init)).astype(BF16)


def _score_bound(gq, gk):
    norm = math.sqrt(DA_HEAD_DIM)
    return (1.01 * norm * norm * DA_HEAD_DIM ** -0.5 * LOG2_E
            * jnp.max(jnp.abs(gq)) * jnp.max(jnp.abs(gk))).astype(F32).reshape(1, 1)


def _diff_attn(z, vt, bound, lam_rows, subln, B, S, tq, lambda_init):
    T = B * S
    nq = S // tq
    kern = functools.partial(_diff_attn_kernel, tq=tq, lambda_init=lambda_init)
    return pl.pallas_call(
        kern,
        grid=(B, DA_HEADS, nq),
        in_specs=[pl.BlockSpec((1, 1), lambda b, h, i: (0, 0), memory_space=pltpu.SMEM),
                  pl.BlockSpec((tq, LANES), lambda b, h, i: (b * nq + i, OFF_QA // LANES + h)),
                  pl.BlockSpec((S, LANES), lambda b, h, i: (b, OFF_KA // LANES + h)),
                  pl.BlockSpec((nq, 1, DA_VT_ROWS, tq), lambda b, h, i: (b, h, 0, 0)),
                  _const_spec((4, DA_HEAD_DIM)), _const_spec((1, DA_V_DIM))],
        out_specs=pl.BlockSpec((tq, LANES), lambda b, h, i: (b * nq + i, h)),
        out_shape=jax.ShapeDtypeStruct((T, DA_HEADS * DA_V_DIM), BF16),
        scratch_shapes=[pltpu.VMEM((2, DA_VT_ROWS, tq), F32), pltpu.VMEM((4, tq, tq // 2), F32)],
        compiler_params=_cparams(("parallel", "parallel", "arbitrary")),
        name="diff_attn",
    )(bound, z, z, vt, lam_rows, subln)


def _retention_kernel(q_ref, k_ref, v_ref, g_ref, dec_ref, qdec_ref, kdec_ref, cdec_ref, o_ref, r_sc, o_sc, *,
                      tc):
    @pl.when(pl.program_id(1) == 0)
    def _():
        r_sc[...] = jnp.zeros(r_sc.shape, F32)

    C = RET_CHUNK
    lane = lax.broadcasted_iota(jnp.int32, (C, LANES), 1)
    rowi = lax.broadcasted_iota(jnp.int32, (C, LANES), 0)
    units = [(n, j) for n in range(tc // C) for j in range(RET_HEADS // 2)]
    rows = lambda n: slice(n * C, (n + 1) * C)
    cols = lambda j: slice(j * LANES, (j + 1) * LANES)
    vcol = lambda h: slice(h * RET_V_DIM, (h + 1) * RET_V_DIM)
    own = lambda hh: (lane >= RET_QK_DIM) if hh else (lane < RET_QK_DIM)
    s, kv = {}, {}
    for n, j in units:
        qp = q_ref[rows(n), cols(j)]
        kp = k_ref[rows(n), cols(j)]
        kdt = (kp.astype(F32) * kdec_ref[:, cols(j)]).T.astype(BF16)
        both = []
        for hh in range(2):
            h = 2 * j + hh
            s[n, h] = lax.dot_general(jnp.where(own(hh), qp, jnp.zeros_like(qp)), kp, (((1,), (1,)), ((), ())),
                                      preferred_element_type=F32)
            both.append(jnp.dot(kdt, v_ref[rows(n), vcol(h)], preferred_element_type=F32))
        kv[n, j] = jnp.where(rowi < RET_QK_DIM, both[0], both[1])
    for n, j in units:
        for hh in range(2):
            h = 2 * j + hh
            inner = (s[n, h] * dec_ref[h]).astype(BF16)
            o_sc[rows(n), vcol(h)] = jnp.dot(inner, v_ref[rows(n), vcol(h)], preferred_element_type=F32)
    state = [r_sc[j] for j in range(RET_HEADS // 2)]
    for n, j in units:
        qp = q_ref[rows(n), cols(j)]
        qd = (qp.astype(F32) * qdec_ref[:, cols(j)]).astype(BF16)
        r_b = state[j].astype(BF16)
        for hh in range(2):
            h = 2 * j + hh
            o_sc[rows(n), vcol(h)] += jnp.dot(jnp.where(own(hh), qd, jnp.zeros_like(qd)), r_b,
                                              preferred_element_type=F32)
        state[j] = state[j] * cdec_ref[j] + kv[n, j]
    for j in range(RET_HEADS // 2):
        r_sc[j] = state[j]
    for h in range(RET_HEADS):
        gs = slice(h * RET_V_DIM, (h + 1) * RET_V_DIM)
        o = o_sc[:, gs]
        oc = o - jnp.mean(o, axis=-1, keepdims=True)
        y = oc * lax.rsqrt(jnp.mean(oc * oc, axis=-1, keepdims=True) + EPS)
        o_ref[:, gs] = (y * g_ref[:, gs].astype(F32)).astype(BF16)


def _retention(z, consts, B, S, tc):
    T = B * S
    n = S // tc
    dec, qdec, kdec, cdec = consts
    kern = functools.partial(_retention_kernel, tc=tc)
    blk = lambda width, off: pl.BlockSpec((tc, width), lambda b, i: (b * n + i, off // width))
    return pl.pallas_call(
        kern,
        grid=(B, n),
        in_specs=[blk(RET_QK_W, OFF_QR), blk(RET_QK_W, OFF_KR), blk(RET_W, OFF_VR), blk(RET_W, OFF_GR),
                  _const_spec(dec.shape), _const_spec(qdec.shape), _const_spec(kdec.shape),
                  _const_spec(cdec.shape)],
        out_specs=pl.BlockSpec((tc, RET_W), lambda b, i: (b * n + i, 0)),
        out_shape=jax.ShapeDtypeStruct((T, RET_HEADS * RET_V_DIM), BF16),
        scratch_shapes=[pltpu.VMEM((RET_HEADS // 2, LANES, LANES), F32),
                        pltpu.VMEM((tc, RET_HEADS * RET_V_DIM), F32)],
        compiler_params=_cparams(("parallel", "arbitrary")),
        name="retention",
    )(z, z, z, z, dec, qdec, kdec, cdec)


def _retention_consts():
    C = RET_CHUNK
    log_gamma = jnp.log1p(-jnp.exp2(-5.0 - jnp.arange(RET_HEADS, dtype=F32)))
    idx = jnp.arange(C, dtype=F32)
    rel = idx[:, None] - idx[None, :]
    dec = jnp.where(rel[None] >= 0, jnp.exp(log_gamma[:, None, None] * jnp.maximum(rel, 0.0)[None]), 0.0)
    q_dec = jnp.exp(log_gamma[None, :] * (idx + 1.0)[:, None])
    k_dec = jnp.exp(log_gamma[None, :] * (C - 1.0 - idx)[:, None])
    qdec = jnp.repeat(q_dec, RET_QK_DIM, axis=1)
    kdec = jnp.repeat(k_dec, RET_QK_DIM, axis=1)
    chunk_decay = jnp.repeat(jnp.exp(log_gamma * C), RET_QK_DIM)
    cdec = jnp.broadcast_to(chunk_decay.reshape(RET_HEADS // 2, LANES, 1), (RET_HEADS // 2, LANES, LANES))
    return dec.astype(F32), qdec.astype(F32), kdec.astype(F32), cdec.astype(F32)


R_I1, R_I2, R_W1, R_W2, R_RANK1, R_RANK2 = range(6)


def _route(hn, wr_ref, rows, info_ref, carry_sc):
    w = wr_ref[...]
    hn_hi, w_hi = hn.astype(BF16), w.astype(BF16)
    hn_lo, w_lo = (hn - hn_hi.astype(F32)).astype(BF16), (w - w_hi.astype(F32)).astype(BF16)
    logits = (jnp.dot(hn_hi, w_hi, preferred_element_type=F32) + jnp.dot(hn_hi, w_lo, preferred_element_type=F32)
              + jnp.dot(hn_lo, w_hi, preferred_element_type=F32))
    lane = lax.broadcasted_iota(jnp.int32, logits.shape, 1)
    logits = jnp.where(lane < N_EXPERTS, logits, -jnp.inf)
    m1 = jnp.max(logits, axis=1, keepdims=True)
    i1 = jnp.min(jnp.where(logits == m1, lane, LANES), axis=1, keepdims=True)
    rest = jnp.where(lane == i1, -jnp.inf, logits)
    m2 = jnp.max(rest, axis=1, keepdims=True)
    i2 = jnp.min(jnp.where(rest == m2, lane, LANES), axis=1, keepdims=True)
    e = jnp.exp(m2 - m1)
    w1 = 1.0 / (1.0 + e)
    w2 = e / (1.0 + e)
    sel = jnp.where(lane == i1, 1.0, jnp.where(lane == i2, 1.0, 0.0))
    n = sel.shape[0]
    rr = lax.broadcasted_iota(jnp.int32, (n, n), 0)
    cc = lax.broadcasted_iota(jnp.int32, (n, n), 1)
    ltri = jnp.where(cc < rr, 1.0, 0.0).astype(BF16)
    before = jnp.dot(ltri, sel.astype(BF16), preferred_element_type=F32) + carry_sc[0:1, :]
    r1 = jnp.sum(jnp.where(lane == i1, before, 0.0), axis=1, keepdims=True)
    r2 = jnp.sum(jnp.where(lane == i2, before, 0.0), axis=1, keepdims=True)
    info = jnp.where(lane == R_I1, i1.astype(F32), 0.0)
    for ln, val in ((R_I2, i2.astype(F32)), (R_W1, w1), (R_W2, w2), (R_RANK1, r1), (R_RANK2, r2)):
        info = jnp.where(lane == ln, val, info)
    info_ref[rows, :] = info
    carry_sc[...] = carry_sc[...] + jnp.sum(sel, axis=0, keepdims=True)


def _merge_kernel(*refs, route):
    if route:
        (ya_ref, yr_ref, ga_ref, gb_ref, h_ref, wa_ref, wr_ref, wo_ref, gf_ref, wrt_ref,
         h_out, info_out, cnt_out, carry_sc) = refs

        @pl.when(pl.program_id(0) == 0)
        def _():
            carry_sc[...] = jnp.zeros(carry_sc.shape, F32)
    else:
        ya_ref, yr_ref, ga_ref, gb_ref, h_ref, wa_ref, wr_ref, wo_ref, gf_ref, h_out, hn_out = refs
    half = ya_ref.shape[0] // 2
    parts = [slice(0, half), slice(half, 2 * half)]
    branch = [(jnp.dot(ya_ref[r, :], wa_ref[...], preferred_element_type=F32),
               jnp.dot(yr_ref[r, :], wr_ref[...], preferred_element_type=F32)) for r in parts]
    for r, (pa, pr) in zip(parts, branch):
        merged = ga_ref[r, :].astype(F32) * pa + gb_ref[r, :].astype(F32) * pr
        hnew = h_ref[r, :] + jnp.dot(merged.astype(BF16), wo_ref[...], preferred_element_type=F32)
        h_out[r, :] = hnew
        ms = jnp.mean(hnew * hnew, axis=-1, keepdims=True)
        hn = hnew * lax.rsqrt(ms + EPS) * gf_ref[...]
        if route:
            _route(hn, wrt_ref, r, info_out, carry_sc)
        else:
            hn_out[r, :] = hn.astype(BF16)
    if route:
        cnt_out[...] = carry_sc[...]


def _merge(ya, yr, z, h, wa, wr, wo, gf, tm, w_router=None):
    T = h.shape[0]
    route = w_router is not None
    row = lambda width, cb=0: pl.BlockSpec((tm, width), lambda i: (i, cb))
    in_specs = [row(DA_W), row(RET_W), row(D_MODEL, OFF_GA // D_MODEL), row(D_MODEL, OFF_GB // D_MODEL),
                row(D_MODEL), _const_spec(wa.shape), _const_spec(wr.shape), _const_spec(wo.shape),
                _const_spec((1, D_MODEL))]
    args = [ya, yr, z, z, h, wa, wr, wo, gf]
    if route:
        in_specs.append(_const_spec((D_MODEL, LANES)))
        args.append(jnp.pad(w_router, ((0, 0), (0, LANES - N_EXPERTS))))
        out_specs = [row(D_MODEL), row(LANES), _const_spec((SUBLANES, LANES))]
        out_shape = [jax.ShapeDtypeStruct((T, D_MODEL), F32), jax.ShapeDtypeStruct((T, LANES), F32),
                     jax.ShapeDtypeStruct((SUBLANES, LANES), F32)]
        scratch = [pltpu.VMEM((SUBLANES, LANES), F32)]
    else:
        out_specs = [row(D_MODEL), row(D_MODEL)]
        out_shape = [jax.ShapeDtypeStruct((T, D_MODEL), F32), jax.ShapeDtypeStruct((T, D_MODEL), BF16)]
        scratch = []
    return pl.pallas_call(
        functools.partial(_merge_kernel, route=route),
        grid=(T // tm,),
        in_specs=in_specs,
        out_specs=out_specs,
        out_shape=out_shape,
        scratch_shapes=scratch,
        compiler_params=_cparams(("arbitrary",) if route else ("parallel",)),
        name="merge_route" if route else "merge_out",
    )(*args)


def _swiglu_into(acc_sc, x, w1_ref, w3_ref, w2_ref):
    chunks = [slice(c * FF_CHUNK, (c + 1) * FF_CHUNK) for c in range(D_FF // FF_CHUNK)]
    up = lambda cs: (jnp.dot(x, w1_ref[:, cs], preferred_element_type=F32),
                     jnp.dot(x, w3_ref[:, cs], preferred_element_type=F32))
    uv = up(chunks[0])
    for c, cs in enumerate(chunks):
        nxt = up(chunks[c + 1]) if c + 1 < len(chunks) else None
        a = (uv[0] * _sigmoid(uv[0]) * uv[1]).astype(BF16)
        d = jnp.dot(a, w2_ref[cs, :], preferred_element_type=F32)
        if c == 0:
            acc_sc[...] = d
        else:
            acc_sc[...] += d
        uv = nxt


def _ffn_kernel(hn_ref, prev_ref, w1_ref, w3_ref, w2_ref, o_ref, acc_sc):
    _swiglu_into(acc_sc, hn_ref[...], w1_ref, w3_ref, w2_ref)
    o_ref[...] = prev_ref[...] + acc_sc[...]


def _ffn(hn, prev, w1, w3, w2, tm):
    T = hn.shape[0]
    row = lambda width: pl.BlockSpec((tm, width), lambda i: (i, 0))
    wspec = lambda w: pl.BlockSpec(w.shape, lambda i: (0, 0), pipeline_mode=pl.Buffered(1))
    return pl.pallas_call(
        _ffn_kernel,
        grid=(T // tm,),
        in_specs=[row(D_MODEL), row(D_MODEL), wspec(w1), wspec(w3), wspec(w2)],
        out_specs=row(D_MODEL),
        out_shape=jax.ShapeDtypeStruct((T, D_MODEL), F32),
        scratch_shapes=[pltpu.VMEM((tm, D_MODEL), F32)],
        input_output_aliases={1: 0},
        compiler_params=_cparams(("parallel",)),
        name="ffn",
    )(hn, prev, w1, w3, w2)


def _dispatch_kernel(pad_lo_ref, pad_hi_ref, pos_ref, h_ref, xs_hbm, zero_sc, sem, pad_sem, *, tb):
    @pl.when(pl.program_id(0) == 0)
    def _():
        zero_sc[...] = jnp.zeros(zero_sc.shape, F32)
        zero_row = zero_sc.at[pl.ds(0, 1)]
        for e in range(N_EXPERTS):
            def fill(r, carry):
                pltpu.make_async_copy(zero_row, xs_hbm.at[pl.ds(r, 1)], pad_sem).start()
                return carry

            lax.fori_loop(pad_lo_ref[e], pad_hi_ref[e], fill, 0)
        for e in range(N_EXPERTS):
            def drain(r, carry):
                pltpu.make_async_copy(zero_row, xs_hbm.at[pl.ds(r, 1)], pad_sem).wait()
                return carry

            lax.fori_loop(pad_lo_ref[e], pad_hi_ref[e], drain, 0)

    def issue(i, carry):
        t0 = pl.multiple_of(i * SUBLANES, SUBLANES)
        for u in range(SUBLANES):
            for k in range(2):
                p = pos_ref[0, 0, k * tb + t0 + u]
                pltpu.make_async_copy(h_ref.at[pl.ds(t0 + u, 1)], xs_hbm.at[pl.ds(p, 1)], sem).start(priority=k)
        return carry

    lax.fori_loop(0, tb // SUBLANES, issue, 0)

    for k in range(2):
        pltpu.make_async_copy(h_ref, xs_hbm.at[pl.ds(0, tb)], sem).wait()


def _dispatch(pad_lo, pad_hi, pos_blocks, h, n_rows, tb):
    T = h.shape[0]
    return pl.pallas_call(
        functools.partial(_dispatch_kernel, tb=tb),
        grid_spec=pltpu.PrefetchScalarGridSpec(
            num_scalar_prefetch=2,
            grid=(T // tb,),
            in_specs=[pl.BlockSpec((1, 1, 2 * tb), lambda i, lo, hi: (i, 0, 0), memory_space=pltpu.SMEM),
                      pl.BlockSpec((tb, D_MODEL), lambda i, lo, hi: (i, 0))],
            out_specs=pl.BlockSpec(memory_space=pl.ANY),
            scratch_shapes=[pltpu.VMEM((SUBLANES, D_MODEL), F32), pltpu.SemaphoreType.DMA(()),
                            pltpu.SemaphoreType.DMA(())]),
        out_shape=jax.ShapeDtypeStruct((n_rows, D_MODEL), F32),
        compiler_params=pltpu.CompilerParams(dimension_semantics=("arbitrary",), has_side_effects=True),
        name="moe_dispatch",
    )(pad_lo, pad_hi, pos_blocks, h)


def _expert_kernel(te_ref, nu_ref, xs_ref, g_ref, w1_ref, w3_ref, w2_ref, y_ref, acc_sc):
    del te_ref
    i = pl.program_id(0)

    @pl.when(i < nu_ref[0])
    def _():
        half = xs_ref.shape[0] // 2
        for r in (slice(0, half), slice(half, 2 * half)):
            x = xs_ref[r, :]
            ms = jnp.mean(x * x, axis=-1, keepdims=True)
            hn = (x * lax.rsqrt(ms + EPS) * g_ref[...]).astype(BF16)
            _swiglu_into(acc_sc.at[r, :], hn, w1_ref.at[0], w3_ref.at[0], w2_ref.at[0])
            y_ref[r, :] = acc_sc[r, :]

    @pl.when(i >= nu_ref[0])
    def _():
        y_ref[...] = jnp.zeros(y_ref.shape, F32)


def _experts(tile_expert, n_used, xs, g, w1, w3, w2, te_rows):
    P = xs.shape[0]
    row = pl.BlockSpec((te_rows, D_MODEL), lambda i, te, nu: (jnp.minimum(i, nu[0] - 1), 0))
    wspec = lambda w: pl.BlockSpec((1,) + w.shape[1:], lambda i, te, nu: (te[i], 0, 0))
    return pl.pallas_call(
        _expert_kernel,
        grid_spec=pltpu.PrefetchScalarGridSpec(
            num_scalar_prefetch=2,
            grid=(P // te_rows,),
            in_specs=[row, pl.BlockSpec((1, D_MODEL), lambda i, te, nu: (0, 0)), wspec(w1), wspec(w3), wspec(w2)],
            out_specs=pl.BlockSpec((te_rows, D_MODEL), lambda i, te, nu: (i, 0)),
            scratch_shapes=[pltpu.VMEM((te_rows, D_MODEL), F32)]),
        out_shape=jax.ShapeDtypeStruct((P, D_MODEL), F32),
        compiler_params=_cparams(("arbitrary",)),
        name="moe_experts",
    )(tile_expert, n_used, xs, g, w1, w3, w2)


def _combine_kernel(pos_ref, info_ref, h_ref, y_hbm, o_ref, g_sc, sem, *, tb):
    def issue(i, carry):
        t0 = pl.multiple_of(i * SUBLANES, SUBLANES)
        for u in range(SUBLANES):
            for k in range(2):
                p = pos_ref[0, 0, k * tb + t0 + u]
                pltpu.make_async_copy(y_hbm.at[pl.ds(p, 1)], g_sc.at[k, pl.ds(t0 + u, 1)], sem).start(priority=k)
        return carry

    lax.fori_loop(0, tb // SUBLANES, issue, 0)

    for k in range(2):
        pltpu.make_async_copy(y_hbm.at[pl.ds(0, tb)], g_sc.at[k], sem).wait()
    info = info_ref[...]
    o_ref[...] = h_ref[...] + info[:, R_W1:R_W1 + 1] * g_sc[0] + info[:, R_W2:R_W2 + 1] * g_sc[1]


def _combine(pos_blocks, info, h, y, tb):
    T = h.shape[0]
    row = lambda width: pl.BlockSpec((tb, width), lambda i: (i, 0))
    return pl.pallas_call(
        functools.partial(_combine_kernel, tb=tb),
        grid=(T // tb,),
        in_specs=[pl.BlockSpec((1, 1, 2 * tb), lambda i: (i, 0, 0), memory_space=pltpu.SMEM),
                  row(LANES), row(D_MODEL), pl.BlockSpec(memory_space=pl.ANY)],
        out_specs=row(D_MODEL),
        out_shape=jax.ShapeDtypeStruct((T, D_MODEL), F32),
        scratch_shapes=[pltpu.VMEM((2, tb, D_MODEL), F32), pltpu.SemaphoreType.DMA(())],
        compiler_params=_cparams(("arbitrary",)),
        name="moe_combine",
    )(pos_blocks, info, h, y)


def _pos_blocks(pos1, pos2, tb):
    nb = pos1.shape[0] // tb
    return jnp.concatenate([pos1.reshape(nb, 1, tb), pos2.reshape(nb, 1, tb)], axis=2)


def _moe(h, info, cnt, g, w1, w3, w2):
    T = h.shape[0]
    te_rows = min(EXPERT_TILE, T)
    counts = cnt[0, :N_EXPERTS].astype(jnp.int32)
    padded = ((counts + te_rows - 1) // te_rows) * te_rows
    ends = jnp.cumsum(padded)
    base = ends - padded
    i1 = info[:, R_I1].astype(jnp.int32)
    i2 = info[:, R_I2].astype(jnp.int32)
    pos1 = jnp.take(base, i1) + info[:, R_RANK1].astype(jnp.int32)
    pos2 = jnp.take(base, i2) + info[:, R_RANK2].astype(jnp.int32)
    n_tiles = (2 * T) // te_rows + N_EXPERTS
    tile_expert = jnp.minimum(jnp.searchsorted(ends // te_rows, jnp.arange(n_tiles), side="right"),
                              N_EXPERTS - 1).astype(jnp.int32)
    n_used = (ends[-1:] // te_rows).astype(jnp.int32)

    tb_d = min(DISPATCH_BLOCK, T)
    xs = _dispatch(base + counts, ends, _pos_blocks(pos1, pos2, tb_d), h, n_tiles * te_rows, tb_d)
    y = _experts(tile_expert, n_used, xs, g, w1, w3, w2, te_rows)
    tb_c = min(COMBINE_BLOCK, T)
    return _combine(_pos_blocks(pos1, pos2, tb_c), info, h, y, tb_c)


def kernel(x, positions, attn_norm, w_in, da_q_norm, da_k_norm, lam_q1, lam_k1, lam_q2, lam_k2, da_subln,
           w_proj_da, w_proj_ret, w_out, ffn_norm, dense_w1, dense_w3, dense_w2, w_router, moe_w1, moe_w3,
           moe_w2):
    B, S, D = x.shape
    T = B * S
    depth = attn_norm.shape[0]
    tm = tq = tc = min(ROW_BLOCK, S)
    assert D == D_MODEL and S % tm == 0 and tm % (2 * MXU_DIM) == 0 and tm % RET_CHUNK == 0, (x.shape, tm)

    def rope_tables(pos):
        pos_b = jnp.broadcast_to(pos.reshape(-1, 1).astype(F32), (pos.size, LANES))
        return (_rope_tables(pos_b, *_rope_rows(ROPE_ROT_DIM, ROPE_THETA), tm)
                + _rope_tables(pos_b, *_rope_rows(RET_QK_DIM, RET_ROPE_THETA), tm))

    tabs = lax.cond(jnp.all(positions == positions[:1]),
                    lambda: tuple(jnp.tile(t, (B, 1)) for t in rope_tables(positions[0])),
                    lambda: tuple(rope_tables(positions.reshape(T))))
    ret_consts = _retention_consts()
    gsum = jnp.asarray(np.kron(np.eye(MXU_DIM // HEAD), np.ones((HEAD, HEAD))), BF16)

    h = x.reshape(T, D)
    for layer in range(depth):
        lambda_init = 0.8 - 0.6 * math.exp(-0.3 * layer)
        gq = jnp.tile(da_q_norm[layer], PROJ_CHUNK // DA_HEAD_DIM).reshape(1, PROJ_CHUNK)
        gk = jnp.tile(da_k_norm[layer], PROJ_CHUNK // DA_HEAD_DIM).reshape(1, PROJ_CHUNK)
        z, vt = _in_proj(h, attn_norm[layer].reshape(1, D), w_in[layer].astype(BF16), gq, gk, gsum, tabs, tm)
        lam_rows = jnp.stack([lam_q1[layer], lam_k1[layer], lam_q2[layer], lam_k2[layer]])
        ya = _diff_attn(z, vt, _score_bound(da_q_norm[layer], da_k_norm[layer]), lam_rows,
                        da_subln[layer].reshape(1, DA_V_DIM), B, S, tq, lambda_init)
        yr = _retention(z, ret_consts, B, S, min(2 * tc, S))
        j = layer // 2
        merge_args = (ya, yr, z, h, w_proj_da[layer].astype(BF16), w_proj_ret[layer].astype(BF16),
                      w_out[layer].astype(BF16), ffn_norm[layer].reshape(1, D))
        if layer % 2 == 0:
            h, hn = _merge(*merge_args, min(2 * tm, S))
            h = _ffn(hn, h, dense_w1[j].astype(BF16), dense_w3[j].astype(BF16), dense_w2[j].astype(BF16), tm)
        else:
            h, info, cnt = _merge(*merge_args, tm, w_router=w_router[j])
            h = _moe(h, info, cnt, ffn_norm[layer].reshape(1, D), moe_w1[j].astype(BF16),
                     moe_w3[j].astype(BF16), moe_w2[j].astype(BF16))
    return h.reshape(B, S, D)
```

```python
import functools
import math

import jax
import jax.numpy as jnp
import numpy as np
from jax import lax
from jax.experimental import pallas as pl
from jax.experimental.pallas import tpu as pltpu

F32 = jnp.float32
BF16 = jnp.bfloat16

D_MODEL = 1024
DA_HEADS = 4
DA_HEAD_DIM = 64
DA_V_DIM = 128
ROPE_THETA = 500000.0
ROPE_ROT_DIM = 16
RET_HEADS = 4
RET_QK_DIM = 64
RET_V_DIM = 128
RET_CHUNK = 128
RET_ROPE_THETA = 10000.0
D_FF = 2816
N_EXPERTS = 8
EPS = 1e-6
LOG2_E = math.log2(math.e)
DA_VT_ROWS = DA_V_DIM + 16
FIXED_REF_MAX_BOUND = 60.0
GROUPS = (8, 4)

LANES = 128
SUBLANES = 8
MXU_DIM = 256
HEAD = 64
assert DA_HEAD_DIM == RET_QK_DIM == HEAD and LANES == 2 * HEAD
DA_W = DA_HEADS * DA_V_DIM
RET_W = RET_HEADS * RET_V_DIM
RET_QK_W = RET_HEADS * RET_QK_DIM
ROW_BLOCK = 512
EXPERT_TILE = 512
DISPATCH_BLOCK = 2048
COMBINE_BLOCK = 1024
OFF_QA, OFF_KA, OFF_VA = 0, 512, 1024
OFF_QR, OFF_KR, OFF_VR, OFF_GR = 1536, 1792, 2048, 2560
OFF_GA, OFF_GB = 3072, 4096
IN_W = 5120
PROJ_CHUNK = 512
FF_CHUNK = 256
VMEM_LIMIT = 56 * 1024 * 1024


def _cparams(sem):
    return pltpu.CompilerParams(dimension_semantics=sem, vmem_limit_bytes=VMEM_LIMIT)


def _const_spec(shape):
    nd = len(shape)
    return pl.BlockSpec(shape, lambda *_: (0,) * nd)


def _rope_table_kernel(pos_ref, inv_ref, sgn_ref, cos_ref, sin_ref):
    ang = pos_ref[...] * inv_ref[...]
    cos_ref[...] = jnp.cos(ang)
    sin_ref[...] = jnp.sin(ang) * sgn_ref[...]


def _rope_tables(pos_b, inv_row, sgn_row, tm):
    T = pos_b.shape[0]
    row = pl.BlockSpec((tm, LANES), lambda i: (i, 0))
    return pl.pallas_call(
        _rope_table_kernel,
        grid=(T // tm,),
        in_specs=[row, _const_spec((1, LANES)), _const_spec((1, LANES))],
        out_specs=[row, row],
        out_shape=[jax.ShapeDtypeStruct((T, LANES), F32)] * 2,
        compiler_params=_cparams(("parallel",)),
        name="rope_tables",
    )(pos_b, inv_row, sgn_row)


def _rope_rows(rot_dim, theta):
    half = rot_dim // 2
    inv = 1.0 / (theta ** (jnp.arange(half, dtype=F32) / half))
    jj = np.arange(LANES) % HEAD
    inv_row = jnp.where(jj < rot_dim, inv[jj % half], 0.0).astype(F32)
    sgn_row = np.where(jj < half, -1.0, np.where(jj < rot_dim, 1.0, 0.0)).astype(np.float32)
    return inv_row.reshape(1, LANES), jnp.asarray(sgn_row).reshape(1, LANES)


def _rotate_half(x, half):
    jj = lax.broadcasted_iota(jnp.int32, x.shape, 1) % HEAD
    return jnp.where(jj < half, pltpu.roll(x, LANES - half, 1), pltpu.roll(x, half, 1))


def _sigmoid(x):
    return 1.0 / (1.0 + jnp.exp(-x))


def _in_proj_kernel(h_ref, g_ref, w_ref, gq_ref, gk_ref, gsum_ref, ca_ref, sa_ref, cr_ref, sr_ref, z_ref, vt_ref):
    part = vt_ref.shape[-1]
    for p in range(h_ref.shape[0] // part):
        _in_proj_rows(p, slice(p * part, (p + 1) * part), h_ref, g_ref, w_ref, gq_ref, gk_ref, gsum_ref, ca_ref,
                      sa_ref, cr_ref, sr_ref, z_ref, vt_ref)


def _in_proj_rows(p, r, h_ref, g_ref, w_ref, gq_ref, gk_ref, gsum_ref, ca_ref, sa_ref, cr_ref, sr_ref, z_ref,
                  vt_ref):
    x = h_ref[r, :]
    ms = jnp.mean(x * x, axis=-1, keepdims=True)
    hn = (x * lax.rsqrt(ms + EPS) * g_ref[...]).astype(BF16)
    ca, sa, cr, sr = ca_ref[r, :], sa_ref[r, :], cr_ref[r, :], sr_ref[r, :]

    def head_rms(zc, gain):
        sq = (zc * zc).astype(BF16)
        outs = []
        for s in range(PROJ_CHUNK // MXU_DIM):
            sl = slice(s * MXU_DIM, (s + 1) * MXU_DIM)
            tot = jnp.dot(sq[:, sl], gsum_ref[...], preferred_element_type=F32)
            outs.append(zc[:, sl] * lax.rsqrt(tot * (1.0 / HEAD) + EPS))
        return jnp.concatenate(outs, axis=1) * gain

    def rope(y, cos, sin, half):
        outs = []
        for s in range(y.shape[1] // LANES):
            ys = y[:, s * LANES:(s + 1) * LANES]
            outs.append(ys * cos + _rotate_half(ys, half) * sin)
        return jnp.concatenate(outs, axis=1)

    for c in range(IN_W // PROJ_CHUNK):
        c0 = c * PROJ_CHUNK
        zc = jnp.dot(hn, w_ref[:, c0:c0 + PROJ_CHUNK], preferred_element_type=F32)
        if c0 == OFF_QA:
            zc = rope(head_rms(zc, gq_ref[...]), ca, sa, ROPE_ROT_DIM // 2)
            zc = zc * (DA_HEAD_DIM ** -0.5 * LOG2_E)
        elif c0 == OFF_KA:
            zc = rope(head_rms(zc, gk_ref[...]), ca, sa, ROPE_ROT_DIM // 2)
        elif c0 == OFF_VA:
            zt = zc.T.astype(BF16)
            extra = lax.broadcasted_iota(jnp.int32, (DA_VT_ROWS - DA_V_DIM, zt.shape[1]), 0)
            for hd in range(DA_HEADS):
                vt_ref[p, hd, 0:DA_V_DIM, :] = zt[hd * DA_V_DIM:(hd + 1) * DA_V_DIM, :]
                vt_ref[p, hd, DA_V_DIM:DA_VT_ROWS, :] = jnp.where(extra == 0, 1.0, 0.0).astype(BF16)
        elif c0 == OFF_QR:
            zc = rope(zc, cr, sr, RET_QK_DIM // 2)
            col = lax.broadcasted_iota(jnp.int32, zc.shape, 1)
            zc = jnp.where(col >= OFF_KR - OFF_QR, zc * (RET_QK_DIM ** -0.5), zc)
        elif c0 == OFF_GR:
            zc = zc * _sigmoid(zc)
        elif c0 >= OFF_GA:
            zc = _sigmoid(zc)
        z_ref[r, c0:c0 + PROJ_CHUNK] = zc.astype(BF16)


def _in_proj(h, g, w, gq, gk, gsum, tabs, tm, tk):
    T = h.shape[0]
    row = lambda width: pl.BlockSpec((tm, width), lambda i: (i, 0))
    return pl.pallas_call(
        _in_proj_kernel,
        grid=(T // tm,),
        in_specs=[row(D_MODEL), _const_spec((1, D_MODEL)),
                  pl.BlockSpec((D_MODEL, IN_W), lambda i: (0, 0), pipeline_mode=pl.Buffered(1)),
                  _const_spec((1, PROJ_CHUNK)), _const_spec((1, PROJ_CHUNK)), _const_spec((MXU_DIM, MXU_DIM)),
                  row(LANES), row(LANES), row(LANES), row(LANES)],
        out_specs=[row(IN_W), pl.BlockSpec((tm // tk, DA_HEADS, DA_VT_ROWS, tk), lambda i: (i, 0, 0, 0))],
        out_shape=[jax.ShapeDtypeStruct((T, IN_W), BF16),
                   jax.ShapeDtypeStruct((T // tk, DA_HEADS, DA_VT_ROWS, tk), BF16)],
        compiler_params=_cparams(("parallel",)),
        name="in_proj",
    )(h, g, w, gq, gk, gsum, *tabs)


def _diff_attn_kernel(bound_ref, q_ref, k_ref, vt_ref, lam_ref, subln_ref, o_ref, acc_sc, st_sc, *, tq,
                      lambda_init):
    qi = pl.program_id(2)
    q = q_ref[...]
    lane = lax.broadcasted_iota(jnp.int32, q.shape, 1)
    qs = (jnp.where(lane < DA_HEAD_DIM, q, jnp.zeros_like(q)),
          jnp.where(lane >= DA_HEAD_DIM, q, jnp.zeros_like(q)))
    acc_sc[...] = jnp.zeros(acc_sc.shape, F32)

    tqh = tq // 2
    streams = [(c, hf) for hf in range(2) for c in range(2)]
    qsub = [qs[c][hf * tqh:(hf + 1) * tqh, :] for c, hf in streams]

    def scores(ki, diagonal=False):
        k = k_ref[pl.ds(pl.multiple_of(ki * tq, tq), tq), :]
        return [lax.dot_general(k[0:tqh] if diagonal and hf == 0 else k, qq, (((1,), (1,)), ((), ())),
                                preferred_element_type=F32) for qq, (c, hf) in zip(qsub, streams)]

    def stash(sts):
        for s in range(len(streams)):
            st_sc[s] = sts[s]

    bound = bound_ref[0, 0]

    def masked_scores(st, hf):
        key = lax.broadcasted_iota(jnp.int32, st.shape, 0)
        qry = lax.broadcasted_iota(jnp.int32, st.shape, 1) + hf * tqh
        return jnp.where(key <= qry, st, -jnp.inf)

    def consume_online(sts, ki, stats, masked):
        vt = vt_ref[ki, 0]
        out = []
        for s, (c, hf) in enumerate(streams):
            st = st_sc[s] if sts is None else sts[s]
            if masked:
                st = masked_scores(st, hf)
            cols = slice(hf * tqh, (hf + 1) * tqh)
            m_old = stats[s]
            m_new = jnp.maximum(m_old, jnp.max(st, axis=0, keepdims=True))
            alpha = jnp.exp2(m_old - m_new)
            pt = jnp.exp2((st - m_new).astype(BF16))
            acc_sc[c, :, cols] = alpha * acc_sc[c, :, cols] + jnp.dot(vt, pt, preferred_element_type=F32)
            out.append(m_new)
        return tuple(out)

    def run_fixed_ref():
        def accumulate(blocks, last_masked):
            diag = [last_masked and u == len(blocks) - 1 for u in range(len(blocks))]
            sts = [scores(b, dg) for b, dg in zip(blocks, diag)]
            tot = [None] * len(streams)
            den = [None] * len(streams)
            for b, st4, dg in zip(blocks, sts, diag):
                vt = vt_ref[b, 0, 0:DA_V_DIM, :]
                for s, (c, hf) in enumerate(streams):
                    st = masked_scores(st4[s], hf) if dg else st4[s]
                    keys = st.shape[0]
                    pt = jnp.exp2(st - bound)
                    part = jnp.sum(pt.reshape(keys // SUBLANES, SUBLANES, tqh), axis=0)
                    d = jnp.dot(vt[:, 0:keys], pt.astype(BF16), preferred_element_type=F32)
                    tot[s] = d if tot[s] is None else tot[s] + d
                    den[s] = part if den[s] is None else den[s] + part
            for s, (c, hf) in enumerate(streams):
                cols = slice(hf * tqh, (hf + 1) * tqh)
                acc_sc[c, 0:DA_V_DIM, cols] += tot[s]
                acc_sc[c, DA_V_DIM:DA_V_DIM + 1, cols] += jnp.sum(den[s], axis=0, keepdims=True)

        done = 0
        for g in GROUPS:
            def trip(j, carry, g=g, done=done):
                accumulate([done + g * j + u for u in range(g)], False)
                return carry

            ntrips = (qi - done) // g
            lax.fori_loop(0, ntrips, trip, 0)
            done = done + g * ntrips
        for rest in range(GROUPS[-1]):
            pl.when(qi - done == rest)(
                functools.partial(accumulate, [qi - rest + u for u in range(rest + 1)], True))

    def run_online():
        npairs = qi // 2
        stash(scores(0))

        def pair(j, stats):
            b0 = 2 * j
            odd = scores(b0 + 1)
            stats = consume_online(None, b0, stats, False)
            stash(scores(b0 + 2))
            return consume_online(odd, b0 + 1, stats, False)

        def leftover(stats):
            stats = consume_online(None, qi - 1, stats, False)
            stash(scores(qi))
            return stats

        init = (jnp.full((1, tqh), -jnp.inf, F32),) * len(streams)
        stats = lax.fori_loop(0, npairs, pair, init)
        stats = lax.cond(qi - 2 * npairs == 1, leftover, lambda st: st, stats)
        consume_online(None, qi, stats, True)

    pl.when(bound <= FIXED_REF_MAX_BOUND)(run_fixed_ref)
    pl.when(bound > FIXED_REF_MAX_BOUND)(run_online)

    lv = lam_ref[...]
    lam = (jnp.exp(jnp.sum(lv[0:1] * lv[1:2], axis=1, keepdims=True))
           - jnp.exp(jnp.sum(lv[2:3] * lv[3:4], axis=1, keepdims=True)) + lambda_init)
    dv = DA_V_DIM
    ot = (acc_sc[0, 0:dv, :] / acc_sc[0, dv:dv + 1, :]
          - lam * (acc_sc[1, 0:dv, :] / acc_sc[1, dv:dv + 1, :]))
    yt = ot * lax.rsqrt(jnp.mean(ot * ot, axis=0, keepdims=True) + EPS)
    o_ref[...] = (yt.T * subln_ref[...] * (1.0 - lambda_init)).astype(BF16)


def _score_bound(gq, gk):
    norm = math.sqrt(DA_HEAD_DIM)
    return (1.01 * norm * norm * DA_HEAD_DIM ** -0.5 * LOG2_E
            * jnp.max(jnp.abs(gq)) * jnp.max(jnp.abs(gk))).astype(F32).reshape(1, 1)


def _diff_attn(z, vt, bound, lam_rows, subln, B, S, tq, lambda_init):
    T = B * S
    nq = S // tq
    kern = functools.partial(_diff_attn_kernel, tq=tq, lambda_init=lambda_init)
    return pl.pallas_call(
        kern,
        grid=(B, DA_HEADS, nq),
        in_specs=[pl.BlockSpec((1, 1), lambda b, h, i: (0, 0), memory_space=pltpu.SMEM),
                  pl.BlockSpec((tq, LANES), lambda b, h, i: (b * nq + i, OFF_QA // LANES + h)),
                  pl.BlockSpec((S, LANES), lambda b, h, i: (b, OFF_KA // LANES + h)),
                  pl.BlockSpec((nq, 1, DA_VT_ROWS, tq), lambda b, h, i: (b, h, 0, 0)),
                  _const_spec((4, DA_HEAD_DIM)), _const_spec((1, DA_V_DIM))],
        out_specs=pl.BlockSpec((tq, LANES), lambda b, h, i: (b * nq + i, h)),
        out_shape=jax.ShapeDtypeStruct((T, DA_HEADS * DA_V_DIM), BF16),
        scratch_shapes=[pltpu.VMEM((2, DA_VT_ROWS, tq), F32), pltpu.VMEM((4, tq, tq // 2), F32)],
        compiler_params=_cparams(("parallel", "parallel", "arbitrary")),
        name="diff_attn",
    )(bound, z, z, vt, lam_rows, subln)


def _retention_kernel(q_ref, k_ref, v_ref, g_ref, dec_ref, qdec_ref, kdec_ref, cdec_ref, o_ref, r_sc, o_sc, *,
                      tc):
    @pl.when(pl.program_id(1) == 0)
    def _():
        r_sc[...] = jnp.zeros(r_sc.shape, F32)

    C = RET_CHUNK
    lane = lax.broadcasted_iota(jnp.int32, (C, LANES), 1)
    rowi = lax.broadcasted_iota(jnp.int32, (C, LANES), 0)
    units = [(n, j) for n in range(tc // C) for j in range(RET_HEADS // 2)]
    rows = lambda n: slice(n * C, (n + 1) * C)
    cols = lambda j: slice(j * LANES, (j + 1) * LANES)
    vcol = lambda h: slice(h * RET_V_DIM, (h + 1) * RET_V_DIM)
    own = lambda hh: (lane >= RET_QK_DIM) if hh else (lane < RET_QK_DIM)
    s, kv = {}, {}
    for n, j in units:
        qp = q_ref[rows(n), cols(j)]
        kp = k_ref[rows(n), cols(j)]
        kdt = (kp.astype(F32) * kdec_ref[:, cols(j)]).T.astype(BF16)
        both = []
        for hh in range(2):
            h = 2 * j + hh
            s[n, h] = lax.dot_general(jnp.where(own(hh), qp, jnp.zeros_like(qp)), kp, (((1,), (1,)), ((), ())),
                                      preferred_element_type=F32)
            both.append(jnp.dot(kdt, v_ref[rows(n), vcol(h)], preferred_element_type=F32))
        kv[n, j] = jnp.where(rowi < RET_QK_DIM, both[0], both[1])
    for n, j in units:
        for hh in range(2):
            h = 2 * j + hh
            inner = (s[n, h] * dec_ref[h]).astype(BF16)
            o_sc[rows(n), vcol(h)] = jnp.dot(inner, v_ref[rows(n), vcol(h)], preferred_element_type=F32)
    state = [r_sc[j] for j in range(RET_HEADS // 2)]
    for n, j in units:
        qp = q_ref[rows(n), cols(j)]
        qd = (qp.astype(F32) * qdec_ref[:, cols(j)]).astype(BF16)
        r_b = state[j].astype(BF16)
        for hh in range(2):
            h = 2 * j + hh
            o_sc[rows(n), vcol(h)] += jnp.dot(jnp.where(own(hh), qd, jnp.zeros_like(qd)), r_b,
                                              preferred_element_type=F32)
        state[j] = state[j] * cdec_ref[j] + kv[n, j]
    for j in range(RET_HEADS // 2):
        r_sc[j] = state[j]
    for h in range(RET_HEADS):
        gs = slice(h * RET_V_DIM, (h + 1) * RET_V_DIM)
        o = o_sc[:, gs]
        oc = o - jnp.mean(o, axis=-1, keepdims=True)
        y = oc * lax.rsqrt(jnp.mean(oc * oc, axis=-1, keepdims=True) + EPS)
        o_ref[:, gs] = (y * g_ref[:, gs].astype(F32)).astype(BF16)


def _retention(z, consts, B, S, tc):
    T = B * S
    n = S // tc
    dec, qdec, kdec, cdec = consts
    kern = functools.partial(_retention_kernel, tc=tc)
    blk = lambda width, off: pl.BlockSpec((tc, width), lambda b, i: (b * n + i, off // width))
    return pl.pallas_call(
        kern,
        grid=(B, n),
        in_specs=[blk(RET_QK_W, OFF_QR), blk(RET_QK_W, OFF_KR), blk(RET_W, OFF_VR), blk(RET_W, OFF_GR),
                  _const_spec(dec.shape), _const_spec(qdec.shape), _const_spec(kdec.shape),
                  _const_spec(cdec.shape)],
        out_specs=pl.BlockSpec((tc, RET_W), lambda b, i: (b * n + i, 0)),
        out_shape=jax.ShapeDtypeStruct((T, RET_HEADS * RET_V_DIM), BF16),
        scratch_shapes=[pltpu.VMEM((RET_HEADS // 2, LANES, LANES), F32),
                        pltpu.VMEM((tc, RET_HEADS * RET_V_DIM), F32)],
        compiler_params=_cparams(("parallel", "arbitrary")),
        name="retention",
    )(z, z, z, z, dec, qdec, kdec, cdec)


def _retention_consts():
    C = RET_CHUNK
    log_gamma = jnp.log1p(-jnp.exp2(-5.0 - jnp.arange(RET_HEADS, dtype=F32)))
    idx = jnp.arange(C, dtype=F32)
    rel = idx[:, None] - idx[None, :]
    dec = jnp.where(rel[None] >= 0, jnp.exp(log_gamma[:, None, None] * jnp.maximum(rel, 0.0)[None]), 0.0)
    q_dec = jnp.exp(log_gamma[None, :] * (idx + 1.0)[:, None])
    k_dec = jnp.exp(log_gamma[None, :] * (C - 1.0 - idx)[:, None])
    qdec = jnp.repeat(q_dec, RET_QK_DIM, axis=1)
    kdec = jnp.repeat(k_dec, RET_QK_DIM, axis=1)
    chunk_decay = jnp.repeat(jnp.exp(log_gamma * C), RET_QK_DIM)
    cdec = jnp.broadcast_to(chunk_decay.reshape(RET_HEADS // 2, LANES, 1), (RET_HEADS // 2, LANES, LANES))
    return dec.astype(F32), qdec.astype(F32), kdec.astype(F32), cdec.astype(F32)


R_I1, R_I2, R_W1, R_W2, R_RANK1, R_RANK2 = range(6)


def _route(hn, wr_ref, rows, info_ref, carry_sc):
    w = wr_ref[...]
    hn_hi, w_hi = hn.astype(BF16), w.astype(BF16)
    hn_lo, w_lo = (hn - hn_hi.astype(F32)).astype(BF16), (w - w_hi.astype(F32)).astype(BF16)
    logits = (jnp.dot(hn_hi, w_hi, preferred_element_type=F32) + jnp.dot(hn_hi, w_lo, preferred_element_type=F32)
              + jnp.dot(hn_lo, w_hi, preferred_element_type=F32))
    lane = lax.broadcasted_iota(jnp.int32, logits.shape, 1)
    logits = jnp.where(lane < N_EXPERTS, logits, -jnp.inf)
    m1 = jnp.max(logits, axis=1, keepdims=True)
    i1 = jnp.min(jnp.where(logits == m1, lane, LANES), axis=1, keepdims=True)
    rest = jnp.where(lane == i1, -jnp.inf, logits)
    m2 = jnp.max(rest, axis=1, keepdims=True)
    i2 = jnp.min(jnp.where(rest == m2, lane, LANES), axis=1, keepdims=True)
    e = jnp.exp(m2 - m1)
    w1 = 1.0 / (1.0 + e)
    w2 = e / (1.0 + e)
    sel = jnp.where(lane == i1, 1.0, jnp.where(lane == i2, 1.0, 0.0))
    n = sel.shape[0]
    rr = lax.broadcasted_iota(jnp.int32, (n, n), 0)
    cc = lax.broadcasted_iota(jnp.int32, (n, n), 1)
    ltri = jnp.where(cc < rr, 1.0, 0.0).astype(BF16)
    before = jnp.dot(ltri, sel.astype(BF16), preferred_element_type=F32) + carry_sc[0:1, :]
    r1 = jnp.sum(jnp.where(lane == i1, before, 0.0), axis=1, keepdims=True)
    r2 = jnp.sum(jnp.where(lane == i2, before, 0.0), axis=1, keepdims=True)
    info = jnp.where(lane == R_I1, i1.astype(F32), 0.0)
    for ln, val in ((R_I2, i2.astype(F32)), (R_W1, w1), (R_W2, w2), (R_RANK1, r1), (R_RANK2, r2)):
        info = jnp.where(lane == ln, val, info)
    info_ref[rows, :] = info
    carry_sc[...] = carry_sc[...] + jnp.sum(sel, axis=0, keepdims=True)


def _merge_kernel(*refs, route):
    if route:
        (ya_ref, yr_ref, ga_ref, gb_ref, h_ref, wa_ref, wr_ref, wo_ref, gf_ref, wrt_ref,
         h_out, info_out, cnt_out, carry_sc) = refs

        @pl.when(pl.program_id(0) == 0)
        def _():
            carry_sc[...] = jnp.zeros(carry_sc.shape, F32)
    else:
        ya_ref, yr_ref, ga_ref, gb_ref, h_ref, wa_ref, wr_ref, wo_ref, gf_ref, h_out, hn_out = refs
    rows = ya_ref.shape[0]
    size = MXU_DIM if route else rows // 2
    parts = [slice(r0, r0 + size) for r0 in range(0, rows, size)]
    branch = [(jnp.dot(ya_ref[r, :], wa_ref[...], preferred_element_type=F32),
               jnp.dot(yr_ref[r, :], wr_ref[...], preferred_element_type=F32)) for r in parts]
    for r, (pa, pr) in zip(parts, branch):
        merged = ga_ref[r, :].astype(F32) * pa + gb_ref[r, :].astype(F32) * pr
        hnew = h_ref[r, :] + jnp.dot(merged.astype(BF16), wo_ref[...], preferred_element_type=F32)
        h_out[r, :] = hnew
        ms = jnp.mean(hnew * hnew, axis=-1, keepdims=True)
        hn = hnew * lax.rsqrt(ms + EPS) * gf_ref[...]
        if route:
            _route(hn, wrt_ref, r, info_out, carry_sc)
        else:
            hn_out[r, :] = hn.astype(BF16)
    if route:
        cnt_out[...] = carry_sc[...]


def _merge(ya, yr, z, h, wa, wr, wo, gf, tm, w_router=None):
    T = h.shape[0]
    route = w_router is not None
    row = lambda width, cb=0: pl.BlockSpec((tm, width), lambda i: (i, cb))
    in_specs = [row(DA_W), row(RET_W), row(D_MODEL, OFF_GA // D_MODEL), row(D_MODEL, OFF_GB // D_MODEL),
                row(D_MODEL), _const_spec(wa.shape), _const_spec(wr.shape), _const_spec(wo.shape),
                _const_spec((1, D_MODEL))]
    args = [ya, yr, z, z, h, wa, wr, wo, gf]
    if route:
        in_specs.append(_const_spec((D_MODEL, LANES)))
        args.append(jnp.pad(w_router, ((0, 0), (0, LANES - N_EXPERTS))))
        out_specs = [row(D_MODEL), row(LANES), _const_spec((SUBLANES, LANES))]
        out_shape = [jax.ShapeDtypeStruct((T, D_MODEL), F32), jax.ShapeDtypeStruct((T, LANES), F32),
                     jax.ShapeDtypeStruct((SUBLANES, LANES), F32)]
        scratch = [pltpu.VMEM((SUBLANES, LANES), F32)]
    else:
        out_specs = [row(D_MODEL), row(D_MODEL)]
        out_shape = [jax.ShapeDtypeStruct((T, D_MODEL), F32), jax.ShapeDtypeStruct((T, D_MODEL), BF16)]
        scratch = []
    return pl.pallas_call(
        functools.partial(_merge_kernel, route=route),
        grid=(T // tm,),
        in_specs=in_specs,
        out_specs=out_specs,
        out_shape=out_shape,
        scratch_shapes=scratch,
        compiler_params=_cparams(("arbitrary",) if route else ("parallel",)),
        name="merge_route" if route else "merge_out",
    )(*args)


def _swiglu_into(acc_sc, x, w1_ref, w3_ref, w2_ref):
    chunks = [slice(c * FF_CHUNK, (c + 1) * FF_CHUNK) for c in range(D_FF // FF_CHUNK)]
    up = lambda cs: (jnp.dot(x, w1_ref[:, cs], preferred_element_type=F32),
                     jnp.dot(x, w3_ref[:, cs], preferred_element_type=F32))
    uv = up(chunks[0])
    for c, cs in enumerate(chunks):
        nxt = up(chunks[c + 1]) if c + 1 < len(chunks) else None
        a = (uv[0] * _sigmoid(uv[0]) * uv[1]).astype(BF16)
        d = jnp.dot(a, w2_ref[cs, :], preferred_element_type=F32)
        if c == 0:
            acc_sc[...] = d
        else:
            acc_sc[...] += d
        uv = nxt


def _ffn_kernel(hn_ref, prev_ref, w1_ref, w3_ref, w2_ref, o_ref, acc_sc):
    _swiglu_into(acc_sc, hn_ref[...], w1_ref, w3_ref, w2_ref)
    o_ref[...] = prev_ref[...] + acc_sc[...]


def _ffn(hn, prev, w1, w3, w2, tm):
    T = hn.shape[0]
    row = lambda width: pl.BlockSpec((tm, width), lambda i: (i, 0))
    wspec = lambda w: pl.BlockSpec(w.shape, lambda i: (0, 0), pipeline_mode=pl.Buffered(1))
    return pl.pallas_call(
        _ffn_kernel,
        grid=(T // tm,),
        in_specs=[row(D_MODEL), row(D_MODEL), wspec(w1), wspec(w3), wspec(w2)],
        out_specs=row(D_MODEL),
        out_shape=jax.ShapeDtypeStruct((T, D_MODEL), F32),
        scratch_shapes=[pltpu.VMEM((tm, D_MODEL), F32)],
        input_output_aliases={1: 0},
        compiler_params=_cparams(("parallel",)),
        name="ffn",
    )(hn, prev, w1, w3, w2)


def _dispatch_kernel(pad_lo_ref, pad_hi_ref, pos_ref, h_ref, xs_hbm, zero_sc, sem, pad_sem, *, tb):
    @pl.when(pl.program_id(0) == 0)
    def _():
        zero_sc[...] = jnp.zeros(zero_sc.shape, F32)
        zero_row = zero_sc.at[pl.ds(0, 1)]
        for e in range(N_EXPERTS):
            def fill(r, carry):
                pltpu.make_async_copy(zero_row, xs_hbm.at[pl.ds(r, 1)], pad_sem).start()
                return carry

            lax.fori_loop(pad_lo_ref[e], pad_hi_ref[e], fill, 0)
        for e in range(N_EXPERTS):
            def drain(r, carry):
                pltpu.make_async_copy(zero_row, xs_hbm.at[pl.ds(r, 1)], pad_sem).wait()
                return carry

            lax.fori_loop(pad_lo_ref[e], pad_hi_ref[e], drain, 0)

    def issue(i, carry):
        t0 = pl.multiple_of(i * SUBLANES, SUBLANES)
        for u in range(SUBLANES):
            for k in range(2):
                p = pos_ref[0, 0, k * tb + t0 + u]
                pltpu.make_async_copy(h_ref.at[pl.ds(t0 + u, 1)], xs_hbm.at[pl.ds(p, 1)], sem).start(priority=k)
        return carry

    lax.fori_loop(0, tb // SUBLANES, issue, 0)

    for k in range(2):
        pltpu.make_async_copy(h_ref, xs_hbm.at[pl.ds(0, tb)], sem).wait()


def _dispatch(pad_lo, pad_hi, pos_blocks, h, n_rows, tb):
    T = h.shape[0]
    return pl.pallas_call(
        functools.partial(_dispatch_kernel, tb=tb),
        grid_spec=pltpu.PrefetchScalarGridSpec(
            num_scalar_prefetch=2,
            grid=(T // tb,),
            in_specs=[pl.BlockSpec((1, 1, 2 * tb), lambda i, lo, hi: (i, 0, 0), memory_space=pltpu.SMEM),
                      pl.BlockSpec((tb, D_MODEL), lambda i, lo, hi: (i, 0))],
            out_specs=pl.BlockSpec(memory_space=pl.ANY),
            scratch_shapes=[pltpu.VMEM((SUBLANES, D_MODEL), F32), pltpu.SemaphoreType.DMA(()),
                            pltpu.SemaphoreType.DMA(())]),
        out_shape=jax.ShapeDtypeStruct((n_rows, D_MODEL), F32),
        compiler_params=pltpu.CompilerParams(dimension_semantics=("arbitrary",), has_side_effects=True),
        name="moe_dispatch",
    )(pad_lo, pad_hi, pos_blocks, h)


def _expert_kernel(te_ref, nu_ref, xs_ref, g_ref, w1_ref, w3_ref, w2_ref, y_ref, acc_sc):
    del te_ref
    i = pl.program_id(0)

    @pl.when(i < nu_ref[0])
    def _():
        half = xs_ref.shape[0] // 2
        for r in (slice(0, half), slice(half, 2 * half)):
            x = xs_ref[r, :]
            ms = jnp.mean(x * x, axis=-1, keepdims=True)
            hn = (x * lax.rsqrt(ms + EPS) * g_ref[...]).astype(BF16)
            _swiglu_into(acc_sc.at[r, :], hn, w1_ref.at[0], w3_ref.at[0], w2_ref.at[0])
            y_ref[r, :] = acc_sc[r, :]

    @pl.when(i >= nu_ref[0])
    def _():
        y_ref[...] = jnp.zeros(y_ref.shape, F32)


def _experts(tile_expert, n_used, xs, g, w1, w3, w2, te_rows):
    P = xs.shape[0]
    row = pl.BlockSpec((te_rows, D_MODEL), lambda i, te, nu: (jnp.minimum(i, nu[0] - 1), 0))
    wspec = lambda w: pl.BlockSpec((1,) + w.shape[1:], lambda i, te, nu: (te[i], 0, 0))
    return pl.pallas_call(
        _expert_kernel,
        grid_spec=pltpu.PrefetchScalarGridSpec(
            num_scalar_prefetch=2,
            grid=(P // te_rows,),
            in_specs=[row, pl.BlockSpec((1, D_MODEL), lambda i, te, nu: (0, 0)), wspec(w1), wspec(w3), wspec(w2)],
            out_specs=pl.BlockSpec((te_rows, D_MODEL), lambda i, te, nu: (i, 0)),
            scratch_shapes=[pltpu.VMEM((te_rows, D_MODEL), F32)]),
        out_shape=jax.ShapeDtypeStruct((P, D_MODEL), F32),
        compiler_params=_cparams(("arbitrary",)),
        name="moe_experts",
    )(tile_expert, n_used, xs, g, w1, w3, w2)


def _combine_kernel(pos_ref, info_ref, h_ref, y_hbm, o_ref, g_sc, sem, *, tb):
    def issue(i, carry):
        t0 = pl.multiple_of(i * SUBLANES, SUBLANES)
        for u in range(SUBLANES):
            for k in range(2):
                p = pos_ref[0, 0, k * tb + t0 + u]
                pltpu.make_async_copy(y_hbm.at[pl.ds(p, 1)], g_sc.at[k, pl.ds(t0 + u, 1)], sem).start(priority=k)
        return carry

    lax.fori_loop(0, tb // SUBLANES, issue, 0)

    for k in range(2):
        pltpu.make_async_copy(y_hbm.at[pl.ds(0, tb)], g_sc.at[k], sem).wait()
    info = info_ref[...]
    o_ref[...] = h_ref[...] + info[:, R_W1:R_W1 + 1] * g_sc[0] + info[:, R_W2:R_W2 + 1] * g_sc[1]


def _combine(pos_blocks, info, h, y, tb):
    T = h.shape[0]
    row = lambda width: pl.BlockSpec((tb, width), lambda i: (i, 0))
    return pl.pallas_call(
        functools.partial(_combine_kernel, tb=tb),
        grid=(T // tb,),
        in_specs=[pl.BlockSpec((1, 1, 2 * tb), lambda i: (i, 0, 0), memory_space=pltpu.SMEM),
                  row(LANES), row(D_MODEL), pl.BlockSpec(memory_space=pl.ANY)],
        out_specs=row(D_MODEL),
        out_shape=jax.ShapeDtypeStruct((T, D_MODEL), F32),
        scratch_shapes=[pltpu.VMEM((2, tb, D_MODEL), F32), pltpu.SemaphoreType.DMA(())],
        compiler_params=_cparams(("arbitrary",)),
        name="moe_combine",
    )(pos_blocks, info, h, y)


def _pos_blocks(pos1, pos2, tb):
    nb = pos1.shape[0] // tb
    return jnp.concatenate([pos1.reshape(nb, 1, tb), pos2.reshape(nb, 1, tb)], axis=2)


def _moe(h, info, cnt, g, w1, w3, w2):
    T = h.shape[0]
    te_rows = min(EXPERT_TILE, T)
    counts = cnt[0, :N_EXPERTS].astype(jnp.int32)
    padded = ((counts + te_rows - 1) // te_rows) * te_rows
    ends = jnp.cumsum(padded)
    base = ends - padded
    i1 = info[:, R_I1].astype(jnp.int32)
    i2 = info[:, R_I2].astype(jnp.int32)
    pos1 = jnp.take(base, i1) + info[:, R_RANK1].astype(jnp.int32)
    pos2 = jnp.take(base, i2) + info[:, R_RANK2].astype(jnp.int32)
    n_tiles = (2 * T) // te_rows + N_EXPERTS
    tile_expert = jnp.minimum(jnp.searchsorted(ends // te_rows, jnp.arange(n_tiles), side="right"),
                              N_EXPERTS - 1).astype(jnp.int32)
    n_used = (ends[-1:] // te_rows).astype(jnp.int32)

    tb_d = min(DISPATCH_BLOCK, T)
    xs = _dispatch(base + counts, ends, _pos_blocks(pos1, pos2, tb_d), h, n_tiles * te_rows, tb_d)
    y = _experts(tile_expert, n_used, xs, g, w1, w3, w2, te_rows)
    tb_c = min(COMBINE_BLOCK, T)
    return _combine(_pos_blocks(pos1, pos2, tb_c), info, h, y, tb_c)


def kernel(x, positions, attn_norm, w_in, da_q_norm, da_k_norm, lam_q1, lam_k1, lam_q2, lam_k2, da_subln,
           w_proj_da, w_proj_ret, w_out, ffn_norm, dense_w1, dense_w3, dense_w2, w_router, moe_w1, moe_w3,
           moe_w2):
    B, S, D = x.shape
    T = B * S
    depth = attn_norm.shape[0]
    tm = tq = min(ROW_BLOCK, S)
    big = min(2 * ROW_BLOCK, S)
    assert D == D_MODEL and S % big == 0 and tm % (2 * MXU_DIM) == 0 and tm % RET_CHUNK == 0, (x.shape, tm)

    def rope_tables(pos):
        pos_b = jnp.broadcast_to(pos.reshape(-1, 1).astype(F32), (pos.size, LANES))
        return (_rope_tables(pos_b, *_rope_rows(ROPE_ROT_DIM, ROPE_THETA), tm)
                + _rope_tables(pos_b, *_rope_rows(RET_QK_DIM, RET_ROPE_THETA), tm))

    tabs = lax.cond(jnp.all(positions == positions[:1]),
                    lambda: tuple(jnp.tile(t, (B, 1)) for t in rope_tables(positions[0])),
                    lambda: tuple(rope_tables(positions.reshape(T))))
    ret_consts = _retention_consts()
    gsum = jnp.asarray(np.kron(np.eye(MXU_DIM // HEAD), np.ones((HEAD, HEAD))), BF16)

    h = x.reshape(T, D)
    for layer in range(depth):
        lambda_init = 0.8 - 0.6 * math.exp(-0.3 * layer)
        gq = jnp.tile(da_q_norm[layer], PROJ_CHUNK // DA_HEAD_DIM).reshape(1, PROJ_CHUNK)
        gk = jnp.tile(da_k_norm[layer], PROJ_CHUNK // DA_HEAD_DIM).reshape(1, PROJ_CHUNK)
        z, vt = _in_proj(h, attn_norm[layer].reshape(1, D), w_in[layer].astype(BF16), gq, gk, gsum, tabs, big, tq)
        lam_rows = jnp.stack([lam_q1[layer], lam_k1[layer], lam_q2[layer], lam_k2[layer]])
        ya = _diff_attn(z, vt, _score_bound(da_q_norm[layer], da_k_norm[layer]), lam_rows,
                        da_subln[layer].reshape(1, DA_V_DIM), B, S, tq, lambda_init)
        yr = _retention(z, ret_consts, B, S, big)
        j = layer // 2
        merge_args = (ya, yr, z, h, w_proj_da[layer].astype(BF16), w_proj_ret[layer].astype(BF16),
                      w_out[layer].astype(BF16), ffn_norm[layer].reshape(1, D))
        if layer % 2 == 0:
            h, hn = _merge(*merge_args, big)
            h = _ffn(hn, h, dense_w1[j].astype(BF16), dense_w3[j].astype(BF16), dense_w2[j].astype(BF16), tm)
        else:
            h, info, cnt = _merge(*merge_args, big, w_router=w_router[j])
            h = _moe(h, info, cnt, ffn_norm[layer].reshape(1, D), moe_w1[j].astype(BF16),
                     moe_w3[j].astype(BF16), moe_w2[j].astype(BF16))
    return h.reshape(B, S, D)
```

```python
import functools
import math

import jax
import jax.numpy as jnp
import numpy as np
from jax import lax
from jax.experimental import pallas as pl
from jax.experimental.pallas import tpu as pltpu

F32 = jnp.float32
BF16 = jnp.bfloat16

D_MODEL = 1024
DA_HEADS = 4
DA_HEAD_DIM = 64
DA_V_DIM = 128
ROPE_THETA = 500000.0
ROPE_ROT_DIM = 16
RET_HEADS = 4
RET_QK_DIM = 64
RET_V_DIM = 128
RET_CHUNK = 128
RET_ROPE_THETA = 10000.0
D_FF = 2816
N_EXPERTS = 8
EPS = 1e-6
LOG2_E = math.log2(math.e)
DA_VT_ROWS = DA_V_DIM + 16
FIXED_REF_MAX_BOUND = 60.0
GROUPS = (8, 4)

LANES = 128
SUBLANES = 8
MXU_DIM = 256
HEAD = 64
assert DA_HEAD_DIM == RET_QK_DIM == HEAD and LANES == 2 * HEAD
DA_W = DA_HEADS * DA_V_DIM
RET_W = RET_HEADS * RET_V_DIM
RET_QK_W = RET_HEADS * RET_QK_DIM
ROW_BLOCK = 512
EXPERT_TILE = 512
DISPATCH_BLOCK = 2048
COMBINE_BLOCK = 1024
OFF_QA, OFF_KA, OFF_VA = 0, 512, 1024
OFF_QR, OFF_KR, OFF_VR, OFF_GR = 1536, 1792, 2048, 2560
OFF_GA, OFF_GB = 3072, 4096
IN_W = 5120
PROJ_CHUNK = 512
FF_CHUNK = 256
VMEM_LIMIT = 56 * 1024 * 1024


def _cparams(sem):
    return pltpu.CompilerParams(dimension_semantics=sem, vmem_limit_bytes=VMEM_LIMIT)


def _const_spec(shape):
    nd = len(shape)
    return pl.BlockSpec(shape, lambda *_: (0,) * nd)


def _rope_table_kernel(pos_ref, inv_ref, sgn_ref, cos_ref, sin_ref):
    ang = pos_ref[...] * inv_ref[...]
    cos_ref[...] = jnp.cos(ang)
    sin_ref[...] = jnp.sin(ang) * sgn_ref[...]


def _rope_tables(pos_b, inv_row, sgn_row, tm):
    T = pos_b.shape[0]
    row = pl.BlockSpec((tm, LANES), lambda i: (i, 0))
    return pl.pallas_call(
        _rope_table_kernel,
        grid=(T // tm,),
        in_specs=[row, _const_spec((1, LANES)), _const_spec((1, LANES))],
        out_specs=[row, row],
        out_shape=[jax.ShapeDtypeStruct((T, LANES), F32)] * 2,
        compiler_params=_cparams(("parallel",)),
        name="rope_tables",
    )(pos_b, inv_row, sgn_row)


def _rope_rows(rot_dim, theta):
    half = rot_dim // 2
    inv = 1.0 / (theta ** (jnp.arange(half, dtype=F32) / half))
    jj = np.arange(LANES) % HEAD
    inv_row = jnp.where(jj < rot_dim, inv[jj % half], 0.0).astype(F32)
    sgn_row = np.where(jj < half, -1.0, np.where(jj < rot_dim, 1.0, 0.0)).astype(np.float32)
    return inv_row.reshape(1, LANES), jnp.asarray(sgn_row).reshape(1, LANES)


def _rotate_half(x, half):
    jj = lax.broadcasted_iota(jnp.int32, x.shape, 1) % HEAD
    return jnp.where(jj < half, pltpu.roll(x, LANES - half, 1), pltpu.roll(x, half, 1))


def _sigmoid(x):
    return 1.0 / (1.0 + jnp.exp(-x))


def _in_proj_kernel(h_ref, g_ref, w_ref, gq_ref, gk_ref, gsum_ref, ca_ref, sa_ref, cr_ref, sr_ref, z_ref, vt_ref):
    part = vt_ref.shape[-1]
    for p in range(h_ref.shape[0] // part):
        _in_proj_rows(p, slice(p * part, (p + 1) * part), h_ref, g_ref, w_ref, gq_ref, gk_ref, gsum_ref, ca_ref,
                      sa_ref, cr_ref, sr_ref, z_ref, vt_ref)


def _in_proj_rows(p, r, h_ref, g_ref, w_ref, gq_ref, gk_ref, gsum_ref, ca_ref, sa_ref, cr_ref, sr_ref, z_ref,
                  vt_ref):
    x = h_ref[r, :]
    ms = jnp.mean(x * x, axis=-1, keepdims=True)
    hn = (x * lax.rsqrt(ms + EPS) * g_ref[...]).astype(BF16)
    ca, sa, cr, sr = ca_ref[r, :], sa_ref[r, :], cr_ref[r, :], sr_ref[r, :]

    def head_rms(zc, gain):
        sq = (zc * zc).astype(BF16)
        outs = []
        for s in range(PROJ_CHUNK // MXU_DIM):
            sl = slice(s * MXU_DIM, (s + 1) * MXU_DIM)
            tot = jnp.dot(sq[:, sl], gsum_ref[...], preferred_element_type=F32)
            outs.append(zc[:, sl] * lax.rsqrt(tot * (1.0 / HEAD) + EPS))
        return jnp.concatenate(outs, axis=1) * gain

    def rope(y, cos, sin, half):
        outs = []
        for s in range(y.shape[1] // LANES):
            ys = y[:, s * LANES:(s + 1) * LANES]
            outs.append(ys * cos + _rotate_half(ys, half) * sin)
        return jnp.concatenate(outs, axis=1)

    for c in range(IN_W // PROJ_CHUNK):
        c0 = c * PROJ_CHUNK
        zc = jnp.dot(hn, w_ref[:, c0:c0 + PROJ_CHUNK], preferred_element_type=F32)
        if c0 == OFF_QA:
            zc = rope(head_rms(zc, gq_ref[...]), ca, sa, ROPE_ROT_DIM // 2)
            zc = zc * (DA_HEAD_DIM ** -0.5 * LOG2_E)
        elif c0 == OFF_KA:
            zc = rope(head_rms(zc, gk_ref[...]), ca, sa, ROPE_ROT_DIM // 2)
        elif c0 == OFF_VA:
            zt = zc.T.astype(BF16)
            extra = lax.broadcasted_iota(jnp.int32, (DA_VT_ROWS - DA_V_DIM, zt.shape[1]), 0)
            for hd in range(DA_HEADS):
                vt_ref[p, hd, 0:DA_V_DIM, :] = zt[hd * DA_V_DIM:(hd + 1) * DA_V_DIM, :]
                vt_ref[p, hd, DA_V_DIM:DA_VT_ROWS, :] = jnp.where(extra == 0, 1.0, 0.0).astype(BF16)
        elif c0 == OFF_QR:
            zc = rope(zc, cr, sr, RET_QK_DIM // 2)
            col = lax.broadcasted_iota(jnp.int32, zc.shape, 1)
            zc = jnp.where(col >= OFF_KR - OFF_QR, zc * (RET_QK_DIM ** -0.5), zc)
        elif c0 == OFF_GR:
            zc = zc * _sigmoid(zc)
        elif c0 >= OFF_GA:
            zc = _sigmoid(zc)
        z_ref[r, c0:c0 + PROJ_CHUNK] = zc.astype(BF16)


def _in_proj(h, g, w, gq, gk, gsum, tabs, tm, tk):
    T = h.shape[0]
    row = lambda width: pl.BlockSpec((tm, width), lambda i: (i, 0))
    return pl.pallas_call(
        _in_proj_kernel,
        grid=(T // tm,),
        in_specs=[row(D_MODEL), _const_spec((1, D_MODEL)),
                  pl.BlockSpec((D_MODEL, IN_W), lambda i: (0, 0), pipeline_mode=pl.Buffered(1)),
                  _const_spec((1, PROJ_CHUNK)), _const_spec((1, PROJ_CHUNK)), _const_spec((MXU_DIM, MXU_DIM)),
                  row(LANES), row(LANES), row(LANES), row(LANES)],
        out_specs=[row(IN_W), pl.BlockSpec((tm // tk, DA_HEADS, DA_VT_ROWS, tk), lambda i: (i, 0, 0, 0))],
        out_shape=[jax.ShapeDtypeStruct((T, IN_W), BF16),
                   jax.ShapeDtypeStruct((T // tk, DA_HEADS, DA_VT_ROWS, tk), BF16)],
        compiler_params=_cparams(("parallel",)),
        name="in_proj",
    )(h, g, w, gq, gk, gsum, *tabs)


def _diff_attn_kernel(bound_ref, q_ref, k_ref, vt_ref, lam_ref, subln_ref, o_ref, acc_sc, st_sc, *, tq,
                      lambda_init):
    qi = pl.program_id(2)
    q = q_ref[...]
    lane = lax.broadcasted_iota(jnp.int32, q.shape, 1)
    qs = (jnp.where(lane < DA_HEAD_DIM, q, jnp.zeros_like(q)),
          jnp.where(lane >= DA_HEAD_DIM, q, jnp.zeros_like(q)))
    acc_sc[...] = jnp.zeros(acc_sc.shape, F32)

    tqh = tq // 2
    streams = [(c, hf) for hf in range(2) for c in range(2)]
    qsub = [qs[c][hf * tqh:(hf + 1) * tqh, :] for c, hf in streams]

    def scores(ki, diagonal=False):
        k = k_ref[pl.ds(pl.multiple_of(ki * tq, tq), tq), :]
        return [lax.dot_general(k[0:tqh] if diagonal and hf == 0 else k, qq, (((1,), (1,)), ((), ())),
                                preferred_element_type=F32) for qq, (c, hf) in zip(qsub, streams)]

    def stash(sts):
        for s in range(len(streams)):
            st_sc[s] = sts[s]

    bound = bound_ref[0, 0]

    def masked_scores(st, hf):
        key = lax.broadcasted_iota(jnp.int32, st.shape, 0)
        qry = lax.broadcasted_iota(jnp.int32, st.shape, 1) + hf * tqh
        return jnp.where(key <= qry, st, -jnp.inf)

    def consume_online(sts, ki, stats, masked):
        vt = vt_ref[ki, 0]
        out = []
        for s, (c, hf) in enumerate(streams):
            st = st_sc[s] if sts is None else sts[s]
            if masked:
                st = masked_scores(st, hf)
            cols = slice(hf * tqh, (hf + 1) * tqh)
            m_old = stats[s]
            m_new = jnp.maximum(m_old, jnp.max(st, axis=0, keepdims=True))
            alpha = jnp.exp2(m_old - m_new)
            pt = jnp.exp2((st - m_new).astype(BF16))
            acc_sc[c, :, cols] = alpha * acc_sc[c, :, cols] + jnp.dot(vt, pt, preferred_element_type=F32)
            out.append(m_new)
        return tuple(out)

    def run_fixed_ref():
        def accumulate(blocks, last_masked):
            diag = [last_masked and u == len(blocks) - 1 for u in range(len(blocks))]
            sts = [scores(b, dg) for b, dg in zip(blocks, diag)]
            tot = [None] * len(streams)
            den = [None] * len(streams)
            for b, st4, dg in zip(blocks, sts, diag):
                vt = vt_ref[b, 0, 0:DA_V_DIM, :]
                for s, (c, hf) in enumerate(streams):
                    st = masked_scores(st4[s], hf) if dg else st4[s]
                    keys = st.shape[0]
                    pt = jnp.exp2(st - bound)
                    part = jnp.sum(pt.reshape(keys // SUBLANES, SUBLANES, tqh), axis=0)
                    d = jnp.dot(vt[:, 0:keys], pt.astype(BF16), preferred_element_type=F32)
                    tot[s] = d if tot[s] is None else tot[s] + d
                    den[s] = part if den[s] is None else den[s] + part
            for s, (c, hf) in enumerate(streams):
                cols = slice(hf * tqh, (hf + 1) * tqh)
                acc_sc[c, 0:DA_V_DIM, cols] += tot[s]
                acc_sc[c, DA_V_DIM:DA_V_DIM + 1, cols] += jnp.sum(den[s], axis=0, keepdims=True)

        done = 0
        for g in GROUPS:
            def trip(j, carry, g=g, done=done):
                accumulate([done + g * j + u for u in range(g)], False)
                return carry

            ntrips = (qi - done) // g
            lax.fori_loop(0, ntrips, trip, 0)
            done = done + g * ntrips
        for rest in range(GROUPS[-1]):
            pl.when(qi - done == rest)(
                functools.partial(accumulate, [qi - rest + u for u in range(rest + 1)], True))

    def run_online():
        npairs = qi // 2
        stash(scores(0))

        def pair(j, stats):
            b0 = 2 * j
            odd = scores(b0 + 1)
            stats = consume_online(None, b0, stats, False)
            stash(scores(b0 + 2))
            return consume_online(odd, b0 + 1, stats, False)

        def leftover(stats):
            stats = consume_online(None, qi - 1, stats, False)
            stash(scores(qi))
            return stats

        init = (jnp.full((1, tqh), -jnp.inf, F32),) * len(streams)
        stats = lax.fori_loop(0, npairs, pair, init)
        stats = lax.cond(qi - 2 * npairs == 1, leftover, lambda st: st, stats)
        consume_online(None, qi, stats, True)

    pl.when(bound <= FIXED_REF_MAX_BOUND)(run_fixed_ref)
    pl.when(bound > FIXED_REF_MAX_BOUND)(run_online)

    lv = lam_ref[...]
    lam = (jnp.exp(jnp.sum(lv[0:1] * lv[1:2], axis=1, keepdims=True))
           - jnp.exp(jnp.sum(lv[2:3] * lv[3:4], axis=1, keepdims=True)) + lambda_init)
    dv = DA_V_DIM
    ot = (acc_sc[0, 0:dv, :] / acc_sc[0, dv:dv + 1, :]
          - lam * (acc_sc[1, 0:dv, :] / acc_sc[1, dv:dv + 1, :]))
    yt = ot * lax.rsqrt(jnp.mean(ot * ot, axis=0, keepdims=True) + EPS)
    o_ref[...] = (yt.T * subln_ref[...] * (1.0 - lambda_init)).astype(BF16)


def _score_bound(gq, gk):
    norm = math.sqrt(DA_HEAD_DIM)
    return (1.01 * norm * norm * DA_HEAD_DIM ** -0.5 * LOG2_E
            * jnp.max(jnp.abs(gq)) * jnp.max(jnp.abs(gk))).astype(F32).reshape(1, 1)


def _diff_attn(z, vt, bound, lam_rows, subln, B, S, tq, lambda_init):
    T = B * S
    nq = S // tq
    kern = functools.partial(_diff_attn_kernel, tq=tq, lambda_init=lambda_init)
    return pl.pallas_call(
        kern,
        grid=(B, DA_HEADS, nq),
        in_specs=[pl.BlockSpec((1, 1), lambda b, h, i: (0, 0), memory_space=pltpu.SMEM),
                  pl.BlockSpec((tq, LANES), lambda b, h, i: (b * nq + i, OFF_QA // LANES + h)),
                  pl.BlockSpec((S, LANES), lambda b, h, i: (b, OFF_KA // LANES + h)),
                  pl.BlockSpec((nq, 1, DA_VT_ROWS, tq), lambda b, h, i: (b, h, 0, 0)),
                  _const_spec((4, DA_HEAD_DIM)), _const_spec((1, DA_V_DIM))],
        out_specs=pl.BlockSpec((tq, LANES), lambda b, h, i: (b * nq + i, h)),
        out_shape=jax.ShapeDtypeStruct((T, DA_HEADS * DA_V_DIM), BF16),
        scratch_shapes=[pltpu.VMEM((2, DA_VT_ROWS, tq), F32), pltpu.VMEM((4, tq, tq // 2), F32)],
        compiler_params=_cparams(("parallel", "parallel", "arbitrary")),
        name="diff_attn",
    )(bound, z, z, vt, lam_rows, subln)


def _retention_kernel(q_ref, k_ref, v_ref, g_ref, dec_ref, qdec_ref, kdec_ref, cdec_ref, o_ref, r_sc, o_sc, *,
                      tc):
    @pl.when(pl.program_id(1) == 0)
    def _():
        r_sc[...] = jnp.zeros(r_sc.shape, F32)

    C = RET_CHUNK
    lane = lax.broadcasted_iota(jnp.int32, (C, LANES), 1)
    rowi = lax.broadcasted_iota(jnp.int32, (C, LANES), 0)
    units = [(n, j) for n in range(tc // C) for j in range(RET_HEADS // 2)]
    rows = lambda n: slice(n * C, (n + 1) * C)
    cols = lambda j: slice(j * LANES, (j + 1) * LANES)
    vcol = lambda h: slice(h * RET_V_DIM, (h + 1) * RET_V_DIM)
    own = lambda hh: (lane >= RET_QK_DIM) if hh else (lane < RET_QK_DIM)
    s, kv = {}, {}
    for n, j in units:
        qp = q_ref[rows(n), cols(j)]
        kp = k_ref[rows(n), cols(j)]
        kdt = (kp.astype(F32) * kdec_ref[:, cols(j)]).T.astype(BF16)
        both = []
        for hh in range(2):
            h = 2 * j + hh
            s[n, h] = lax.dot_general(jnp.where(own(hh), qp, jnp.zeros_like(qp)), kp, (((1,), (1,)), ((), ())),
                                      preferred_element_type=F32)
            both.append(jnp.dot(kdt, v_ref[rows(n), vcol(h)], preferred_element_type=F32))
        kv[n, j] = jnp.where(rowi < RET_QK_DIM, both[0], both[1])
    for n, j in units:
        for hh in range(2):
            h = 2 * j + hh
            inner = (s[n, h] * dec_ref[h]).astype(BF16)
            o_sc[rows(n), vcol(h)] = jnp.dot(inner, v_ref[rows(n), vcol(h)], preferred_element_type=F32)
    state = [r_sc[j] for j in range(RET_HEADS // 2)]
    for n, j in units:
        qp = q_ref[rows(n), cols(j)]
        qd = (qp.astype(F32) * qdec_ref[:, cols(j)]).astype(BF16)
        r_b = state[j].astype(BF16)
        for hh in range(2):
            h = 2 * j + hh
            o_sc[rows(n), vcol(h)] += jnp.dot(jnp.where(own(hh), qd, jnp.zeros_like(qd)), r_b,
                                              preferred_element_type=F32)
        state[j] = state[j] * cdec_ref[j] + kv[n, j]
    for j in range(RET_HEADS // 2):
        r_sc[j] = state[j]
    for h in range(RET_HEADS):
        gs = slice(h * RET_V_DIM, (h + 1) * RET_V_DIM)
        o = o_sc[:, gs]
        oc = o - jnp.mean(o, axis=-1, keepdims=True)
        y = oc * lax.rsqrt(jnp.mean(oc * oc, axis=-1, keepdims=True) + EPS)
        o_ref[:, gs] = (y * g_ref[:, gs].astype(F32)).astype(BF16)


def _retention(z, consts, B, S, tc):
    T = B * S
    n = S // tc
    dec, qdec, kdec, cdec = consts
    kern = functools.partial(_retention_kernel, tc=tc)
    blk = lambda width, off: pl.BlockSpec((tc, width), lambda b, i: (b * n + i, off // width))
    return pl.pallas_call(
        kern,
        grid=(B, n),
        in_specs=[blk(RET_QK_W, OFF_QR), blk(RET_QK_W, OFF_KR), blk(RET_W, OFF_VR), blk(RET_W, OFF_GR),
                  _const_spec(dec.shape), _const_spec(qdec.shape), _const_spec(kdec.shape),
                  _const_spec(cdec.shape)],
        out_specs=pl.BlockSpec((tc, RET_W), lambda b, i: (b * n + i, 0)),
        out_shape=jax.ShapeDtypeStruct((T, RET_HEADS * RET_V_DIM), BF16),
        scratch_shapes=[pltpu.VMEM((RET_HEADS // 2, LANES, LANES), F32),
                        pltpu.VMEM((tc, RET_HEADS * RET_V_DIM), F32)],
        compiler_params=_cparams(("parallel", "arbitrary")),
        name="retention",
    )(z, z, z, z, dec, qdec, kdec, cdec)


def _retention_consts():
    C = RET_CHUNK
    log_gamma = jnp.log1p(-jnp.exp2(-5.0 - jnp.arange(RET_HEADS, dtype=F32)))
    idx = jnp.arange(C, dtype=F32)
    rel = idx[:, None] - idx[None, :]
    dec = jnp.where(rel[None] >= 0, jnp.exp(log_gamma[:, None, None] * jnp.maximum(rel, 0.0)[None]), 0.0)
    q_dec = jnp.exp(log_gamma[None, :] * (idx + 1.0)[:, None])
    k_dec = jnp.exp(log_gamma[None, :] * (C - 1.0 - idx)[:, None])
    qdec = jnp.repeat(q_dec, RET_QK_DIM, axis=1)
    kdec = jnp.repeat(k_dec, RET_QK_DIM, axis=1)
    chunk_decay = jnp.repeat(jnp.exp(log_gamma * C), RET_QK_DIM)
    cdec = jnp.broadcast_to(chunk_decay.reshape(RET_HEADS // 2, LANES, 1), (RET_HEADS // 2, LANES, LANES))
    return dec.astype(F32), qdec.astype(F32), kdec.astype(F32), cdec.astype(F32)


R_I1, R_I2, R_W1, R_W2, R_RANK1, R_RANK2 = range(6)


def _route(hn, wr_ref, rows, info_ref, carry_sc):
    w = wr_ref[...]
    hn_hi, w_hi = hn.astype(BF16), w.astype(BF16)
    hn_lo, w_lo = (hn - hn_hi.astype(F32)).astype(BF16), (w - w_hi.astype(F32)).astype(BF16)
    logits = (jnp.dot(hn_hi, w_hi, preferred_element_type=F32) + jnp.dot(hn_hi, w_lo, preferred_element_type=F32)
              + jnp.dot(hn_lo, w_hi, preferred_element_type=F32))
    lane = lax.broadcasted_iota(jnp.int32, logits.shape, 1)
    logits = jnp.where(lane < N_EXPERTS, logits, -jnp.inf)
    m1 = jnp.max(logits, axis=1, keepdims=True)
    i1 = jnp.min(jnp.where(logits == m1, lane, LANES), axis=1, keepdims=True)
    rest = jnp.where(lane == i1, -jnp.inf, logits)
    m2 = jnp.max(rest, axis=1, keepdims=True)
    i2 = jnp.min(jnp.where(rest == m2, lane, LANES), axis=1, keepdims=True)
    e = jnp.exp(m2 - m1)
    w1 = 1.0 / (1.0 + e)
    w2 = e / (1.0 + e)
    sel = jnp.where(lane == i1, 1.0, jnp.where(lane == i2, 1.0, 0.0))
    n = sel.shape[0]
    rr = lax.broadcasted_iota(jnp.int32, (n, n), 0)
    cc = lax.broadcasted_iota(jnp.int32, (n, n), 1)
    ltri = jnp.where(cc < rr, 1.0, 0.0).astype(BF16)
    before = jnp.dot(ltri, sel.astype(BF16), preferred_element_type=F32) + carry_sc[0:1, :]
    r1 = jnp.sum(jnp.where(lane == i1, before, 0.0), axis=1, keepdims=True)
    r2 = jnp.sum(jnp.where(lane == i2, before, 0.0), axis=1, keepdims=True)
    info = jnp.where(lane == R_I1, i1.astype(F32), 0.0)
    for ln, val in ((R_I2, i2.astype(F32)), (R_W1, w1), (R_W2, w2), (R_RANK1, r1), (R_RANK2, r2)):
        info = jnp.where(lane == ln, val, info)
    info_ref[rows, :] = info
    carry_sc[...] = carry_sc[...] + jnp.sum(sel, axis=0, keepdims=True)


def _merge_kernel(*refs, route):
    if route:
        (ya_ref, yr_ref, ga_ref, gb_ref, h_ref, wa_ref, wr_ref, wo_ref, gf_ref, wrt_ref,
         h_out, info_out, cnt_out, carry_sc) = refs

        @pl.when(pl.program_id(0) == 0)
        def _():
            carry_sc[...] = jnp.zeros(carry_sc.shape, F32)
    else:
        ya_ref, yr_ref, ga_ref, gb_ref, h_ref, wa_ref, wr_ref, wo_ref, gf_ref, h_out, hn_out = refs
    rows = ya_ref.shape[0]
    size = MXU_DIM if route else rows // 2
    parts = [slice(r0, r0 + size) for r0 in range(0, rows, size)]
    branch = [(jnp.dot(ya_ref[r, :], wa_ref[...], preferred_element_type=F32),
               jnp.dot(yr_ref[r, :], wr_ref[...], preferred_element_type=F32)) for r in parts]
    for r, (pa, pr) in zip(parts, branch):
        merged = ga_ref[r, :].astype(F32) * pa + gb_ref[r, :].astype(F32) * pr
        hnew = h_ref[r, :] + jnp.dot(merged.astype(BF16), wo_ref[...], preferred_element_type=F32)
        h_out[r, :] = hnew
        ms = jnp.mean(hnew * hnew, axis=-1, keepdims=True)
        hn = hnew * lax.rsqrt(ms + EPS) * gf_ref[...]
        if route:
            _route(hn, wrt_ref, r, info_out, carry_sc)
        else:
            hn_out[r, :] = hn.astype(BF16)
    if route:
        cnt_out[...] = carry_sc[...]


def _merge(ya, yr, z, h, wa, wr, wo, gf, tm, w_router=None):
    T = h.shape[0]
    route = w_router is not None
    row = lambda width, cb=0: pl.BlockSpec((tm, width), lambda i: (i, cb))
    in_specs = [row(DA_W), row(RET_W), row(D_MODEL, OFF_GA // D_MODEL), row(D_MODEL, OFF_GB // D_MODEL),
                row(D_MODEL), _const_spec(wa.shape), _const_spec(wr.shape), _const_spec(wo.shape),
                _const_spec((1, D_MODEL))]
    args = [ya, yr, z, z, h, wa, wr, wo, gf]
    if route:
        in_specs.append(_const_spec((D_MODEL, LANES)))
        args.append(jnp.pad(w_router, ((0, 0), (0, LANES - N_EXPERTS))))
        out_specs = [row(D_MODEL), row(LANES), _const_spec((SUBLANES, LANES))]
        out_shape = [jax.ShapeDtypeStruct((T, D_MODEL), F32), jax.ShapeDtypeStruct((T, LANES), F32),
                     jax.ShapeDtypeStruct((SUBLANES, LANES), F32)]
        scratch = [pltpu.VMEM((SUBLANES, LANES), F32)]
    else:
        out_specs = [row(D_MODEL), row(D_MODEL)]
        out_shape = [jax.ShapeDtypeStruct((T, D_MODEL), F32), jax.ShapeDtypeStruct((T, D_MODEL), BF16)]
        scratch = []
    return pl.pallas_call(
        functools.partial(_merge_kernel, route=route),
        grid=(T // tm,),
        in_specs=in_specs,
        out_specs=out_specs,
        out_shape=out_shape,
        scratch_shapes=scratch,
        compiler_params=_cparams(("arbitrary",) if route else ("parallel",)),
        name="merge_route" if route else "merge_out",
    )(*args)


def _swiglu_into(acc_sc, x, w1_ref, w3_ref, w2_ref):
    chunks = [slice(c * FF_CHUNK, (c + 1) * FF_CHUNK) for c in range(D_FF // FF_CHUNK)]
    up = lambda cs: (jnp.dot(x, w1_ref[:, cs], preferred_element_type=F32),
                     jnp.dot(x, w3_ref[:, cs], preferred_element_type=F32))
    uv = up(chunks[0])
    for c, cs in enumerate(chunks):
        nxt = up(chunks[c + 1]) if c + 1 < len(chunks) else None
        a = (uv[0] * _sigmoid(uv[0]) * uv[1]).astype(BF16)
        d = jnp.dot(a, w2_ref[cs, :], preferred_element_type=F32)
        if c == 0:
            acc_sc[...] = d
        else:
            acc_sc[...] += d
        uv = nxt


def _ffn_kernel(hn_ref, prev_ref, w1_ref, w3_ref, w2_ref, o_ref, acc_sc):
    _swiglu_into(acc_sc, hn_ref[...], w1_ref, w3_ref, w2_ref)
    o_ref[...] = prev_ref[...] + acc_sc[...]


def _ffn(hn, prev, w1, w3, w2, tm):
    T = hn.shape[0]
    row = lambda width: pl.BlockSpec((tm, width), lambda i: (i, 0))
    wspec = lambda w: pl.BlockSpec(w.shape, lambda i: (0, 0), pipeline_mode=pl.Buffered(1))
    return pl.pallas_call(
        _ffn_kernel,
        grid=(T // tm,),
        in_specs=[row(D_MODEL), row(D_MODEL), wspec(w1), wspec(w3), wspec(w2)],
        out_specs=row(D_MODEL),
        out_shape=jax.ShapeDtypeStruct((T, D_MODEL), F32),
        scratch_shapes=[pltpu.VMEM((tm, D_MODEL), F32)],
        input_output_aliases={1: 0},
        compiler_params=_cparams(("parallel",)),
        name="ffn",
    )(hn, prev, w1, w3, w2)


def _dispatch_kernel(pad_lo_ref, pad_hi_ref, pos_ref, h_ref, xs_hbm, zero_sc, sem, pad_sem, *, tb):
    @pl.when(pl.program_id(0) == 0)
    def _():
        zero_sc[...] = jnp.zeros(zero_sc.shape, F32)
        zero_row = zero_sc.at[pl.ds(0, 1)]
        for e in range(N_EXPERTS):
            def fill(r, carry):
                pltpu.make_async_copy(zero_row, xs_hbm.at[pl.ds(r, 1)], pad_sem).start()
                return carry

            lax.fori_loop(pad_lo_ref[e], pad_hi_ref[e], fill, 0)
        for e in range(N_EXPERTS):
            def drain(r, carry):
                pltpu.make_async_copy(zero_row, xs_hbm.at[pl.ds(r, 1)], pad_sem).wait()
                return carry

            lax.fori_loop(pad_lo_ref[e], pad_hi_ref[e], drain, 0)

    def issue(i, carry):
        t0 = pl.multiple_of(i * SUBLANES, SUBLANES)
        for u in range(SUBLANES):
            for k in range(2):
                p = pos_ref[0, 0, k * tb + t0 + u]
                pltpu.make_async_copy(h_ref.at[pl.ds(t0 + u, 1)], xs_hbm.at[pl.ds(p, 1)], sem).start(priority=k)
        return carry

    lax.fori_loop(0, tb // SUBLANES, issue, 0)

    for k in range(2):
        pltpu.make_async_copy(h_ref, xs_hbm.at[pl.ds(0, tb)], sem).wait()


def _dispatch(pad_lo, pad_hi, pos_blocks, h, n_rows, tb):
    T = h.shape[0]
    return pl.pallas_call(
        functools.partial(_dispatch_kernel, tb=tb),
        grid_spec=pltpu.PrefetchScalarGridSpec(
            num_scalar_prefetch=2,
            grid=(T // tb,),
            in_specs=[pl.BlockSpec((1, 1, 2 * tb), lambda i, lo, hi: (i, 0, 0), memory_space=pltpu.SMEM),
                      pl.BlockSpec((tb, D_MODEL), lambda i, lo, hi: (i, 0))],
            out_specs=pl.BlockSpec(memory_space=pl.ANY),
            scratch_shapes=[pltpu.VMEM((SUBLANES, D_MODEL), F32), pltpu.SemaphoreType.DMA(()),
                            pltpu.SemaphoreType.DMA(())]),
        out_shape=jax.ShapeDtypeStruct((n_rows, D_MODEL), F32),
        compiler_params=pltpu.CompilerParams(dimension_semantics=("arbitrary",), has_side_effects=True),
        name="moe_dispatch",
    )(pad_lo, pad_hi, pos_blocks, h)


def _expert_kernel(te_ref, nu_ref, xs_ref, g_ref, w1_ref, w3_ref, w2_ref, y_ref, acc_sc):
    del te_ref
    i = pl.program_id(0)

    @pl.when(i < nu_ref[0])
    def _():
        half = xs_ref.shape[0] // 2
        for r in (slice(0, half), slice(half, 2 * half)):
            x = xs_ref[r, :]
            ms = jnp.mean(x * x, axis=-1, keepdims=True)
            hn = (x * lax.rsqrt(ms + EPS) * g_ref[...]).astype(BF16)
            _swiglu_into(acc_sc.at[r, :], hn, w1_ref.at[0], w3_ref.at[0], w2_ref.at[0])
            y_ref[r, :] = acc_sc[r, :]

    @pl.when(i >= nu_ref[0])
    def _():
        y_ref[...] = jnp.zeros(y_ref.shape, F32)


def _experts(tile_expert, n_used, xs, g, w1, w3, w2, te_rows):
    P = xs.shape[0]
    row = pl.BlockSpec((te_rows, D_MODEL), lambda i, te, nu: (jnp.minimum(i, nu[0] - 1), 0))
    wspec = lambda w: pl.BlockSpec((1,) + w.shape[1:], lambda i, te, nu: (te[i], 0, 0))
    return pl.pallas_call(
        _expert_kernel,
        grid_spec=pltpu.PrefetchScalarGridSpec(
            num_scalar_prefetch=2,
            grid=(P // te_rows,),
            in_specs=[row, pl.BlockSpec((1, D_MODEL), lambda i, te, nu: (0, 0)), wspec(w1), wspec(w3), wspec(w2)],
            out_specs=pl.BlockSpec((te_rows, D_MODEL), lambda i, te, nu: (i, 0)),
            scratch_shapes=[pltpu.VMEM((te_rows, D_MODEL), F32)]),
        out_shape=jax.ShapeDtypeStruct((P, D_MODEL), F32),
        compiler_params=_cparams(("arbitrary",)),
        name="moe_experts",
    )(tile_expert, n_used, xs, g, w1, w3, w2)


def _combine_kernel(pos_ref, info_ref, h_ref, y_hbm, o_ref, g_sc, sem, *, tb):
    def issue(i, carry):
        t0 = pl.multiple_of(i * SUBLANES, SUBLANES)
        for u in range(SUBLANES):
            for k in range(2):
                p = pos_ref[0, 0, k * tb + t0 + u]
                pltpu.make_async_copy(y_hbm.at[pl.ds(p, 1)], g_sc.at[k, pl.ds(t0 + u, 1)], sem).start(priority=k)
        return carry

    lax.fori_loop(0, tb // SUBLANES, issue, 0)

    for k in range(2):
        pltpu.make_async_copy(y_hbm.at[pl.ds(0, tb)], g_sc.at[k], sem).wait()
    info = info_ref[...]
    o_ref[...] = h_ref[...] + info[:, R_W1:R_W1 + 1] * g_sc[0] + info[:, R_W2:R_W2 + 1] * g_sc[1]


def _combine(pos_blocks, info, h, y, tb):
    T = h.shape[0]
    row = lambda width: pl.BlockSpec((tb, width), lambda i: (i, 0))
    return pl.pallas_call(
        functools.partial(_combine_kernel, tb=tb),
        grid=(T // tb,),
        in_specs=[pl.BlockSpec((1, 1, 2 * tb), lambda i: (i, 0, 0), memory_space=pltpu.SMEM),
                  row(LANES), row(D_MODEL), pl.BlockSpec(memory_space=pl.ANY)],
        out_specs=row(D_MODEL),
        out_shape=jax.ShapeDtypeStruct((T, D_MODEL), F32),
        scratch_shapes=[pltpu.VMEM((2, tb, D_MODEL), F32), pltpu.SemaphoreType.DMA(())],
        compiler_params=_cparams(("arbitrary",)),
        name="moe_combine",
    )(pos_blocks, info, h, y)


def _pos_blocks(pos1, pos2, tb):
    nb = pos1.shape[0] // tb
    return jnp.concatenate([pos1.reshape(nb, 1, tb), pos2.reshape(nb, 1, tb)], axis=2)


def _moe(h, info, cnt, g, w1, w3, w2):
    T = h.shape[0]
    te_rows = min(EXPERT_TILE, T)
    counts = cnt[0, :N_EXPERTS].astype(jnp.int32)
    padded = ((counts + te_rows - 1) // te_rows) * te_rows
    ends = jnp.cumsum(padded)
    base = ends - padded
    i1 = info[:, R_I1].astype(jnp.int32)
    i2 = info[:, R_I2].astype(jnp.int32)
    pos1 = jnp.take(base, i1) + info[:, R_RANK1].astype(jnp.int32)
    pos2 = jnp.take(base, i2) + info[:, R_RANK2].astype(jnp.int32)
    n_tiles = (2 * T) // te_rows + N_EXPERTS
    tile_expert = jnp.minimum(jnp.searchsorted(ends // te_rows, jnp.arange(n_tiles), side="right"),
                              N_EXPERTS - 1).astype(jnp.int32)
    n_used = (ends[-1:] // te_rows).astype(jnp.int32)

    tb_d = min(DISPATCH_BLOCK, T)
    xs = _dispatch(base + counts, ends, _pos_blocks(pos1, pos2, tb_d), h, n_tiles * te_rows, tb_d)
    y = _experts(tile_expert, n_used, xs, g, w1, w3, w2, te_rows)
    tb_c = min(COMBINE_BLOCK, T)
    return _combine(_pos_blocks(pos1, pos2, tb_c), info, h, y, tb_c)


def kernel(x, positions, attn_norm, w_in, da_q_norm, da_k_norm, lam_q1, lam_k1, lam_q2, lam_k2, da_subln,
           w_proj_da, w_proj_ret, w_out, ffn_norm, dense_w1, dense_w3, dense_w2, w_router, moe_w1, moe_w3,
           moe_w2):
    B, S, D = x.shape
    T = B * S
    depth = attn_norm.shape[0]
    tm = tq = min(ROW_BLOCK, S)
    big = min(2 * ROW_BLOCK, S)
    assert D == D_MODEL and S % big == 0 and tm % (2 * MXU_DIM) == 0 and tm % RET_CHUNK == 0, (x.shape, tm)

    def rope_tables(pos):
        pos_b = jnp.broadcast_to(pos.reshape(-1, 1).astype(F32), (pos.size, LANES))
        return (_rope_tables(pos_b, *_rope_rows(ROPE_ROT_DIM, ROPE_THETA), tm)
                + _rope_tables(pos_b, *_rope_rows(RET_QK_DIM, RET_ROPE_THETA), tm))

    tabs = lax.cond(jnp.all(positions == positions[:1]),
                    lambda: tuple(jnp.tile(t, (B, 1)) for t in rope_tables(positions[0])),
                    lambda: tuple(rope_tables(positions.reshape(T))))
    ret_consts = _retention_consts()
    gsum = jnp.asarray(np.kron(np.eye(MXU_DIM // HEAD), np.ones((HEAD, HEAD))), BF16)

    h = x.reshape(T, D)
    for layer in range(depth):
        lambda_init = 0.8 - 0.6 * math.exp(-0.3 * layer)
        gq = jnp.tile(da_q_norm[layer], PROJ_CHUNK // DA_HEAD_DIM).reshape(1, PROJ_CHUNK)
        gk = jnp.tile(da_k_norm[layer], PROJ_CHUNK // DA_HEAD_DIM).reshape(1, PROJ_CHUNK)
        z, vt = _in_proj(h, attn_norm[layer].reshape(1, D), w_in[layer].astype(BF16), gq, gk, gsum, tabs, big, tq)
        lam_rows = jnp.stack([lam_q1[layer], lam_k1[layer], lam_q2[layer], lam_k2[layer]])
        ya = _diff_attn(z, vt, _score_bound(da_q_norm[layer], da_k_norm[layer]), lam_rows,
                        da_subln[layer].reshape(1, DA_V_DIM), B, S, tq, lambda_init)
        yr = _retention(z, ret_consts, B, S, big)
        j = layer // 2
        merge_args = (ya, yr, z, h, w_proj_da[layer].astype(BF16), w_proj_ret[layer].astype(BF16),
                      w_out[layer].astype(BF16), ffn_norm[layer].reshape(1, D))
        if layer % 2 == 0:
            h, hn = _merge(*merge_args, big)
            h = _ffn(hn, h, dense_w1[j].astype(BF16), dense_w3[j].astype(BF16), dense_w2[j].astype(BF16), big)
        else:
            h, info, cnt = _merge(*merge_args, big, w_router=w_router[j])
            h = _moe(h, info, cnt, ffn_norm[layer].reshape(1, D), moe_w1[j].astype(BF16),
                     moe_w3[j].astype(BF16), moe_w2[j].astype(BF16))
    return h.reshape(B, S, D)
```
